```python
import math
import jax, jax.numpy as jnp
from jax import lax
import numpy as np

D_MODEL = 1024
BATCH = 16
SEQ = 2048
DEPTH = 4

CHUNK = 64
N_MEM = 256
MIX_WIDTH = D_MODEL
M_WIDTH = MIX_WIDTH // 2
M_HEADS = 4
M_HDIM = M_WIDTH // M_HEADS
M_CONV = 4
F_BIAS_LO = 3.0
F_BIAS_HI = 6.0
R_WIDTH = MIX_WIDTH - M_WIDTH
R_HDIM = 64
R_HEADS = R_WIDTH // R_HDIM
R_DECAY_LORA = 64
R_AAA_LORA = 64
R_GATE_LORA = 160
DECAY_SCALE = math.exp(-0.5)
X_HEADS = 4
X_HDIM = D_MODEL // X_HEADS
D_FF = 2816
FFN_CONV = 3
NORM_EPS = 1e-6
GN_EPS = 64e-5
M_COLS = 4 * M_WIDTH + 2 * M_HEADS
R_COLS = 3 * R_WIDTH + R_DECAY_LORA + R_AAA_LORA + R_GATE_LORA
IN_COLS = M_COLS + R_COLS

kernel_name = 'hybrid_mlstm_rwkv7_encoder'


def rmsnorm(x, g, eps=NORM_EPS):
    xf = x.astype(jnp.float32)
    y = xf * lax.rsqrt(jnp.mean(xf * xf, -1, keepdims=True) + eps)
    return (y * g.astype(jnp.float32)).astype(x.dtype)


def standardize_heads(y, eps):
    yf = y.astype(jnp.float32)
    mu = jnp.mean(yf, -1, keepdims=True)
    var = jnp.mean(jnp.square(yf - mu), -1, keepdims=True)
    return (yf - mu) * lax.rsqrt(var + eps)


def causal_dwconv(x, w, b):
    width = w.shape[0]
    seq = x.shape[1]
    xp = jnp.pad(x, ((0, 0), (width - 1, 0), (0, 0)))
    y = b
    for j in range(width):
        y = y + xp[:, j:j + seq] * w[j]
    return y


def token_shift(p, mu):
    prev = jnp.pad(p, ((0, 0), (1, 0), (0, 0)))[:, :-1]
    return p + (prev - p) * mu


def mlstm_chunkwise(q, k, v, i_pre, f_pre):
    B, H, S, dh = q.shape
    nc = S // CHUNK
    f32 = jnp.float32
    q = q.astype(f32).reshape(B, H, nc, CHUNK, dh) * (dh ** -0.5)
    k = k.astype(f32).reshape(B, H, nc, CHUNK, dh)
    v = v.astype(f32).reshape(B, H, nc, CHUNK, dh)
    logi = i_pre.astype(f32).reshape(B, H, nc, CHUNK)
    logf = jax.nn.log_sigmoid(f_pre.astype(f32)).reshape(B, H, nc, CHUNK)
    b = jnp.cumsum(logf, -1)
    g = b[..., -1]
    a = g[..., None] - b + logi
    m_loc = jnp.max(a, -1)
    wa = jnp.exp(a - m_loc[..., None])
    c_loc = jnp.einsum('bhclv,bhclk->bhcvk', v * wa[..., None], k)
    n_loc = jnp.einsum('bhcl,bhclk->bhck', wa, k)

    def step(carry, inp):
        c, n, m = carry
        g_c, m_l, c_l, n_l = inp
        m_new = jnp.maximum(g_c + m, m_l)
        s_old = jnp.exp(g_c + m - m_new)
        s_loc = jnp.exp(m_l - m_new)
        c_new = s_old[..., None, None] * c + s_loc[..., None, None] * c_l
        n_new = s_old[..., None] * n + s_loc[..., None] * n_l
        return (c_new, n_new, m_new), (c, n, m)

    init = (jnp.zeros((B, H, dh, dh), f32), jnp.zeros((B, H, dh), f32), jnp.zeros((B, H), f32))
    xs = (jnp.moveaxis(g, 2, 0), jnp.moveaxis(m_loc, 2, 0),
          jnp.moveaxis(c_loc, 2, 0), jnp.moveaxis(n_loc, 2, 0))
    _, (c_prev, n_prev, m_prev) = lax.scan(step, init, xs)
    c_prev = jnp.moveaxis(c_prev, 0, 2)
    n_prev = jnp.moveaxis(n_prev, 0, 2)
    m_prev = jnp.moveaxis(m_prev, 0, 2)

    inter = b + m_prev[..., None]
    causal = jnp.tril(jnp.ones((CHUNK, CHUNK), dtype=bool))
    d = jnp.where(causal, b[..., :, None] - b[..., None, :] + logi[..., None, :], -jnp.inf)
    m_t = jnp.maximum(inter, jnp.max(d, -1))
    s_int = jnp.exp(inter - m_t)
    p = jnp.exp(d - m_t[..., None]) * jnp.einsum('bhctd,bhcsd->bhcts', q, k)
    num = (s_int[..., None] * jnp.einsum('bhcvk,bhctk->bhctv', c_prev, q)
           + jnp.einsum('bhcts,bhcsv->bhctv', p, v))
    den = s_int * jnp.einsum('bhck,bhctk->bhct', n_prev, q) + jnp.sum(p, -1)
    h = num / jnp.maximum(jnp.abs(den), jnp.exp(-m_t))[..., None]
    return h.reshape(B, H, S, dh)


def rwkv7_scan(r, w, k, v, kk, a):
    B, S, H, N = r.shape

    def step(state, inp):
        r_t, w_t, k_t, v_t, kk_t, a_t = inp
        sk = jnp.einsum('bhvk,bhk->bhv', state, kk_t)
        state = (state * w_t[:, :, None, :]
                 - sk[..., None] * (kk_t * a_t)[:, :, None, :]
                 + v_t[..., None] * k_t[:, :, None, :])
        y = jnp.einsum('bhvk,bhk->bhv', state, r_t)
        return state, y

    xs = (jnp.moveaxis(r, 1, 0), jnp.moveaxis(w, 1, 0), jnp.moveaxis(k, 1, 0),
          jnp.moveaxis(v, 1, 0), jnp.moveaxis(kk, 1, 0), jnp.moveaxis(a, 1, 0))
    _, y = lax.scan(step, jnp.zeros((B, H, N, N), jnp.float32), xs)
    return jnp.moveaxis(y, 0, 1)


def hybrid_token_mixer(h, w_in, m_conv_w, m_conv_b, m_gate_b, m_norm_g, r_mu, r_w0, r_w_up,
                       r_a0, r_a_up, r_g_up, r_kk, r_ka, r_rk, r_gn_g, r_gn_b, w_out):
    B, S, _ = h.shape
    f32 = jnp.float32
    p = h @ w_in
    pm, pr = p[..., :M_COLS], p[..., M_COLS:]

    qk = jax.nn.silu(causal_dwconv(pm[..., :2 * M_WIDTH], m_conv_w, m_conv_b))
    q, k = qk[..., :M_WIDTH], qk[..., M_WIDTH:]
    v = pm[..., 2 * M_WIDTH:3 * M_WIDTH]
    o = pm[..., 3 * M_WIDTH:4 * M_WIDTH]
    gates = pm[..., 4 * M_WIDTH:] + m_gate_b
    i_pre, f_pre = gates[..., :M_HEADS], gates[..., M_HEADS:]

    def to_heads(t):
        return t.reshape(B, S, M_HEADS, M_HDIM).transpose(0, 2, 1, 3)

    hm = mlstm_chunkwise(to_heads(q), to_heads(k), to_heads(v),
                         i_pre.transpose(0, 2, 1), f_pre.transpose(0, 2, 1))
    hm = standardize_heads(hm.transpose(0, 2, 1, 3), NORM_EPS).reshape(B, S, M_WIDTH)
    y_m = (jax.nn.sigmoid(o.astype(f32)) * hm * m_norm_g.astype(f32)).astype(h.dtype)

    pr = token_shift(pr, r_mu)
    rr = pr[..., :R_WIDTH]
    kr = pr[..., R_WIDTH:2 * R_WIDTH]
    vr = pr[..., 2 * R_WIDTH:3 * R_WIDTH]
    off = 3 * R_WIDTH
    wd = pr[..., off:off + R_DECAY_LORA]
    off = off + R_DECAY_LORA
    ad = pr[..., off:off + R_AAA_LORA]
    off = off + R_AAA_LORA
    gd = pr[..., off:off + R_GATE_LORA]
    w = jnp.exp(-DECAY_SCALE * jax.nn.sigmoid((r_w0 + jnp.tanh(wd) @ r_w_up).astype(f32)))
    a = jax.nn.sigmoid((r_a0 + ad @ r_a_up).astype(f32))
    g = jax.nn.sigmoid(gd) @ r_g_up

    def rh(t):
        return t.astype(f32).reshape(B, S, R_HEADS, R_HDIM)

    kk = rh(kr * r_kk)
    kk = kk / jnp.maximum(jnp.sqrt(jnp.sum(kk * kk, -1, keepdims=True)), 1e-12)
    kr = kr.astype(f32) * (1.0 + (a - 1.0) * r_ka.astype(f32))
    rr, kr, vr, w, a = rh(rr), rh(kr), rh(vr), rh(w), rh(a)
    yr = rwkv7_scan(rr, w, kr, vr, kk, a)
    yr = (standardize_heads(yr, GN_EPS) * r_gn_g.astype(f32).reshape(R_HEADS, R_HDIM)
          + r_gn_b.astype(f32).reshape(R_HEADS, R_HDIM))
    yr = yr + jnp.sum(rr * kr * r_rk.astype(f32), -1, keepdims=True) * vr
    y_r = (yr.reshape(B, S, R_WIDTH) * g.astype(f32)).astype(h.dtype)

    return jnp.concatenate([y_m, y_r], -1) @ w_out


def mem_cross_attn(h, memn, w_q, w_kv, w_o):
    B, S, _ = h.shape
    M = memn.shape[1]
    q = (h @ w_q).reshape(B, S, X_HEADS, X_HDIM)
    kv = memn @ w_kv
    k = kv[..., :D_MODEL].reshape(B, M, X_HEADS, X_HDIM)
    v = kv[..., D_MODEL:].reshape(B, M, X_HEADS, X_HDIM)
    s = jnp.einsum('bshd,bmhd->bhsm', q, k).astype(jnp.float32) * (X_HDIM ** -0.5)
    pr = jax.nn.softmax(s, -1).astype(v.dtype)
    o = jnp.einsum('bhsm,bmhd->bshd', pr, v).reshape(B, S, D_MODEL)
    return o @ w_o


def conv_glu_ffn(h, w_up, conv_w, conv_b, w_down):
    u = h @ w_up
    gate, val = u[..., :D_FF], u[..., D_FF:]
    gate = causal_dwconv(gate, conv_w, conv_b)
    return (jax.nn.silu(gate) * val) @ w_down


def setup_inputs(seed: int = 0) -> dict:
    key = jax.random.key(seed)
    ks = iter(jax.random.split(key, 48))
    L = DEPTH

    def nrm(shape, scale):
        return jax.random.normal(next(ks), shape, jnp.float32) * scale

    def gain(shape):
        return 1.0 + nrm(shape, 0.02)

    def unif(shape, lo, hi):
        return jax.random.uniform(next(ks), shape, jnp.float32, minval=lo, maxval=hi)

    x = nrm((BATCH, SEQ, D_MODEL), 1.0)
    mem = nrm((BATCH, N_MEM, D_MODEL), 1.0)
    f_bias = jnp.linspace(F_BIAS_LO, F_BIAS_HI, M_HEADS, dtype=jnp.float32)
    m_gate_b = jnp.concatenate([nrm((L, M_HEADS), 0.1), f_bias + nrm((L, M_HEADS), 0.1)], -1)
    return {
        'x': x,
        'mem': mem,
        'norm_mix': gain((L, D_MODEL)),
        'w_in': nrm((L, D_MODEL, IN_COLS), D_MODEL ** -0.5),
        'm_conv_w': nrm((L, M_CONV, 2 * M_WIDTH), M_CONV ** -0.5),
        'm_conv_b': nrm((L, 2 * M_WIDTH), 0.02),
        'm_gate_b': m_gate_b,
        'm_norm_g': gain((L, M_WIDTH)),
        'r_mu': unif((L, R_COLS), 0.0, 1.0),
        'r_w0': unif((L, R_WIDTH), -2.0, 2.0),
        'r_w_up': nrm((L, R_DECAY_LORA, R_WIDTH), R_DECAY_LORA ** -0.5),
        'r_a0': nrm((L, R_WIDTH), 0.1),
        'r_a_up': nrm((L, R_AAA_LORA, R_WIDTH), R_AAA_LORA ** -0.5),
        'r_g_up': nrm((L, R_GATE_LORA, R_WIDTH), R_GATE_LORA ** -0.5),
        'r_kk': 0.85 + nrm((L, R_WIDTH), 0.05),
        'r_ka': 1.0 + nrm((L, R_WIDTH), 0.05),
        'r_rk': nrm((L, R_HEADS, R_HDIM), 0.1),
        'r_gn_g': gain((L, R_WIDTH)),
        'r_gn_b': nrm((L, R_WIDTH), 0.02),
        'w_out': nrm((L, MIX_WIDTH, D_MODEL), MIX_WIDTH ** -0.5),
        'norm_x': gain((L, D_MODEL)),
        'norm_mem': gain((L, D_MODEL)),
        'x_wq': nrm((L, D_MODEL, D_MODEL), D_MODEL ** -0.5),
        'x_wkv': nrm((L, D_MODEL, 2 * D_MODEL), D_MODEL ** -0.5),
        'x_wo': nrm((L, D_MODEL, D_MODEL), D_MODEL ** -0.5),
        'norm_ffn': gain((L, D_MODEL)),
        'f_up': nrm((L, D_MODEL, 2 * D_FF), D_MODEL ** -0.5),
        'f_conv_w': nrm((L, FFN_CONV, D_FF), FFN_CONV ** -0.5),
        'f_conv_b': nrm((L, D_FF), 0.02),
        'f_down': nrm((L, D_FF, D_MODEL), D_FF ** -0.5),
        'norm_final': gain((D_MODEL,)),
    }


def reference(x, mem, norm_mix, w_in, m_conv_w, m_conv_b, m_gate_b, m_norm_g, r_mu, r_w0,
              r_w_up, r_a0, r_a_up, r_g_up, r_kk, r_ka, r_rk, r_gn_g, r_gn_b, w_out,
              norm_x, norm_mem, x_wq, x_wkv, x_wo, norm_ffn, f_up, f_conv_w, f_conv_b,
              f_down, norm_final):
    for l in range(DEPTH):
        h = rmsnorm(x, norm_mix[l])
        x = x + hybrid_token_mixer(h, w_in[l], m_conv_w[l], m_conv_b[l], m_gate_b[l],
                                   m_norm_g[l], r_mu[l], r_w0[l], r_w_up[l], r_a0[l],
                                   r_a_up[l], r_g_up[l], r_kk[l], r_ka[l], r_rk[l],
                                   r_gn_g[l], r_gn_b[l], w_out[l])
        h = rmsnorm(x, norm_x[l])
        memn = rmsnorm(mem, norm_mem[l])
        x = x + mem_cross_attn(h, memn, x_wq[l], x_wkv[l], x_wo[l])
        h = rmsnorm(x, norm_ffn[l])
        x = x + conv_glu_ffn(h, f_up[l], f_conv_w[l], f_conv_b[l], f_down[l])
    return rmsnorm(x, norm_final)
```

```python
import functools
import math

import jax
import jax.numpy as jnp
from jax import lax
from jax.experimental import pallas as pl
from jax.experimental.pallas import tpu as pltpu

F32 = jnp.float32
BF16 = jnp.bfloat16

D_MODEL = 1024
M_WIDTH = 512
M_HEADS = 4
M_HDIM = 128
M_CONV = 4
R_WIDTH = 512
R_HDIM = 64
R_HEADS = 8
R_DECAY_LORA = 64
R_AAA_LORA = 64
R_GATE_LORA = 160
DECAY_SCALE = math.exp(-0.5)
X_HEADS = 4
X_HDIM = 256
D_FF = 2816
FFN_CONV = 3
NORM_EPS = 1e-6
GN_EPS = 64e-5
CHUNK = 64

LANES = 128
GATE_PAD = LANES
GLORA_PAD = 2 * LANES
OFF_MQ, OFF_MK, OFF_MV, OFF_MO = 0, 512, 1024, 1536
OFF_MG = 2048
OFF_RR = OFF_MG + GATE_PAD
OFF_RK = OFF_RR + R_WIDTH
OFF_RV = OFF_RK + R_WIDTH
OFF_RWA = OFF_RV + R_WIDTH
OFF_RG = OFF_RWA + LANES
IN_COLS_P = OFF_RG + GLORA_PAD

VMEM_LIMIT = 48 * 1024 * 1024

HI = lax.Precision.HIGHEST


def _dot(a, b, precision=None):
    return jnp.dot(a, b, preferred_element_type=F32, precision=precision)


def _dot_nt(a, b, precision=None):
    return lax.dot_general(a, b, (((1,), (1,)), ((), ())),
                           preferred_element_type=F32, precision=precision)


def _dot_tn(a, b, precision=None):
    return lax.dot_general(a, b, (((0,), (0,)), ((), ())),
                           preferred_element_type=F32, precision=precision)


def _sigmoid(x):
    return 1.0 / (1.0 + jnp.exp(-x))


def _shift_rows(x, sh):
    row = lax.broadcasted_iota(jnp.int32, x.shape, 0)
    return jnp.where(row >= sh, pltpu.roll(x, sh, 0), 0.0)


def _mm_kernel(*refs, has_gain, has_resid):
    x_ref, w_ref = refs[0], refs[1]
    pos = 2
    g_ref = r_ref = None
    if has_gain:
        g_ref = refs[pos]
        pos += 1
    if has_resid:
        r_ref = refs[pos]
        pos += 1
    o_ref, h_scr = refs[pos], refs[pos + 1]

    @pl.when(pl.program_id(1) == 0)
    def _():
        x = x_ref[...].astype(F32)
        if has_gain:
            ms = jnp.mean(x * x, axis=-1, keepdims=True)
            x = x * lax.rsqrt(ms + NORM_EPS) * g_ref[...]
        h_scr[...] = x.astype(BF16)

    acc = _dot(h_scr[...], w_ref[...])
    if has_resid:
        acc = acc + r_ref[...]
    o_ref[...] = acc.astype(o_ref.dtype)


def _mm(x, w, gain=None, resid=None, out_dtype=F32, tm=512, tn=512):
    T, K = x.shape
    N = w.shape[1]
    tm = min(tm, T)
    tn = min(tn, N)
    assert T % tm == 0 and N % tn == 0
    in_specs = [pl.BlockSpec((tm, K), lambda i, j: (i, 0)),
                pl.BlockSpec((K, tn), lambda i, j: (0, j))]
    args = [x, w]
    if gain is not None:
        in_specs.append(pl.BlockSpec((1, K), lambda i, j: (0, 0)))
        args.append(gain.reshape(1, K))
    if resid is not None:
        in_specs.append(pl.BlockSpec((tm, tn), lambda i, j: (i, j)))
        args.append(resid)
    return pl.pallas_call(
        functools.partial(_mm_kernel, has_gain=gain is not None, has_resid=resid is not None),
        out_shape=jax.ShapeDtypeStruct((T, N), out_dtype),
        grid=(T // tm, N // tn),
        in_specs=in_specs,
        out_specs=pl.BlockSpec((tm, tn), lambda i, j: (i, j)),
        scratch_shapes=[pltpu.VMEM((tm, K), BF16)],
        compiler_params=pltpu.CompilerParams(
            dimension_semantics=("parallel", "arbitrary"), vmem_limit_bytes=VMEM_LIMIT),
        name="mm",
    )(*args)


def _rms_kernel(x_ref, g_ref, o_ref):
    x = x_ref[...]
    ms = jnp.mean(x * x, axis=-1, keepdims=True)
    o_ref[...] = x * lax.rsqrt(ms + NORM_EPS) * g_ref[...]


def _rmsnorm(x, gain, tm=1024):
    T, K = x.shape
    tm = min(tm, T)
    return pl.pallas_call(
        _rms_kernel,
        out_shape=jax.ShapeDtypeStruct((T, K), F32),
        grid=(T // tm,),
        in_specs=[pl.BlockSpec((tm, K), lambda i: (i, 0)),
                  pl.BlockSpec((1, K), lambda i: (0, 0))],
        out_specs=pl.BlockSpec((tm, K), lambda i: (i, 0)),
        compiler_params=pltpu.CompilerParams(
            dimension_semantics=("parallel",), vmem_limit_bytes=VMEM_LIMIT),
        name="rmsnorm",
    )(x, gain.reshape(1, K))


def _mlstm_kernel(q_ref, k_ref, v_ref, o_ref, grow_ref, gcol_ref, brow_ref, bcol_ref,
                  cwq_ref, cwk_ref, cbq_ref, cbk_ref, ng_ref, out_ref,
                  qc_scr, kc_scr, c_scr, n_scr, m_scr):
    S = q_ref.shape[1]
    L = CHUNK
    nc = S // L

    def conv_silu(x, w_ref, b_ref):
        y = b_ref[...] + x * w_ref[M_CONV - 1:M_CONV, :]
        for j in range(M_CONV - 1):
            y = y + _shift_rows(x, M_CONV - 1 - j) * w_ref[j:j + 1, :]
        return y * _sigmoid(y)

    qc_scr[...] = conv_silu(q_ref[0], cwq_ref, cbq_ref) * (M_HDIM ** -0.5)
    kc_scr[...] = conv_silu(k_ref[0], cwk_ref, cbk_ref)
    c_scr[...] = jnp.zeros_like(c_scr)
    n_scr[...] = jnp.zeros_like(n_scr)
    m_scr[...] = jnp.zeros_like(m_scr)

    ti = lax.broadcasted_iota(jnp.int32, (L, L), 0)
    si = lax.broadcasted_iota(jnp.int32, (L, L), 1)
    causal = si <= ti

    def log_sigmoid(x):
        return jnp.minimum(x, 0.0) - jnp.log1p(jnp.exp(-jnp.abs(x)))

    def body(c, carry):
        r0 = pl.multiple_of(c * L, L)
        q = qc_scr[pl.ds(r0, L), :]
        k = kc_scr[pl.ds(r0, L), :]
        v = v_ref[0, pl.ds(r0, L), :]
        gr = grow_ref[0, 0, c] + brow_ref[0]
        gc = gcol_ref[0, 0, c] + bcol_ref[0]
        logi_r = gr[0:1, :]
        logf_r = log_sigmoid(gr[1:2, :])
        logi_c = gc[:, 0:1]
        logf_c = log_sigmoid(gc[:, 1:2])
        b_c = jnp.sum(jnp.where(causal, logf_r, 0.0), axis=1, keepdims=True)
        b_r = jnp.sum(jnp.where(ti <= si, logf_c, 0.0), axis=0, keepdims=True)
        g = jnp.sum(logf_r, axis=1, keepdims=True)
        a_r = g - b_r + logi_r
        a_c = g - b_c + logi_c
        m_loc = jnp.max(a_r, axis=1, keepdims=True)
        wa_c = jnp.exp(a_c - m_loc)
        c_loc = _dot_tn(v * wa_c, k, HI)
        n_loc = jnp.sum(k * wa_c, axis=0, keepdims=True)

        c_prev = c_scr[...]
        n_prev = n_scr[...]
        m_prev = m_scr[...]
        inter = b_c + m_prev
        d = jnp.where(causal, b_c - b_r + logi_r, -jnp.inf)
        m_t = jnp.maximum(inter, jnp.max(d, axis=1, keepdims=True))
        s_int = jnp.exp(inter - m_t)
        p = jnp.exp(d - m_t) * _dot_nt(q, k, HI)
        num = s_int * _dot_nt(q, c_prev, HI) + _dot(p, v, HI)
        den = (s_int * jnp.sum(q * n_prev, axis=1, keepdims=True)
               + jnp.sum(p, axis=1, keepdims=True))
        h = num / jnp.maximum(jnp.abs(den), jnp.exp(-m_t))
        mu = jnp.mean(h, axis=-1, keepdims=True)
        hc = h - mu
        var = jnp.mean(hc * hc, axis=-1, keepdims=True)
        hn = hc * lax.rsqrt(var + NORM_EPS)
        o = o_ref[0, pl.ds(r0, L), :]
        out_ref[0, pl.ds(r0, L), :] = (_sigmoid(o) * hn * ng_ref[...]).astype(out_ref.dtype)

        m_new = jnp.maximum(g + m_prev, m_loc)
        s_old = jnp.exp(g + m_prev - m_new)
        s_loc = jnp.exp(m_loc - m_new)
        c_scr[...] = s_old * c_prev + s_loc * c_loc
        n_scr[...] = s_old * n_prev + s_loc * n_loc
        m_scr[...] = m_new
        return carry

    lax.fori_loop(0, nc, body, 0)


def _mlstm(p_all, g_row, g_col, b_row, b_col, conv_w, conv_b, norm_g):
    B, S, _ = p_all.shape
    nc = S // CHUNK
    nb = M_WIDTH // LANES
    seq = lambda off: pl.BlockSpec((1, S, LANES), lambda b, h, off=off: (b, 0, off + h))
    return pl.pallas_call(
        _mlstm_kernel,
        out_shape=jax.ShapeDtypeStruct((B, S, M_WIDTH), BF16),
        grid=(B, M_HEADS),
        in_specs=[
            seq(OFF_MQ // LANES), seq(OFF_MK // LANES), seq(OFF_MV // LANES), seq(OFF_MO // LANES),
            pl.BlockSpec((1, 1, nc, 2, CHUNK), lambda b, h: (b, h, 0, 0, 0)),
            pl.BlockSpec((1, 1, nc, CHUNK, 2), lambda b, h: (b, h, 0, 0, 0)),
            pl.BlockSpec((1, 2, 1), lambda b, h: (h, 0, 0)),
            pl.BlockSpec((1, 1, 2), lambda b, h: (h, 0, 0)),
            pl.BlockSpec((M_CONV, LANES), lambda b, h: (0, h)),
            pl.BlockSpec((M_CONV, LANES), lambda b, h: (0, nb + h)),
            pl.BlockSpec((1, LANES), lambda b, h: (0, h)),
            pl.BlockSpec((1, LANES), lambda b, h: (0, nb + h)),
            pl.BlockSpec((1, LANES), lambda b, h: (0, h)),
        ],
        out_specs=pl.BlockSpec((1, S, LANES), lambda b, h: (b, 0, h)),
        scratch_shapes=[pltpu.VMEM((S, M_HDIM), F32), pltpu.VMEM((S, M_HDIM), F32),
                        pltpu.VMEM((M_HDIM, M_HDIM), F32), pltpu.VMEM((1, M_HDIM), F32),
                        pltpu.VMEM((1, 1), F32)],
        compiler_params=pltpu.CompilerParams(
            dimension_semantics=("parallel", "parallel"), vmem_limit_bytes=VMEM_LIMIT),
        name="mlstm",
    )(p_all, p_all, p_all, p_all, g_row, g_col, b_row, b_col,
      conv_w, conv_w, conv_b, conv_b, norm_g)


def _inv_unit_lower(a):
    L = a.shape[0]
    ti = lax.broadcasted_iota(jnp.int32, (L, L), 0)
    si = lax.broadcasted_iota(jnp.int32, (L, L), 1)

    def off_mask(b):
        sh = (2 * b).bit_length() - 1
        same = jnp.right_shift(ti, sh) == jnp.right_shift(si, sh)
        return same & (jnp.bitwise_and(ti, b) != 0) & (jnp.bitwise_and(si, b) == 0)

    eye = (ti == si).astype(F32)
    inv = eye - jnp.where(off_mask(1), a, 0.0)
    b = 2
    while b < L:
        off = jnp.where(off_mask(b), a, 0.0)
        inv = inv - _dot(_dot(inv, off, HI), inv, HI)
        b *= 2
    return inv


def _rwkv_kernel(pr_ref, pk_ref, pv_ref, pwa_ref, pg_ref,
                 mur_ref, muk_ref, muv_ref, muwa_ref, mug_ref,
                 w0_ref, wup_ref, a0_ref, aup_ref, gup_ref,
                 kkp_ref, ka_ref, rk_ref, gng_ref, gnb_ref,
                 out_ref,
                 rr_scr, km_scr, vr_scr, kk_scr, be_scr, lw_scr, g_scr, y_scr, st_scr):
    S = pr_ref.shape[1]
    L = CHUNK
    nc = S // L
    N = R_HDIM

    def tshift(p_ref, mu_ref):
        p = p_ref[0]
        return p + (_shift_rows(p, 1) - p) * mu_ref[...]

    hsh = N.bit_length() - 1
    ones_bd = (jnp.right_shift(lax.broadcasted_iota(jnp.int32, (LANES, LANES), 0), hsh)
               == jnp.right_shift(lax.broadcasted_iota(jnp.int32, (LANES, LANES), 1), hsh)
               ).astype(F32)

    rr = tshift(pr_ref, mur_ref)
    kr = tshift(pk_ref, muk_ref)
    vr = tshift(pv_ref, muv_ref)
    wa = tshift(pwa_ref, muwa_ref)
    gd = tshift(pg_ref, mug_ref)
    logw = -DECAY_SCALE * _sigmoid(w0_ref[...] + _dot(jnp.tanh(wa).astype(BF16), wup_ref[...]))
    a = _sigmoid(a0_ref[...] + _dot(wa.astype(BF16), aup_ref[...]))
    g = _dot(_sigmoid(gd).astype(BF16), gup_ref[...])
    kkraw = kr * kkp_ref[...]
    ss = _dot(kkraw * kkraw, ones_bd, HI)
    kk = kkraw / jnp.maximum(jnp.sqrt(ss), 1e-12)
    km = kr * (1.0 + (a - 1.0) * ka_ref[...])
    rr_scr[...] = rr
    km_scr[...] = km
    vr_scr[...] = vr
    kk_scr[...] = kk
    be_scr[...] = kk * a
    lw_scr[...] = logw
    g_scr[...] = g
    st_scr[...] = jnp.zeros_like(st_scr)

    ti = lax.broadcasted_iota(jnp.int32, (L, L), 0)
    si = lax.broadcasted_iota(jnp.int32, (L, L), 1)
    strict = si < ti
    incl = si <= ti
    tril = incl.astype(F32)
    h0 = lax.broadcasted_iota(jnp.int32, (L, LANES), 1) < N

    def per_head(x0, x1):
        return jnp.where(h0, x0, x1)

    def body(c, carry):
        r0 = pl.multiple_of(c * L, L)
        rows = pl.ds(r0, L)
        rr_c = rr_scr[rows, :]
        km_c = km_scr[rows, :]
        v_c = vr_scr[rows, :]
        kk_c = kk_scr[rows, :]
        be_c = be_scr[rows, :]
        lw = lw_scr[rows, :]
        bincl = _dot(tril, lw, HI)
        e_in = jnp.exp(bincl)
        e_ex = jnp.exp(bincl - lw)
        e_ng = jnp.exp(-bincl)
        kt = kk_c * e_ex
        rt = rr_c * e_in
        bh = be_c * e_ng
        kh = km_c * e_ng
        p_end = e_in[L - 1:L, :]

        lhs = jnp.concatenate([jnp.where(h0, kt, 0.0), jnp.where(h0, 0.0, kt),
                               jnp.where(h0, rt, 0.0), jnp.where(h0, 0.0, rt)], axis=0)
        gb = _dot_nt(lhs, bh, HI)
        gk = _dot_nt(lhs, kh, HI)
        a0 = jnp.where(strict, gb[0:L], 0.0)
        a1 = jnp.where(strict, gb[L:2 * L], 0.0)
        arb0 = jnp.where(incl, gb[2 * L:3 * L], 0.0)
        arb1 = jnp.where(incl, gb[3 * L:4 * L], 0.0)
        bk0 = jnp.where(strict, gk[0:L], 0.0)
        bk1 = jnp.where(strict, gk[L:2 * L], 0.0)
        ark0 = jnp.where(incl, gk[2 * L:3 * L], 0.0)
        ark1 = jnp.where(incl, gk[3 * L:4 * L], 0.0)
        t0 = _inv_unit_lower(a0)
        t1 = _inv_unit_lower(a1)
        bkv = per_head(_dot(bk0, v_c, HI), _dot(bk1, v_c, HI))
        x = jnp.concatenate([kt, bkv], axis=1)
        tx0 = _dot(t0, x, HI)
        tx1 = _dot(t1, x, HI)
        w = per_head(tx0[:, :LANES], tx1[:, :LANES])
        u0 = -per_head(tx0[:, LANES:], tx1[:, LANES:])
        arkv = per_head(_dot(ark0, v_c, HI), _dot(ark1, v_c, HI))

        st = st_scr[...]
        rw = _dot_nt(jnp.concatenate([rt, w], axis=0), st, HI)
        u = u0 - rw[L:]
        y = rw[:L] + per_head(_dot(arb0, u, HI), _dot(arb1, u, HI)) + arkv
        y_scr[rows, :] = y
        upd = _dot_tn(jnp.concatenate([u, v_c], axis=0),
                      jnp.concatenate([bh * p_end, kh * p_end], axis=0), HI)
        st_scr[...] = st * p_end + upd * ones_bd
        return carry

    lax.fori_loop(0, nc, body, 0)

    y = y_scr[...]
    inv_n = 1.0 / N
    mu = _dot(y, ones_bd, HI) * inv_n
    yc = y - mu
    var = _dot(yc * yc, ones_bd, HI) * inv_n
    yn = yc * lax.rsqrt(var + GN_EPS) * gng_ref[...] + gnb_ref[...]
    bonus = _dot(rr_scr[...] * km_scr[...] * rk_ref[...], ones_bd, HI) * vr_scr[...]
    out_ref[0] = ((yn + bonus) * g_scr[...]).astype(out_ref.dtype)


def _rwkv(p_all, mu, w0, w_up, a0, a_up, g_up, kkp, ka, rk, gn_g, gn_b):
    B, S, _ = p_all.shape
    npair = R_WIDTH // LANES
    seq = lambda off: pl.BlockSpec((1, S, LANES), lambda b, h, off=off: (b, 0, off + h))
    vec = lambda off: pl.BlockSpec((1, LANES), lambda b, h, off=off: (0, off + h))
    fix = lambda blk: pl.BlockSpec((1, LANES), lambda b, h, blk=blk: (0, blk))
    return pl.pallas_call(
        _rwkv_kernel,
        out_shape=jax.ShapeDtypeStruct((B, S, R_WIDTH), BF16),
        grid=(B, npair),
        in_specs=[
            seq(OFF_RR // LANES), seq(OFF_RK // LANES), seq(OFF_RV // LANES),
            pl.BlockSpec((1, S, LANES), lambda b, h: (b, 0, OFF_RWA // LANES)),
            pl.BlockSpec((1, S, GLORA_PAD), lambda b, h: (b, 0, OFF_RG // GLORA_PAD)),
            vec(OFF_RR // LANES), vec(OFF_RK // LANES), vec(OFF_RV // LANES),
            fix(OFF_RWA // LANES),
            pl.BlockSpec((1, GLORA_PAD), lambda b, h: (0, OFF_RG // GLORA_PAD)),
            vec(0),
            pl.BlockSpec((LANES, LANES), lambda b, h: (0, h)),
            vec(0),
            pl.BlockSpec((LANES, LANES), lambda b, h: (0, h)),
            pl.BlockSpec((GLORA_PAD, LANES), lambda b, h: (0, h)),
            vec(0), vec(0), vec(0), vec(0), vec(0),
        ],
        out_specs=pl.BlockSpec((1, S, LANES), lambda b, h: (b, 0, h)),
        scratch_shapes=[pltpu.VMEM((S, LANES), F32) for _ in range(8)]
        + [pltpu.VMEM((LANES, LANES), F32)],
        compiler_params=pltpu.CompilerParams(
            dimension_semantics=("parallel", "parallel"), vmem_limit_bytes=VMEM_LIMIT),
        name="rwkv",
    )(p_all, p_all, p_all, p_all, p_all, mu, mu, mu, mu, mu,
      w0, w_up, a0, a_up, g_up, kkp, ka, rk, gn_g, gn_b)


def _xattn_kernel(q_ref, k_ref, v_ref, o_ref):
    q = q_ref[0]
    s = _dot_nt(q, k_ref[0]) * (X_HDIM ** -0.5)
    s = s - jnp.max(s, axis=-1, keepdims=True)
    e = jnp.exp(s)
    p = e / jnp.sum(e, axis=-1, keepdims=True)
    o_ref[0] = _dot(p.astype(BF16), v_ref[0]).astype(o_ref.dtype)


def _xattn(q, kv, ts=1024):
    B, S, D = q.shape
    M = kv.shape[1]
    nh = D // X_HDIM
    ts = min(ts, S)
    return pl.pallas_call(
        _xattn_kernel,
        out_shape=jax.ShapeDtypeStruct((B, S, D), BF16),
        grid=(B, nh, S // ts),
        in_specs=[pl.BlockSpec((1, ts, X_HDIM), lambda b, h, t: (b, t, h)),
                  pl.BlockSpec((1, M, X_HDIM), lambda b, h, t: (b, 0, h)),
                  pl.BlockSpec((1, M, X_HDIM), lambda b, h, t: (b, 0, nh + h))],
        out_specs=pl.BlockSpec((1, ts, X_HDIM), lambda b, h, t: (b, t, h)),
        compiler_params=pltpu.CompilerParams(
            dimension_semantics=("parallel", "parallel", "parallel"),
            vmem_limit_bytes=VMEM_LIMIT),
        name="xattn",
    )(q, kv, kv)


def _glu_kernel(g_ref, v_ref, w_ref, b_ref, o_ref):
    x = g_ref[0]
    y = b_ref[...] + x * w_ref[FFN_CONV - 1:FFN_CONV, :]
    for j in range(FFN_CONV - 1):
        y = y + _shift_rows(x, FFN_CONV - 1 - j) * w_ref[j:j + 1, :]
    o_ref[0] = (y * _sigmoid(y) * v_ref[0]).astype(o_ref.dtype)


def _glu(u, conv_w, conv_b, tc=256):
    B, S, _ = u.shape
    nb = D_FF // tc
    return pl.pallas_call(
        _glu_kernel,
        out_shape=jax.ShapeDtypeStruct((B, S, D_FF), BF16),
        grid=(B, nb),
        in_specs=[pl.BlockSpec((1, S, tc), lambda b, j: (b, 0, j)),
                  pl.BlockSpec((1, S, tc), lambda b, j: (b, 0, nb + j)),
                  pl.BlockSpec((FFN_CONV, tc), lambda b, j: (0, j)),
                  pl.BlockSpec((1, tc), lambda b, j: (0, j))],
        out_specs=pl.BlockSpec((1, S, tc), lambda b, j: (b, 0, j)),
        compiler_params=pltpu.CompilerParams(
            dimension_semantics=("parallel", "parallel"), vmem_limit_bytes=VMEM_LIMIT),
        name="glu",
    )(u, u, conv_w, conv_b.reshape(1, D_FF))


def _pad_cols(w, n):
    return jnp.pad(w, ((0, 0), (0, n - w.shape[1])))


def _relayout_in(w):
    m_main = w[:, :4 * M_WIDTH]
    m_gate = w[:, 4 * M_WIDTH:4 * M_WIDTH + 2 * M_HEADS]
    r0 = 4 * M_WIDTH + 2 * M_HEADS
    r_main = w[:, r0:r0 + 3 * R_WIDTH + R_DECAY_LORA + R_AAA_LORA]
    r_gate = w[:, r0 + 3 * R_WIDTH + R_DECAY_LORA + R_AAA_LORA:]
    return jnp.concatenate(
        [m_main, _pad_cols(m_gate, GATE_PAD), r_main, _pad_cols(r_gate, GLORA_PAD)], axis=1)


def kernel(x, mem, norm_mix, w_in, m_conv_w, m_conv_b, m_gate_b, m_norm_g, r_mu, r_w0,
           r_w_up, r_a0, r_a_up, r_g_up, r_kk, r_ka, r_rk, r_gn_g, r_gn_b, w_out,
           norm_x, norm_mem, x_wq, x_wkv, x_wo, norm_ffn, f_up, f_conv_w, f_conv_b,
           f_down, norm_final):
    B, S, D = x.shape
    M = mem.shape[1]
    depth = w_in.shape[0]
    T = B * S
    nc = S // CHUNK
    xf = x.reshape(T, D)
    memf = mem.reshape(B * M, D)

    for l in range(depth):
        w_in_p = _relayout_in(w_in[l]).astype(BF16)
        p_all = _mm(xf, w_in_p, gain=norm_mix[l]).reshape(B, S, IN_COLS_P)

        gates = p_all[:, :, OFF_MG:OFF_MG + 2 * M_HEADS].reshape(B, nc, CHUNK, 2, M_HEADS)
        g_col = gates.transpose(0, 4, 1, 2, 3)
        g_row = gates.transpose(0, 4, 1, 3, 2)
        gb = m_gate_b[l].reshape(2, M_HEADS).T
        y_m = _mlstm(p_all, g_row, g_col, gb.reshape(M_HEADS, 2, 1), gb.reshape(M_HEADS, 1, 2),
                     m_conv_w[l], m_conv_b[l].reshape(1, -1), m_norm_g[l].reshape(1, -1))

        mu = _relayout_in(jnp.pad(r_mu[l].reshape(1, -1), ((0, 0), (4 * M_WIDTH + 2 * M_HEADS, 0))))
        w_up = jnp.pad(r_w_up[l], ((0, R_AAA_LORA), (0, 0))).astype(BF16)
        a_up = jnp.pad(r_a_up[l], ((R_DECAY_LORA, 0), (0, 0))).astype(BF16)
        g_up = jnp.pad(r_g_up[l], ((0, GLORA_PAD - R_GATE_LORA), (0, 0))).astype(BF16)
        y_r = _rwkv(p_all, mu, r_w0[l].reshape(1, -1), w_up, r_a0[l].reshape(1, -1), a_up, g_up,
                    r_kk[l].reshape(1, -1), r_ka[l].reshape(1, -1), r_rk[l].reshape(1, -1),
                    r_gn_g[l].reshape(1, -1), r_gn_b[l].reshape(1, -1))

        y = jnp.concatenate([y_m, y_r], axis=-1).reshape(T, D)
        xf = _mm(y, w_out[l].astype(BF16), resid=xf)

        q = _mm(xf, x_wq[l].astype(BF16), gain=norm_x[l], out_dtype=BF16)
        kv = _mm(memf, x_wkv[l].astype(BF16), gain=norm_mem[l], out_dtype=BF16)
        o = _xattn(q.reshape(B, S, D), kv.reshape(B, M, 2 * D))
        xf = _mm(o.reshape(T, D), x_wo[l].astype(BF16), resid=xf)

        u = _mm(xf, f_up[l].astype(BF16), gain=norm_ffn[l])
        act = _glu(u.reshape(B, S, 2 * D_FF), f_conv_w[l], f_conv_b[l])
        xf = _mm(act.reshape(T, D_FF), f_down[l].astype(BF16), resid=xf)

    return _rmsnorm(xf, norm_final).reshape(B, S, D)
```

```python
import functools
import math

import jax
import jax.numpy as jnp
from jax import lax
from jax.experimental import pallas as pl
from jax.experimental.pallas import tpu as pltpu

F32 = jnp.float32
BF16 = jnp.bfloat16

D_MODEL = 1024
M_WIDTH = 512
M_HEADS = 4
M_HDIM = 128
M_CONV = 4
R_WIDTH = 512
R_HDIM = 64
R_HEADS = 8
R_DECAY_LORA = 64
R_AAA_LORA = 64
R_GATE_LORA = 160
DECAY_SCALE = math.exp(-0.5)
X_HEADS = 4
X_HDIM = 256
D_FF = 2816
FFN_CONV = 3
NORM_EPS = 1e-6
GN_EPS = 64e-5
CHUNK = 64

LANES = 128
SUBLANES = 8
GATE_PAD = LANES
GLORA_PAD = 2 * LANES
OFF_MQ, OFF_MK, OFF_MV, OFF_MO = 0, 512, 1024, 1536
OFF_RR = 2048
OFF_RK = OFF_RR + R_WIDTH
OFF_RV = OFF_RK + R_WIDTH
OFF_RWA = OFF_RV + R_WIDTH
OFF_MG = OFF_RWA + LANES
OFF_RG = OFF_MG + GATE_PAD
IN_COLS_P = OFF_RG + GLORA_PAD

SEQ_TILE = 256
VMEM_LIMIT = 48 * 1024 * 1024


def _dot(a, b):
    return jnp.dot(a, b, preferred_element_type=F32)


def _bdot(a, b):
    return jnp.dot(a.astype(BF16), b.astype(BF16), preferred_element_type=F32)


def _bdot_nt(a, b):
    return lax.dot_general(a.astype(BF16), b.astype(BF16), (((1,), (1,)), ((), ())),
                           preferred_element_type=F32)


def _bdot_tn(a, b):
    return lax.dot_general(a.astype(BF16), b.astype(BF16), (((0,), (0,)), ((), ())),
                           preferred_element_type=F32)


def _split2(x):
    hi = x.astype(BF16)
    lo = (x - hi.astype(F32)).astype(BF16)
    return hi, lo


def _sigmoid(x):
    return 1.0 / (1.0 + jnp.exp(-x))


def _shift_rows(x, sh):
    row = lax.broadcasted_iota(jnp.int32, x.shape, 0)
    return jnp.where(row >= sh, pltpu.roll(x, sh, 0), 0.0)


def _shift_rows_carry(x, prev, sh):
    ext = jnp.concatenate([prev, x], axis=0)
    return pltpu.roll(ext, sh, 0)[SUBLANES:]


def _mm_kernel(*refs, has_gain, has_resid):
    x_ref, w_ref = refs[0], refs[1]
    pos = 2
    g_ref = r_ref = None
    if has_gain:
        g_ref = refs[pos]
        pos += 1
    if has_resid:
        r_ref = refs[pos]
        pos += 1
    o_ref, h_scr = refs[pos], refs[pos + 1]

    @pl.when(pl.program_id(1) == 0)
    def _():
        x = x_ref[...].astype(F32)
        if has_gain:
            ms = jnp.mean(x * x, axis=-1, keepdims=True)
            x = x * lax.rsqrt(ms + NORM_EPS) * g_ref[...]
        h_scr[...] = x.astype(BF16)

    acc = _dot(h_scr[...], w_ref[...])
    if has_resid:
        acc = acc + r_ref[...]
    o_ref[...] = acc.astype(o_ref.dtype)


def _mm(x, w, gain=None, resid=None, out_dtype=F32, tm=512, tn=512):
    T, K = x.shape
    N = w.shape[1]
    tm = min(tm, T)
    tn = min(tn, N)
    assert T % tm == 0 and N % tn == 0
    in_specs = [pl.BlockSpec((tm, K), lambda i, j: (i, 0)),
                pl.BlockSpec((K, tn), lambda i, j: (0, j))]
    args = [x, w]
    if gain is not None:
        in_specs.append(pl.BlockSpec((1, K), lambda i, j: (0, 0)))
        args.append(gain.reshape(1, K))
    if resid is not None:
        in_specs.append(pl.BlockSpec((tm, tn), lambda i, j: (i, j)))
        args.append(resid)
    return pl.pallas_call(
        functools.partial(_mm_kernel, has_gain=gain is not None, has_resid=resid is not None),
        out_shape=jax.ShapeDtypeStruct((T, N), out_dtype),
        grid=(T // tm, N // tn),
        in_specs=in_specs,
        out_specs=pl.BlockSpec((tm, tn), lambda i, j: (i, j)),
        scratch_shapes=[pltpu.VMEM((tm, K), BF16)],
        compiler_params=pltpu.CompilerParams(
            dimension_semantics=("parallel", "arbitrary"), vmem_limit_bytes=VMEM_LIMIT),
        name="mm",
    )(*args)


def _rms_kernel(x_ref, g_ref, o_ref):
    x = x_ref[...]
    ms = jnp.mean(x * x, axis=-1, keepdims=True)
    o_ref[...] = x * lax.rsqrt(ms + NORM_EPS) * g_ref[...]


def _rmsnorm(x, gain, tm=1024):
    T, K = x.shape
    tm = min(tm, T)
    return pl.pallas_call(
        _rms_kernel,
        out_shape=jax.ShapeDtypeStruct((T, K), F32),
        grid=(T // tm,),
        in_specs=[pl.BlockSpec((tm, K), lambda i: (i, 0)),
                  pl.BlockSpec((1, K), lambda i: (0, 0))],
        out_specs=pl.BlockSpec((tm, K), lambda i: (i, 0)),
        compiler_params=pltpu.CompilerParams(
            dimension_semantics=("parallel",), vmem_limit_bytes=VMEM_LIMIT),
        name="rmsnorm",
    )(x, gain.reshape(1, K))


def _mlstm_kernel(q_ref, k_ref, v_ref, o_ref, grow_ref, gcol_ref, brow_ref, bcol_ref,
                  cw_ref, cb_ref, ng_ref, out_ref,
                  qp_scr, kp_scr, c_scr, n_scr, m_scr):
    TS = q_ref.shape[1]
    L = CHUNK
    nch = TS // L

    @pl.when(pl.program_id(1) == 0)
    def _():
        qp_scr[...] = jnp.zeros_like(qp_scr)
        kp_scr[...] = jnp.zeros_like(kp_scr)
        c_scr[...] = jnp.zeros_like(c_scr)
        n_scr[...] = jnp.zeros_like(n_scr)
        m_scr[...] = jnp.zeros_like(m_scr)

    def conv_silu(x, prev, w, b):
        y = b + x * w[M_CONV - 1:M_CONV, :]
        for j in range(M_CONV - 1):
            y = y + _shift_rows_carry(x, prev, M_CONV - 1 - j) * w[j:j + 1, :]
        return y * _sigmoid(y)

    q_raw = q_ref[0]
    k_raw = k_ref[0]
    qc_all = conv_silu(q_raw, qp_scr[...], cw_ref[:, :M_WIDTH], cb_ref[:, :M_WIDTH]) * (M_HDIM ** -0.5)
    kc_all = conv_silu(k_raw, kp_scr[...], cw_ref[:, M_WIDTH:], cb_ref[:, M_WIDTH:])
    qp_scr[...] = q_raw[TS - SUBLANES:]
    kp_scr[...] = k_raw[TS - SUBLANES:]
    v_all = v_ref[0]
    o_all = o_ref[0]

    ti = lax.broadcasted_iota(jnp.int32, (L, L), 0)
    si = lax.broadcasted_iota(jnp.int32, (L, L), 1)
    causal = si <= ti
    upper = ti <= si

    def log_sigmoid(x):
        return jnp.minimum(x, 0.0) - jnp.log1p(jnp.exp(-jnp.abs(x)))

    for h in range(M_HEADS):
        ls = slice(h * M_HDIM, (h + 1) * M_HDIM)
        c_prev = c_scr[h]
        n_prev = n_scr[h]
        m_prev = m_scr[h]
        for c in range(nch):
            rs = slice(c * L, (c + 1) * L)
            q = qc_all[rs, ls]
            k = kc_all[rs, ls]
            v = v_all[rs, ls]
            gr = grow_ref[0, c] + brow_ref[...]
            gc = gcol_ref[0, c] + bcol_ref[...]
            logi_r = gr[h:h + 1, :]
            logf_r = log_sigmoid(gr[M_HEADS + h:M_HEADS + h + 1, :])
            logi_c = gc[:, h:h + 1]
            logf_c = log_sigmoid(gc[:, M_HEADS + h:M_HEADS + h + 1])
            b_c = jnp.sum(jnp.where(causal, logf_r, 0.0), axis=1, keepdims=True)
            b_r = jnp.sum(jnp.where(upper, logf_c, 0.0), axis=0, keepdims=True)
            g = jnp.sum(logf_r, axis=1, keepdims=True)
            a_r = g - b_r + logi_r
            a_c = g - b_c + logi_c
            m_loc = jnp.max(a_r, axis=1, keepdims=True)
            wa_c = jnp.exp(a_c - m_loc)
            c_loc = _bdot_tn(v * wa_c, k)
            n_loc = jnp.sum(k * wa_c, axis=0, keepdims=True)

            inter = b_c + m_prev
            d = jnp.where(causal, b_c - b_r + logi_r, -jnp.inf)
            m_t = jnp.maximum(inter, jnp.max(d, axis=1, keepdims=True))
            s_int = jnp.exp(inter - m_t)
            p = jnp.exp(d - m_t) * _bdot_nt(q, k)
            num = s_int * _bdot_nt(q, c_prev) + _bdot(p, v)
            den = (s_int * jnp.sum(q * n_prev, axis=1, keepdims=True)
                   + jnp.sum(p, axis=1, keepdims=True))
            hh = num / jnp.maximum(jnp.abs(den), jnp.exp(-m_t))
            mu = jnp.mean(hh, axis=-1, keepdims=True)
            hc = hh - mu
            var = jnp.mean(hc * hc, axis=-1, keepdims=True)
            hn = hc * lax.rsqrt(var + NORM_EPS)
            out_ref[0, rs, ls] = (_sigmoid(o_all[rs, ls]) * hn * ng_ref[:, ls]).astype(out_ref.dtype)

            m_new = jnp.maximum(g + m_prev, m_loc)
            s_old = jnp.exp(g + m_prev - m_new)
            s_loc = jnp.exp(m_loc - m_new)
            c_prev = s_old * c_prev + s_loc * c_loc
            n_prev = s_old * n_prev + s_loc * n_loc
            m_prev = m_new
        c_scr[h] = c_prev
        n_scr[h] = n_prev
        m_scr[h] = m_prev


def _mlstm(p_all, g_row, g_col, b_row, b_col, conv_w, conv_b, norm_g):
    B, S, _ = p_all.shape
    ts = min(SEQ_TILE, S)
    nch = ts // CHUNK
    seq = lambda off: pl.BlockSpec((1, ts, M_WIDTH), lambda b, s, off=off: (b, s, off))
    full = lambda a: pl.BlockSpec(a.shape, lambda b, s: (0,) * a.ndim)
    return pl.pallas_call(
        _mlstm_kernel,
        out_shape=jax.ShapeDtypeStruct((B, S, M_WIDTH), BF16),
        grid=(B, S // ts),
        in_specs=[
            seq(OFF_MQ // M_WIDTH), seq(OFF_MK // M_WIDTH), seq(OFF_MV // M_WIDTH),
            seq(OFF_MO // M_WIDTH),
            pl.BlockSpec((1, nch, 2 * M_HEADS, CHUNK), lambda b, s: (b, s, 0, 0)),
            pl.BlockSpec((1, nch, CHUNK, 2 * M_HEADS), lambda b, s: (b, s, 0, 0)),
            full(b_row), full(b_col), full(conv_w), full(conv_b), full(norm_g),
        ],
        out_specs=pl.BlockSpec((1, ts, M_WIDTH), lambda b, s: (b, s, 0)),
        scratch_shapes=[pltpu.VMEM((SUBLANES, M_WIDTH), F32), pltpu.VMEM((SUBLANES, M_WIDTH), F32),
                        pltpu.VMEM((M_HEADS, M_HDIM, M_HDIM), F32),
                        pltpu.VMEM((M_HEADS, 1, M_HDIM), F32),
                        pltpu.VMEM((M_HEADS, 1, 1), F32)],
        compiler_params=pltpu.CompilerParams(
            dimension_semantics=("parallel", "arbitrary"), vmem_limit_bytes=VMEM_LIMIT),
        name="mlstm",
    )(p_all, p_all, p_all, p_all, g_row, g_col, b_row, b_col, conv_w, conv_b, norm_g)


def _inv_unit_lower_blocks(a, blk):
    n = a.shape[0]
    ti = lax.broadcasted_iota(jnp.int32, (n, n), 0)
    si = lax.broadcasted_iota(jnp.int32, (n, n), 1)

    def off_mask(b):
        sh = (2 * b).bit_length() - 1
        same = jnp.right_shift(ti, sh) == jnp.right_shift(si, sh)
        return same & (jnp.bitwise_and(ti, b) != 0) & (jnp.bitwise_and(si, b) == 0)

    inv = jnp.where(ti == si, 1.0, 0.0) - jnp.where(off_mask(1), a, 0.0)
    b = 2
    while b < blk:
        off = jnp.where(off_mask(b), a, 0.0)
        inv = inv - _bdot(_bdot(inv, off), inv)
        b *= 2
    return inv


def _rwkv_kernel(pr_ref, pk_ref, pv_ref, pwa_ref, pg_ref,
                 mur_ref, muk_ref, muv_ref, muwa_ref, mug_ref,
                 w0_ref, wup_ref, a0_ref, aup_ref, gup_ref,
                 kkp_ref, ka_ref, rk_ref, gng_ref, gnb_ref,
                 out_ref,
                 cr_scr, ck_scr, cv_scr, cwa_scr, cg_scr, st_scr):
    TS = pr_ref.shape[1]
    L = CHUNK
    N = R_HDIM
    nch = TS // L
    npair = R_WIDTH // LANES
    Q = 2 * LANES
    nquad = TS // (2 * L)

    @pl.when(pl.program_id(1) == 0)
    def _():
        for scr in (cr_scr, ck_scr, cv_scr, cwa_scr, cg_scr, st_scr):
            scr[...] = jnp.zeros_like(scr)

    def tshift(p_ref, mu_ref, c_scr):
        p = p_ref[0]
        prev = _shift_rows_carry(p, c_scr[...], 1)
        c_scr[...] = p[TS - SUBLANES:]
        return p + (prev - p) * mu_ref[...]

    hsh = N.bit_length() - 1
    li = lax.broadcasted_iota(jnp.int32, (LANES, LANES), 0)
    lj = lax.broadcasted_iota(jnp.int32, (LANES, LANES), 1)
    same_head = jnp.right_shift(li, hsh) == jnp.right_shift(lj, hsh)
    ones_bd = jnp.where(same_head, 1.0, 0.0).astype(BF16)

    def seg_sum(x):
        outs = []
        for p in range(x.shape[1] // LANES):
            hi, lo = _split2(x[:, p * LANES:(p + 1) * LANES])
            outs.append(_dot(hi, ones_bd) + _dot(lo, ones_bd))
        return jnp.concatenate(outs, axis=1)

    rr = tshift(pr_ref, mur_ref, cr_scr)
    kr = tshift(pk_ref, muk_ref, ck_scr)
    vr = tshift(pv_ref, muv_ref, cv_scr)
    wa = tshift(pwa_ref, muwa_ref, cwa_scr)
    gd = tshift(pg_ref, mug_ref, cg_scr)
    logw = -DECAY_SCALE * _sigmoid(w0_ref[...] + _bdot(jnp.tanh(wa), wup_ref[...]))
    a = _sigmoid(a0_ref[...] + _bdot(wa, aup_ref[...]))
    g = _bdot(_sigmoid(gd), gup_ref[...])
    kkraw = kr * kkp_ref[...]
    kk = kkraw / jnp.maximum(jnp.sqrt(seg_sum(kkraw * kkraw)), 1e-12)
    km = kr * (1.0 + (a - 1.0) * ka_ref[...])
    be = kk * a

    ti = lax.broadcasted_iota(jnp.int32, (Q, Q), 0)
    si = lax.broadcasted_iota(jnp.int32, (Q, Q), 1)
    same_blk = jnp.right_shift(ti, hsh) == jnp.right_shift(si, hsh)
    strict = same_blk & (si < ti)
    incl = same_blk & (si <= ti)
    tril = jnp.where(incl, 1.0, 0.0).astype(BF16)
    lw_hi, lw_lo = _split2(logw)
    bincl = jnp.concatenate(
        [_dot(tril, lw_hi[q * Q:(q + 1) * Q]) + _dot(tril, lw_lo[q * Q:(q + 1) * Q])
         for q in range(TS // Q)], axis=0)
    e_in = jnp.exp(bincl)
    e_ng = jnp.exp(-bincl)
    kt = kk * jnp.exp(bincl - logw)
    rt = rr * e_in
    bh = be * e_ng
    kh = km * e_ng

    h0 = lax.broadcasted_iota(jnp.int32, (L, LANES), 1) < N

    def stack_heads(x):
        x0, x1 = x[:L], x[L:]
        return jnp.concatenate([jnp.where(h0, x0, 0.0), jnp.where(h0, 0.0, x0),
                                jnp.where(h0, x1, 0.0), jnp.where(h0, 0.0, x1)], axis=0)

    def stack_dup(x):
        return jnp.concatenate([x[:L], x[:L], x[L:], x[L:]], axis=0)

    def comb(x):
        return jnp.concatenate([jnp.where(h0, x[0:L], x[L:2 * L]),
                                jnp.where(h0, x[2 * L:3 * L], x[3 * L:4 * L])], axis=0)

    w_ch = [[None] * nch for _ in range(npair)]
    u0_ch = [[None] * nch for _ in range(npair)]
    arkv_ch = [[None] * nch for _ in range(npair)]
    arb_ch = [[None] * nch for _ in range(npair)]
    for p in range(npair):
        ls = slice(p * LANES, (p + 1) * LANES)
        for qd in range(nquad):
            rs = slice(qd * 2 * L, (qd + 1) * 2 * L)
            lk = stack_heads(kt[rs, ls]).astype(BF16)
            lr = stack_heads(rt[rs, ls]).astype(BF16)
            rb = stack_dup(bh[rs, ls]).astype(BF16)
            rk = stack_dup(kh[rs, ls]).astype(BF16)
            vs = stack_dup(vr[rs, ls]).astype(BF16)
            a_bd = jnp.where(strict, _bdot_nt(lk, rb), 0.0)
            bk_bd = jnp.where(strict, _bdot_nt(lk, rk), 0.0)
            arb_bd = jnp.where(incl, _bdot_nt(lr, rb), 0.0)
            ark_bd = jnp.where(incl, _bdot_nt(lr, rk), 0.0)
            t_bd = _inv_unit_lower_blocks(a_bd, L)
            bkv = comb(_bdot(bk_bd, vs))
            xs = jnp.concatenate([stack_dup(kt[rs, ls]), stack_dup(bkv)], axis=1)
            tx = _bdot(t_bd, xs)
            w_q = comb(tx[:, :LANES])
            u0_q = -comb(tx[:, LANES:])
            arkv_q = comb(_bdot(ark_bd, vs))
            for j in range(2):
                c = qd * 2 + j
                w_ch[p][c] = w_q[j * L:(j + 1) * L]
                u0_ch[p][c] = u0_q[j * L:(j + 1) * L]
                arkv_ch[p][c] = arkv_q[j * L:(j + 1) * L]
                arb_ch[p][c] = arb_bd[j * 2 * L:(j + 1) * 2 * L, j * 2 * L:(j + 1) * 2 * L]

    y_rows = []
    st = [st_scr[p] for p in range(npair)]
    ones_f = jnp.where(same_head, 1.0, 0.0)
    for c in range(nch):
        rs = slice(c * L, (c + 1) * L)
        p_end = e_in[c * L + L - 1:c * L + L, :]
        y_p = []
        for p in range(npair):
            ls = slice(p * LANES, (p + 1) * LANES)
            pe = p_end[:, ls]
            rw = _bdot_nt(jnp.concatenate([rt[rs, ls], w_ch[p][c]], axis=0), st[p])
            u = u0_ch[p][c] - rw[L:]
            au = _bdot(arb_ch[p][c], jnp.concatenate([u, u], axis=0))
            y_p.append(rw[:L] + jnp.where(h0, au[:L], au[L:]) + arkv_ch[p][c])
            upd = _bdot_tn(jnp.concatenate([u, vr[rs, ls]], axis=0),
                           jnp.concatenate([bh[rs, ls] * pe, kh[rs, ls] * pe], axis=0))
            st[p] = st[p] * pe + upd * ones_f
        y_rows.append(jnp.concatenate(y_p, axis=1))
    for p in range(npair):
        st_scr[p] = st[p]
    y = jnp.concatenate(y_rows, axis=0)

    inv_n = 1.0 / N
    mu = seg_sum(y) * inv_n
    yc = y - mu
    var = seg_sum(yc * yc) * inv_n
    yn = yc * lax.rsqrt(var + GN_EPS) * gng_ref[...] + gnb_ref[...]
    bonus = seg_sum(rr * km * rk_ref[...]) * vr
    out_ref[0] = ((yn + bonus) * g).astype(out_ref.dtype)


def _rwkv(p_all, mu, w0, w_up, a0, a_up, g_up, kkp, ka, rk, gn_g, gn_b):
    B, S, _ = p_all.shape
    ts = min(SEQ_TILE, S)
    npair = R_WIDTH // LANES
    seq = lambda w, off: pl.BlockSpec((1, ts, w), lambda b, s, off=off: (b, s, off))
    vec = lambda w, off: pl.BlockSpec((1, w), lambda b, s, off=off: (0, off))
    full = lambda a: pl.BlockSpec(a.shape, lambda b, s: (0,) * a.ndim)
    return pl.pallas_call(
        _rwkv_kernel,
        out_shape=jax.ShapeDtypeStruct((B, S, R_WIDTH), BF16),
        grid=(B, S // ts),
        in_specs=[
            seq(R_WIDTH, OFF_RR // R_WIDTH), seq(R_WIDTH, OFF_RK // R_WIDTH),
            seq(R_WIDTH, OFF_RV // R_WIDTH), seq(LANES, OFF_RWA // LANES),
            seq(GLORA_PAD, OFF_RG // GLORA_PAD),
            vec(R_WIDTH, OFF_RR // R_WIDTH), vec(R_WIDTH, OFF_RK // R_WIDTH),
            vec(R_WIDTH, OFF_RV // R_WIDTH), vec(LANES, OFF_RWA // LANES),
            vec(GLORA_PAD, OFF_RG // GLORA_PAD),
            full(w0), full(w_up), full(a0), full(a_up), full(g_up),
            full(kkp), full(ka), full(rk), full(gn_g), full(gn_b),
        ],
        out_specs=pl.BlockSpec((1, ts, R_WIDTH), lambda b, s: (b, s, 0)),
        scratch_shapes=[pltpu.VMEM((SUBLANES, R_WIDTH), F32), pltpu.VMEM((SUBLANES, R_WIDTH), F32),
                        pltpu.VMEM((SUBLANES, R_WIDTH), F32), pltpu.VMEM((SUBLANES, LANES), F32),
                        pltpu.VMEM((SUBLANES, GLORA_PAD), F32),
                        pltpu.VMEM((npair, LANES, LANES), F32)],
        compiler_params=pltpu.CompilerParams(
            dimension_semantics=("parallel", "arbitrary"), vmem_limit_bytes=VMEM_LIMIT),
        name="rwkv",
    )(p_all, p_all, p_all, p_all, p_all, mu, mu, mu, mu, mu,
      w0, w_up, a0, a_up, g_up, kkp, ka, rk, gn_g, gn_b)


def _xattn_kernel(q_ref, k_ref, v_ref, o_ref):
    q = q_ref[0]
    s = lax.dot_general(q, k_ref[0], (((1,), (1,)), ((), ())),
                        preferred_element_type=F32) * (X_HDIM ** -0.5)
    s = s - jnp.max(s, axis=-1, keepdims=True)
    e = jnp.exp(s)
    p = e / jnp.sum(e, axis=-1, keepdims=True)
    o_ref[0] = _dot(p.astype(BF16), v_ref[0]).astype(o_ref.dtype)


def _xattn(q, kv, ts=1024):
    B, S, D = q.shape
    M = kv.shape[1]
    nh = D // X_HDIM
    ts = min(ts, S)
    return pl.pallas_call(
        _xattn_kernel,
        out_shape=jax.ShapeDtypeStruct((B, S, D), BF16),
        grid=(B, nh, S // ts),
        in_specs=[pl.BlockSpec((1, ts, X_HDIM), lambda b, h, t: (b, t, h)),
                  pl.BlockSpec((1, M, X_HDIM), lambda b, h, t: (b, 0, h)),
                  pl.BlockSpec((1, M, X_HDIM), lambda b, h, t: (b, 0, nh + h))],
        out_specs=pl.BlockSpec((1, ts, X_HDIM), lambda b, h, t: (b, t, h)),
        compiler_params=pltpu.CompilerParams(
            dimension_semantics=("parallel", "parallel", "parallel"),
            vmem_limit_bytes=VMEM_LIMIT),
        name="xattn",
    )(q, kv, kv)


def _glu_kernel(g_ref, v_ref, w_ref, b_ref, o_ref):
    x = g_ref[0]
    y = b_ref[...] + x * w_ref[FFN_CONV - 1:FFN_CONV, :]
    for j in range(FFN_CONV - 1):
        y = y + _shift_rows(x, FFN_CONV - 1 - j) * w_ref[j:j + 1, :]
    o_ref[0] = (y * _sigmoid(y) * v_ref[0]).astype(o_ref.dtype)


def _glu(u, conv_w, conv_b, tc=256):
    B, S, _ = u.shape
    nb = D_FF // tc
    return pl.pallas_call(
        _glu_kernel,
        out_shape=jax.ShapeDtypeStruct((B, S, D_FF), BF16),
        grid=(B, nb),
        in_specs=[pl.BlockSpec((1, S, tc), lambda b, j: (b, 0, j)),
                  pl.BlockSpec((1, S, tc), lambda b, j: (b, 0, nb + j)),
                  pl.BlockSpec((FFN_CONV, tc), lambda b, j: (0, j)),
                  pl.BlockSpec((1, tc), lambda b, j: (0, j))],
        out_specs=pl.BlockSpec((1, S, tc), lambda b, j: (b, 0, j)),
        compiler_params=pltpu.CompilerParams(
            dimension_semantics=("parallel", "parallel"), vmem_limit_bytes=VMEM_LIMIT),
        name="glu",
    )(u, u, conv_w, conv_b.reshape(1, D_FF))


def _pad_cols(w, n):
    return jnp.pad(w, ((0, 0), (0, n - w.shape[1])))


def _relayout_in(w):
    m_main = w[:, :4 * M_WIDTH]
    m_gate = w[:, 4 * M_WIDTH:4 * M_WIDTH + 2 * M_HEADS]
    r0 = 4 * M_WIDTH + 2 * M_HEADS
    r_main = w[:, r0:r0 + 3 * R_WIDTH + R_DECAY_LORA + R_AAA_LORA]
    r_gate = w[:, r0 + 3 * R_WIDTH + R_DECAY_LORA + R_AAA_LORA:]
    return jnp.concatenate(
        [m_main, r_main, _pad_cols(m_gate, GATE_PAD), _pad_cols(r_gate, GLORA_PAD)], axis=1)


def kernel(x, mem, norm_mix, w_in, m_conv_w, m_conv_b, m_gate_b, m_norm_g, r_mu, r_w0,
           r_w_up, r_a0, r_a_up, r_g_up, r_kk, r_ka, r_rk, r_gn_g, r_gn_b, w_out,
           norm_x, norm_mem, x_wq, x_wkv, x_wo, norm_ffn, f_up, f_conv_w, f_conv_b,
           f_down, norm_final):
    B, S, D = x.shape
    M = mem.shape[1]
    depth = w_in.shape[0]
    T = B * S
    nc = S // CHUNK
    xf = x.reshape(T, D)
    memf = mem.reshape(B * M, D)
    row = lambda a: a.reshape(1, -1)

    for l in range(depth):
        w_in_p = _relayout_in(w_in[l]).astype(BF16)
        p_all = _mm(xf, w_in_p, gain=norm_mix[l]).reshape(B, S, IN_COLS_P)

        g_col = p_all[:, :, OFF_MG:OFF_MG + 2 * M_HEADS].reshape(B, nc, CHUNK, 2 * M_HEADS)
        g_row = g_col.swapaxes(-1, -2)
        y_m = _mlstm(p_all, g_row, g_col, m_gate_b[l].reshape(-1, 1), row(m_gate_b[l]),
                     m_conv_w[l], row(m_conv_b[l]), row(m_norm_g[l]))

        mu = _relayout_in(jnp.pad(row(r_mu[l]), ((0, 0), (4 * M_WIDTH + 2 * M_HEADS, 0))))
        w_up = jnp.pad(r_w_up[l], ((0, R_AAA_LORA), (0, 0))).astype(BF16)
        a_up = jnp.pad(r_a_up[l], ((R_DECAY_LORA, 0), (0, 0))).astype(BF16)
        g_up = jnp.pad(r_g_up[l], ((0, GLORA_PAD - R_GATE_LORA), (0, 0))).astype(BF16)
        y_r = _rwkv(p_all, mu, row(r_w0[l]), w_up, row(r_a0[l]), a_up, g_up,
                    row(r_kk[l]), row(r_ka[l]), row(r_rk[l]), row(r_gn_g[l]), row(r_gn_b[l]))

        y = jnp.concatenate([y_m, y_r], axis=-1).reshape(T, D)
        xf = _mm(y, w_out[l].astype(BF16), resid=xf)

        q = _mm(xf, x_wq[l].astype(BF16), gain=norm_x[l], out_dtype=BF16)
        kv = _mm(memf, x_wkv[l].astype(BF16), gain=norm_mem[l], out_dtype=BF16)
        o = _xattn(q.reshape(B, S, D), kv.reshape(B, M, 2 * D))
        xf = _mm(o.reshape(T, D), x_wo[l].astype(BF16), resid=xf)

        u = _mm(xf, f_up[l].astype(BF16), gain=norm_ffn[l])
        act = _glu(u.reshape(B, S, 2 * D_FF), f_conv_w[l], f_conv_b[l])
        xf = _mm(act.reshape(T, D_FF), f_down[l].astype(BF16), resid=xf)

    return _rmsnorm(xf, norm_final).reshape(B, S, D)
```

```python
import functools
import math

import jax
import jax.numpy as jnp
from jax import lax
from jax.experimental import pallas as pl
from jax.experimental.pallas import tpu as pltpu

F32 = jnp.float32
BF16 = jnp.bfloat16

D_MODEL = 1024
M_WIDTH = 512
M_HEADS = 4
M_HDIM = 128
M_CONV = 4
R_WIDTH = 512
R_HDIM = 64
R_HEADS = 8
R_DECAY_LORA = 64
R_AAA_LORA = 64
R_GATE_LORA = 160
DECAY_SCALE = math.exp(-0.5)
X_HEADS = 4
X_HDIM = 256
D_FF = 2816
FFN_CONV = 3
NORM_EPS = 1e-6
GN_EPS = 64e-5
CHUNK = 64

LANES = 128
SUBLANES = 8
GATE_PAD = LANES
GLORA_PAD = 2 * LANES
OFF_MQ, OFF_MK, OFF_MV, OFF_MO = 0, 512, 1024, 1536
OFF_RR = 2048
OFF_RK = OFF_RR + R_WIDTH
OFF_RV = OFF_RK + R_WIDTH
OFF_RWA = OFF_RV + R_WIDTH
OFF_MG = OFF_RWA + LANES
OFF_RG = OFF_MG + GATE_PAD
IN_COLS_P = OFF_RG + GLORA_PAD

SEQ_TILE = 256
VMEM_LIMIT = 48 * 1024 * 1024


def _dot(a, b):
    return jnp.dot(a, b, preferred_element_type=F32)


def _bdot(a, b):
    return jnp.dot(a.astype(BF16), b.astype(BF16), preferred_element_type=F32)


def _bdot_nt(a, b):
    return lax.dot_general(a.astype(BF16), b.astype(BF16), (((1,), (1,)), ((), ())),
                           preferred_element_type=F32)


def _bdot_tn(a, b):
    return lax.dot_general(a.astype(BF16), b.astype(BF16), (((0,), (0,)), ((), ())),
                           preferred_element_type=F32)


def _split2(x):
    hi = x.astype(BF16)
    lo = (x - hi.astype(F32)).astype(BF16)
    return hi, lo


def _sigmoid(x):
    return 1.0 / (1.0 + jnp.exp(-x))


def _shift_rows(x, sh):
    row = lax.broadcasted_iota(jnp.int32, x.shape, 0)
    return jnp.where(row >= sh, pltpu.roll(x, sh, 0), 0.0)


def _shift_rows_carry(x, prev, sh):
    ext = jnp.concatenate([prev, x], axis=0)
    return pltpu.roll(ext, sh, 0)[SUBLANES:]


def _mm_kernel(*refs, has_gain, has_resid, tn):
    x_ref, w_ref = refs[0], refs[1]
    pos = 2
    g_ref = r_ref = None
    if has_gain:
        g_ref = refs[pos]
        pos += 1
    if has_resid:
        r_ref = refs[pos]
        pos += 1
    o_ref, h_scr = refs[pos], refs[pos + 1]

    x = x_ref[...].astype(F32)
    if has_gain:
        ms = jnp.mean(x * x, axis=-1, keepdims=True)
        x = x * lax.rsqrt(ms + NORM_EPS) * g_ref[...]
    h_scr[...] = x.astype(BF16)
    for c in range(w_ref.shape[1] // tn):
        cols = slice(c * tn, (c + 1) * tn)
        acc = _dot(h_scr[...], w_ref[:, cols])
        if has_resid:
            acc = acc + r_ref[:, cols]
        o_ref[:, cols] = acc.astype(o_ref.dtype)


def _mm_tile_rows(K, N, x_bytes, out_bytes, has_resid):
    budget = (VMEM_LIMIT * 3) // 4 - K * N * 2
    for tm in (512, 256, 128):
        per_row = 2 * K * x_bytes + 2 * N * out_bytes + K * 2 + (2 * N * 4 if has_resid else 0)
        if tm * per_row <= budget:
            return tm
    raise ValueError("weight does not fit in VMEM")


def _mm(x, w, gain=None, resid=None, out_dtype=F32, tn=512):
    T, K = x.shape
    N = w.shape[1]
    tm = min(T, _mm_tile_rows(K, N, x.dtype.itemsize, jnp.dtype(out_dtype).itemsize,
                              resid is not None))
    tn = min(tn, N)
    assert T % tm == 0 and N % tn == 0
    in_specs = [pl.BlockSpec((tm, K), lambda i: (i, 0)),
                pl.BlockSpec((K, N), lambda i: (0, 0), pipeline_mode=pl.Buffered(1))]
    args = [x, w]
    if gain is not None:
        in_specs.append(pl.BlockSpec((1, K), lambda i: (0, 0)))
        args.append(gain.reshape(1, K))
    if resid is not None:
        in_specs.append(pl.BlockSpec((tm, N), lambda i: (i, 0)))
        args.append(resid)
    return pl.pallas_call(
        functools.partial(_mm_kernel, has_gain=gain is not None, has_resid=resid is not None,
                          tn=tn),
        out_shape=jax.ShapeDtypeStruct((T, N), out_dtype),
        grid=(T // tm,),
        in_specs=in_specs,
        out_specs=pl.BlockSpec((tm, N), lambda i: (i, 0)),
        scratch_shapes=[pltpu.VMEM((tm, K), BF16)],
        compiler_params=pltpu.CompilerParams(
            dimension_semantics=("parallel",), vmem_limit_bytes=VMEM_LIMIT),
        name="mm",
    )(*args)


def _rms_kernel(x_ref, g_ref, o_ref):
    x = x_ref[...]
    ms = jnp.mean(x * x, axis=-1, keepdims=True)
    o_ref[...] = x * lax.rsqrt(ms + NORM_EPS) * g_ref[...]


def _rmsnorm(x, gain, tm=1024):
    T, K = x.shape
    tm = min(tm, T)
    return pl.pallas_call(
        _rms_kernel,
        out_shape=jax.ShapeDtypeStruct((T, K), F32),
        grid=(T // tm,),
        in_specs=[pl.BlockSpec((tm, K), lambda i: (i, 0)),
                  pl.BlockSpec((1, K), lambda i: (0, 0))],
        out_specs=pl.BlockSpec((tm, K), lambda i: (i, 0)),
        compiler_params=pltpu.CompilerParams(
            dimension_semantics=("parallel",), vmem_limit_bytes=VMEM_LIMIT),
        name="rmsnorm",
    )(x, gain.reshape(1, K))


def _mlstm_kernel(q_ref, k_ref, v_ref, o_ref, grow_ref, gcol_ref, brow_ref, bcol_ref,
                  cw_ref, cb_ref, ng_ref, out_ref,
                  qp_scr, kp_scr, c_scr, n_scr, m_scr):
    TS = q_ref.shape[1]
    L = CHUNK
    nch = TS // L

    @pl.when(pl.program_id(1) == 0)
    def _():
        qp_scr[...] = jnp.zeros_like(qp_scr)
        kp_scr[...] = jnp.zeros_like(kp_scr)
        c_scr[...] = jnp.zeros_like(c_scr)
        n_scr[...] = jnp.zeros_like(n_scr)
        m_scr[...] = jnp.zeros_like(m_scr)

    def conv_silu(x, prev, w, b):
        y = b + x * w[M_CONV - 1:M_CONV, :]
        for j in range(M_CONV - 1):
            y = y + _shift_rows_carry(x, prev, M_CONV - 1 - j) * w[j:j + 1, :]
        return y * _sigmoid(y)

    q_raw = q_ref[0]
    k_raw = k_ref[0]
    qc_all = conv_silu(q_raw, qp_scr[...], cw_ref[:, :M_WIDTH], cb_ref[:, :M_WIDTH]) * (M_HDIM ** -0.5)
    kc_all = conv_silu(k_raw, kp_scr[...], cw_ref[:, M_WIDTH:], cb_ref[:, M_WIDTH:])
    qp_scr[...] = q_raw[TS - SUBLANES:]
    kp_scr[...] = k_raw[TS - SUBLANES:]
    v_all = v_ref[0]
    o_all = o_ref[0]

    ti = lax.broadcasted_iota(jnp.int32, (L, L), 0)
    si = lax.broadcasted_iota(jnp.int32, (L, L), 1)
    causal = si <= ti
    upper = ti <= si

    def log_sigmoid(x):
        return jnp.minimum(x, 0.0) - jnp.log1p(jnp.exp(-jnp.abs(x)))

    for h in range(M_HEADS):
        ls = slice(h * M_HDIM, (h + 1) * M_HDIM)
        c_prev = c_scr[h]
        n_prev = n_scr[h]
        m_prev = m_scr[h]
        for c in range(nch):
            rs = slice(c * L, (c + 1) * L)
            q = qc_all[rs, ls]
            k = kc_all[rs, ls]
            v = v_all[rs, ls]
            gr = grow_ref[0, c] + brow_ref[...]
            gc = gcol_ref[0, c] + bcol_ref[...]
            logi_r = gr[h:h + 1, :]
            logf_r = log_sigmoid(gr[M_HEADS + h:M_HEADS + h + 1, :])
            logi_c = gc[:, h:h + 1]
            logf_c = log_sigmoid(gc[:, M_HEADS + h:M_HEADS + h + 1])
            b_c = jnp.sum(jnp.where(causal, logf_r, 0.0), axis=1, keepdims=True)
            b_r = jnp.sum(jnp.where(upper, logf_c, 0.0), axis=0, keepdims=True)
            g = jnp.sum(logf_r, axis=1, keepdims=True)
            a_r = g - b_r + logi_r
            a_c = g - b_c + logi_c
            m_loc = jnp.max(a_r, axis=1, keepdims=True)
            wa_c = jnp.exp(a_c - m_loc)
            c_loc = _bdot_tn(v * wa_c, k)
            n_loc = jnp.sum(k * wa_c, axis=0, keepdims=True)

            inter = b_c + m_prev
            d = jnp.where(causal, b_c - b_r + logi_r, -jnp.inf)
            m_t = jnp.maximum(inter, jnp.max(d, axis=1, keepdims=True))
            s_int = jnp.exp(inter - m_t)
            p = jnp.exp(d - m_t) * _bdot_nt(q, k)
            num = s_int * _bdot_nt(q, c_prev) + _bdot(p, v)
            den = (s_int * jnp.sum(q * n_prev, axis=1, keepdims=True)
                   + jnp.sum(p, axis=1, keepdims=True))
            hh = num / jnp.maximum(jnp.abs(den), jnp.exp(-m_t))
            mu = jnp.mean(hh, axis=-1, keepdims=True)
            hc = hh - mu
            var = jnp.mean(hc * hc, axis=-1, keepdims=True)
            hn = hc * lax.rsqrt(var + NORM_EPS)
            out_ref[0, rs, ls] = (_sigmoid(o_all[rs, ls]) * hn * ng_ref[:, ls]).astype(out_ref.dtype)

            m_new = jnp.maximum(g + m_prev, m_loc)
            s_old = jnp.exp(g + m_prev - m_new)
            s_loc = jnp.exp(m_loc - m_new)
            c_prev = s_old * c_prev + s_loc * c_loc
            n_prev = s_old * n_prev + s_loc * n_loc
            m_prev = m_new
        c_scr[h] = c_prev
        n_scr[h] = n_prev
        m_scr[h] = m_prev


def _mlstm(p_all, g_row, g_col, b_row, b_col, conv_w, conv_b, norm_g):
    B, S, _ = p_all.shape
    ts = min(SEQ_TILE, S)
    nch = ts // CHUNK
    seq = lambda off: pl.BlockSpec((1, ts, M_WIDTH), lambda b, s, off=off: (b, s, off))
    full = lambda a: pl.BlockSpec(a.shape, lambda b, s: (0,) * a.ndim)
    return pl.pallas_call(
        _mlstm_kernel,
        out_shape=jax.ShapeDtypeStruct((B, S, M_WIDTH), BF16),
        grid=(B, S // ts),
        in_specs=[
            seq(OFF_MQ // M_WIDTH), seq(OFF_MK // M_WIDTH), seq(OFF_MV // M_WIDTH),
            seq(OFF_MO // M_WIDTH),
            pl.BlockSpec((1, nch, 2 * M_HEADS, CHUNK), lambda b, s: (b, s, 0, 0)),
            pl.BlockSpec((1, nch, CHUNK, 2 * M_HEADS), lambda b, s: (b, s, 0, 0)),
            full(b_row), full(b_col), full(conv_w), full(conv_b), full(norm_g),
        ],
        out_specs=pl.BlockSpec((1, ts, M_WIDTH), lambda b, s: (b, s, 0)),
        scratch_shapes=[pltpu.VMEM((SUBLANES, M_WIDTH), F32), pltpu.VMEM((SUBLANES, M_WIDTH), F32),
                        pltpu.VMEM((M_HEADS, M_HDIM, M_HDIM), F32),
                        pltpu.VMEM((M_HEADS, 1, M_HDIM), F32),
                        pltpu.VMEM((M_HEADS, 1, 1), F32)],
        compiler_params=pltpu.CompilerParams(
            dimension_semantics=("parallel", "arbitrary"), vmem_limit_bytes=VMEM_LIMIT),
        name="mlstm",
    )(p_all, p_all, p_all, p_all, g_row, g_col, b_row, b_col, conv_w, conv_b, norm_g)


def _inv_unit_lower_blocks(a, blk):
    n = a.shape[0]
    ti = lax.broadcasted_iota(jnp.int32, (n, n), 0)
    si = lax.broadcasted_iota(jnp.int32, (n, n), 1)

    def off_mask(b):
        sh = (2 * b).bit_length() - 1
        same = jnp.right_shift(ti, sh) == jnp.right_shift(si, sh)
        return same & (jnp.bitwise_and(ti, b) != 0) & (jnp.bitwise_and(si, b) == 0)

    inv = jnp.where(ti == si, 1.0, 0.0) - jnp.where(off_mask(1), a, 0.0)
    b = 2
    while b < blk:
        off = jnp.where(off_mask(b), a, 0.0)
        inv = inv - _bdot(_bdot(inv, off), inv)
        b *= 2
    return inv


def _rwkv_kernel(pr_ref, pk_ref, pv_ref, pwa_ref, pg_ref,
                 mur_ref, muk_ref, muv_ref, muwa_ref, mug_ref,
                 w0_ref, wup_ref, a0_ref, aup_ref, gup_ref,
                 kkp_ref, ka_ref, rk_ref, gng_ref, gnb_ref,
                 out_ref,
                 cr_scr, ck_scr, cv_scr, cwa_scr, cg_scr, st_scr):
    TS = pr_ref.shape[1]
    L = CHUNK
    N = R_HDIM
    nch = TS // L
    npair = R_WIDTH // LANES
    Q = 2 * LANES
    nquad = TS // (2 * L)

    @pl.when(pl.program_id(1) == 0)
    def _():
        for scr in (cr_scr, ck_scr, cv_scr, cwa_scr, cg_scr, st_scr):
            scr[...] = jnp.zeros_like(scr)

    def tshift(p_ref, mu_ref, c_scr):
        p = p_ref[0]
        prev = _shift_rows_carry(p, c_scr[...], 1)
        c_scr[...] = p[TS - SUBLANES:]
        return p + (prev - p) * mu_ref[...]

    hsh = N.bit_length() - 1
    li = lax.broadcasted_iota(jnp.int32, (LANES, LANES), 0)
    lj = lax.broadcasted_iota(jnp.int32, (LANES, LANES), 1)
    same_head = jnp.right_shift(li, hsh) == jnp.right_shift(lj, hsh)
    ones_bd = jnp.where(same_head, 1.0, 0.0).astype(BF16)

    def seg_sum(x):
        outs = []
        for p in range(x.shape[1] // LANES):
            hi, lo = _split2(x[:, p * LANES:(p + 1) * LANES])
            outs.append(_dot(hi, ones_bd) + _dot(lo, ones_bd))
        return jnp.concatenate(outs, axis=1)

    rr = tshift(pr_ref, mur_ref, cr_scr)
    kr = tshift(pk_ref, muk_ref, ck_scr)
    vr = tshift(pv_ref, muv_ref, cv_scr)
    wa = tshift(pwa_ref, muwa_ref, cwa_scr)
    gd = tshift(pg_ref, mug_ref, cg_scr)
    logw = -DECAY_SCALE * _sigmoid(w0_ref[...] + _bdot(jnp.tanh(wa), wup_ref[...]))
    a = _sigmoid(a0_ref[...] + _bdot(wa, aup_ref[...]))
    g = _bdot(_sigmoid(gd), gup_ref[...])
    kkraw = kr * kkp_ref[...]
    kk = kkraw / jnp.maximum(jnp.sqrt(seg_sum(kkraw * kkraw)), 1e-12)
    km = kr * (1.0 + (a - 1.0) * ka_ref[...])
    be = kk * a

    ti = lax.broadcasted_iota(jnp.int32, (Q, Q), 0)
    si = lax.broadcasted_iota(jnp.int32, (Q, Q), 1)
    same_blk = jnp.right_shift(ti, hsh) == jnp.right_shift(si, hsh)
    strict = same_blk & (si < ti)
    incl = same_blk & (si <= ti)
    tril = jnp.where(incl, 1.0, 0.0).astype(BF16)
    lw_hi, lw_lo = _split2(logw)
    bincl = jnp.concatenate(
        [_dot(tril, lw_hi[q * Q:(q + 1) * Q]) + _dot(tril, lw_lo[q * Q:(q + 1) * Q])
         for q in range(TS // Q)], axis=0)
    e_in = jnp.exp(bincl)
    e_ng = jnp.exp(-bincl)
    kt = kk * jnp.exp(bincl - logw)
    rt = rr * e_in
    bh = be * e_ng
    kh = km * e_ng

    h0 = lax.broadcasted_iota(jnp.int32, (L, LANES), 1) < N

    def stack_heads(x):
        x0, x1 = x[:L], x[L:]
        return jnp.concatenate([jnp.where(h0, x0, 0.0), jnp.where(h0, 0.0, x0),
                                jnp.where(h0, x1, 0.0), jnp.where(h0, 0.0, x1)], axis=0)

    def stack_dup(x):
        return jnp.concatenate([x[:L], x[:L], x[L:], x[L:]], axis=0)

    def comb(x):
        return jnp.concatenate([jnp.where(h0, x[0:L], x[L:2 * L]),
                                jnp.where(h0, x[2 * L:3 * L], x[3 * L:4 * L])], axis=0)

    w_ch = [[None] * nch for _ in range(npair)]
    u0_ch = [[None] * nch for _ in range(npair)]
    arkv_ch = [[None] * nch for _ in range(npair)]
    arb_ch = [[None] * nch for _ in range(npair)]
    for p in range(npair):
        ls = slice(p * LANES, (p + 1) * LANES)
        for qd in range(nquad):
            rs = slice(qd * 2 * L, (qd + 1) * 2 * L)
            lk = stack_heads(kt[rs, ls]).astype(BF16)
            lr = stack_heads(rt[rs, ls]).astype(BF16)
            rb = stack_dup(bh[rs, ls]).astype(BF16)
            rk = stack_dup(kh[rs, ls]).astype(BF16)
            vs = stack_dup(vr[rs, ls]).astype(BF16)
            a_bd = jnp.where(strict, _bdot_nt(lk, rb), 0.0)
            bk_bd = jnp.where(strict, _bdot_nt(lk, rk), 0.0)
            arb_bd = jnp.where(incl, _bdot_nt(lr, rb), 0.0)
            ark_bd = jnp.where(incl, _bdot_nt(lr, rk), 0.0)
            t_bd = _inv_unit_lower_blocks(a_bd, L)
            bkv = comb(_bdot(bk_bd, vs))
            xs = jnp.concatenate([stack_dup(kt[rs, ls]), stack_dup(bkv)], axis=1)
            tx = _bdot(t_bd, xs)
            w_q = comb(tx[:, :LANES])
            u0_q = -comb(tx[:, LANES:])
            arkv_q = comb(_bdot(ark_bd, vs))
            for j in range(2):
                c = qd * 2 + j
                w_ch[p][c] = w_q[j * L:(j + 1) * L]
                u0_ch[p][c] = u0_q[j * L:(j + 1) * L]
                arkv_ch[p][c] = arkv_q[j * L:(j + 1) * L]
                arb_ch[p][c] = arb_bd[j * 2 * L:(j + 1) * 2 * L, j * 2 * L:(j + 1) * 2 * L]

    y_rows = []
    st = [st_scr[p] for p in range(npair)]
    ones_f = jnp.where(same_head, 1.0, 0.0)
    for c in range(nch):
        rs = slice(c * L, (c + 1) * L)
        p_end = e_in[c * L + L - 1:c * L + L, :]
        y_p = []
        for p in range(npair):
            ls = slice(p * LANES, (p + 1) * LANES)
            pe = p_end[:, ls]
            rw = _bdot_nt(jnp.concatenate([rt[rs, ls], w_ch[p][c]], axis=0), st[p])
            u = u0_ch[p][c] - rw[L:]
            au = _bdot(arb_ch[p][c], jnp.concatenate([u, u], axis=0))
            y_p.append(rw[:L] + jnp.where(h0, au[:L], au[L:]) + arkv_ch[p][c])
            upd = _bdot_tn(jnp.concatenate([u, vr[rs, ls]], axis=0),
                           jnp.concatenate([bh[rs, ls] * pe, kh[rs, ls] * pe], axis=0))
            st[p] = st[p] * pe + upd * ones_f
        y_rows.append(jnp.concatenate(y_p, axis=1))
    for p in range(npair):
        st_scr[p] = st[p]
    y = jnp.concatenate(y_rows, axis=0)

    inv_n = 1.0 / N
    mu = seg_sum(y) * inv_n
    yc = y - mu
    var = seg_sum(yc * yc) * inv_n
    yn = yc * lax.rsqrt(var + GN_EPS) * gng_ref[...] + gnb_ref[...]
    bonus = seg_sum(rr * km * rk_ref[...]) * vr
    out_ref[0] = ((yn + bonus) * g).astype(out_ref.dtype)


def _rwkv(p_all, mu, w0, w_up, a0, a_up, g_up, kkp, ka, rk, gn_g, gn_b):
    B, S, _ = p_all.shape
    ts = min(SEQ_TILE, S)
    npair = R_WIDTH // LANES
    seq = lambda w, off: pl.BlockSpec((1, ts, w), lambda b, s, off=off: (b, s, off))
    vec = lambda w, off: pl.BlockSpec((1, w), lambda b, s, off=off: (0, off))
    full = lambda a: pl.BlockSpec(a.shape, lambda b, s: (0,) * a.ndim)
    return pl.pallas_call(
        _rwkv_kernel,
        out_shape=jax.ShapeDtypeStruct((B, S, R_WIDTH), BF16),
        grid=(B, S // ts),
        in_specs=[
            seq(R_WIDTH, OFF_RR // R_WIDTH), seq(R_WIDTH, OFF_RK // R_WIDTH),
            seq(R_WIDTH, OFF_RV // R_WIDTH), seq(LANES, OFF_RWA // LANES),
            seq(GLORA_PAD, OFF_RG // GLORA_PAD),
            vec(R_WIDTH, OFF_RR // R_WIDTH), vec(R_WIDTH, OFF_RK // R_WIDTH),
            vec(R_WIDTH, OFF_RV // R_WIDTH), vec(LANES, OFF_RWA // LANES),
            vec(GLORA_PAD, OFF_RG // GLORA_PAD),
            full(w0), full(w_up), full(a0), full(a_up), full(g_up),
            full(kkp), full(ka), full(rk), full(gn_g), full(gn_b),
        ],
        out_specs=pl.BlockSpec((1, ts, R_WIDTH), lambda b, s: (b, s, 0)),
        scratch_shapes=[pltpu.VMEM((SUBLANES, R_WIDTH), F32), pltpu.VMEM((SUBLANES, R_WIDTH), F32),
                        pltpu.VMEM((SUBLANES, R_WIDTH), F32), pltpu.VMEM((SUBLANES, LANES), F32),
                        pltpu.VMEM((SUBLANES, GLORA_PAD), F32),
                        pltpu.VMEM((npair, LANES, LANES), F32)],
        compiler_params=pltpu.CompilerParams(
            dimension_semantics=("parallel", "arbitrary"), vmem_limit_bytes=VMEM_LIMIT),
        name="rwkv",
    )(p_all, p_all, p_all, p_all, p_all, mu, mu, mu, mu, mu,
      w0, w_up, a0, a_up, g_up, kkp, ka, rk, gn_g, gn_b)


def _xattn_kernel(q_ref, k_ref, v_ref, o_ref):
    q = q_ref[0]
    s = lax.dot_general(q, k_ref[0], (((1,), (1,)), ((), ())),
                        preferred_element_type=F32) * (X_HDIM ** -0.5)
    s = s - jnp.max(s, axis=-1, keepdims=True)
    e = jnp.exp(s)
    p = e / jnp.sum(e, axis=-1, keepdims=True)
    o_ref[0] = _dot(p.astype(BF16), v_ref[0]).astype(o_ref.dtype)


def _xattn(q, kv, ts=1024):
    B, S, D = q.shape
    M = kv.shape[1]
    nh = D // X_HDIM
    ts = min(ts, S)
    return pl.pallas_call(
        _xattn_kernel,
        out_shape=jax.ShapeDtypeStruct((B, S, D), BF16),
        grid=(B, nh, S // ts),
        in_specs=[pl.BlockSpec((1, ts, X_HDIM), lambda b, h, t: (b, t, h)),
                  pl.BlockSpec((1, M, X_HDIM), lambda b, h, t: (b, 0, h)),
                  pl.BlockSpec((1, M, X_HDIM), lambda b, h, t: (b, 0, nh + h))],
        out_specs=pl.BlockSpec((1, ts, X_HDIM), lambda b, h, t: (b, t, h)),
        compiler_params=pltpu.CompilerParams(
            dimension_semantics=("parallel", "parallel", "parallel"),
            vmem_limit_bytes=VMEM_LIMIT),
        name="xattn",
    )(q, kv, kv)


def _glu_kernel(g_ref, v_ref, w_ref, b_ref, o_ref):
    x = g_ref[0].astype(F32)
    y = b_ref[...] + x * w_ref[FFN_CONV - 1:FFN_CONV, :]
    for j in range(FFN_CONV - 1):
        y = y + _shift_rows(x, FFN_CONV - 1 - j) * w_ref[j:j + 1, :]
    o_ref[0] = (y * _sigmoid(y) * v_ref[0].astype(F32)).astype(o_ref.dtype)


def _glu(u, conv_w, conv_b, tc=256):
    B, S, _ = u.shape
    nb = D_FF // tc
    return pl.pallas_call(
        _glu_kernel,
        out_shape=jax.ShapeDtypeStruct((B, S, D_FF), BF16),
        grid=(B, nb),
        in_specs=[pl.BlockSpec((1, S, tc), lambda b, j: (b, 0, j)),
                  pl.BlockSpec((1, S, tc), lambda b, j: (b, 0, nb + j)),
                  pl.BlockSpec((FFN_CONV, tc), lambda b, j: (0, j)),
                  pl.BlockSpec((1, tc), lambda b, j: (0, j))],
        out_specs=pl.BlockSpec((1, S, tc), lambda b, j: (b, 0, j)),
        compiler_params=pltpu.CompilerParams(
            dimension_semantics=("parallel", "parallel"), vmem_limit_bytes=VMEM_LIMIT),
        name="glu",
    )(u, u, conv_w, conv_b.reshape(1, D_FF))


def _pad_cols(w, n):
    return jnp.pad(w, ((0, 0), (0, n - w.shape[1])))


def _relayout_in(w):
    m_main = w[:, :4 * M_WIDTH]
    m_gate = w[:, 4 * M_WIDTH:4 * M_WIDTH + 2 * M_HEADS]
    r0 = 4 * M_WIDTH + 2 * M_HEADS
    r_main = w[:, r0:r0 + 3 * R_WIDTH + R_DECAY_LORA + R_AAA_LORA]
    r_gate = w[:, r0 + 3 * R_WIDTH + R_DECAY_LORA + R_AAA_LORA:]
    return jnp.concatenate(
        [m_main, r_main, _pad_cols(m_gate, GATE_PAD), _pad_cols(r_gate, GLORA_PAD)], axis=1)


def kernel(x, mem, norm_mix, w_in, m_conv_w, m_conv_b, m_gate_b, m_norm_g, r_mu, r_w0,
           r_w_up, r_a0, r_a_up, r_g_up, r_kk, r_ka, r_rk, r_gn_g, r_gn_b, w_out,
           norm_x, norm_mem, x_wq, x_wkv, x_wo, norm_ffn, f_up, f_conv_w, f_conv_b,
           f_down, norm_final):
    B, S, D = x.shape
    M = mem.shape[1]
    depth = w_in.shape[0]
    T = B * S
    nc = S // CHUNK
    xf = x.reshape(T, D)
    memf = mem.reshape(B * M, D)
    row = lambda a: a.reshape(1, -1)

    for l in range(depth):
        w_in_p = _relayout_in(w_in[l]).astype(BF16)
        p_all = _mm(xf, w_in_p, gain=norm_mix[l]).reshape(B, S, IN_COLS_P)

        g_col = p_all[:, :, OFF_MG:OFF_MG + 2 * M_HEADS].reshape(B, nc, CHUNK, 2 * M_HEADS)
        g_row = g_col.swapaxes(-1, -2)
        y_m = _mlstm(p_all, g_row, g_col, m_gate_b[l].reshape(-1, 1), row(m_gate_b[l]),
                     m_conv_w[l], row(m_conv_b[l]), row(m_norm_g[l]))

        mu = _relayout_in(jnp.pad(row(r_mu[l]), ((0, 0), (4 * M_WIDTH + 2 * M_HEADS, 0))))
        w_up = jnp.pad(r_w_up[l], ((0, R_AAA_LORA), (0, 0))).astype(BF16)
        a_up = jnp.pad(r_a_up[l], ((R_DECAY_LORA, 0), (0, 0))).astype(BF16)
        g_up = jnp.pad(r_g_up[l], ((0, GLORA_PAD - R_GATE_LORA), (0, 0))).astype(BF16)
        y_r = _rwkv(p_all, mu, row(r_w0[l]), w_up, row(r_a0[l]), a_up, g_up,
                    row(r_kk[l]), row(r_ka[l]), row(r_rk[l]), row(r_gn_g[l]), row(r_gn_b[l]))

        y = jnp.concatenate([y_m, y_r], axis=-1).reshape(T, D)
        xf = _mm(y, w_out[l].astype(BF16), resid=xf)

        q = _mm(xf, x_wq[l].astype(BF16), gain=norm_x[l], out_dtype=BF16)
        kv = _mm(memf, x_wkv[l].astype(BF16), gain=norm_mem[l], out_dtype=BF16)
        o = _xattn(q.reshape(B, S, D), kv.reshape(B, M, 2 * D))
        xf = _mm(o.reshape(T, D), x_wo[l].astype(BF16), resid=xf)

        u = _mm(xf, f_up[l].astype(BF16), gain=norm_ffn[l], out_dtype=BF16)
        act = _glu(u.reshape(B, S, 2 * D_FF), f_conv_w[l], f_conv_b[l])
        xf = _mm(act.reshape(T, D_FF), f_down[l].astype(BF16), resid=xf)

    return _rmsnorm(xf, norm_final).reshape(B, S, D)
```

```python
import functools
import math

import jax
import jax.numpy as jnp
from jax import lax
from jax.experimental import pallas as pl
from jax.experimental.pallas import tpu as pltpu

F32 = jnp.float32
BF16 = jnp.bfloat16

D_MODEL = 1024
M_WIDTH = 512
M_HEADS = 4
M_HDIM = 128
M_CONV = 4
R_WIDTH = 512
R_HDIM = 64
R_HEADS = 8
R_DECAY_LORA = 64
R_AAA_LORA = 64
R_GATE_LORA = 160
DECAY_SCALE = math.exp(-0.5)
X_HEADS = 4
X_HDIM = 256
D_FF = 2816
FFN_CONV = 3
NORM_EPS = 1e-6
GN_EPS = 64e-5
CHUNK = 64

LANES = 128
SUBLANES = 8
GATE_PAD = LANES
GLORA_PAD = 2 * LANES
OFF_MQ, OFF_MK, OFF_MV, OFF_MO = 0, 512, 1024, 1536
OFF_RR = 2048
OFF_RK = OFF_RR + R_WIDTH
OFF_RV = OFF_RK + R_WIDTH
OFF_RWA = OFF_RV + R_WIDTH
OFF_MG = OFF_RWA + LANES
OFF_RG = OFF_MG + GATE_PAD
IN_COLS_P = OFF_RG + GLORA_PAD

SEQ_TILE = 256
VMEM_LIMIT = 48 * 1024 * 1024


def _dot(a, b):
    return jnp.dot(a, b, preferred_element_type=F32)


def _bdot(a, b):
    return jnp.dot(a.astype(BF16), b.astype(BF16), preferred_element_type=F32)


def _bdot_nt(a, b):
    return lax.dot_general(a.astype(BF16), b.astype(BF16), (((1,), (1,)), ((), ())),
                           preferred_element_type=F32)


def _bdot_tn(a, b):
    return lax.dot_general(a.astype(BF16), b.astype(BF16), (((0,), (0,)), ((), ())),
                           preferred_element_type=F32)


def _split2(x):
    hi = x.astype(BF16)
    lo = (x - hi.astype(F32)).astype(BF16)
    return hi, lo


def _sigmoid(x):
    return 1.0 / (1.0 + jnp.exp(-x))


def _shift_rows(x, sh):
    row = lax.broadcasted_iota(jnp.int32, x.shape, 0)
    return jnp.where(row >= sh, pltpu.roll(x, sh, 0), 0.0)


def _shift_rows_carry(x, prev, sh):
    ext = jnp.concatenate([prev, x], axis=0)
    return pltpu.roll(ext, sh, 0)[SUBLANES:]


def _mm_kernel(*refs, has_gain, has_resid, tn):
    x_ref, w_ref = refs[0], refs[1]
    pos = 2
    g_ref = r_ref = None
    if has_gain:
        g_ref = refs[pos]
        pos += 1
    if has_resid:
        r_ref = refs[pos]
        pos += 1
    o_ref, h_scr = refs[pos], refs[pos + 1]

    x = x_ref[...].astype(F32)
    if has_gain:
        ms = jnp.mean(x * x, axis=-1, keepdims=True)
        x = x * lax.rsqrt(ms + NORM_EPS) * g_ref[...]
    h_scr[...] = x.astype(BF16)
    for c in range(w_ref.shape[1] // tn):
        cols = slice(c * tn, (c + 1) * tn)
        acc = _dot(h_scr[...], w_ref[:, cols])
        if has_resid:
            acc = acc + r_ref[:, cols]
        o_ref[:, cols] = acc.astype(o_ref.dtype)


def _mm_tile_rows(K, N, x_bytes, out_bytes, has_resid):
    budget = (VMEM_LIMIT * 3) // 4 - K * N * 2
    for tm in (512, 256, 128):
        per_row = 2 * K * x_bytes + 2 * N * out_bytes + K * 2 + (2 * N * 4 if has_resid else 0)
        if tm * per_row <= budget:
            return tm
    raise ValueError("weight does not fit in VMEM")


def _mm(x, w, gain=None, resid=None, out_dtype=F32, tn=512):
    T, K = x.shape
    N = w.shape[1]
    tm = min(T, _mm_tile_rows(K, N, x.dtype.itemsize, jnp.dtype(out_dtype).itemsize,
                              resid is not None))
    tn = min(tn, N)
    assert T % tm == 0 and N % tn == 0
    in_specs = [pl.BlockSpec((tm, K), lambda i: (i, 0)),
                pl.BlockSpec((K, N), lambda i: (0, 0), pipeline_mode=pl.Buffered(1))]
    args = [x, w]
    if gain is not None:
        in_specs.append(pl.BlockSpec((1, K), lambda i: (0, 0)))
        args.append(gain.reshape(1, K))
    if resid is not None:
        in_specs.append(pl.BlockSpec((tm, N), lambda i: (i, 0)))
        args.append(resid)
    return pl.pallas_call(
        functools.partial(_mm_kernel, has_gain=gain is not None, has_resid=resid is not None,
                          tn=tn),
        out_shape=jax.ShapeDtypeStruct((T, N), out_dtype),
        grid=(T // tm,),
        in_specs=in_specs,
        out_specs=pl.BlockSpec((tm, N), lambda i: (i, 0)),
        scratch_shapes=[pltpu.VMEM((tm, K), BF16)],
        compiler_params=pltpu.CompilerParams(
            dimension_semantics=("parallel",), vmem_limit_bytes=VMEM_LIMIT),
        name="mm",
    )(*args)


def _rms_kernel(x_ref, g_ref, o_ref):
    x = x_ref[...]
    ms = jnp.mean(x * x, axis=-1, keepdims=True)
    o_ref[...] = x * lax.rsqrt(ms + NORM_EPS) * g_ref[...]


def _rmsnorm(x, gain, tm=1024):
    T, K = x.shape
    tm = min(tm, T)
    return pl.pallas_call(
        _rms_kernel,
        out_shape=jax.ShapeDtypeStruct((T, K), F32),
        grid=(T // tm,),
        in_specs=[pl.BlockSpec((tm, K), lambda i: (i, 0)),
                  pl.BlockSpec((1, K), lambda i: (0, 0))],
        out_specs=pl.BlockSpec((tm, K), lambda i: (i, 0)),
        compiler_params=pltpu.CompilerParams(
            dimension_semantics=("parallel",), vmem_limit_bytes=VMEM_LIMIT),
        name="rmsnorm",
    )(x, gain.reshape(1, K))


def _mlstm_kernel(q_ref, k_ref, v_ref, o_ref, grow_ref, gcol_ref, brow_ref, bcol_ref,
                  cw_ref, cb_ref, ng_ref, out_ref,
                  qp_scr, kp_scr, c_scr, n_scr, m_scr):
    TS = q_ref.shape[1]
    L = CHUNK
    nch = TS // L

    @pl.when(pl.program_id(1) == 0)
    def _():
        qp_scr[...] = jnp.zeros_like(qp_scr)
        kp_scr[...] = jnp.zeros_like(kp_scr)
        c_scr[...] = jnp.zeros_like(c_scr)
        n_scr[...] = jnp.zeros_like(n_scr)
        m_scr[...] = jnp.zeros_like(m_scr)

    def conv_silu(x, prev, w, b):
        y = b + x * w[M_CONV - 1:M_CONV, :]
        for j in range(M_CONV - 1):
            y = y + _shift_rows_carry(x, prev, M_CONV - 1 - j) * w[j:j + 1, :]
        return y * _sigmoid(y)

    q_raw = q_ref[0]
    k_raw = k_ref[0]
    qc_all = conv_silu(q_raw, qp_scr[...], cw_ref[:, :M_WIDTH], cb_ref[:, :M_WIDTH]) * (M_HDIM ** -0.5)
    kc_all = conv_silu(k_raw, kp_scr[...], cw_ref[:, M_WIDTH:], cb_ref[:, M_WIDTH:])
    qp_scr[...] = q_raw[TS - SUBLANES:]
    kp_scr[...] = k_raw[TS - SUBLANES:]
    v_all = v_ref[0]
    o_all = o_ref[0]

    ti = lax.broadcasted_iota(jnp.int32, (L, L), 0)
    si = lax.broadcasted_iota(jnp.int32, (L, L), 1)
    causal = si <= ti
    upper = ti <= si

    def log_sigmoid(x):
        return jnp.minimum(x, 0.0) - jnp.log1p(jnp.exp(-jnp.abs(x)))

    hs = range(M_HEADS)
    diag = ti == si

    def heads(x, rs):
        return jnp.stack([x[rs, h * M_HDIM:(h + 1) * M_HDIM] for h in hs])

    def per_head(f, *xs):
        return jnp.stack([f(*[x[h] for x in xs]) for h in hs])

    ng = jnp.stack([ng_ref[:, h * M_HDIM:(h + 1) * M_HDIM] for h in hs])
    c_prev = c_scr[...]
    n_prev = n_scr[...]
    m_prev = m_scr[...]
    for c in range(nch):
        rs = slice(c * L, (c + 1) * L)
        q = heads(qc_all, rs)
        k = heads(kc_all, rs)
        v = heads(v_all, rs)
        gr = grow_ref[0, c] + brow_ref[...]
        gc = gcol_ref[0, c] + bcol_ref[...]
        logi_r = gr[:M_HEADS]
        logf_r = log_sigmoid(gr[M_HEADS:])
        logi_c = jnp.stack([gc[:, h:h + 1] for h in hs])
        b_c = jnp.sum(jnp.where(causal, logf_r, 0.0), axis=2, keepdims=True)
        b_r = jnp.sum(jnp.where(diag, b_c, 0.0), axis=1, keepdims=True)
        g = jnp.sum(logf_r, axis=2, keepdims=True)
        a_r = g - b_r + logi_r
        a_c = g - b_c + logi_c
        m_loc = jnp.max(a_r, axis=2, keepdims=True)
        wa_c = jnp.exp(a_c - m_loc)
        c_loc = per_head(_bdot_tn, v * wa_c, k)
        n_loc = jnp.sum(k * wa_c, axis=1, keepdims=True)

        inter = b_c + m_prev
        d = jnp.where(causal, b_c - b_r + logi_r, -jnp.inf)
        m_t = jnp.maximum(inter, jnp.max(d, axis=2, keepdims=True))
        s_int = jnp.exp(inter - m_t)
        p = jnp.exp(d - m_t) * per_head(_bdot_nt, q, k)
        num = s_int * per_head(_bdot_nt, q, c_prev) + per_head(_bdot, p, v)
        den = (s_int * jnp.sum(q * n_prev, axis=2, keepdims=True)
               + jnp.sum(p, axis=2, keepdims=True))
        hh = num / jnp.maximum(jnp.abs(den), jnp.exp(-m_t))
        mu = jnp.mean(hh, axis=-1, keepdims=True)
        hc = hh - mu
        var = jnp.mean(hc * hc, axis=-1, keepdims=True)
        y = _sigmoid(heads(o_all, rs)) * (hc * lax.rsqrt(var + NORM_EPS)) * ng
        for h in hs:
            out_ref[0, rs, h * M_HDIM:(h + 1) * M_HDIM] = y[h].astype(out_ref.dtype)

        m_new = jnp.maximum(g + m_prev, m_loc)
        s_old = jnp.exp(g + m_prev - m_new)
        s_loc = jnp.exp(m_loc - m_new)
        c_prev = s_old * c_prev + s_loc * c_loc
        n_prev = s_old * n_prev + s_loc * n_loc
        m_prev = m_new
    c_scr[...] = c_prev
    n_scr[...] = n_prev
    m_scr[...] = m_prev


def _mlstm(p_all, g_row, g_col, b_row, b_col, conv_w, conv_b, norm_g):
    B, S, _ = p_all.shape
    ts = min(SEQ_TILE, S)
    nch = ts // CHUNK
    seq = lambda off: pl.BlockSpec((1, ts, M_WIDTH), lambda b, s, off=off: (b, s, off))
    full = lambda a: pl.BlockSpec(a.shape, lambda b, s: (0,) * a.ndim)
    return pl.pallas_call(
        _mlstm_kernel,
        out_shape=jax.ShapeDtypeStruct((B, S, M_WIDTH), BF16),
        grid=(B, S // ts),
        in_specs=[
            seq(OFF_MQ // M_WIDTH), seq(OFF_MK // M_WIDTH), seq(OFF_MV // M_WIDTH),
            seq(OFF_MO // M_WIDTH),
            pl.BlockSpec((1, nch, 2 * M_HEADS, 1, CHUNK), lambda b, s: (b, s, 0, 0, 0)),
            pl.BlockSpec((1, nch, CHUNK, 2 * M_HEADS), lambda b, s: (b, s, 0, 0)),
            full(b_row), full(b_col), full(conv_w), full(conv_b), full(norm_g),
        ],
        out_specs=pl.BlockSpec((1, ts, M_WIDTH), lambda b, s: (b, s, 0)),
        scratch_shapes=[pltpu.VMEM((SUBLANES, M_WIDTH), F32), pltpu.VMEM((SUBLANES, M_WIDTH), F32),
                        pltpu.VMEM((M_HEADS, M_HDIM, M_HDIM), F32),
                        pltpu.VMEM((M_HEADS, 1, M_HDIM), F32),
                        pltpu.VMEM((M_HEADS, 1, 1), F32)],
        compiler_params=pltpu.CompilerParams(
            dimension_semantics=("parallel", "arbitrary"), vmem_limit_bytes=VMEM_LIMIT),
        name="mlstm",
    )(p_all, p_all, p_all, p_all, g_row, g_col, b_row, b_col, conv_w, conv_b, norm_g)


def _inv_unit_lower_blocks(a, blk):
    n = a.shape[0]
    ti = lax.broadcasted_iota(jnp.int32, (n, n), 0)
    si = lax.broadcasted_iota(jnp.int32, (n, n), 1)

    def off_mask(b):
        sh = (2 * b).bit_length() - 1
        same = jnp.right_shift(ti, sh) == jnp.right_shift(si, sh)
        return same & (jnp.bitwise_and(ti, b) != 0) & (jnp.bitwise_and(si, b) == 0)

    inv = jnp.where(ti == si, 1.0, 0.0) - jnp.where(off_mask(1), a, 0.0)
    b = 2
    while b < blk:
        off = jnp.where(off_mask(b), a, 0.0)
        inv = inv - _bdot(_bdot(inv, off), inv)
        b *= 2
    return inv


def _rwkv_kernel(pr_ref, pk_ref, pv_ref, pwa_ref, pg_ref,
                 mur_ref, muk_ref, muv_ref, muwa_ref, mug_ref,
                 w0_ref, wup_ref, a0_ref, aup_ref, gup_ref,
                 kkp_ref, ka_ref, rk_ref, gng_ref, gnb_ref,
                 out_ref,
                 cr_scr, ck_scr, cv_scr, cwa_scr, cg_scr, st_scr):
    TS = pr_ref.shape[1]
    L = CHUNK
    N = R_HDIM
    nch = TS // L
    npair = R_WIDTH // LANES
    Q = 2 * LANES
    nquad = TS // (2 * L)

    @pl.when(pl.program_id(1) == 0)
    def _():
        for scr in (cr_scr, ck_scr, cv_scr, cwa_scr, cg_scr, st_scr):
            scr[...] = jnp.zeros_like(scr)

    def tshift(p_ref, mu_ref, c_scr):
        p = p_ref[0]
        prev = _shift_rows_carry(p, c_scr[...], 1)
        c_scr[...] = p[TS - SUBLANES:]
        return p + (prev - p) * mu_ref[...]

    hsh = N.bit_length() - 1
    li = lax.broadcasted_iota(jnp.int32, (LANES, LANES), 0)
    lj = lax.broadcasted_iota(jnp.int32, (LANES, LANES), 1)
    same_head = jnp.right_shift(li, hsh) == jnp.right_shift(lj, hsh)
    ones_bd = jnp.where(same_head, 1.0, 0.0).astype(BF16)

    def seg_sum(x):
        outs = []
        for p in range(x.shape[1] // LANES):
            hi, lo = _split2(x[:, p * LANES:(p + 1) * LANES])
            outs.append(_dot(hi, ones_bd) + _dot(lo, ones_bd))
        return jnp.concatenate(outs, axis=1)

    rr = tshift(pr_ref, mur_ref, cr_scr)
    kr = tshift(pk_ref, muk_ref, ck_scr)
    vr = tshift(pv_ref, muv_ref, cv_scr)
    wa = tshift(pwa_ref, muwa_ref, cwa_scr)
    gd = tshift(pg_ref, mug_ref, cg_scr)
    logw = -DECAY_SCALE * _sigmoid(w0_ref[...] + _bdot(jnp.tanh(wa), wup_ref[...]))
    a = _sigmoid(a0_ref[...] + _bdot(wa, aup_ref[...]))
    g = _bdot(_sigmoid(gd), gup_ref[...])
    kkraw = kr * kkp_ref[...]
    kk = kkraw / jnp.maximum(jnp.sqrt(seg_sum(kkraw * kkraw)), 1e-12)
    km = kr * (1.0 + (a - 1.0) * ka_ref[...])
    be = kk * a

    ti = lax.broadcasted_iota(jnp.int32, (Q, Q), 0)
    si = lax.broadcasted_iota(jnp.int32, (Q, Q), 1)
    same_blk = jnp.right_shift(ti, hsh) == jnp.right_shift(si, hsh)
    strict = same_blk & (si < ti)
    incl = same_blk & (si <= ti)
    tril = jnp.where(incl, 1.0, 0.0).astype(BF16)
    lw_hi, lw_lo = _split2(logw)
    bincl = jnp.concatenate(
        [_dot(tril, lw_hi[q * Q:(q + 1) * Q]) + _dot(tril, lw_lo[q * Q:(q + 1) * Q])
         for q in range(TS // Q)], axis=0)
    e_in = jnp.exp(bincl)
    e_ng = jnp.exp(-bincl)
    kt = kk * jnp.exp(bincl - logw)
    rt = rr * e_in
    bh = be * e_ng
    kh = km * e_ng

    h0 = lax.broadcasted_iota(jnp.int32, (L, LANES), 1) < N

    def stack_heads(x):
        x0, x1 = x[:L], x[L:]
        return jnp.concatenate([jnp.where(h0, x0, 0.0), jnp.where(h0, 0.0, x0),
                                jnp.where(h0, x1, 0.0), jnp.where(h0, 0.0, x1)], axis=0)

    def stack_dup(x):
        return jnp.concatenate([x[:L], x[:L], x[L:], x[L:]], axis=0)

    def comb(x):
        return jnp.concatenate([jnp.where(h0, x[0:L], x[L:2 * L]),
                                jnp.where(h0, x[2 * L:3 * L], x[3 * L:4 * L])], axis=0)

    w_ch = [[None] * nch for _ in range(npair)]
    u0_ch = [[None] * nch for _ in range(npair)]
    arkv_ch = [[None] * nch for _ in range(npair)]
    arb_ch = [[None] * nch for _ in range(npair)]
    for p in range(npair):
        ls = slice(p * LANES, (p + 1) * LANES)
        for qd in range(nquad):
            rs = slice(qd * 2 * L, (qd + 1) * 2 * L)
            lk = stack_heads(kt[rs, ls]).astype(BF16)
            lr = stack_heads(rt[rs, ls]).astype(BF16)
            rb = stack_dup(bh[rs, ls]).astype(BF16)
            rk = stack_dup(kh[rs, ls]).astype(BF16)
            vs = stack_dup(vr[rs, ls]).astype(BF16)
            a_bd = jnp.where(strict, _bdot_nt(lk, rb), 0.0)
            bk_bd = jnp.where(strict, _bdot_nt(lk, rk), 0.0)
            arb_bd = jnp.where(incl, _bdot_nt(lr, rb), 0.0)
            ark_bd = jnp.where(incl, _bdot_nt(lr, rk), 0.0)
            t_bd = _inv_unit_lower_blocks(a_bd, L)
            bkv = comb(_bdot(bk_bd, vs))
            xs = jnp.concatenate([stack_dup(kt[rs, ls]), stack_dup(bkv)], axis=1)
            tx = _bdot(t_bd, xs)
            w_q = comb(tx[:, :LANES])
            u0_q = -comb(tx[:, LANES:])
            arkv_q = comb(_bdot(ark_bd, vs))
            for j in range(2):
                c = qd * 2 + j
                w_ch[p][c] = w_q[j * L:(j + 1) * L]
                u0_ch[p][c] = u0_q[j * L:(j + 1) * L]
                arkv_ch[p][c] = arkv_q[j * L:(j + 1) * L]
                arb_ch[p][c] = arb_bd[j * 2 * L:(j + 1) * 2 * L, j * 2 * L:(j + 1) * 2 * L]

    y_rows = []
    st = [st_scr[p] for p in range(npair)]
    ones_f = jnp.where(same_head, 1.0, 0.0)
    for c in range(nch):
        rs = slice(c * L, (c + 1) * L)
        p_end = e_in[c * L + L - 1:c * L + L, :]
        y_p = []
        for p in range(npair):
            ls = slice(p * LANES, (p + 1) * LANES)
            pe = p_end[:, ls]
            rw = _bdot_nt(jnp.concatenate([rt[rs, ls], w_ch[p][c]], axis=0), st[p])
            u = u0_ch[p][c] - rw[L:]
            au = _bdot(arb_ch[p][c], jnp.concatenate([u, u], axis=0))
            y_p.append(rw[:L] + jnp.where(h0, au[:L], au[L:]) + arkv_ch[p][c])
            upd = _bdot_tn(jnp.concatenate([u, vr[rs, ls]], axis=0),
                           jnp.concatenate([bh[rs, ls] * pe, kh[rs, ls] * pe], axis=0))
            st[p] = st[p] * pe + upd * ones_f
        y_rows.append(jnp.concatenate(y_p, axis=1))
    for p in range(npair):
        st_scr[p] = st[p]
    y = jnp.concatenate(y_rows, axis=0)

    inv_n = 1.0 / N
    mu = seg_sum(y) * inv_n
    yc = y - mu
    var = seg_sum(yc * yc) * inv_n
    yn = yc * lax.rsqrt(var + GN_EPS) * gng_ref[...] + gnb_ref[...]
    bonus = seg_sum(rr * km * rk_ref[...]) * vr
    out_ref[0] = ((yn + bonus) * g).astype(out_ref.dtype)


def _rwkv(p_all, mu, w0, w_up, a0, a_up, g_up, kkp, ka, rk, gn_g, gn_b):
    B, S, _ = p_all.shape
    ts = min(SEQ_TILE, S)
    npair = R_WIDTH // LANES
    seq = lambda w, off: pl.BlockSpec((1, ts, w), lambda b, s, off=off: (b, s, off))
    vec = lambda w, off: pl.BlockSpec((1, w), lambda b, s, off=off: (0, off))
    full = lambda a: pl.BlockSpec(a.shape, lambda b, s: (0,) * a.ndim)
    return pl.pallas_call(
        _rwkv_kernel,
        out_shape=jax.ShapeDtypeStruct((B, S, R_WIDTH), BF16),
        grid=(B, S // ts),
        in_specs=[
            seq(R_WIDTH, OFF_RR // R_WIDTH), seq(R_WIDTH, OFF_RK // R_WIDTH),
            seq(R_WIDTH, OFF_RV // R_WIDTH), seq(LANES, OFF_RWA // LANES),
            seq(GLORA_PAD, OFF_RG // GLORA_PAD),
            vec(R_WIDTH, OFF_RR // R_WIDTH), vec(R_WIDTH, OFF_RK // R_WIDTH),
            vec(R_WIDTH, OFF_RV // R_WIDTH), vec(LANES, OFF_RWA // LANES),
            vec(GLORA_PAD, OFF_RG // GLORA_PAD),
            full(w0), full(w_up), full(a0), full(a_up), full(g_up),
            full(kkp), full(ka), full(rk), full(gn_g), full(gn_b),
        ],
        out_specs=pl.BlockSpec((1, ts, R_WIDTH), lambda b, s: (b, s, 0)),
        scratch_shapes=[pltpu.VMEM((SUBLANES, R_WIDTH), F32), pltpu.VMEM((SUBLANES, R_WIDTH), F32),
                        pltpu.VMEM((SUBLANES, R_WIDTH), F32), pltpu.VMEM((SUBLANES, LANES), F32),
                        pltpu.VMEM((SUBLANES, GLORA_PAD), F32),
                        pltpu.VMEM((npair, LANES, LANES), F32)],
        compiler_params=pltpu.CompilerParams(
            dimension_semantics=("parallel", "arbitrary"), vmem_limit_bytes=VMEM_LIMIT),
        name="rwkv",
    )(p_all, p_all, p_all, p_all, p_all, mu, mu, mu, mu, mu,
      w0, w_up, a0, a_up, g_up, kkp, ka, rk, gn_g, gn_b)


def _xattn_kernel(q_ref, k_ref, v_ref, o_ref):
    q = q_ref[0]
    s = lax.dot_general(q, k_ref[0], (((1,), (1,)), ((), ())),
                        preferred_element_type=F32) * (X_HDIM ** -0.5)
    s = s - jnp.max(s, axis=-1, keepdims=True)
    e = jnp.exp(s)
    p = e / jnp.sum(e, axis=-1, keepdims=True)
    o_ref[0] = _dot(p.astype(BF16), v_ref[0]).astype(o_ref.dtype)


def _xattn(q, kv, ts=1024):
    B, S, D = q.shape
    M = kv.shape[1]
    nh = D // X_HDIM
    ts = min(ts, S)
    return pl.pallas_call(
        _xattn_kernel,
        out_shape=jax.ShapeDtypeStruct((B, S, D), BF16),
        grid=(B, nh, S // ts),
        in_specs=[pl.BlockSpec((1, ts, X_HDIM), lambda b, h, t: (b, t, h)),
                  pl.BlockSpec((1, M, X_HDIM), lambda b, h, t: (b, 0, h)),
                  pl.BlockSpec((1, M, X_HDIM), lambda b, h, t: (b, 0, nh + h))],
        out_specs=pl.BlockSpec((1, ts, X_HDIM), lambda b, h, t: (b, t, h)),
        compiler_params=pltpu.CompilerParams(
            dimension_semantics=("parallel", "parallel", "parallel"),
            vmem_limit_bytes=VMEM_LIMIT),
        name="xattn",
    )(q, kv, kv)


def _glu_kernel(g_ref, v_ref, w_ref, b_ref, o_ref):
    x = g_ref[0].astype(F32)
    y = b_ref[...] + x * w_ref[FFN_CONV - 1:FFN_CONV, :]
    for j in range(FFN_CONV - 1):
        y = y + _shift_rows(x, FFN_CONV - 1 - j) * w_ref[j:j + 1, :]
    o_ref[0] = (y * _sigmoid(y) * v_ref[0].astype(F32)).astype(o_ref.dtype)


def _glu(u, conv_w, conv_b, tc=256):
    B, S, _ = u.shape
    nb = D_FF // tc
    return pl.pallas_call(
        _glu_kernel,
        out_shape=jax.ShapeDtypeStruct((B, S, D_FF), BF16),
        grid=(B, nb),
        in_specs=[pl.BlockSpec((1, S, tc), lambda b, j: (b, 0, j)),
                  pl.BlockSpec((1, S, tc), lambda b, j: (b, 0, nb + j)),
                  pl.BlockSpec((FFN_CONV, tc), lambda b, j: (0, j)),
                  pl.BlockSpec((1, tc), lambda b, j: (0, j))],
        out_specs=pl.BlockSpec((1, S, tc), lambda b, j: (b, 0, j)),
        compiler_params=pltpu.CompilerParams(
            dimension_semantics=("parallel", "parallel"), vmem_limit_bytes=VMEM_LIMIT),
        name="glu",
    )(u, u, conv_w, conv_b.reshape(1, D_FF))


def _pad_cols(w, n):
    return jnp.pad(w, ((0, 0), (0, n - w.shape[1])))


def _relayout_in(w):
    m_main = w[:, :4 * M_WIDTH]
    m_gate = w[:, 4 * M_WIDTH:4 * M_WIDTH + 2 * M_HEADS]
    r0 = 4 * M_WIDTH + 2 * M_HEADS
    r_main = w[:, r0:r0 + 3 * R_WIDTH + R_DECAY_LORA + R_AAA_LORA]
    r_gate = w[:, r0 + 3 * R_WIDTH + R_DECAY_LORA + R_AAA_LORA:]
    return jnp.concatenate(
        [m_main, r_main, _pad_cols(m_gate, GATE_PAD), _pad_cols(r_gate, GLORA_PAD)], axis=1)


def kernel(x, mem, norm_mix, w_in, m_conv_w, m_conv_b, m_gate_b, m_norm_g, r_mu, r_w0,
           r_w_up, r_a0, r_a_up, r_g_up, r_kk, r_ka, r_rk, r_gn_g, r_gn_b, w_out,
           norm_x, norm_mem, x_wq, x_wkv, x_wo, norm_ffn, f_up, f_conv_w, f_conv_b,
           f_down, norm_final):
    B, S, D = x.shape
    M = mem.shape[1]
    depth = w_in.shape[0]
    T = B * S
    nc = S // CHUNK
    xf = x.reshape(T, D)
    memf = mem.reshape(B * M, D)
    row = lambda a: a.reshape(1, -1)

    for l in range(depth):
        w_in_p = _relayout_in(w_in[l]).astype(BF16)
        p_all = _mm(xf, w_in_p, gain=norm_mix[l]).reshape(B, S, IN_COLS_P)

        g_col = p_all[:, :, OFF_MG:OFF_MG + 2 * M_HEADS].reshape(B, nc, CHUNK, 2 * M_HEADS)
        g_row = g_col.swapaxes(-1, -2)[:, :, :, None, :]
        y_m = _mlstm(p_all, g_row, g_col, m_gate_b[l].reshape(-1, 1, 1), row(m_gate_b[l]),
                     m_conv_w[l], row(m_conv_b[l]), row(m_norm_g[l]))

        mu = _relayout_in(jnp.pad(row(r_mu[l]), ((0, 0), (4 * M_WIDTH + 2 * M_HEADS, 0))))
        w_up = jnp.pad(r_w_up[l], ((0, R_AAA_LORA), (0, 0))).astype(BF16)
        a_up = jnp.pad(r_a_up[l], ((R_DECAY_LORA, 0), (0, 0))).astype(BF16)
        g_up = jnp.pad(r_g_up[l], ((0, GLORA_PAD - R_GATE_LORA), (0, 0))).astype(BF16)
        y_r = _rwkv(p_all, mu, row(r_w0[l]), w_up, row(r_a0[l]), a_up, g_up,
                    row(r_kk[l]), row(r_ka[l]), row(r_rk[l]), row(r_gn_g[l]), row(r_gn_b[l]))

        y = jnp.concatenate([y_m, y_r], axis=-1).reshape(T, D)
        xf = _mm(y, w_out[l].astype(BF16), resid=xf)

        q = _mm(xf, x_wq[l].astype(BF16), gain=norm_x[l], out_dtype=BF16)
        kv = _mm(memf, x_wkv[l].astype(BF16), gain=norm_mem[l], out_dtype=BF16)
        o = _xattn(q.reshape(B, S, D), kv.reshape(B, M, 2 * D))
        xf = _mm(o.reshape(T, D), x_wo[l].astype(BF16), resid=xf)

        u = _mm(xf, f_up[l].astype(BF16), gain=norm_ffn[l], out_dtype=BF16)
        act = _glu(u.reshape(B, S, 2 * D_FF), f_conv_w[l], f_conv_b[l])
        xf = _mm(act.reshape(T, D_FF), f_down[l].astype(BF16), resid=xf)

    return _rmsnorm(xf, norm_final).reshape(B, S, D)
```

```python
import functools
import math

import jax
import jax.numpy as jnp
from jax import lax
from jax.experimental import pallas as pl
from jax.experimental.pallas import tpu as pltpu

F32 = jnp.float32
BF16 = jnp.bfloat16

D_MODEL = 1024
M_WIDTH = 512
M_HEADS = 4
M_HDIM = 128
M_CONV = 4
R_WIDTH = 512
R_HDIM = 64
R_HEADS = 8
R_DECAY_LORA = 64
R_AAA_LORA = 64
R_GATE_LORA = 160
DECAY_SCALE = math.exp(-0.5)
X_HEADS = 4
X_HDIM = 256
D_FF = 2816
FFN_CONV = 3
NORM_EPS = 1e-6
GN_EPS = 64e-5
CHUNK = 64

LANES = 128
SUBLANES = 8
GATE_PAD = LANES
GLORA_PAD = 2 * LANES
OFF_MQ, OFF_MK, OFF_MV, OFF_MO = 0, 512, 1024, 1536
OFF_RR = 2048
OFF_RK = OFF_RR + R_WIDTH
OFF_RV = OFF_RK + R_WIDTH
OFF_RWA = OFF_RV + R_WIDTH
OFF_MG = OFF_RWA + LANES
OFF_RG = OFF_MG + GATE_PAD
IN_COLS_P = OFF_RG + GLORA_PAD

SEQ_TILE = 256
VMEM_LIMIT = 48 * 1024 * 1024


def _dot(a, b):
    return jnp.dot(a, b, preferred_element_type=F32)


def _bdot(a, b):
    return jnp.dot(a.astype(BF16), b.astype(BF16), preferred_element_type=F32)


def _bdot_nt(a, b):
    return lax.dot_general(a.astype(BF16), b.astype(BF16), (((1,), (1,)), ((), ())),
                           preferred_element_type=F32)


def _bdot_tn(a, b):
    return lax.dot_general(a.astype(BF16), b.astype(BF16), (((0,), (0,)), ((), ())),
                           preferred_element_type=F32)


def _split2(x):
    hi = x.astype(BF16)
    lo = (x - hi.astype(F32)).astype(BF16)
    return hi, lo


def _sigmoid(x):
    return 1.0 / (1.0 + jnp.exp(-x))


def _shift_rows(x, sh):
    row = lax.broadcasted_iota(jnp.int32, x.shape, 0)
    return jnp.where(row >= sh, pltpu.roll(x, sh, 0), 0.0)


def _shift_rows_carry(x, prev, sh):
    ext = jnp.concatenate([prev, x], axis=0)
    return pltpu.roll(ext, sh, 0)[SUBLANES:]


def _mm_kernel(*refs, has_gain, has_resid, tn):
    x_ref, w_ref = refs[0], refs[1]
    pos = 2
    g_ref = r_ref = None
    if has_gain:
        g_ref = refs[pos]
        pos += 1
    if has_resid:
        r_ref = refs[pos]
        pos += 1
    o_ref, h_scr = refs[pos], refs[pos + 1]

    x = x_ref[...].astype(F32)
    if has_gain:
        ms = jnp.mean(x * x, axis=-1, keepdims=True)
        x = x * lax.rsqrt(ms + NORM_EPS) * g_ref[...]
    h_scr[...] = x.astype(BF16)
    for c in range(w_ref.shape[1] // tn):
        cols = slice(c * tn, (c + 1) * tn)
        acc = _dot(h_scr[...], w_ref[:, cols])
        if has_resid:
            acc = acc + r_ref[:, cols]
        o_ref[:, cols] = acc.astype(o_ref.dtype)


def _mm_tile_rows(K, N, x_bytes, out_bytes, has_resid):
    budget = (VMEM_LIMIT * 3) // 4 - K * N * 2
    for tm in (512, 256, 128):
        per_row = 2 * K * x_bytes + 2 * N * out_bytes + K * 2 + (2 * N * 4 if has_resid else 0)
        if tm * per_row <= budget:
            return tm
    raise ValueError("weight does not fit in VMEM")


def _mm(x, w, gain=None, resid=None, out_dtype=F32, tn=512):
    T, K = x.shape
    N = w.shape[1]
    tm = min(T, _mm_tile_rows(K, N, x.dtype.itemsize, jnp.dtype(out_dtype).itemsize,
                              resid is not None))
    tn = min(tn, N)
    assert T % tm == 0 and N % tn == 0
    in_specs = [pl.BlockSpec((tm, K), lambda i: (i, 0)),
                pl.BlockSpec((K, N), lambda i: (0, 0), pipeline_mode=pl.Buffered(1))]
    args = [x, w]
    if gain is not None:
        in_specs.append(pl.BlockSpec((1, K), lambda i: (0, 0)))
        args.append(gain.reshape(1, K))
    if resid is not None:
        in_specs.append(pl.BlockSpec((tm, N), lambda i: (i, 0)))
        args.append(resid)
    return pl.pallas_call(
        functools.partial(_mm_kernel, has_gain=gain is not None, has_resid=resid is not None,
                          tn=tn),
        out_shape=jax.ShapeDtypeStruct((T, N), out_dtype),
        grid=(T // tm,),
        in_specs=in_specs,
        out_specs=pl.BlockSpec((tm, N), lambda i: (i, 0)),
        scratch_shapes=[pltpu.VMEM((tm, K), BF16)],
        compiler_params=pltpu.CompilerParams(
            dimension_semantics=("parallel",), vmem_limit_bytes=VMEM_LIMIT),
        name="mm",
    )(*args)


def _rms_kernel(x_ref, g_ref, o_ref):
    x = x_ref[...]
    ms = jnp.mean(x * x, axis=-1, keepdims=True)
    o_ref[...] = x * lax.rsqrt(ms + NORM_EPS) * g_ref[...]


def _rmsnorm(x, gain, tm=1024):
    T, K = x.shape
    tm = min(tm, T)
    return pl.pallas_call(
        _rms_kernel,
        out_shape=jax.ShapeDtypeStruct((T, K), F32),
        grid=(T // tm,),
        in_specs=[pl.BlockSpec((tm, K), lambda i: (i, 0)),
                  pl.BlockSpec((1, K), lambda i: (0, 0))],
        out_specs=pl.BlockSpec((tm, K), lambda i: (i, 0)),
        compiler_params=pltpu.CompilerParams(
            dimension_semantics=("parallel",), vmem_limit_bytes=VMEM_LIMIT),
        name="rmsnorm",
    )(x, gain.reshape(1, K))


def _mlstm_kernel(q_ref, k_ref, v_ref, o_ref, grow_ref, gcol_ref, brow_ref, bcol_ref,
                  cw_ref, cb_ref, ng_ref, out_ref,
                  qp_scr, kp_scr, c_scr, n_scr, m_scr):
    TS = q_ref.shape[1]
    L = CHUNK
    nch = TS // L

    @pl.when(pl.program_id(1) == 0)
    def _():
        qp_scr[...] = jnp.zeros_like(qp_scr)
        kp_scr[...] = jnp.zeros_like(kp_scr)
        c_scr[...] = jnp.zeros_like(c_scr)
        n_scr[...] = jnp.zeros_like(n_scr)
        m_scr[...] = jnp.zeros_like(m_scr)

    def conv_silu(x, prev, w, b):
        y = b + x * w[M_CONV - 1:M_CONV, :]
        for j in range(M_CONV - 1):
            y = y + _shift_rows_carry(x, prev, M_CONV - 1 - j) * w[j:j + 1, :]
        return y * _sigmoid(y)

    q_raw = q_ref[0]
    k_raw = k_ref[0]
    qc_all = conv_silu(q_raw, qp_scr[...], cw_ref[:, :M_WIDTH], cb_ref[:, :M_WIDTH]) * (M_HDIM ** -0.5)
    kc_all = conv_silu(k_raw, kp_scr[...], cw_ref[:, M_WIDTH:], cb_ref[:, M_WIDTH:])
    qp_scr[...] = q_raw[TS - SUBLANES:]
    kp_scr[...] = k_raw[TS - SUBLANES:]
    v_all = v_ref[0]
    o_all = o_ref[0]

    ti = lax.broadcasted_iota(jnp.int32, (L, L), 0)
    si = lax.broadcasted_iota(jnp.int32, (L, L), 1)
    causal = si <= ti
    upper = ti <= si

    def log_sigmoid(x):
        return jnp.minimum(x, 0.0) - jnp.log1p(jnp.exp(-jnp.abs(x)))

    hs = range(M_HEADS)
    diag = ti == si

    def heads(x, rs):
        return jnp.stack([x[rs, h * M_HDIM:(h + 1) * M_HDIM] for h in hs])

    def per_head(f, *xs):
        return jnp.stack([f(*[x[h] for x in xs]) for h in hs])

    ng = jnp.stack([ng_ref[:, h * M_HDIM:(h + 1) * M_HDIM] for h in hs])
    c_prev = c_scr[...]
    n_prev = n_scr[...]
    m_prev = m_scr[...]
    for c in range(nch):
        rs = slice(c * L, (c + 1) * L)
        q = heads(qc_all, rs)
        k = heads(kc_all, rs)
        v = heads(v_all, rs)
        gr = grow_ref[0, c] + brow_ref[...]
        gc = gcol_ref[0, c] + bcol_ref[...]
        logi_r = gr[:M_HEADS]
        logf_r = log_sigmoid(gr[M_HEADS:])
        logi_c = jnp.stack([gc[:, h:h + 1] for h in hs])
        b_c = jnp.sum(jnp.where(causal, logf_r, 0.0), axis=2, keepdims=True)
        b_r = jnp.sum(jnp.where(diag, b_c, 0.0), axis=1, keepdims=True)
        g = jnp.sum(logf_r, axis=2, keepdims=True)
        a_r = g - b_r + logi_r
        a_c = g - b_c + logi_c
        m_loc = jnp.max(a_r, axis=2, keepdims=True)
        wa_c = jnp.exp(a_c - m_loc)
        c_loc = per_head(_bdot_tn, v * wa_c, k)
        n_loc = jnp.sum(k * wa_c, axis=1, keepdims=True)

        inter = b_c + m_prev
        d = jnp.where(causal, b_c - b_r + logi_r, -jnp.inf)
        m_t = jnp.maximum(inter, jnp.max(d, axis=2, keepdims=True))
        s_int = jnp.exp(inter - m_t)
        p = jnp.exp(d - m_t) * per_head(_bdot_nt, q, k)
        num = s_int * per_head(_bdot_nt, q, c_prev) + per_head(_bdot, p, v)
        den = (s_int * jnp.sum(q * n_prev, axis=2, keepdims=True)
               + jnp.sum(p, axis=2, keepdims=True))
        hh = num / jnp.maximum(jnp.abs(den), jnp.exp(-m_t))
        mu = jnp.mean(hh, axis=-1, keepdims=True)
        hc = hh - mu
        var = jnp.mean(hc * hc, axis=-1, keepdims=True)
        y = _sigmoid(heads(o_all, rs)) * (hc * lax.rsqrt(var + NORM_EPS)) * ng
        for h in hs:
            out_ref[0, rs, h * M_HDIM:(h + 1) * M_HDIM] = y[h].astype(out_ref.dtype)

        m_new = jnp.maximum(g + m_prev, m_loc)
        s_old = jnp.exp(g + m_prev - m_new)
        s_loc = jnp.exp(m_loc - m_new)
        c_prev = s_old * c_prev + s_loc * c_loc
        n_prev = s_old * n_prev + s_loc * n_loc
        m_prev = m_new
    c_scr[...] = c_prev
    n_scr[...] = n_prev
    m_scr[...] = m_prev


def _mlstm(p_all, g_row, g_col, b_row, b_col, conv_w, conv_b, norm_g):
    B, S, _ = p_all.shape
    ts = min(SEQ_TILE, S)
    nch = ts // CHUNK
    seq = lambda off: pl.BlockSpec((1, ts, M_WIDTH), lambda b, s, off=off: (b, s, off))
    full = lambda a: pl.BlockSpec(a.shape, lambda b, s: (0,) * a.ndim)
    return pl.pallas_call(
        _mlstm_kernel,
        out_shape=jax.ShapeDtypeStruct((B, S, M_WIDTH), BF16),
        grid=(B, S // ts),
        in_specs=[
            seq(OFF_MQ // M_WIDTH), seq(OFF_MK // M_WIDTH), seq(OFF_MV // M_WIDTH),
            seq(OFF_MO // M_WIDTH),
            pl.BlockSpec((1, nch, 2 * M_HEADS, 1, CHUNK), lambda b, s: (b, s, 0, 0, 0)),
            pl.BlockSpec((1, nch, CHUNK, 2 * M_HEADS), lambda b, s: (b, s, 0, 0)),
            full(b_row), full(b_col), full(conv_w), full(conv_b), full(norm_g),
        ],
        out_specs=pl.BlockSpec((1, ts, M_WIDTH), lambda b, s: (b, s, 0)),
        scratch_shapes=[pltpu.VMEM((SUBLANES, M_WIDTH), F32), pltpu.VMEM((SUBLANES, M_WIDTH), F32),
                        pltpu.VMEM((M_HEADS, M_HDIM, M_HDIM), F32),
                        pltpu.VMEM((M_HEADS, 1, M_HDIM), F32),
                        pltpu.VMEM((M_HEADS, 1, 1), F32)],
        compiler_params=pltpu.CompilerParams(
            dimension_semantics=("parallel", "arbitrary"), vmem_limit_bytes=VMEM_LIMIT),
        name="mlstm",
    )(p_all, p_all, p_all, p_all, g_row, g_col, b_row, b_col, conv_w, conv_b, norm_g)


def _inv_unit_lower_blocks(a, blk):
    n = a[0].shape[0]
    ti = lax.broadcasted_iota(jnp.int32, (n, n), 0)
    si = lax.broadcasted_iota(jnp.int32, (n, n), 1)

    def off_mask(b):
        sh = (2 * b).bit_length() - 1
        same = jnp.right_shift(ti, sh) == jnp.right_shift(si, sh)
        return same & (jnp.bitwise_and(ti, b) != 0) & (jnp.bitwise_and(si, b) == 0)

    eye = jnp.where(ti == si, 1.0, 0.0)
    m1 = off_mask(1)
    invs = [eye - jnp.where(m1, x, 0.0) for x in a]
    b = 2
    while b < blk:
        mb = off_mask(b)
        offs = [jnp.where(mb, x, 0.0).astype(BF16) for x in a]
        xs = [_bdot(i, o) for i, o in zip(invs, offs)]
        ys = [_bdot(x, i) for x, i in zip(xs, invs)]
        invs = [i - y for i, y in zip(invs, ys)]
        b *= 2
    return invs


def _rwkv_kernel(pr_ref, pk_ref, pv_ref, pwa_ref, pg_ref,
                 mur_ref, muk_ref, muv_ref, muwa_ref, mug_ref,
                 w0_ref, wup_ref, a0_ref, aup_ref, gup_ref,
                 kkp_ref, ka_ref, rk_ref, gng_ref, gnb_ref,
                 out_ref,
                 cr_scr, ck_scr, cv_scr, cwa_scr, cg_scr, st_scr):
    TS = pr_ref.shape[1]
    L = CHUNK
    N = R_HDIM
    nch = TS // L
    npair = R_WIDTH // LANES
    Q = 2 * LANES
    nquad = TS // (2 * L)

    @pl.when(pl.program_id(1) == 0)
    def _():
        for scr in (cr_scr, ck_scr, cv_scr, cwa_scr, cg_scr, st_scr):
            scr[...] = jnp.zeros_like(scr)

    def tshift(p_ref, mu_ref, c_scr):
        p = p_ref[0]
        prev = _shift_rows_carry(p, c_scr[...], 1)
        c_scr[...] = p[TS - SUBLANES:]
        return p + (prev - p) * mu_ref[...]

    hsh = N.bit_length() - 1
    li = lax.broadcasted_iota(jnp.int32, (LANES, LANES), 0)
    lj = lax.broadcasted_iota(jnp.int32, (LANES, LANES), 1)
    same_head = jnp.right_shift(li, hsh) == jnp.right_shift(lj, hsh)
    ones_bd = jnp.where(same_head, 1.0, 0.0).astype(BF16)

    def seg_sum(x):
        outs = []
        for p in range(x.shape[1] // LANES):
            hi, lo = _split2(x[:, p * LANES:(p + 1) * LANES])
            outs.append(_dot(hi, ones_bd) + _dot(lo, ones_bd))
        return jnp.concatenate(outs, axis=1)

    rr = tshift(pr_ref, mur_ref, cr_scr)
    kr = tshift(pk_ref, muk_ref, ck_scr)
    vr = tshift(pv_ref, muv_ref, cv_scr)
    wa = tshift(pwa_ref, muwa_ref, cwa_scr)
    gd = tshift(pg_ref, mug_ref, cg_scr)
    logw = -DECAY_SCALE * _sigmoid(w0_ref[...] + _bdot(jnp.tanh(wa), wup_ref[...]))
    a = _sigmoid(a0_ref[...] + _bdot(wa, aup_ref[...]))
    g = _bdot(_sigmoid(gd), gup_ref[...])
    kkraw = kr * kkp_ref[...]
    kk = kkraw / jnp.maximum(jnp.sqrt(seg_sum(kkraw * kkraw)), 1e-12)
    km = kr * (1.0 + (a - 1.0) * ka_ref[...])
    be = kk * a

    ti = lax.broadcasted_iota(jnp.int32, (Q, Q), 0)
    si = lax.broadcasted_iota(jnp.int32, (Q, Q), 1)
    same_blk = jnp.right_shift(ti, hsh) == jnp.right_shift(si, hsh)
    strict = same_blk & (si < ti)
    incl = same_blk & (si <= ti)
    tril = jnp.where(incl, 1.0, 0.0).astype(BF16)
    lw_hi, lw_lo = _split2(logw)
    bincl = jnp.concatenate(
        [_dot(tril, lw_hi[q * Q:(q + 1) * Q]) + _dot(tril, lw_lo[q * Q:(q + 1) * Q])
         for q in range(TS // Q)], axis=0)
    e_in = jnp.exp(bincl)
    e_ng = jnp.exp(-bincl)
    kt = kk * jnp.exp(bincl - logw)
    rt = rr * e_in
    bh = be * e_ng
    kh = km * e_ng

    h0 = lax.broadcasted_iota(jnp.int32, (L, LANES), 1) < N

    def stack_heads(x):
        x0, x1 = x[:L], x[L:]
        return jnp.concatenate([jnp.where(h0, x0, 0.0), jnp.where(h0, 0.0, x0),
                                jnp.where(h0, x1, 0.0), jnp.where(h0, 0.0, x1)], axis=0)

    def stack_dup(x):
        return jnp.concatenate([x[:L], x[:L], x[L:], x[L:]], axis=0)

    def comb(x):
        return jnp.concatenate([jnp.where(h0, x[0:L], x[L:2 * L]),
                                jnp.where(h0, x[2 * L:3 * L], x[3 * L:4 * L])], axis=0)

    w_ch = [[None] * nch for _ in range(npair)]
    u0_ch = [[None] * nch for _ in range(npair)]
    arkv_ch = [[None] * nch for _ in range(npair)]
    arb_ch = [[None] * nch for _ in range(npair)]
    probs = [(p, qd) for p in range(npair) for qd in range(nquad)]

    def tile(x, p, qd):
        return x[qd * 2 * L:(qd + 1) * 2 * L, p * LANES:(p + 1) * LANES]

    lk = [stack_heads(tile(kt, p, qd)).astype(BF16) for p, qd in probs]
    lr = [stack_heads(tile(rt, p, qd)).astype(BF16) for p, qd in probs]
    rb = [stack_dup(tile(bh, p, qd)).astype(BF16) for p, qd in probs]
    rk = [stack_dup(tile(kh, p, qd)).astype(BF16) for p, qd in probs]
    vs = [stack_dup(tile(vr, p, qd)).astype(BF16) for p, qd in probs]
    a_bd = [jnp.where(strict, _bdot_nt(x, y), 0.0) for x, y in zip(lk, rb)]
    bk_bd = [jnp.where(strict, _bdot_nt(x, y), 0.0).astype(BF16) for x, y in zip(lk, rk)]
    arb_bd = [jnp.where(incl, _bdot_nt(x, y), 0.0).astype(BF16) for x, y in zip(lr, rb)]
    ark_bd = [jnp.where(incl, _bdot_nt(x, y), 0.0).astype(BF16) for x, y in zip(lr, rk)]
    t_bd = _inv_unit_lower_blocks(a_bd, L)
    bkv = [comb(_bdot(x, y)) for x, y in zip(bk_bd, vs)]
    xs = [jnp.concatenate([stack_dup(tile(kt, p, qd)), stack_dup(z)], axis=1)
          for (p, qd), z in zip(probs, bkv)]
    tx = [_bdot(x, y) for x, y in zip(t_bd, xs)]
    arkv = [comb(_bdot(x, y)) for x, y in zip(ark_bd, vs)]
    for i, (p, qd) in enumerate(probs):
        w_q = comb(tx[i][:, :LANES])
        u0_q = -comb(tx[i][:, LANES:])
        for j in range(2):
            c = qd * 2 + j
            w_ch[p][c] = w_q[j * L:(j + 1) * L]
            u0_ch[p][c] = u0_q[j * L:(j + 1) * L]
            arkv_ch[p][c] = arkv[i][j * L:(j + 1) * L]
            arb_ch[p][c] = arb_bd[i][j * 2 * L:(j + 1) * 2 * L, j * 2 * L:(j + 1) * 2 * L]

    y_rows = []
    st = [st_scr[p] for p in range(npair)]
    ones_f = jnp.where(same_head, 1.0, 0.0)
    for c in range(nch):
        rs = slice(c * L, (c + 1) * L)
        p_end = e_in[c * L + L - 1:c * L + L, :]
        pairs = range(npair)
        lsl = [slice(p * LANES, (p + 1) * LANES) for p in pairs]
        pe = [p_end[:, ls] for ls in lsl]
        rw = [_bdot_nt(jnp.concatenate([rt[rs, lsl[p]], w_ch[p][c]], axis=0), st[p])
              for p in pairs]
        u = [u0_ch[p][c] - rw[p][L:] for p in pairs]
        au = [_bdot(arb_ch[p][c], jnp.concatenate([u[p], u[p]], axis=0)) for p in pairs]
        upd = [_bdot_tn(jnp.concatenate([u[p], vr[rs, lsl[p]]], axis=0),
                        jnp.concatenate([bh[rs, lsl[p]] * pe[p], kh[rs, lsl[p]] * pe[p]], axis=0))
               for p in pairs]
        st = [st[p] * pe[p] + upd[p] * ones_f for p in pairs]
        y_rows.append(jnp.concatenate(
            [rw[p][:L] + jnp.where(h0, au[p][:L], au[p][L:]) + arkv_ch[p][c] for p in pairs],
            axis=1))
    for p in range(npair):
        st_scr[p] = st[p]
    y = jnp.concatenate(y_rows, axis=0)

    inv_n = 1.0 / N
    mu = seg_sum(y) * inv_n
    yc = y - mu
    var = seg_sum(yc * yc) * inv_n
    yn = yc * lax.rsqrt(var + GN_EPS) * gng_ref[...] + gnb_ref[...]
    bonus = seg_sum(rr * km * rk_ref[...]) * vr
    out_ref[0] = ((yn + bonus) * g).astype(out_ref.dtype)


def _rwkv(p_all, mu, w0, w_up, a0, a_up, g_up, kkp, ka, rk, gn_g, gn_b):
    B, S, _ = p_all.shape
    ts = min(SEQ_TILE, S)
    npair = R_WIDTH // LANES
    seq = lambda w, off: pl.BlockSpec((1, ts, w), lambda b, s, off=off: (b, s, off))
    vec = lambda w, off: pl.BlockSpec((1, w), lambda b, s, off=off: (0, off))
    full = lambda a: pl.BlockSpec(a.shape, lambda b, s: (0,) * a.ndim)
    return pl.pallas_call(
        _rwkv_kernel,
        out_shape=jax.ShapeDtypeStruct((B, S, R_WIDTH), BF16),
        grid=(B, S // ts),
        in_specs=[
            seq(R_WIDTH, OFF_RR // R_WIDTH), seq(R_WIDTH, OFF_RK // R_WIDTH),
            seq(R_WIDTH, OFF_RV // R_WIDTH), seq(LANES, OFF_RWA // LANES),
            seq(GLORA_PAD, OFF_RG // GLORA_PAD),
            vec(R_WIDTH, OFF_RR // R_WIDTH), vec(R_WIDTH, OFF_RK // R_WIDTH),
            vec(R_WIDTH, OFF_RV // R_WIDTH), vec(LANES, OFF_RWA // LANES),
            vec(GLORA_PAD, OFF_RG // GLORA_PAD),
            full(w0), full(w_up), full(a0), full(a_up), full(g_up),
            full(kkp), full(ka), full(rk), full(gn_g), full(gn_b),
        ],
        out_specs=pl.BlockSpec((1, ts, R_WIDTH), lambda b, s: (b, s, 0)),
        scratch_shapes=[pltpu.VMEM((SUBLANES, R_WIDTH), F32), pltpu.VMEM((SUBLANES, R_WIDTH), F32),
                        pltpu.VMEM((SUBLANES, R_WIDTH), F32), pltpu.VMEM((SUBLANES, LANES), F32),
                        pltpu.VMEM((SUBLANES, GLORA_PAD), F32),
                        pltpu.VMEM((npair, LANES, LANES), F32)],
        compiler_params=pltpu.CompilerParams(
            dimension_semantics=("parallel", "arbitrary"), vmem_limit_bytes=VMEM_LIMIT),
        name="rwkv",
    )(p_all, p_all, p_all, p_all, p_all, mu, mu, mu, mu, mu,
      w0, w_up, a0, a_up, g_up, kkp, ka, rk, gn_g, gn_b)


def _xattn_kernel(q_ref, k_ref, v_ref, o_ref):
    q = q_ref[0]
    s = lax.dot_general(q, k_ref[0], (((1,), (1,)), ((), ())),
                        preferred_element_type=F32) * (X_HDIM ** -0.5)
    s = s - jnp.max(s, axis=-1, keepdims=True)
    e = jnp.exp(s)
    p = e / jnp.sum(e, axis=-1, keepdims=True)
    o_ref[0] = _dot(p.astype(BF16), v_ref[0]).astype(o_ref.dtype)


def _xattn(q, kv, ts=1024):
    B, S, D = q.shape
    M = kv.shape[1]
    nh = D // X_HDIM
    ts = min(ts, S)
    return pl.pallas_call(
        _xattn_kernel,
        out_shape=jax.ShapeDtypeStruct((B, S, D), BF16),
        grid=(B, nh, S // ts),
        in_specs=[pl.BlockSpec((1, ts, X_HDIM), lambda b, h, t: (b, t, h)),
                  pl.BlockSpec((1, M, X_HDIM), lambda b, h, t: (b, 0, h)),
                  pl.BlockSpec((1, M, X_HDIM), lambda b, h, t: (b, 0, nh + h))],
        out_specs=pl.BlockSpec((1, ts, X_HDIM), lambda b, h, t: (b, t, h)),
        compiler_params=pltpu.CompilerParams(
            dimension_semantics=("parallel", "parallel", "parallel"),
            vmem_limit_bytes=VMEM_LIMIT),
        name="xattn",
    )(q, kv, kv)


def _glu_kernel(g_ref, v_ref, w_ref, b_ref, o_ref):
    x = g_ref[0].astype(F32)
    y = b_ref[...] + x * w_ref[FFN_CONV - 1:FFN_CONV, :]
    for j in range(FFN_CONV - 1):
        y = y + _shift_rows(x, FFN_CONV - 1 - j) * w_ref[j:j + 1, :]
    o_ref[0] = (y * _sigmoid(y) * v_ref[0].astype(F32)).astype(o_ref.dtype)


def _glu(u, conv_w, conv_b, tc=256):
    B, S, _ = u.shape
    nb = D_FF // tc
    return pl.pallas_call(
        _glu_kernel,
        out_shape=jax.ShapeDtypeStruct((B, S, D_FF), BF16),
        grid=(B, nb),
        in_specs=[pl.BlockSpec((1, S, tc), lambda b, j: (b, 0, j)),
                  pl.BlockSpec((1, S, tc), lambda b, j: (b, 0, nb + j)),
                  pl.BlockSpec((FFN_CONV, tc), lambda b, j: (0, j)),
                  pl.BlockSpec((1, tc), lambda b, j: (0, j))],
        out_specs=pl.BlockSpec((1, S, tc), lambda b, j: (b, 0, j)),
        compiler_params=pltpu.CompilerParams(
            dimension_semantics=("parallel", "parallel"), vmem_limit_bytes=VMEM_LIMIT),
        name="glu",
    )(u, u, conv_w, conv_b.reshape(1, D_FF))


def _pad_cols(w, n):
    return jnp.pad(w, ((0, 0), (0, n - w.shape[1])))


def _relayout_in(w):
    m_main = w[:, :4 * M_WIDTH]
    m_gate = w[:, 4 * M_WIDTH:4 * M_WIDTH + 2 * M_HEADS]
    r0 = 4 * M_WIDTH + 2 * M_HEADS
    r_main = w[:, r0:r0 + 3 * R_WIDTH + R_DECAY_LORA + R_AAA_LORA]
    r_gate = w[:, r0 + 3 * R_WIDTH + R_DECAY_LORA + R_AAA_LORA:]
    return jnp.concatenate(
        [m_main, r_main, _pad_cols(m_gate, GATE_PAD), _pad_cols(r_gate, GLORA_PAD)], axis=1)


def kernel(x, mem, norm_mix, w_in, m_conv_w, m_conv_b, m_gate_b, m_norm_g, r_mu, r_w0,
           r_w_up, r_a0, r_a_up, r_g_up, r_kk, r_ka, r_rk, r_gn_g, r_gn_b, w_out,
           norm_x, norm_mem, x_wq, x_wkv, x_wo, norm_ffn, f_up, f_conv_w, f_conv_b,
           f_down, norm_final):
    B, S, D = x.shape
    M = mem.shape[1]
    depth = w_in.shape[0]
    T = B * S
    nc = S // CHUNK
    xf = x.reshape(T, D)
    memf = mem.reshape(B * M, D)
    row = lambda a: a.reshape(1, -1)

    for l in range(depth):
        w_in_p = _relayout_in(w_in[l]).astype(BF16)
        p_all = _mm(xf, w_in_p, gain=norm_mix[l]).reshape(B, S, IN_COLS_P)

        g_col = p_all[:, :, OFF_MG:OFF_MG + 2 * M_HEADS].reshape(B, nc, CHUNK, 2 * M_HEADS)
        g_row = g_col.swapaxes(-1, -2)[:, :, :, None, :]
        y_m = _mlstm(p_all, g_row, g_col, m_gate_b[l].reshape(-1, 1, 1), row(m_gate_b[l]),
                     m_conv_w[l], row(m_conv_b[l]), row(m_norm_g[l]))

        mu = _relayout_in(jnp.pad(row(r_mu[l]), ((0, 0), (4 * M_WIDTH + 2 * M_HEADS, 0))))
        w_up = jnp.pad(r_w_up[l], ((0, R_AAA_LORA), (0, 0))).astype(BF16)
        a_up = jnp.pad(r_a_up[l], ((R_DECAY_LORA, 0), (0, 0))).astype(BF16)
        g_up = jnp.pad(r_g_up[l], ((0, GLORA_PAD - R_GATE_LORA), (0, 0))).astype(BF16)
        y_r = _rwkv(p_all, mu, row(r_w0[l]), w_up, row(r_a0[l]), a_up, g_up,
                    row(r_kk[l]), row(r_ka[l]), row(r_rk[l]), row(r_gn_g[l]), row(r_gn_b[l]))

        y = jnp.concatenate([y_m, y_r], axis=-1).reshape(T, D)
        xf = _mm(y, w_out[l].astype(BF16), resid=xf)

        q = _mm(xf, x_wq[l].astype(BF16), gain=norm_x[l], out_dtype=BF16)
        kv = _mm(memf, x_wkv[l].astype(BF16), gain=norm_mem[l], out_dtype=BF16)
        o = _xattn(q.reshape(B, S, D), kv.reshape(B, M, 2 * D))
        xf = _mm(o.reshape(T, D), x_wo[l].astype(BF16), resid=xf)

        u = _mm(xf, f_up[l].astype(BF16), gain=norm_ffn[l], out_dtype=BF16)
        act = _glu(u.reshape(B, S, 2 * D_FF), f_conv_w[l], f_conv_b[l])
        xf = _mm(act.reshape(T, D_FF), f_down[l].astype(BF16), resid=xf)

    return _rmsnorm(xf, norm_final).reshape(B, S, D)
```

```python
import functools
import math

import jax
import jax.numpy as jnp
from jax import lax
from jax.experimental import pallas as pl
from jax.experimental.pallas import tpu as pltpu

F32 = jnp.float32
BF16 = jnp.bfloat16

D_MODEL = 1024
M_WIDTH = 512
M_HEADS = 4
M_HDIM = 128
M_CONV = 4
R_WIDTH = 512
R_HDIM = 64
R_HEADS = 8
R_DECAY_LORA = 64
R_AAA_LORA = 64
R_GATE_LORA = 160
DECAY_SCALE = math.exp(-0.5)
X_HEADS = 4
X_HDIM = 256
D_FF = 2816
FFN_CONV = 3
NORM_EPS = 1e-6
GN_EPS = 64e-5
CHUNK = 64

LANES = 128
SUBLANES = 8
GATE_PAD = LANES
GLORA_PAD = 2 * LANES
OFF_MQ, OFF_MK, OFF_MV, OFF_MO = 0, 512, 1024, 1536
OFF_RR = 2048
OFF_RK = OFF_RR + R_WIDTH
OFF_RV = OFF_RK + R_WIDTH
OFF_RWA = OFF_RV + R_WIDTH
OFF_MG = OFF_RWA + LANES
OFF_RG = OFF_MG + GATE_PAD
IN_COLS_P = OFF_RG + GLORA_PAD

SEQ_TILE = 256
VMEM_LIMIT = 48 * 1024 * 1024


def _dot(a, b):
    return jnp.dot(a, b, preferred_element_type=F32)


def _bdot(a, b):
    return jnp.dot(a.astype(BF16), b.astype(BF16), preferred_element_type=F32)


def _bdot_nt(a, b):
    return lax.dot_general(a.astype(BF16), b.astype(BF16), (((1,), (1,)), ((), ())),
                           preferred_element_type=F32)


def _bdot_tn(a, b):
    return lax.dot_general(a.astype(BF16), b.astype(BF16), (((0,), (0,)), ((), ())),
                           preferred_element_type=F32)


def _split2(x):
    hi = x.astype(BF16)
    lo = (x - hi.astype(F32)).astype(BF16)
    return hi, lo


def _sigmoid(x):
    return 1.0 / (1.0 + jnp.exp(-x))


def _shift_rows(x, sh):
    row = lax.broadcasted_iota(jnp.int32, x.shape, 0)
    return jnp.where(row >= sh, pltpu.roll(x, sh, 0), 0.0)


def _shift_rows_carry(x, prev, sh):
    ext = jnp.concatenate([prev, x], axis=0)
    return pltpu.roll(ext, sh, 0)[SUBLANES:]


def _mm_kernel(*refs, has_gain, has_resid, tn):
    x_ref, w_ref = refs[0], refs[1]
    pos = 2
    g_ref = r_ref = None
    if has_gain:
        g_ref = refs[pos]
        pos += 1
    if has_resid:
        r_ref = refs[pos]
        pos += 1
    o_ref, h_scr = refs[pos], refs[pos + 1]

    x = x_ref[...].astype(F32)
    if has_gain:
        ms = jnp.mean(x * x, axis=-1, keepdims=True)
        x = x * lax.rsqrt(ms + NORM_EPS) * g_ref[...]
    h_scr[...] = x.astype(BF16)
    for c in range(w_ref.shape[1] // tn):
        cols = slice(c * tn, (c + 1) * tn)
        acc = _dot(h_scr[...], w_ref[:, cols])
        if has_resid:
            acc = acc + r_ref[:, cols]
        o_ref[:, cols] = acc.astype(o_ref.dtype)


def _mm_tile_rows(K, N, x_bytes, out_bytes, has_resid):
    budget = (VMEM_LIMIT * 3) // 4 - K * N * 2
    for tm in (512, 256, 128):
        per_row = 2 * K * x_bytes + 2 * N * out_bytes + K * 2 + (2 * N * 4 if has_resid else 0)
        if tm * per_row <= budget:
            return tm
    raise ValueError("weight does not fit in VMEM")


def _mm(x, w, gain=None, resid=None, out_dtype=F32, tn=512):
    T, K = x.shape
    N = w.shape[1]
    tm = min(T, _mm_tile_rows(K, N, x.dtype.itemsize, jnp.dtype(out_dtype).itemsize,
                              resid is not None))
    tn = min(tn, N)
    assert T % tm == 0 and N % tn == 0
    in_specs = [pl.BlockSpec((tm, K), lambda i: (i, 0)),
                pl.BlockSpec((K, N), lambda i: (0, 0), pipeline_mode=pl.Buffered(1))]
    args = [x, w]
    if gain is not None:
        in_specs.append(pl.BlockSpec((1, K), lambda i: (0, 0)))
        args.append(gain.reshape(1, K))
    if resid is not None:
        in_specs.append(pl.BlockSpec((tm, N), lambda i: (i, 0)))
        args.append(resid)
    return pl.pallas_call(
        functools.partial(_mm_kernel, has_gain=gain is not None, has_resid=resid is not None,
                          tn=tn),
        out_shape=jax.ShapeDtypeStruct((T, N), out_dtype),
        grid=(T // tm,),
        in_specs=in_specs,
        out_specs=pl.BlockSpec((tm, N), lambda i: (i, 0)),
        scratch_shapes=[pltpu.VMEM((tm, K), BF16)],
        compiler_params=pltpu.CompilerParams(
            dimension_semantics=("parallel",), vmem_limit_bytes=VMEM_LIMIT),
        name="mm",
    )(*args)


def _rms_kernel(x_ref, g_ref, o_ref):
    x = x_ref[...]
    ms = jnp.mean(x * x, axis=-1, keepdims=True)
    o_ref[...] = x * lax.rsqrt(ms + NORM_EPS) * g_ref[...]


def _rmsnorm(x, gain, tm=1024):
    T, K = x.shape
    tm = min(tm, T)
    return pl.pallas_call(
        _rms_kernel,
        out_shape=jax.ShapeDtypeStruct((T, K), F32),
        grid=(T // tm,),
        in_specs=[pl.BlockSpec((tm, K), lambda i: (i, 0)),
                  pl.BlockSpec((1, K), lambda i: (0, 0))],
        out_specs=pl.BlockSpec((tm, K), lambda i: (i, 0)),
        compiler_params=pltpu.CompilerParams(
            dimension_semantics=("parallel",), vmem_limit_bytes=VMEM_LIMIT),
        name="rmsnorm",
    )(x, gain.reshape(1, K))


def _mlstm_kernel(q_ref, k_ref, v_ref, o_ref, grow_ref, gcol_ref, brow_ref, bcol_ref,
                  cw_ref, cb_ref, ng_ref, out_ref,
                  qp_scr, kp_scr, c_scr, n_scr, m_scr):
    TS = q_ref.shape[1]
    L = CHUNK
    nch = TS // L

    @pl.when(pl.program_id(1) == 0)
    def _():
        qp_scr[...] = jnp.zeros_like(qp_scr)
        kp_scr[...] = jnp.zeros_like(kp_scr)
        c_scr[...] = jnp.zeros_like(c_scr)
        n_scr[...] = jnp.zeros_like(n_scr)
        m_scr[...] = jnp.zeros_like(m_scr)

    def conv_silu(x, prev, w, b):
        y = b + x * w[M_CONV - 1:M_CONV, :]
        for j in range(M_CONV - 1):
            y = y + _shift_rows_carry(x, prev, M_CONV - 1 - j) * w[j:j + 1, :]
        return y * _sigmoid(y)

    q_raw = q_ref[0]
    k_raw = k_ref[0]
    qc_all = conv_silu(q_raw, qp_scr[...], cw_ref[:, :M_WIDTH], cb_ref[:, :M_WIDTH]) * (M_HDIM ** -0.5)
    kc_all = conv_silu(k_raw, kp_scr[...], cw_ref[:, M_WIDTH:], cb_ref[:, M_WIDTH:])
    qp_scr[...] = q_raw[TS - SUBLANES:]
    kp_scr[...] = k_raw[TS - SUBLANES:]
    v_all = v_ref[0]
    o_all = o_ref[0]

    ti = lax.broadcasted_iota(jnp.int32, (L, L), 0)
    si = lax.broadcasted_iota(jnp.int32, (L, L), 1)
    causal = si <= ti
    upper = ti <= si

    def log_sigmoid(x):
        return jnp.minimum(x, 0.0) - jnp.log1p(jnp.exp(-jnp.abs(x)))

    hs = range(M_HEADS)
    diag = ti == si

    def heads(x, rs):
        return jnp.stack([x[rs, h * M_HDIM:(h + 1) * M_HDIM] for h in hs])

    def per_head(f, *xs):
        return jnp.stack([f(*[x[h] for x in xs]) for h in hs])

    ng = jnp.stack([ng_ref[:, h * M_HDIM:(h + 1) * M_HDIM] for h in hs])
    c_prev = c_scr[...]
    n_prev = n_scr[...]
    m_prev = m_scr[...]
    for c in range(nch):
        rs = slice(c * L, (c + 1) * L)
        q = heads(qc_all, rs)
        k = heads(kc_all, rs)
        v = heads(v_all, rs)
        gr = grow_ref[0, c] + brow_ref[...]
        gc = gcol_ref[0, c] + bcol_ref[...]
        logi_r = gr[:M_HEADS]
        logf_r = log_sigmoid(gr[M_HEADS:])
        logi_c = jnp.stack([gc[:, h:h + 1] for h in hs])
        b_c = jnp.sum(jnp.where(causal, logf_r, 0.0), axis=2, keepdims=True)
        b_r = jnp.sum(jnp.where(diag, b_c, 0.0), axis=1, keepdims=True)
        g = jnp.sum(logf_r, axis=2, keepdims=True)
        a_r = g - b_r + logi_r
        a_c = g - b_c + logi_c
        m_loc = jnp.max(a_r, axis=2, keepdims=True)
        wa_c = jnp.exp(a_c - m_loc)
        c_loc = per_head(_bdot_tn, v * wa_c, k)
        n_loc = jnp.sum(k * wa_c, axis=1, keepdims=True)

        inter = b_c + m_prev
        d = jnp.where(causal, b_c - b_r + logi_r, -jnp.inf)
        m_t = jnp.maximum(inter, jnp.max(d, axis=2, keepdims=True))
        s_int = jnp.exp(inter - m_t)
        p = jnp.exp(d - m_t) * per_head(_bdot_nt, q, k)
        num = s_int * per_head(_bdot_nt, q, c_prev) + per_head(_bdot, p, v)
        den = (s_int * jnp.sum(q * n_prev, axis=2, keepdims=True)
               + jnp.sum(p, axis=2, keepdims=True))
        hh = num / jnp.maximum(jnp.abs(den), jnp.exp(-m_t))
        mu = jnp.mean(hh, axis=-1, keepdims=True)
        hc = hh - mu
        var = jnp.mean(hc * hc, axis=-1, keepdims=True)
        y = _sigmoid(heads(o_all, rs)) * (hc * lax.rsqrt(var + NORM_EPS)) * ng
        for h in hs:
            out_ref[0, rs, h * M_HDIM:(h + 1) * M_HDIM] = y[h].astype(out_ref.dtype)

        m_new = jnp.maximum(g + m_prev, m_loc)
        s_old = jnp.exp(g + m_prev - m_new)
        s_loc = jnp.exp(m_loc - m_new)
        c_prev = s_old * c_prev + s_loc * c_loc
        n_prev = s_old * n_prev + s_loc * n_loc
        m_prev = m_new
    c_scr[...] = c_prev
    n_scr[...] = n_prev
    m_scr[...] = m_prev


def _mlstm(p_all, g_row, g_col, b_row, b_col, conv_w, conv_b, norm_g):
    B, S, _ = p_all.shape
    ts = min(SEQ_TILE, S)
    nch = ts // CHUNK
    seq = lambda off: pl.BlockSpec((1, ts, M_WIDTH), lambda b, s, off=off: (b, s, off))
    full = lambda a: pl.BlockSpec(a.shape, lambda b, s: (0,) * a.ndim)
    return pl.pallas_call(
        _mlstm_kernel,
        out_shape=jax.ShapeDtypeStruct((B, S, M_WIDTH), BF16),
        grid=(B, S // ts),
        in_specs=[
            seq(OFF_MQ // M_WIDTH), seq(OFF_MK // M_WIDTH), seq(OFF_MV // M_WIDTH),
            seq(OFF_MO // M_WIDTH),
            pl.BlockSpec((1, nch, 2 * M_HEADS, 1, CHUNK), lambda b, s: (b, s, 0, 0, 0)),
            pl.BlockSpec((1, nch, CHUNK, 2 * M_HEADS), lambda b, s: (b, s, 0, 0)),
            full(b_row), full(b_col), full(conv_w), full(conv_b), full(norm_g),
        ],
        out_specs=pl.BlockSpec((1, ts, M_WIDTH), lambda b, s: (b, s, 0)),
        scratch_shapes=[pltpu.VMEM((SUBLANES, M_WIDTH), F32), pltpu.VMEM((SUBLANES, M_WIDTH), F32),
                        pltpu.VMEM((M_HEADS, M_HDIM, M_HDIM), F32),
                        pltpu.VMEM((M_HEADS, 1, M_HDIM), F32),
                        pltpu.VMEM((M_HEADS, 1, 1), F32)],
        compiler_params=pltpu.CompilerParams(
            dimension_semantics=("parallel", "arbitrary"), vmem_limit_bytes=VMEM_LIMIT),
        name="mlstm",
    )(p_all, p_all, p_all, p_all, g_row, g_col, b_row, b_col, conv_w, conv_b, norm_g)


def _inv_unit_lower_blocks(a, blk):
    n = a[0].shape[0]
    ti = lax.broadcasted_iota(jnp.int32, (n, n), 0)
    si = lax.broadcasted_iota(jnp.int32, (n, n), 1)

    def off_mask(b):
        sh = (2 * b).bit_length() - 1
        same = jnp.right_shift(ti, sh) == jnp.right_shift(si, sh)
        return same & (jnp.bitwise_and(ti, b) != 0) & (jnp.bitwise_and(si, b) == 0)

    eye = jnp.where(ti == si, 1.0, 0.0)
    m1 = off_mask(1)
    invs = [eye - jnp.where(m1, x, 0.0) for x in a]
    b = 2
    while b < blk:
        mb = off_mask(b)
        offs = [jnp.where(mb, x, 0.0).astype(BF16) for x in a]
        xs = [_bdot(i, o) for i, o in zip(invs, offs)]
        ys = [_bdot(x, i) for x, i in zip(xs, invs)]
        invs = [i - y for i, y in zip(invs, ys)]
        b *= 2
    return invs


def _rwkv_kernel(pr_ref, pk_ref, pv_ref, pwa_ref, pg_ref,
                 mur_ref, muk_ref, muv_ref, muwa_ref, mug_ref,
                 w0_ref, wup_ref, a0_ref, aup_ref, gup_ref,
                 kkp_ref, ka_ref, rk_ref, gng_ref, gnb_ref,
                 out_ref,
                 cr_scr, ck_scr, cv_scr, cwa_scr, cg_scr, st_scr):
    TS = pr_ref.shape[1]
    L = CHUNK
    N = R_HDIM
    nch = TS // L
    npair = R_WIDTH // LANES
    Q = 2 * LANES
    nquad = TS // (2 * L)

    @pl.when(pl.program_id(1) == 0)
    def _():
        for scr in (cr_scr, ck_scr, cv_scr, cwa_scr, cg_scr, st_scr):
            scr[...] = jnp.zeros_like(scr)

    def tshift(p_ref, mu_ref, c_scr):
        p = p_ref[0]
        prev = _shift_rows_carry(p, c_scr[...], 1)
        c_scr[...] = p[TS - SUBLANES:]
        return p + (prev - p) * mu_ref[...]

    hsh = N.bit_length() - 1
    li = lax.broadcasted_iota(jnp.int32, (LANES, LANES), 0)
    lj = lax.broadcasted_iota(jnp.int32, (LANES, LANES), 1)
    same_head = jnp.right_shift(li, hsh) == jnp.right_shift(lj, hsh)
    ones_bd = jnp.where(same_head, 1.0, 0.0).astype(BF16)

    def seg_sum(x):
        outs = []
        for p in range(x.shape[1] // LANES):
            hi, lo = _split2(x[:, p * LANES:(p + 1) * LANES])
            outs.append(_dot(hi, ones_bd) + _dot(lo, ones_bd))
        return jnp.concatenate(outs, axis=1)

    rr = tshift(pr_ref, mur_ref, cr_scr)
    kr = tshift(pk_ref, muk_ref, ck_scr)
    vr = tshift(pv_ref, muv_ref, cv_scr)
    wa = tshift(pwa_ref, muwa_ref, cwa_scr)
    gd = tshift(pg_ref, mug_ref, cg_scr)
    logw = -DECAY_SCALE * _sigmoid(w0_ref[...] + _bdot(jnp.tanh(wa), wup_ref[...]))
    a = _sigmoid(a0_ref[...] + _bdot(wa, aup_ref[...]))
    g = _bdot(_sigmoid(gd), gup_ref[...])
    kkraw = kr * kkp_ref[...]
    kk = kkraw / jnp.maximum(jnp.sqrt(seg_sum(kkraw * kkraw)), 1e-12)
    km = kr * (1.0 + (a - 1.0) * ka_ref[...])
    be = kk * a

    ti = lax.broadcasted_iota(jnp.int32, (Q, Q), 0)
    si = lax.broadcasted_iota(jnp.int32, (Q, Q), 1)
    same_blk = jnp.right_shift(ti, hsh) == jnp.right_shift(si, hsh)
    strict = same_blk & (si < ti)
    incl = same_blk & (si <= ti)
    tril = jnp.where(incl, 1.0, 0.0).astype(BF16)
    lw_hi, lw_lo = _split2(logw)
    bincl = jnp.concatenate(
        [_dot(tril, lw_hi[q * Q:(q + 1) * Q]) + _dot(tril, lw_lo[q * Q:(q + 1) * Q])
         for q in range(TS // Q)], axis=0)
    e_in = jnp.exp(bincl)
    e_ng = jnp.exp(-bincl)
    kt = kk * jnp.exp(bincl - logw)
    rt = rr * e_in
    bh = be * e_ng
    kh = km * e_ng

    h0 = lax.broadcasted_iota(jnp.int32, (L, LANES), 1) < N

    def stack_heads(x):
        x0, x1 = x[:L], x[L:]
        return jnp.concatenate([jnp.where(h0, x0, 0.0), jnp.where(h0, 0.0, x0),
                                jnp.where(h0, x1, 0.0), jnp.where(h0, 0.0, x1)], axis=0)

    def stack_dup(x):
        return jnp.concatenate([x[:L], x[:L], x[L:], x[L:]], axis=0)

    def comb(x):
        return jnp.concatenate([jnp.where(h0, x[0:L], x[L:2 * L]),
                                jnp.where(h0, x[2 * L:3 * L], x[3 * L:4 * L])], axis=0)

    w_ch = [[None] * nch for _ in range(npair)]
    u0_ch = [[None] * nch for _ in range(npair)]
    arkv_ch = [[None] * nch for _ in range(npair)]
    arb_ch = [[None] * nch for _ in range(npair)]
    probs = [(p, qd) for p in range(npair) for qd in range(nquad)]

    def tile(x, p, qd):
        return x[qd * 2 * L:(qd + 1) * 2 * L, p * LANES:(p + 1) * LANES]

    lk = [stack_heads(tile(kt, p, qd)).astype(BF16) for p, qd in probs]
    lr = [stack_heads(tile(rt, p, qd)).astype(BF16) for p, qd in probs]
    rb = [stack_dup(tile(bh, p, qd)).astype(BF16) for p, qd in probs]
    rk = [stack_dup(tile(kh, p, qd)).astype(BF16) for p, qd in probs]
    vs = [stack_dup(tile(vr, p, qd)).astype(BF16) for p, qd in probs]
    a_bd = [jnp.where(strict, _bdot_nt(x, y), 0.0) for x, y in zip(lk, rb)]
    bk_bd = [jnp.where(strict, _bdot_nt(x, y), 0.0).astype(BF16) for x, y in zip(lk, rk)]
    arb_bd = [jnp.where(incl, _bdot_nt(x, y), 0.0).astype(BF16) for x, y in zip(lr, rb)]
    ark_bd = [jnp.where(incl, _bdot_nt(x, y), 0.0).astype(BF16) for x, y in zip(lr, rk)]
    t_bd = _inv_unit_lower_blocks(a_bd, L)
    bkv = [comb(_bdot(x, y)) for x, y in zip(bk_bd, vs)]
    xs = [jnp.concatenate([stack_dup(tile(kt, p, qd)), stack_dup(z)], axis=1)
          for (p, qd), z in zip(probs, bkv)]
    tx = [_bdot(x, y) for x, y in zip(t_bd, xs)]
    arkv = [comb(_bdot(x, y)) for x, y in zip(ark_bd, vs)]
    for i, (p, qd) in enumerate(probs):
        w_q = comb(tx[i][:, :LANES])
        u0_q = -comb(tx[i][:, LANES:])
        for j in range(2):
            c = qd * 2 + j
            w_ch[p][c] = w_q[j * L:(j + 1) * L]
            u0_ch[p][c] = u0_q[j * L:(j + 1) * L]
            arkv_ch[p][c] = arkv[i][j * L:(j + 1) * L]
            arb_ch[p][c] = arb_bd[i][j * 2 * L:(j + 1) * 2 * L, j * 2 * L:(j + 1) * 2 * L]

    y_rows = []
    st = [st_scr[p] for p in range(npair)]
    ones_f = jnp.where(same_head, 1.0, 0.0)
    for c in range(nch):
        rs = slice(c * L, (c + 1) * L)
        p_end = e_in[c * L + L - 1:c * L + L, :]
        pairs = range(npair)
        lsl = [slice(p * LANES, (p + 1) * LANES) for p in pairs]
        pe = [p_end[:, ls] for ls in lsl]
        rw = [_bdot_nt(jnp.concatenate([rt[rs, lsl[p]], w_ch[p][c]], axis=0), st[p])
              for p in pairs]
        u = [u0_ch[p][c] - rw[p][L:] for p in pairs]
        au = [_bdot(arb_ch[p][c], jnp.concatenate([u[p], u[p]], axis=0)) for p in pairs]
        upd = [_bdot_tn(jnp.concatenate([u[p], vr[rs, lsl[p]]], axis=0),
                        jnp.concatenate([bh[rs, lsl[p]] * pe[p], kh[rs, lsl[p]] * pe[p]], axis=0))
               for p in pairs]
        st = [st[p] * pe[p] + upd[p] * ones_f for p in pairs]
        y_rows.append(jnp.concatenate(
            [rw[p][:L] + jnp.where(h0, au[p][:L], au[p][L:]) + arkv_ch[p][c] for p in pairs],
            axis=1))
    for p in range(npair):
        st_scr[p] = st[p]
    y = jnp.concatenate(y_rows, axis=0)

    inv_n = 1.0 / N
    mu = seg_sum(y) * inv_n
    yc = y - mu
    var = seg_sum(yc * yc) * inv_n
    yn = yc * lax.rsqrt(var + GN_EPS) * gng_ref[...] + gnb_ref[...]
    bonus = seg_sum(rr * km * rk_ref[...]) * vr
    out_ref[0] = ((yn + bonus) * g).astype(out_ref.dtype)


def _rwkv(p_all, mu, w0, w_up, a0, a_up, g_up, kkp, ka, rk, gn_g, gn_b):
    B, S, _ = p_all.shape
    ts = min(SEQ_TILE, S)
    npair = R_WIDTH // LANES
    seq = lambda w, off: pl.BlockSpec((1, ts, w), lambda b, s, off=off: (b, s, off))
    vec = lambda w, off: pl.BlockSpec((1, w), lambda b, s, off=off: (0, off))
    full = lambda a: pl.BlockSpec(a.shape, lambda b, s: (0,) * a.ndim)
    return pl.pallas_call(
        _rwkv_kernel,
        out_shape=jax.ShapeDtypeStruct((B, S, R_WIDTH), BF16),
        grid=(B, S // ts),
        in_specs=[
            seq(R_WIDTH, OFF_RR // R_WIDTH), seq(R_WIDTH, OFF_RK // R_WIDTH),
            seq(R_WIDTH, OFF_RV // R_WIDTH), seq(LANES, OFF_RWA // LANES),
            seq(GLORA_PAD, OFF_RG // GLORA_PAD),
            vec(R_WIDTH, OFF_RR // R_WIDTH), vec(R_WIDTH, OFF_RK // R_WIDTH),
            vec(R_WIDTH, OFF_RV // R_WIDTH), vec(LANES, OFF_RWA // LANES),
            vec(GLORA_PAD, OFF_RG // GLORA_PAD),
            full(w0), full(w_up), full(a0), full(a_up), full(g_up),
            full(kkp), full(ka), full(rk), full(gn_g), full(gn_b),
        ],
        out_specs=pl.BlockSpec((1, ts, R_WIDTH), lambda b, s: (b, s, 0)),
        scratch_shapes=[pltpu.VMEM((SUBLANES, R_WIDTH), F32), pltpu.VMEM((SUBLANES, R_WIDTH), F32),
                        pltpu.VMEM((SUBLANES, R_WIDTH), F32), pltpu.VMEM((SUBLANES, LANES), F32),
                        pltpu.VMEM((SUBLANES, GLORA_PAD), F32),
                        pltpu.VMEM((npair, LANES, LANES), F32)],
        compiler_params=pltpu.CompilerParams(
            dimension_semantics=("parallel", "arbitrary"), vmem_limit_bytes=VMEM_LIMIT),
        name="rwkv",
    )(p_all, p_all, p_all, p_all, p_all, mu, mu, mu, mu, mu,
      w0, w_up, a0, a_up, g_up, kkp, ka, rk, gn_g, gn_b)


def _xattn_kernel(x_ref, g_ref, wq_ref, kv_ref, wo_ref, o_ref):
    D = x_ref.shape[1]
    x = x_ref[...]
    ms = jnp.mean(x * x, axis=-1, keepdims=True)
    h = (x * lax.rsqrt(ms + NORM_EPS) * g_ref[...]).astype(BF16)
    q = _dot(h, wq_ref[...]).astype(BF16)
    hsl = [slice(hd * X_HDIM, (hd + 1) * X_HDIM) for hd in range(D // X_HDIM)]
    s = [lax.dot_general(q[:, ls], kv_ref[0, :, ls], (((1,), (1,)), ((), ())),
                         preferred_element_type=F32) * (X_HDIM ** -0.5) for ls in hsl]
    e = [jnp.exp(si - jnp.max(si, axis=-1, keepdims=True)) for si in s]
    p = [(ei / jnp.sum(ei, axis=-1, keepdims=True)).astype(BF16) for ei in e]
    heads = [_dot(pi, kv_ref[0, :, D + ls.start:D + ls.stop]).astype(BF16)
             for pi, ls in zip(p, hsl)]
    o_ref[...] = x + _dot(jnp.concatenate(heads, axis=1), wo_ref[...])


def _xattn(x, gain, wq, kv, wo, seq_len, tm=512):
    T, D = x.shape
    M = kv.shape[1]
    tm = min(tm, seq_len)
    per_seq = seq_len // tm
    const = lambda a: pl.BlockSpec(a.shape, lambda i: (0,) * a.ndim, pipeline_mode=pl.Buffered(1))
    return pl.pallas_call(
        _xattn_kernel,
        out_shape=jax.ShapeDtypeStruct((T, D), F32),
        grid=(T // tm,),
        in_specs=[pl.BlockSpec((tm, D), lambda i: (i, 0)),
                  pl.BlockSpec((1, D), lambda i: (0, 0)),
                  const(wq),
                  pl.BlockSpec((1, M, 2 * D), lambda i: (i // per_seq, 0, 0)),
                  const(wo)],
        out_specs=pl.BlockSpec((tm, D), lambda i: (i, 0)),
        compiler_params=pltpu.CompilerParams(
            dimension_semantics=("parallel",), vmem_limit_bytes=VMEM_LIMIT),
        name="xattn",
    )(x, gain.reshape(1, D), wq, kv, wo)


def _ffn_kernel(x_ref, g_ref, wup_ref, cw_ref, cb_ref, wdn_ref, o_ref, tail_scr, *, per_seq, tc):
    tm = x_ref.shape[0]

    @pl.when(lax.rem(pl.program_id(0), per_seq) == 0)
    def _():
        tail_scr[...] = jnp.zeros_like(tail_scr)

    x = x_ref[...]
    ms = jnp.mean(x * x, axis=-1, keepdims=True)
    h = (x * lax.rsqrt(ms + NORM_EPS) * g_ref[...]).astype(BF16)
    chunks = [slice(c * tc, (c + 1) * tc) for c in range(D_FF // tc)]
    gates = [_dot(h, wup_ref[:, cols]) for cols in chunks]
    vals = [_dot(h, wup_ref[:, D_FF + cols.start:D_FF + cols.stop]) for cols in chunks]
    acts = []
    for cols, gate, val in zip(chunks, gates, vals):
        prev = tail_scr[:, cols]
        y = cb_ref[:, cols] + gate * cw_ref[FFN_CONV - 1:FFN_CONV, cols]
        for j in range(FFN_CONV - 1):
            y = y + _shift_rows_carry(gate, prev, FFN_CONV - 1 - j) * cw_ref[j:j + 1, cols]
        tail_scr[:, cols] = gate[tm - SUBLANES:]
        acts.append((y * _sigmoid(y) * val).astype(BF16))
    acc = x
    for cols, act in zip(chunks, acts):
        acc = acc + _dot(act, wdn_ref[cols, :])
    o_ref[...] = acc


def _ffn(x, gain, w_up, conv_w, conv_b, w_down, seq_len, tm=256, tc=256):
    T, D = x.shape
    tm = min(tm, seq_len)
    conv_b = conv_b.reshape(1, D_FF)
    const = lambda a: pl.BlockSpec(a.shape, lambda i: (0,) * a.ndim, pipeline_mode=pl.Buffered(1))
    return pl.pallas_call(
        functools.partial(_ffn_kernel, per_seq=seq_len // tm, tc=tc),
        out_shape=jax.ShapeDtypeStruct((T, D), F32),
        grid=(T // tm,),
        in_specs=[pl.BlockSpec((tm, D), lambda i: (i, 0)),
                  pl.BlockSpec((1, D), lambda i: (0, 0)),
                  const(w_up), const(conv_w), const(conv_b), const(w_down)],
        out_specs=pl.BlockSpec((tm, D), lambda i: (i, 0)),
        scratch_shapes=[pltpu.VMEM((SUBLANES, D_FF), F32)],
        compiler_params=pltpu.CompilerParams(
            dimension_semantics=("arbitrary",), vmem_limit_bytes=VMEM_LIMIT),
        name="ffn",
    )(x, gain.reshape(1, D), w_up, conv_w, conv_b, w_down)


def _pad_cols(w, n):
    return jnp.pad(w, ((0, 0), (0, n - w.shape[1])))


def _relayout_in(w):
    m_main = w[:, :4 * M_WIDTH]
    m_gate = w[:, 4 * M_WIDTH:4 * M_WIDTH + 2 * M_HEADS]
    r0 = 4 * M_WIDTH + 2 * M_HEADS
    r_main = w[:, r0:r0 + 3 * R_WIDTH + R_DECAY_LORA + R_AAA_LORA]
    r_gate = w[:, r0 + 3 * R_WIDTH + R_DECAY_LORA + R_AAA_LORA:]
    return jnp.concatenate(
        [m_main, r_main, _pad_cols(m_gate, GATE_PAD), _pad_cols(r_gate, GLORA_PAD)], axis=1)


def kernel(x, mem, norm_mix, w_in, m_conv_w, m_conv_b, m_gate_b, m_norm_g, r_mu, r_w0,
           r_w_up, r_a0, r_a_up, r_g_up, r_kk, r_ka, r_rk, r_gn_g, r_gn_b, w_out,
           norm_x, norm_mem, x_wq, x_wkv, x_wo, norm_ffn, f_up, f_conv_w, f_conv_b,
           f_down, norm_final):
    B, S, D = x.shape
    M = mem.shape[1]
    depth = w_in.shape[0]
    T = B * S
    nc = S // CHUNK
    xf = x.reshape(T, D)
    memf = mem.reshape(B * M, D)
    row = lambda a: a.reshape(1, -1)

    for l in range(depth):
        w_in_p = _relayout_in(w_in[l]).astype(BF16)
        p_all = _mm(xf, w_in_p, gain=norm_mix[l]).reshape(B, S, IN_COLS_P)

        g_col = p_all[:, :, OFF_MG:OFF_MG + 2 * M_HEADS].reshape(B, nc, CHUNK, 2 * M_HEADS)
        g_row = g_col.swapaxes(-1, -2)[:, :, :, None, :]
        y_m = _mlstm(p_all, g_row, g_col, m_gate_b[l].reshape(-1, 1, 1), row(m_gate_b[l]),
                     m_conv_w[l], row(m_conv_b[l]), row(m_norm_g[l]))

        mu = _relayout_in(jnp.pad(row(r_mu[l]), ((0, 0), (4 * M_WIDTH + 2 * M_HEADS, 0))))
        w_up = jnp.pad(r_w_up[l], ((0, R_AAA_LORA), (0, 0))).astype(BF16)
        a_up = jnp.pad(r_a_up[l], ((R_DECAY_LORA, 0), (0, 0))).astype(BF16)
        g_up = jnp.pad(r_g_up[l], ((0, GLORA_PAD - R_GATE_LORA), (0, 0))).astype(BF16)
        y_r = _rwkv(p_all, mu, row(r_w0[l]), w_up, row(r_a0[l]), a_up, g_up,
                    row(r_kk[l]), row(r_ka[l]), row(r_rk[l]), row(r_gn_g[l]), row(r_gn_b[l]))

        y = jnp.concatenate([y_m, y_r], axis=-1).reshape(T, D)
        xf = _mm(y, w_out[l].astype(BF16), resid=xf)

        kv = _mm(memf, x_wkv[l].astype(BF16), gain=norm_mem[l], out_dtype=BF16)
        xf = _xattn(xf, norm_x[l], x_wq[l].astype(BF16), kv.reshape(B, M, 2 * D),
                    x_wo[l].astype(BF16), S)

        xf = _ffn(xf, norm_ffn[l], f_up[l].astype(BF16), f_conv_w[l], f_conv_b[l],
                  f_down[l].astype(BF16), S)

    return _rmsnorm(xf, norm_final).reshape(B, S, D)
```

```python
import functools
import math

import jax
import jax.numpy as jnp
from jax import lax
from jax.experimental import pallas as pl
from jax.experimental.pallas import tpu as pltpu

F32 = jnp.float32
BF16 = jnp.bfloat16

D_MODEL = 1024
M_WIDTH = 512
M_HEADS = 4
M_HDIM = 128
M_CONV = 4
R_WIDTH = 512
R_HDIM = 64
R_HEADS = 8
R_DECAY_LORA = 64
R_AAA_LORA = 64
R_GATE_LORA = 160
DECAY_SCALE = math.exp(-0.5)
X_HEADS = 4
X_HDIM = 256
D_FF = 2816
FFN_CONV = 3
NORM_EPS = 1e-6
GN_EPS = 64e-5
CHUNK = 64

LANES = 128
SUBLANES = 8
GATE_PAD = LANES
GLORA_PAD = 2 * LANES
OFF_MQ, OFF_MK, OFF_MV, OFF_MO = 0, 512, 1024, 1536
OFF_RR = 2048
OFF_RK = OFF_RR + R_WIDTH
OFF_RV = OFF_RK + R_WIDTH
OFF_RWA = OFF_RV + R_WIDTH
OFF_MG = OFF_RWA + LANES
OFF_RG = OFF_MG + GATE_PAD
IN_COLS_P = OFF_RG + GLORA_PAD

SEQ_TILE = 256
VMEM_LIMIT = 48 * 1024 * 1024


def _dot(a, b):
    return jnp.dot(a, b, preferred_element_type=F32)


def _bdot(a, b):
    return jnp.dot(a.astype(BF16), b.astype(BF16), preferred_element_type=F32)


def _bdot_nt(a, b):
    return lax.dot_general(a.astype(BF16), b.astype(BF16), (((1,), (1,)), ((), ())),
                           preferred_element_type=F32)


def _bdot_tn(a, b):
    return lax.dot_general(a.astype(BF16), b.astype(BF16), (((0,), (0,)), ((), ())),
                           preferred_element_type=F32)


def _split2(x):
    hi = x.astype(BF16)
    lo = (x - hi.astype(F32)).astype(BF16)
    return hi, lo


def _sigmoid(x):
    return 1.0 / (1.0 + jnp.exp(-x))


def _shift_rows(x, sh):
    row = lax.broadcasted_iota(jnp.int32, x.shape, 0)
    return jnp.where(row >= sh, pltpu.roll(x, sh, 0), 0.0)


def _shift_rows_carry(x, prev, sh):
    ext = jnp.concatenate([prev, x], axis=0)
    return pltpu.roll(ext, sh, 0)[SUBLANES:]


def _mm_kernel(*refs, has_gain, has_resid, tn):
    x_ref, w_ref = refs[0], refs[1]
    pos = 2
    g_ref = r_ref = None
    if has_gain:
        g_ref = refs[pos]
        pos += 1
    if has_resid:
        r_ref = refs[pos]
        pos += 1
    o_ref, h_scr = refs[pos], refs[pos + 1]

    x = x_ref[...].astype(F32)
    if has_gain:
        ms = jnp.mean(x * x, axis=-1, keepdims=True)
        x = x * lax.rsqrt(ms + NORM_EPS) * g_ref[...]
    h_scr[...] = x.astype(BF16)
    for c in range(w_ref.shape[1] // tn):
        cols = slice(c * tn, (c + 1) * tn)
        acc = _dot(h_scr[...], w_ref[:, cols])
        if has_resid:
            acc = acc + r_ref[:, cols]
        o_ref[:, cols] = acc.astype(o_ref.dtype)


def _mm_tile_rows(K, N, x_bytes, out_bytes, has_resid):
    budget = (VMEM_LIMIT * 3) // 4 - K * N * 2
    for tm in (512, 256, 128):
        per_row = 2 * K * x_bytes + 2 * N * out_bytes + K * 2 + (2 * N * 4 if has_resid else 0)
        if tm * per_row <= budget:
            return tm
    raise ValueError("weight does not fit in VMEM")


def _mm(x, w, gain=None, resid=None, out_dtype=F32, tn=512):
    T, K = x.shape
    N = w.shape[1]
    tm = min(T, _mm_tile_rows(K, N, x.dtype.itemsize, jnp.dtype(out_dtype).itemsize,
                              resid is not None))
    tn = min(tn, N)
    assert T % tm == 0 and N % tn == 0
    in_specs = [pl.BlockSpec((tm, K), lambda i: (i, 0)),
                pl.BlockSpec((K, N), lambda i: (0, 0), pipeline_mode=pl.Buffered(1))]
    args = [x, w]
    if gain is not None:
        in_specs.append(pl.BlockSpec((1, K), lambda i: (0, 0)))
        args.append(gain.reshape(1, K))
    if resid is not None:
        in_specs.append(pl.BlockSpec((tm, N), lambda i: (i, 0)))
        args.append(resid)
    return pl.pallas_call(
        functools.partial(_mm_kernel, has_gain=gain is not None, has_resid=resid is not None,
                          tn=tn),
        out_shape=jax.ShapeDtypeStruct((T, N), out_dtype),
        grid=(T // tm,),
        in_specs=in_specs,
        out_specs=pl.BlockSpec((tm, N), lambda i: (i, 0)),
        scratch_shapes=[pltpu.VMEM((tm, K), BF16)],
        compiler_params=pltpu.CompilerParams(
            dimension_semantics=("parallel",), vmem_limit_bytes=VMEM_LIMIT),
        name="mm",
    )(*args)


def _rms_kernel(x_ref, g_ref, o_ref):
    x = x_ref[...]
    ms = jnp.mean(x * x, axis=-1, keepdims=True)
    o_ref[...] = x * lax.rsqrt(ms + NORM_EPS) * g_ref[...]


def _rmsnorm(x, gain, tm=1024):
    T, K = x.shape
    tm = min(tm, T)
    return pl.pallas_call(
        _rms_kernel,
        out_shape=jax.ShapeDtypeStruct((T, K), F32),
        grid=(T // tm,),
        in_specs=[pl.BlockSpec((tm, K), lambda i: (i, 0)),
                  pl.BlockSpec((1, K), lambda i: (0, 0))],
        out_specs=pl.BlockSpec((tm, K), lambda i: (i, 0)),
        compiler_params=pltpu.CompilerParams(
            dimension_semantics=("parallel",), vmem_limit_bytes=VMEM_LIMIT),
        name="rmsnorm",
    )(x, gain.reshape(1, K))


def _mlstm_kernel(q_ref, k_ref, v_ref, o_ref, grow_ref, gcol_ref, brow_ref, bcol_ref,
                  cw_ref, cb_ref, ng_ref, out_ref,
                  qp_scr, kp_scr, c_scr, n_scr, m_scr):
    TS = q_ref.shape[1]
    L = CHUNK
    nch = TS // L

    @pl.when(pl.program_id(1) == 0)
    def _():
        qp_scr[...] = jnp.zeros_like(qp_scr)
        kp_scr[...] = jnp.zeros_like(kp_scr)
        c_scr[...] = jnp.zeros_like(c_scr)
        n_scr[...] = jnp.zeros_like(n_scr)
        m_scr[...] = jnp.zeros_like(m_scr)

    def conv_silu(x, prev, w, b):
        y = b + x * w[M_CONV - 1:M_CONV, :]
        for j in range(M_CONV - 1):
            y = y + _shift_rows_carry(x, prev, M_CONV - 1 - j) * w[j:j + 1, :]
        return y * _sigmoid(y)

    q_raw = q_ref[0]
    k_raw = k_ref[0]
    qc_all = conv_silu(q_raw, qp_scr[...], cw_ref[:, :M_WIDTH], cb_ref[:, :M_WIDTH]) * (M_HDIM ** -0.5)
    kc_all = conv_silu(k_raw, kp_scr[...], cw_ref[:, M_WIDTH:], cb_ref[:, M_WIDTH:])
    qp_scr[...] = q_raw[TS - SUBLANES:]
    kp_scr[...] = k_raw[TS - SUBLANES:]
    v_all = v_ref[0]
    o_all = o_ref[0]

    ti = lax.broadcasted_iota(jnp.int32, (L, L), 0)
    si = lax.broadcasted_iota(jnp.int32, (L, L), 1)
    causal = si <= ti
    upper = ti <= si

    def log_sigmoid(x):
        return jnp.minimum(x, 0.0) - jnp.log1p(jnp.exp(-jnp.abs(x)))

    hs = range(M_HEADS)
    diag = ti == si

    def heads(x, rs):
        return jnp.stack([x[rs, h * M_HDIM:(h + 1) * M_HDIM] for h in hs])

    def per_head(f, *xs):
        return jnp.stack([f(*[x[h] for x in xs]) for h in hs])

    ng = jnp.stack([ng_ref[:, h * M_HDIM:(h + 1) * M_HDIM] for h in hs])
    c_prev = c_scr[...]
    n_prev = n_scr[...]
    m_prev = m_scr[...]
    for c in range(nch):
        rs = slice(c * L, (c + 1) * L)
        q = heads(qc_all, rs)
        k = heads(kc_all, rs)
        v = heads(v_all, rs)
        gr = grow_ref[0, c] + brow_ref[...]
        gc = gcol_ref[0, c] + bcol_ref[...]
        logi_r = gr[:M_HEADS]
        logf_r = log_sigmoid(gr[M_HEADS:])
        logi_c = jnp.stack([gc[:, h:h + 1] for h in hs])
        b_c = jnp.sum(jnp.where(causal, logf_r, 0.0), axis=2, keepdims=True)
        b_r = jnp.sum(jnp.where(diag, b_c, 0.0), axis=1, keepdims=True)
        g = jnp.sum(logf_r, axis=2, keepdims=True)
        a_r = g - b_r + logi_r
        a_c = g - b_c + logi_c
        m_loc = jnp.max(a_r, axis=2, keepdims=True)
        wa_c = jnp.exp(a_c - m_loc)
        c_loc = per_head(_bdot_tn, v * wa_c, k)
        n_loc = jnp.sum(k * wa_c, axis=1, keepdims=True)

        inter = b_c + m_prev
        d = jnp.where(causal, b_c - b_r + logi_r, -jnp.inf)
        m_t = jnp.maximum(inter, jnp.max(d, axis=2, keepdims=True))
        s_int = jnp.exp(inter - m_t)
        p = jnp.exp(d - m_t) * per_head(_bdot_nt, q, k)
        num = s_int * per_head(_bdot_nt, q, c_prev) + per_head(_bdot, p, v)
        den = (s_int * jnp.sum(q * n_prev, axis=2, keepdims=True)
               + jnp.sum(p, axis=2, keepdims=True))
        hh = num / jnp.maximum(jnp.abs(den), jnp.exp(-m_t))
        mu = jnp.mean(hh, axis=-1, keepdims=True)
        hc = hh - mu
        var = jnp.mean(hc * hc, axis=-1, keepdims=True)
        y = _sigmoid(heads(o_all, rs)) * (hc * lax.rsqrt(var + NORM_EPS)) * ng
        for h in hs:
            out_ref[0, rs, h * M_HDIM:(h + 1) * M_HDIM] = y[h].astype(out_ref.dtype)

        m_new = jnp.maximum(g + m_prev, m_loc)
        s_old = jnp.exp(g + m_prev - m_new)
        s_loc = jnp.exp(m_loc - m_new)
        c_prev = s_old * c_prev + s_loc * c_loc
        n_prev = s_old * n_prev + s_loc * n_loc
        m_prev = m_new
    c_scr[...] = c_prev
    n_scr[...] = n_prev
    m_scr[...] = m_prev


def _mlstm(p_all, g_row, g_col, b_row, b_col, conv_w, conv_b, norm_g):
    B, S, _ = p_all.shape
    ts = min(SEQ_TILE, S)
    nch = ts // CHUNK
    seq = lambda off: pl.BlockSpec((1, ts, M_WIDTH), lambda b, s, off=off: (b, s, off))
    full = lambda a: pl.BlockSpec(a.shape, lambda b, s: (0,) * a.ndim)
    return pl.pallas_call(
        _mlstm_kernel,
        out_shape=jax.ShapeDtypeStruct((B, S, M_WIDTH), BF16),
        grid=(B, S // ts),
        in_specs=[
            seq(OFF_MQ // M_WIDTH), seq(OFF_MK // M_WIDTH), seq(OFF_MV // M_WIDTH),
            seq(OFF_MO // M_WIDTH),
            pl.BlockSpec((1, nch, 2 * M_HEADS, 1, CHUNK), lambda b, s: (b, s, 0, 0, 0)),
            pl.BlockSpec((1, nch, CHUNK, 2 * M_HEADS), lambda b, s: (b, s, 0, 0)),
            full(b_row), full(b_col), full(conv_w), full(conv_b), full(norm_g),
        ],
        out_specs=pl.BlockSpec((1, ts, M_WIDTH), lambda b, s: (b, s, 0)),
        scratch_shapes=[pltpu.VMEM((SUBLANES, M_WIDTH), F32), pltpu.VMEM((SUBLANES, M_WIDTH), F32),
                        pltpu.VMEM((M_HEADS, M_HDIM, M_HDIM), F32),
                        pltpu.VMEM((M_HEADS, 1, M_HDIM), F32),
                        pltpu.VMEM((M_HEADS, 1, 1), F32)],
        compiler_params=pltpu.CompilerParams(
            dimension_semantics=("parallel", "arbitrary"), vmem_limit_bytes=VMEM_LIMIT),
        name="mlstm",
    )(p_all, p_all, p_all, p_all, g_row, g_col, b_row, b_col, conv_w, conv_b, norm_g)


def _inv_unit_lower_blocks(a, blk):
    n = a[0].shape[0]
    ti = lax.broadcasted_iota(jnp.int32, (n, n), 0)
    si = lax.broadcasted_iota(jnp.int32, (n, n), 1)

    def off_mask(b):
        sh = (2 * b).bit_length() - 1
        same = jnp.right_shift(ti, sh) == jnp.right_shift(si, sh)
        return same & (jnp.bitwise_and(ti, b) != 0) & (jnp.bitwise_and(si, b) == 0)

    eye = jnp.where(ti == si, 1.0, 0.0)
    m1 = off_mask(1)
    invs = [eye - jnp.where(m1, x, 0.0) for x in a]
    b = 2
    while b < blk:
        mb = off_mask(b)
        offs = [jnp.where(mb, x, 0.0).astype(BF16) for x in a]
        xs = [_bdot(i, o) for i, o in zip(invs, offs)]
        ys = [_bdot(x, i) for x, i in zip(xs, invs)]
        invs = [i - y for i, y in zip(invs, ys)]
        b *= 2
    return invs


def _rwkv_kernel(pr_ref, pk_ref, pv_ref, pwa_ref, pg_ref,
                 mur_ref, muk_ref, muv_ref, muwa_ref, mug_ref,
                 w0_ref, wup_ref, a0_ref, aup_ref, gup_ref,
                 kkp_ref, ka_ref, rk_ref, gng_ref, gnb_ref,
                 out_ref,
                 cr_scr, ck_scr, cv_scr, cwa_scr, cg_scr, st_scr):
    TS = pr_ref.shape[1]
    L = CHUNK
    N = R_HDIM
    nch = TS // L
    npair = R_WIDTH // LANES
    Q = 2 * LANES

    @pl.when(pl.program_id(1) == 0)
    def _():
        for scr in (cr_scr, ck_scr, cv_scr, cwa_scr, cg_scr, st_scr):
            scr[...] = jnp.zeros_like(scr)

    def tshift(p_ref, mu_ref, c_scr):
        p = p_ref[0]
        prev = _shift_rows_carry(p, c_scr[...], 1)
        c_scr[...] = p[TS - SUBLANES:]
        return p + (prev - p) * mu_ref[...]

    hsh = N.bit_length() - 1
    li = lax.broadcasted_iota(jnp.int32, (LANES, LANES), 0)
    lj = lax.broadcasted_iota(jnp.int32, (LANES, LANES), 1)
    same_head = jnp.right_shift(li, hsh) == jnp.right_shift(lj, hsh)
    ones_bd = jnp.where(same_head, 1.0, 0.0).astype(BF16)

    def seg_sum(x):
        outs = []
        for p in range(x.shape[1] // LANES):
            hi, lo = _split2(x[:, p * LANES:(p + 1) * LANES])
            outs.append(_dot(hi, ones_bd) + _dot(lo, ones_bd))
        return jnp.concatenate(outs, axis=1)

    rr = tshift(pr_ref, mur_ref, cr_scr)
    kr = tshift(pk_ref, muk_ref, ck_scr)
    vr = tshift(pv_ref, muv_ref, cv_scr)
    wa = tshift(pwa_ref, muwa_ref, cwa_scr)
    gd = tshift(pg_ref, mug_ref, cg_scr)
    logw = -DECAY_SCALE * _sigmoid(w0_ref[...] + _bdot(jnp.tanh(wa), wup_ref[...]))
    a = _sigmoid(a0_ref[...] + _bdot(wa, aup_ref[...]))
    g = _bdot(_sigmoid(gd), gup_ref[...])
    kkraw = kr * kkp_ref[...]
    kk = kkraw / jnp.maximum(jnp.sqrt(seg_sum(kkraw * kkraw)), 1e-12)
    km = kr * (1.0 + (a - 1.0) * ka_ref[...])
    be = kk * a

    tq = lax.broadcasted_iota(jnp.int32, (Q, Q), 0)
    sq = lax.broadcasted_iota(jnp.int32, (Q, Q), 1)
    tril = jnp.where((jnp.right_shift(tq, hsh) == jnp.right_shift(sq, hsh)) & (sq <= tq),
                     1.0, 0.0).astype(BF16)
    lw_hi, lw_lo = _split2(logw)
    bincl = jnp.concatenate(
        [_dot(tril, lw_hi[q * Q:(q + 1) * Q]) + _dot(tril, lw_lo[q * Q:(q + 1) * Q])
         for q in range(TS // Q)], axis=0)
    e_in = jnp.exp(bincl)
    e_ng = jnp.exp(-bincl)
    kt = kk * jnp.exp(bincl - logw)
    rt = rr * e_in
    bh = be * e_ng
    kh = km * e_ng

    h0 = lax.broadcasted_iota(jnp.int32, (L, LANES), 1) < N
    ti = lax.broadcasted_iota(jnp.int32, (2 * L, 2 * L), 0)
    si = lax.broadcasted_iota(jnp.int32, (2 * L, 2 * L), 1)
    same_blk = jnp.right_shift(ti, hsh) == jnp.right_shift(si, hsh)
    strict = same_blk & (si < ti)
    incl = same_blk & (si <= ti)

    def stack_heads(x):
        return jnp.concatenate([jnp.where(h0, x, 0.0), jnp.where(h0, 0.0, x)], axis=0)

    def stack_dup(x):
        return jnp.concatenate([x, x], axis=0)

    def comb(x):
        return jnp.where(h0, x[:L], x[L:])

    w_ch = [[None] * nch for _ in range(npair)]
    u0_ch = [[None] * nch for _ in range(npair)]
    arkv_ch = [[None] * nch for _ in range(npair)]
    arb_ch = [[None] * nch for _ in range(npair)]
    probs = [(p, c) for p in range(npair) for c in range(nch)]

    def tile(x, p, c):
        return x[c * L:(c + 1) * L, p * LANES:(p + 1) * LANES]

    lk = [stack_heads(tile(kt, p, c)).astype(BF16) for p, c in probs]
    lr = [stack_heads(tile(rt, p, c)).astype(BF16) for p, c in probs]
    rb = [stack_dup(tile(bh, p, c)).astype(BF16) for p, c in probs]
    rk = [stack_dup(tile(kh, p, c)).astype(BF16) for p, c in probs]
    vs = [stack_dup(tile(vr, p, c)).astype(BF16) for p, c in probs]
    a_bd = [jnp.where(strict, _bdot_nt(x, y), 0.0) for x, y in zip(lk, rb)]
    bk_bd = [jnp.where(strict, _bdot_nt(x, y), 0.0).astype(BF16) for x, y in zip(lk, rk)]
    arb_bd = [jnp.where(incl, _bdot_nt(x, y), 0.0).astype(BF16) for x, y in zip(lr, rb)]
    ark_bd = [jnp.where(incl, _bdot_nt(x, y), 0.0).astype(BF16) for x, y in zip(lr, rk)]
    t_bd = _inv_unit_lower_blocks(a_bd, L)
    bkv = [comb(_bdot(x, y)) for x, y in zip(bk_bd, vs)]
    xs = [jnp.concatenate([stack_dup(tile(kt, p, c)), stack_dup(z)], axis=1)
          for (p, c), z in zip(probs, bkv)]
    tx = [_bdot(x, y) for x, y in zip(t_bd, xs)]
    arkv = [comb(_bdot(x, y)) for x, y in zip(ark_bd, vs)]
    for i, (p, c) in enumerate(probs):
        w_ch[p][c] = comb(tx[i][:, :LANES])
        u0_ch[p][c] = -comb(tx[i][:, LANES:])
        arkv_ch[p][c] = arkv[i]
        arb_ch[p][c] = arb_bd[i]

    y_rows = []
    st = [st_scr[p] for p in range(npair)]
    ones_f = jnp.where(same_head, 1.0, 0.0)
    for c in range(nch):
        rs = slice(c * L, (c + 1) * L)
        p_end = e_in[c * L + L - 1:c * L + L, :]
        pairs = range(npair)
        lsl = [slice(p * LANES, (p + 1) * LANES) for p in pairs]
        pe = [p_end[:, ls] for ls in lsl]
        rw = [_bdot_nt(jnp.concatenate([rt[rs, lsl[p]], w_ch[p][c]], axis=0), st[p])
              for p in pairs]
        u = [u0_ch[p][c] - rw[p][L:] for p in pairs]
        au = [_bdot(arb_ch[p][c], jnp.concatenate([u[p], u[p]], axis=0)) for p in pairs]
        upd = [_bdot_tn(jnp.concatenate([u[p], vr[rs, lsl[p]]], axis=0),
                        jnp.concatenate([bh[rs, lsl[p]] * pe[p], kh[rs, lsl[p]] * pe[p]], axis=0))
               for p in pairs]
        st = [st[p] * pe[p] + upd[p] * ones_f for p in pairs]
        y_rows.append(jnp.concatenate(
            [rw[p][:L] + jnp.where(h0, au[p][:L], au[p][L:]) + arkv_ch[p][c] for p in pairs],
            axis=1))
    for p in range(npair):
        st_scr[p] = st[p]
    y = jnp.concatenate(y_rows, axis=0)

    inv_n = 1.0 / N
    mu = seg_sum(y) * inv_n
    yc = y - mu
    var = seg_sum(yc * yc) * inv_n
    yn = yc * lax.rsqrt(var + GN_EPS) * gng_ref[...] + gnb_ref[...]
    bonus = seg_sum(rr * km * rk_ref[...]) * vr
    out_ref[0] = ((yn + bonus) * g).astype(out_ref.dtype)


def _rwkv(p_all, mu, w0, w_up, a0, a_up, g_up, kkp, ka, rk, gn_g, gn_b):
    B, S, _ = p_all.shape
    ts = min(SEQ_TILE, S)
    npair = R_WIDTH // LANES
    seq = lambda w, off: pl.BlockSpec((1, ts, w), lambda b, s, off=off: (b, s, off))
    vec = lambda w, off: pl.BlockSpec((1, w), lambda b, s, off=off: (0, off))
    full = lambda a: pl.BlockSpec(a.shape, lambda b, s: (0,) * a.ndim)
    return pl.pallas_call(
        _rwkv_kernel,
        out_shape=jax.ShapeDtypeStruct((B, S, R_WIDTH), BF16),
        grid=(B, S // ts),
        in_specs=[
            seq(R_WIDTH, OFF_RR // R_WIDTH), seq(R_WIDTH, OFF_RK // R_WIDTH),
            seq(R_WIDTH, OFF_RV // R_WIDTH), seq(LANES, OFF_RWA // LANES),
            seq(GLORA_PAD, OFF_RG // GLORA_PAD),
            vec(R_WIDTH, OFF_RR // R_WIDTH), vec(R_WIDTH, OFF_RK // R_WIDTH),
            vec(R_WIDTH, OFF_RV // R_WIDTH), vec(LANES, OFF_RWA // LANES),
            vec(GLORA_PAD, OFF_RG // GLORA_PAD),
            full(w0), full(w_up), full(a0), full(a_up), full(g_up),
            full(kkp), full(ka), full(rk), full(gn_g), full(gn_b),
        ],
        out_specs=pl.BlockSpec((1, ts, R_WIDTH), lambda b, s: (b, s, 0)),
        scratch_shapes=[pltpu.VMEM((SUBLANES, R_WIDTH), F32), pltpu.VMEM((SUBLANES, R_WIDTH), F32),
                        pltpu.VMEM((SUBLANES, R_WIDTH), F32), pltpu.VMEM((SUBLANES, LANES), F32),
                        pltpu.VMEM((SUBLANES, GLORA_PAD), F32),
                        pltpu.VMEM((npair, LANES, LANES), F32)],
        compiler_params=pltpu.CompilerParams(
            dimension_semantics=("parallel", "arbitrary"), vmem_limit_bytes=VMEM_LIMIT),
        name="rwkv",
    )(p_all, p_all, p_all, p_all, p_all, mu, mu, mu, mu, mu,
      w0, w_up, a0, a_up, g_up, kkp, ka, rk, gn_g, gn_b)


def _xattn_kernel(x_ref, g_ref, wq_ref, kv_ref, wo_ref, o_ref):
    D = x_ref.shape[1]
    x = x_ref[...]
    ms = jnp.mean(x * x, axis=-1, keepdims=True)
    h = (x * lax.rsqrt(ms + NORM_EPS) * g_ref[...]).astype(BF16)
    q = _dot(h, wq_ref[...]).astype(BF16)
    hsl = [slice(hd * X_HDIM, (hd + 1) * X_HDIM) for hd in range(D // X_HDIM)]
    s = [lax.dot_general(q[:, ls], kv_ref[0, :, ls], (((1,), (1,)), ((), ())),
                         preferred_element_type=F32) * (X_HDIM ** -0.5) for ls in hsl]
    e = [jnp.exp(si - jnp.max(si, axis=-1, keepdims=True)) for si in s]
    p = [(ei / jnp.sum(ei, axis=-1, keepdims=True)).astype(BF16) for ei in e]
    heads = [_dot(pi, kv_ref[0, :, D + ls.start:D + ls.stop]).astype(BF16)
             for pi, ls in zip(p, hsl)]
    o_ref[...] = x + _dot(jnp.concatenate(heads, axis=1), wo_ref[...])


def _xattn(x, gain, wq, kv, wo, seq_len, tm=512):
    T, D = x.shape
    M = kv.shape[1]
    tm = min(tm, seq_len)
    per_seq = seq_len // tm
    const = lambda a: pl.BlockSpec(a.shape, lambda i: (0,) * a.ndim, pipeline_mode=pl.Buffered(1))
    return pl.pallas_call(
        _xattn_kernel,
        out_shape=jax.ShapeDtypeStruct((T, D), F32),
        grid=(T // tm,),
        in_specs=[pl.BlockSpec((tm, D), lambda i: (i, 0)),
                  pl.BlockSpec((1, D), lambda i: (0, 0)),
                  const(wq),
                  pl.BlockSpec((1, M, 2 * D), lambda i: (i // per_seq, 0, 0)),
                  const(wo)],
        out_specs=pl.BlockSpec((tm, D), lambda i: (i, 0)),
        compiler_params=pltpu.CompilerParams(
            dimension_semantics=("parallel",), vmem_limit_bytes=VMEM_LIMIT),
        name="xattn",
    )(x, gain.reshape(1, D), wq, kv, wo)


def _ffn_kernel(x_ref, g_ref, wup_ref, cw_ref, cb_ref, wdn_ref, o_ref, tail_scr, *, per_seq, tc):
    tm = x_ref.shape[0]

    @pl.when(lax.rem(pl.program_id(0), per_seq) == 0)
    def _():
        tail_scr[...] = jnp.zeros_like(tail_scr)

    x = x_ref[...]
    ms = jnp.mean(x * x, axis=-1, keepdims=True)
    h = (x * lax.rsqrt(ms + NORM_EPS) * g_ref[...]).astype(BF16)
    chunks = [slice(c * tc, (c + 1) * tc) for c in range(D_FF // tc)]
    gates = [_dot(h, wup_ref[:, cols]) for cols in chunks]
    vals = [_dot(h, wup_ref[:, D_FF + cols.start:D_FF + cols.stop]) for cols in chunks]
    acts = []
    for cols, gate, val in zip(chunks, gates, vals):
        prev = tail_scr[:, cols]
        y = cb_ref[:, cols] + gate * cw_ref[FFN_CONV - 1:FFN_CONV, cols]
        for j in range(FFN_CONV - 1):
            y = y + _shift_rows_carry(gate, prev, FFN_CONV - 1 - j) * cw_ref[j:j + 1, cols]
        tail_scr[:, cols] = gate[tm - SUBLANES:]
        acts.append((y * _sigmoid(y) * val).astype(BF16))
    acc = x
    for cols, act in zip(chunks, acts):
        acc = acc + _dot(act, wdn_ref[cols, :])
    o_ref[...] = acc


def _ffn(x, gain, w_up, conv_w, conv_b, w_down, seq_len, tm=256, tc=256):
    T, D = x.shape
    tm = min(tm, seq_len)
    conv_b = conv_b.reshape(1, D_FF)
    const = lambda a: pl.BlockSpec(a.shape, lambda i: (0,) * a.ndim, pipeline_mode=pl.Buffered(1))
    return pl.pallas_call(
        functools.partial(_ffn_kernel, per_seq=seq_len // tm, tc=tc),
        out_shape=jax.ShapeDtypeStruct((T, D), F32),
        grid=(T // tm,),
        in_specs=[pl.BlockSpec((tm, D), lambda i: (i, 0)),
                  pl.BlockSpec((1, D), lambda i: (0, 0)),
                  const(w_up), const(conv_w), const(conv_b), const(w_down)],
        out_specs=pl.BlockSpec((tm, D), lambda i: (i, 0)),
        scratch_shapes=[pltpu.VMEM((SUBLANES, D_FF), F32)],
        compiler_params=pltpu.CompilerParams(
            dimension_semantics=("arbitrary",), vmem_limit_bytes=VMEM_LIMIT),
        name="ffn",
    )(x, gain.reshape(1, D), w_up, conv_w, conv_b, w_down)


def _pad_cols(w, n):
    return jnp.pad(w, ((0, 0), (0, n - w.shape[1])))


def _relayout_in(w):
    m_main = w[:, :4 * M_WIDTH]
    m_gate = w[:, 4 * M_WIDTH:4 * M_WIDTH + 2 * M_HEADS]
    r0 = 4 * M_WIDTH + 2 * M_HEADS
    r_main = w[:, r0:r0 + 3 * R_WIDTH + R_DECAY_LORA + R_AAA_LORA]
    r_gate = w[:, r0 + 3 * R_WIDTH + R_DECAY_LORA + R_AAA_LORA:]
    return jnp.concatenate(
        [m_main, r_main, _pad_cols(m_gate, GATE_PAD), _pad_cols(r_gate, GLORA_PAD)], axis=1)


def kernel(x, mem, norm_mix, w_in, m_conv_w, m_conv_b, m_gate_b, m_norm_g, r_mu, r_w0,
           r_w_up, r_a0, r_a_up, r_g_up, r_kk, r_ka, r_rk, r_gn_g, r_gn_b, w_out,
           norm_x, norm_mem, x_wq, x_wkv, x_wo, norm_ffn, f_up, f_conv_w, f_conv_b,
           f_down, norm_final):
    B, S, D = x.shape
    M = mem.shape[1]
    depth = w_in.shape[0]
    T = B * S
    nc = S // CHUNK
    xf = x.reshape(T, D)
    memf = mem.reshape(B * M, D)
    row = lambda a: a.reshape(1, -1)

    for l in range(depth):
        w_in_p = _relayout_in(w_in[l]).astype(BF16)
        p_all = _mm(xf, w_in_p, gain=norm_mix[l]).reshape(B, S, IN_COLS_P)

        g_col = p_all[:, :, OFF_MG:OFF_MG + 2 * M_HEADS].reshape(B, nc, CHUNK, 2 * M_HEADS)
        g_row = g_col.swapaxes(-1, -2)[:, :, :, None, :]
        y_m = _mlstm(p_all, g_row, g_col, m_gate_b[l].reshape(-1, 1, 1), row(m_gate_b[l]),
                     m_conv_w[l], row(m_conv_b[l]), row(m_norm_g[l]))

        mu = _relayout_in(jnp.pad(row(r_mu[l]), ((0, 0), (4 * M_WIDTH + 2 * M_HEADS, 0))))
        w_up = jnp.pad(r_w_up[l], ((0, R_AAA_LORA), (0, 0))).astype(BF16)
        a_up = jnp.pad(r_a_up[l], ((R_DECAY_LORA, 0), (0, 0))).astype(BF16)
        g_up = jnp.pad(r_g_up[l], ((0, GLORA_PAD - R_GATE_LORA), (0, 0))).astype(BF16)
        y_r = _rwkv(p_all, mu, row(r_w0[l]), w_up, row(r_a0[l]), a_up, g_up,
                    row(r_kk[l]), row(r_ka[l]), row(r_rk[l]), row(r_gn_g[l]), row(r_gn_b[l]))

        y = jnp.concatenate([y_m, y_r], axis=-1).reshape(T, D)
        xf = _mm(y, w_out[l].astype(BF16), resid=xf)

        kv = _mm(memf, x_wkv[l].astype(BF16), gain=norm_mem[l], out_dtype=BF16)
        xf = _xattn(xf, norm_x[l], x_wq[l].astype(BF16), kv.reshape(B, M, 2 * D),
                    x_wo[l].astype(BF16), S)

        xf = _ffn(xf, norm_ffn[l], f_up[l].astype(BF16), f_conv_w[l], f_conv_b[l],
                  f_down[l].astype(BF16), S)

    return _rmsnorm(xf, norm_final).reshape(B, S, D)
```

```python
import functools
import math

import jax
import jax.numpy as jnp
from jax import lax
from jax.experimental import pallas as pl
from jax.experimental.pallas import tpu as pltpu

F32 = jnp.float32
BF16 = jnp.bfloat16

D_MODEL = 1024
M_WIDTH = 512
M_HEADS = 4
M_HDIM = 128
M_CONV = 4
R_WIDTH = 512
R_HDIM = 64
R_HEADS = 8
R_DECAY_LORA = 64
R_AAA_LORA = 64
R_GATE_LORA = 160
DECAY_SCALE = math.exp(-0.5)
X_HEADS = 4
X_HDIM = 256
D_FF = 2816
FFN_CONV = 3
NORM_EPS = 1e-6
GN_EPS = 64e-5
CHUNK = 64

LANES = 128
SUBLANES = 8
GATE_PAD = LANES
GLORA_PAD = 2 * LANES
OFF_MQ, OFF_MK, OFF_MV, OFF_MO = 0, 512, 1024, 1536
OFF_RR = 2048
OFF_RK = OFF_RR + R_WIDTH
OFF_RV = OFF_RK + R_WIDTH
OFF_RWA = OFF_RV + R_WIDTH
OFF_MG = OFF_RWA + LANES
OFF_RG = OFF_MG + GATE_PAD
IN_COLS_P = OFF_RG + GLORA_PAD

SEQ_TILE = 256
VMEM_LIMIT = 48 * 1024 * 1024


def _dot(a, b):
    return jnp.dot(a, b, preferred_element_type=F32)


def _bdot(a, b):
    return jnp.dot(a.astype(BF16), b.astype(BF16), preferred_element_type=F32)


def _bdot_nt(a, b):
    return lax.dot_general(a.astype(BF16), b.astype(BF16), (((1,), (1,)), ((), ())),
                           preferred_element_type=F32)


def _bdot_tn(a, b):
    return lax.dot_general(a.astype(BF16), b.astype(BF16), (((0,), (0,)), ((), ())),
                           preferred_element_type=F32)


def _split2(x):
    hi = x.astype(BF16)
    lo = (x - hi.astype(F32)).astype(BF16)
    return hi, lo


def _sigmoid(x):
    return 0.5 * jnp.tanh(0.5 * x) + 0.5


def _shift_rows(x, sh):
    row = lax.broadcasted_iota(jnp.int32, x.shape, 0)
    return jnp.where(row >= sh, pltpu.roll(x, sh, 0), 0.0)


def _shift_rows_carry(x, prev, sh):
    ext = jnp.concatenate([prev, x], axis=0)
    return pltpu.roll(ext, sh, 0)[SUBLANES:]


def _mm_kernel(*refs, has_gain, has_resid, tn):
    x_ref, w_ref = refs[0], refs[1]
    pos = 2
    g_ref = r_ref = None
    if has_gain:
        g_ref = refs[pos]
        pos += 1
    if has_resid:
        r_ref = refs[pos]
        pos += 1
    o_ref, h_scr = refs[pos], refs[pos + 1]

    x = x_ref[...].astype(F32)
    if has_gain:
        ms = jnp.mean(x * x, axis=-1, keepdims=True)
        x = x * lax.rsqrt(ms + NORM_EPS) * g_ref[...]
    h_scr[...] = x.astype(BF16)
    for c in range(w_ref.shape[1] // tn):
        cols = slice(c * tn, (c + 1) * tn)
        acc = _dot(h_scr[...], w_ref[:, cols])
        if has_resid:
            acc = acc + r_ref[:, cols]
        o_ref[:, cols] = acc.astype(o_ref.dtype)


def _mm_tile_rows(K, N, x_bytes, out_bytes, has_resid):
    budget = (VMEM_LIMIT * 3) // 4 - K * N * 2
    for tm in (512, 256, 128):
        per_row = 2 * K * x_bytes + 2 * N * out_bytes + K * 2 + (2 * N * 4 if has_resid else 0)
        if tm * per_row <= budget:
            return tm
    raise ValueError("weight does not fit in VMEM")


def _mm(x, w, gain=None, resid=None, out_dtype=F32, tn=512):
    T, K = x.shape
    N = w.shape[1]
    tm = min(T, _mm_tile_rows(K, N, x.dtype.itemsize, jnp.dtype(out_dtype).itemsize,
                              resid is not None))
    tn = min(tn, N)
    assert T % tm == 0 and N % tn == 0
    in_specs = [pl.BlockSpec((tm, K), lambda i: (i, 0)),
                pl.BlockSpec((K, N), lambda i: (0, 0), pipeline_mode=pl.Buffered(1))]
    args = [x, w]
    if gain is not None:
        in_specs.append(pl.BlockSpec((1, K), lambda i: (0, 0)))
        args.append(gain.reshape(1, K))
    if resid is not None:
        in_specs.append(pl.BlockSpec((tm, N), lambda i: (i, 0)))
        args.append(resid)
    return pl.pallas_call(
        functools.partial(_mm_kernel, has_gain=gain is not None, has_resid=resid is not None,
                          tn=tn),
        out_shape=jax.ShapeDtypeStruct((T, N), out_dtype),
        grid=(T // tm,),
        in_specs=in_specs,
        out_specs=pl.BlockSpec((tm, N), lambda i: (i, 0)),
        scratch_shapes=[pltpu.VMEM((tm, K), BF16)],
        compiler_params=pltpu.CompilerParams(
            dimension_semantics=("parallel",), vmem_limit_bytes=VMEM_LIMIT),
        name="mm",
    )(*args)


def _rms_kernel(x_ref, g_ref, o_ref):
    x = x_ref[...]
    ms = jnp.mean(x * x, axis=-1, keepdims=True)
    o_ref[...] = x * lax.rsqrt(ms + NORM_EPS) * g_ref[...]


def _rmsnorm(x, gain, tm=1024):
    T, K = x.shape
    tm = min(tm, T)
    return pl.pallas_call(
        _rms_kernel,
        out_shape=jax.ShapeDtypeStruct((T, K), F32),
        grid=(T // tm,),
        in_specs=[pl.BlockSpec((tm, K), lambda i: (i, 0)),
                  pl.BlockSpec((1, K), lambda i: (0, 0))],
        out_specs=pl.BlockSpec((tm, K), lambda i: (i, 0)),
        compiler_params=pltpu.CompilerParams(
            dimension_semantics=("parallel",), vmem_limit_bytes=VMEM_LIMIT),
        name="rmsnorm",
    )(x, gain.reshape(1, K))


def _mlstm_kernel(q_ref, k_ref, v_ref, o_ref, grow_ref, gcol_ref, brow_ref, bcol_ref,
                  cw_ref, cb_ref, ng_ref, out_ref,
                  qp_scr, kp_scr, c_scr, n_scr, m_scr):
    TS = q_ref.shape[1]
    L = CHUNK
    nch = TS // L

    @pl.when(pl.program_id(1) == 0)
    def _():
        qp_scr[...] = jnp.zeros_like(qp_scr)
        kp_scr[...] = jnp.zeros_like(kp_scr)
        c_scr[...] = jnp.zeros_like(c_scr)
        n_scr[...] = jnp.zeros_like(n_scr)
        m_scr[...] = jnp.zeros_like(m_scr)

    def conv_silu(x, prev, w, b):
        y = b + x * w[M_CONV - 1:M_CONV, :]
        for j in range(M_CONV - 1):
            y = y + _shift_rows_carry(x, prev, M_CONV - 1 - j) * w[j:j + 1, :]
        return y * _sigmoid(y)

    q_raw = q_ref[0]
    k_raw = k_ref[0]
    qc_all = conv_silu(q_raw, qp_scr[...], cw_ref[:, :M_WIDTH], cb_ref[:, :M_WIDTH]) * (M_HDIM ** -0.5)
    kc_all = conv_silu(k_raw, kp_scr[...], cw_ref[:, M_WIDTH:], cb_ref[:, M_WIDTH:])
    qp_scr[...] = q_raw[TS - SUBLANES:]
    kp_scr[...] = k_raw[TS - SUBLANES:]
    v_all = v_ref[0]
    o_all = o_ref[0]

    ti = lax.broadcasted_iota(jnp.int32, (L, L), 0)
    si = lax.broadcasted_iota(jnp.int32, (L, L), 1)
    causal = si <= ti
    upper = ti <= si

    def log_sigmoid(x):
        return jnp.minimum(x, 0.0) - jnp.log1p(jnp.exp(-jnp.abs(x)))

    hs = range(M_HEADS)
    diag = ti == si

    H = M_HEADS
    G = nch * H

    def groups(x):
        return jnp.stack([x[c * L:(c + 1) * L, h * M_HDIM:(h + 1) * M_HDIM]
                          for c in range(nch) for h in hs])

    def per_group(f, *xs):
        return jnp.stack([f(*[x[i] for x in xs]) for i in range(G)])

    q = groups(qc_all)
    k = groups(kc_all)
    v = groups(v_all)
    gr = grow_ref[0] + brow_ref[...]
    gc = gcol_ref[0] + bcol_ref[...]
    logi_r = gr[:, :H].reshape(G, 1, L)
    logf_r = log_sigmoid(gr[:, H:].reshape(G, 1, L))
    logi_c = jnp.stack([gc[c, :, h:h + 1] for c in range(nch) for h in hs])
    b_c = jnp.sum(jnp.where(causal, logf_r, 0.0), axis=2, keepdims=True)
    b_r = jnp.sum(jnp.where(diag, b_c, 0.0), axis=1, keepdims=True)
    g = jnp.sum(logf_r, axis=2, keepdims=True)
    a_r = g - b_r + logi_r
    a_c = g - b_c + logi_c
    m_loc = jnp.max(a_r, axis=2, keepdims=True)
    wa_c = jnp.exp(a_c - m_loc)
    c_loc = per_group(_bdot_tn, v * wa_c, k)
    n_loc = jnp.sum(k * wa_c, axis=1, keepdims=True)
    d = jnp.where(causal, b_c - b_r + logi_r, -jnp.inf)
    d_max = jnp.max(d, axis=2, keepdims=True)
    qk = per_group(_bdot_nt, q, k)

    c_prev = c_scr[...]
    n_prev = n_scr[...]
    m_prev = m_scr[...]
    c_in, n_in, m_in = [], [], []
    for c in range(nch):
        gs = slice(c * H, (c + 1) * H)
        c_in.append(c_prev)
        n_in.append(n_prev)
        m_in.append(m_prev)
        m_new = jnp.maximum(g[gs] + m_prev, m_loc[gs])
        s_old = jnp.exp(g[gs] + m_prev - m_new)
        s_loc = jnp.exp(m_loc[gs] - m_new)
        c_prev = s_old * c_prev + s_loc * c_loc[gs]
        n_prev = s_old * n_prev + s_loc * n_loc[gs]
        m_prev = m_new
    c_scr[...] = c_prev
    n_scr[...] = n_prev
    m_scr[...] = m_prev
    c_in = jnp.concatenate(c_in, axis=0)
    n_in = jnp.concatenate(n_in, axis=0)
    m_in = jnp.concatenate(m_in, axis=0)

    inter = b_c + m_in
    m_t = jnp.maximum(inter, d_max)
    s_int = jnp.exp(inter - m_t)
    p = jnp.exp(d - m_t) * qk
    num = s_int * per_group(_bdot_nt, q, c_in) + per_group(_bdot, p, v)
    den = (s_int * jnp.sum(q * n_in, axis=2, keepdims=True)
           + jnp.sum(p, axis=2, keepdims=True))
    hh = num / jnp.maximum(jnp.abs(den), jnp.exp(-m_t))
    mu = jnp.mean(hh, axis=-1, keepdims=True)
    hc = hh - mu
    var = jnp.mean(hc * hc, axis=-1, keepdims=True)
    ng = jnp.stack([ng_ref[:, h * M_HDIM:(h + 1) * M_HDIM] for h in hs] * nch)
    y = _sigmoid(groups(o_all)) * (hc * lax.rsqrt(var + NORM_EPS)) * ng
    for c in range(nch):
        for h in hs:
            out_ref[0, c * L:(c + 1) * L, h * M_HDIM:(h + 1) * M_HDIM] = (
                y[c * H + h].astype(out_ref.dtype))


def _mlstm(p_all, g_row, g_col, b_row, b_col, conv_w, conv_b, norm_g):
    B, S, _ = p_all.shape
    ts = min(SEQ_TILE, S)
    nch = ts // CHUNK
    seq = lambda off: pl.BlockSpec((1, ts, M_WIDTH), lambda b, s, off=off: (b, s, off))
    full = lambda a: pl.BlockSpec(a.shape, lambda b, s: (0,) * a.ndim)
    return pl.pallas_call(
        _mlstm_kernel,
        out_shape=jax.ShapeDtypeStruct((B, S, M_WIDTH), BF16),
        grid=(B, S // ts),
        in_specs=[
            seq(OFF_MQ // M_WIDTH), seq(OFF_MK // M_WIDTH), seq(OFF_MV // M_WIDTH),
            seq(OFF_MO // M_WIDTH),
            pl.BlockSpec((1, nch, 2 * M_HEADS, 1, CHUNK), lambda b, s: (b, s, 0, 0, 0)),
            pl.BlockSpec((1, nch, CHUNK, 2 * M_HEADS), lambda b, s: (b, s, 0, 0)),
            full(b_row), full(b_col), full(conv_w), full(conv_b), full(norm_g),
        ],
        out_specs=pl.BlockSpec((1, ts, M_WIDTH), lambda b, s: (b, s, 0)),
        scratch_shapes=[pltpu.VMEM((SUBLANES, M_WIDTH), F32), pltpu.VMEM((SUBLANES, M_WIDTH), F32),
                        pltpu.VMEM((M_HEADS, M_HDIM, M_HDIM), F32),
                        pltpu.VMEM((M_HEADS, 1, M_HDIM), F32),
                        pltpu.VMEM((M_HEADS, 1, 1), F32)],
        compiler_params=pltpu.CompilerParams(
            dimension_semantics=("parallel", "arbitrary"), vmem_limit_bytes=VMEM_LIMIT),
        name="mlstm",
    )(p_all, p_all, p_all, p_all, g_row, g_col, b_row, b_col, conv_w, conv_b, norm_g)


def _inv_unit_lower_blocks(a, blk):
    n = a[0].shape[0]
    ti = lax.broadcasted_iota(jnp.int32, (n, n), 0)
    si = lax.broadcasted_iota(jnp.int32, (n, n), 1)

    def off_mask(b):
        sh = (2 * b).bit_length() - 1
        same = jnp.right_shift(ti, sh) == jnp.right_shift(si, sh)
        return same & (jnp.bitwise_and(ti, b) != 0) & (jnp.bitwise_and(si, b) == 0)

    eye = jnp.where(ti == si, 1.0, 0.0)
    m1 = off_mask(1)
    invs = [eye - jnp.where(m1, x, 0.0) for x in a]
    b = 2
    while b < blk:
        mb = off_mask(b)
        offs = [jnp.where(mb, x, 0.0).astype(BF16) for x in a]
        xs = [_bdot(i, o) for i, o in zip(invs, offs)]
        ys = [_bdot(x, i) for x, i in zip(xs, invs)]
        invs = [i - y for i, y in zip(invs, ys)]
        b *= 2
    return invs


def _rwkv_kernel(pr_ref, pk_ref, pv_ref, pwa_ref, pg_ref,
                 mur_ref, muk_ref, muv_ref, muwa_ref, mug_ref,
                 w0_ref, wup_ref, a0_ref, aup_ref, gup_ref,
                 kkp_ref, ka_ref, rk_ref, gng_ref, gnb_ref,
                 out_ref,
                 cr_scr, ck_scr, cv_scr, cwa_scr, cg_scr, st_scr):
    TS = pr_ref.shape[1]
    L = CHUNK
    N = R_HDIM
    nch = TS // L
    npair = R_WIDTH // LANES
    Q = 2 * LANES

    @pl.when(pl.program_id(1) == 0)
    def _():
        for scr in (cr_scr, ck_scr, cv_scr, cwa_scr, cg_scr, st_scr):
            scr[...] = jnp.zeros_like(scr)

    def tshift(p_ref, mu_ref, c_scr):
        p = p_ref[0]
        prev = _shift_rows_carry(p, c_scr[...], 1)
        c_scr[...] = p[TS - SUBLANES:]
        return p + (prev - p) * mu_ref[...]

    hsh = N.bit_length() - 1
    li = lax.broadcasted_iota(jnp.int32, (LANES, LANES), 0)
    lj = lax.broadcasted_iota(jnp.int32, (LANES, LANES), 1)
    same_head = jnp.right_shift(li, hsh) == jnp.right_shift(lj, hsh)
    ones_bd = jnp.where(same_head, 1.0, 0.0).astype(BF16)

    def seg_sum(x):
        outs = []
        for p in range(x.shape[1] // LANES):
            hi, lo = _split2(x[:, p * LANES:(p + 1) * LANES])
            outs.append(_dot(hi, ones_bd) + _dot(lo, ones_bd))
        return jnp.concatenate(outs, axis=1)

    rr = tshift(pr_ref, mur_ref, cr_scr)
    kr = tshift(pk_ref, muk_ref, ck_scr)
    vr = tshift(pv_ref, muv_ref, cv_scr)
    wa = tshift(pwa_ref, muwa_ref, cwa_scr)
    gd = tshift(pg_ref, mug_ref, cg_scr)
    logw = -DECAY_SCALE * _sigmoid(w0_ref[...] + _bdot(jnp.tanh(wa), wup_ref[...]))
    a = _sigmoid(a0_ref[...] + _bdot(wa, aup_ref[...]))
    g = _bdot(_sigmoid(gd), gup_ref[...])
    kkraw = kr * kkp_ref[...]
    kk = kkraw / jnp.maximum(jnp.sqrt(seg_sum(kkraw * kkraw)), 1e-12)
    km = kr * (1.0 + (a - 1.0) * ka_ref[...])
    be = kk * a

    tq = lax.broadcasted_iota(jnp.int32, (Q, Q), 0)
    sq = lax.broadcasted_iota(jnp.int32, (Q, Q), 1)
    tril = jnp.where((jnp.right_shift(tq, hsh) == jnp.right_shift(sq, hsh)) & (sq <= tq),
                     1.0, 0.0).astype(BF16)
    lw_hi, lw_lo = _split2(logw)
    bincl = jnp.concatenate(
        [_dot(tril, lw_hi[q * Q:(q + 1) * Q]) + _dot(tril, lw_lo[q * Q:(q + 1) * Q])
         for q in range(TS // Q)], axis=0)
    e_in = jnp.exp(bincl)
    e_ng = jnp.exp(-bincl)
    kt = kk * jnp.exp(bincl - logw)
    rt = rr * e_in
    bh = be * e_ng
    kh = km * e_ng

    h0 = lax.broadcasted_iota(jnp.int32, (L, LANES), 1) < N
    ti = lax.broadcasted_iota(jnp.int32, (2 * L, 2 * L), 0)
    si = lax.broadcasted_iota(jnp.int32, (2 * L, 2 * L), 1)
    same_blk = jnp.right_shift(ti, hsh) == jnp.right_shift(si, hsh)
    strict = same_blk & (si < ti)
    incl = same_blk & (si <= ti)

    def stack_heads(x):
        return jnp.concatenate([jnp.where(h0, x, 0.0), jnp.where(h0, 0.0, x)], axis=0)

    def stack_dup(x):
        return jnp.concatenate([x, x], axis=0)

    def comb(x):
        return jnp.where(h0, x[:L], x[L:])

    w_ch = [[None] * nch for _ in range(npair)]
    u0_ch = [[None] * nch for _ in range(npair)]
    arkv_ch = [[None] * nch for _ in range(npair)]
    arb_ch = [[None] * nch for _ in range(npair)]
    probs = [(p, c) for p in range(npair) for c in range(nch)]

    def tile(x, p, c):
        return x[c * L:(c + 1) * L, p * LANES:(p + 1) * LANES]

    lk = [stack_heads(tile(kt, p, c)).astype(BF16) for p, c in probs]
    lr = [stack_heads(tile(rt, p, c)).astype(BF16) for p, c in probs]
    rb = [stack_dup(tile(bh, p, c)).astype(BF16) for p, c in probs]
    rk = [stack_dup(tile(kh, p, c)).astype(BF16) for p, c in probs]
    vs = [stack_dup(tile(vr, p, c)).astype(BF16) for p, c in probs]
    a_bd = [jnp.where(strict, _bdot_nt(x, y), 0.0) for x, y in zip(lk, rb)]
    bk_bd = [jnp.where(strict, _bdot_nt(x, y), 0.0).astype(BF16) for x, y in zip(lk, rk)]
    arb_bd = [jnp.where(incl, _bdot_nt(x, y), 0.0).astype(BF16) for x, y in zip(lr, rb)]
    ark_bd = [jnp.where(incl, _bdot_nt(x, y), 0.0).astype(BF16) for x, y in zip(lr, rk)]
    t_bd = _inv_unit_lower_blocks(a_bd, L)
    bkv = [comb(_bdot(x, y)) for x, y in zip(bk_bd, vs)]
    xs = [jnp.concatenate([stack_dup(tile(kt, p, c)), stack_dup(z)], axis=1)
          for (p, c), z in zip(probs, bkv)]
    tx = [_bdot(x, y) for x, y in zip(t_bd, xs)]
    arkv = [comb(_bdot(x, y)) for x, y in zip(ark_bd, vs)]
    for i, (p, c) in enumerate(probs):
        w_ch[p][c] = comb(tx[i][:, :LANES])
        u0_ch[p][c] = -comb(tx[i][:, LANES:])
        arkv_ch[p][c] = arkv[i]
        arb_ch[p][c] = arb_bd[i]

    y_rows = []
    st = [st_scr[p] for p in range(npair)]
    ones_f = jnp.where(same_head, 1.0, 0.0)
    for c in range(nch):
        rs = slice(c * L, (c + 1) * L)
        p_end = e_in[c * L + L - 1:c * L + L, :]
        pairs = range(npair)
        lsl = [slice(p * LANES, (p + 1) * LANES) for p in pairs]
        pe = [p_end[:, ls] for ls in lsl]
        rw = [_bdot_nt(jnp.concatenate([rt[rs, lsl[p]], w_ch[p][c]], axis=0), st[p])
              for p in pairs]
        u = [u0_ch[p][c] - rw[p][L:] for p in pairs]
        au = [_bdot(arb_ch[p][c], jnp.concatenate([u[p], u[p]], axis=0)) for p in pairs]
        upd = [_bdot_tn(jnp.concatenate([u[p], vr[rs, lsl[p]]], axis=0),
                        jnp.concatenate([bh[rs, lsl[p]] * pe[p], kh[rs, lsl[p]] * pe[p]], axis=0))
               for p in pairs]
        st = [st[p] * pe[p] + upd[p] * ones_f for p in pairs]
        y_rows.append(jnp.concatenate(
            [rw[p][:L] + jnp.where(h0, au[p][:L], au[p][L:]) + arkv_ch[p][c] for p in pairs],
            axis=1))
    for p in range(npair):
        st_scr[p] = st[p]
    y = jnp.concatenate(y_rows, axis=0)

    inv_n = 1.0 / N
    mu = seg_sum(y) * inv_n
    yc = y - mu
    var = seg_sum(yc * yc) * inv_n
    yn = yc * lax.rsqrt(var + GN_EPS) * gng_ref[...] + gnb_ref[...]
    bonus = seg_sum(rr * km * rk_ref[...]) * vr
    out_ref[0] = ((yn + bonus) * g).astype(out_ref.dtype)


def _rwkv(p_all, mu, w0, w_up, a0, a_up, g_up, kkp, ka, rk, gn_g, gn_b):
    B, S, _ = p_all.shape
    ts = min(SEQ_TILE, S)
    npair = R_WIDTH // LANES
    seq = lambda w, off: pl.BlockSpec((1, ts, w), lambda b, s, off=off: (b, s, off))
    vec = lambda w, off: pl.BlockSpec((1, w), lambda b, s, off=off: (0, off))
    full = lambda a: pl.BlockSpec(a.shape, lambda b, s: (0,) * a.ndim)
    return pl.pallas_call(
        _rwkv_kernel,
        out_shape=jax.ShapeDtypeStruct((B, S, R_WIDTH), BF16),
        grid=(B, S // ts),
        in_specs=[
            seq(R_WIDTH, OFF_RR // R_WIDTH), seq(R_WIDTH, OFF_RK // R_WIDTH),
            seq(R_WIDTH, OFF_RV // R_WIDTH), seq(LANES, OFF_RWA // LANES),
            seq(GLORA_PAD, OFF_RG // GLORA_PAD),
            vec(R_WIDTH, OFF_RR // R_WIDTH), vec(R_WIDTH, OFF_RK // R_WIDTH),
            vec(R_WIDTH, OFF_RV // R_WIDTH), vec(LANES, OFF_RWA // LANES),
            vec(GLORA_PAD, OFF_RG // GLORA_PAD),
            full(w0), full(w_up), full(a0), full(a_up), full(g_up),
            full(kkp), full(ka), full(rk), full(gn_g), full(gn_b),
        ],
        out_specs=pl.BlockSpec((1, ts, R_WIDTH), lambda b, s: (b, s, 0)),
        scratch_shapes=[pltpu.VMEM((SUBLANES, R_WIDTH), F32), pltpu.VMEM((SUBLANES, R_WIDTH), F32),
                        pltpu.VMEM((SUBLANES, R_WIDTH), F32), pltpu.VMEM((SUBLANES, LANES), F32),
                        pltpu.VMEM((SUBLANES, GLORA_PAD), F32),
                        pltpu.VMEM((npair, LANES, LANES), F32)],
        compiler_params=pltpu.CompilerParams(
            dimension_semantics=("parallel", "arbitrary"), vmem_limit_bytes=VMEM_LIMIT),
        name="rwkv",
    )(p_all, p_all, p_all, p_all, p_all, mu, mu, mu, mu, mu,
      w0, w_up, a0, a_up, g_up, kkp, ka, rk, gn_g, gn_b)


def _xattn_kernel(x_ref, g_ref, wq_ref, kv_ref, wo_ref, o_ref):
    D = x_ref.shape[1]
    x = x_ref[...]
    ms = jnp.mean(x * x, axis=-1, keepdims=True)
    h = (x * lax.rsqrt(ms + NORM_EPS) * g_ref[...]).astype(BF16)
    q = _dot(h, wq_ref[...]).astype(BF16)
    hsl = [slice(hd * X_HDIM, (hd + 1) * X_HDIM) for hd in range(D // X_HDIM)]
    s = [lax.dot_general(q[:, ls], kv_ref[0, :, ls], (((1,), (1,)), ((), ())),
                         preferred_element_type=F32) * (X_HDIM ** -0.5) for ls in hsl]
    e = [jnp.exp(si - jnp.max(si, axis=-1, keepdims=True)) for si in s]
    p = [(ei / jnp.sum(ei, axis=-1, keepdims=True)).astype(BF16) for ei in e]
    heads = [_dot(pi, kv_ref[0, :, D + ls.start:D + ls.stop]).astype(BF16)
             for pi, ls in zip(p, hsl)]
    o_ref[...] = x + _dot(jnp.concatenate(heads, axis=1), wo_ref[...])


def _xattn(x, gain, wq, kv, wo, seq_len, tm=512):
    T, D = x.shape
    M = kv.shape[1]
    tm = min(tm, seq_len)
    per_seq = seq_len // tm
    const = lambda a: pl.BlockSpec(a.shape, lambda i: (0,) * a.ndim, pipeline_mode=pl.Buffered(1))
    return pl.pallas_call(
        _xattn_kernel,
        out_shape=jax.ShapeDtypeStruct((T, D), F32),
        grid=(T // tm,),
        in_specs=[pl.BlockSpec((tm, D), lambda i: (i, 0)),
                  pl.BlockSpec((1, D), lambda i: (0, 0)),
                  const(wq),
                  pl.BlockSpec((1, M, 2 * D), lambda i: (i // per_seq, 0, 0)),
                  const(wo)],
        out_specs=pl.BlockSpec((tm, D), lambda i: (i, 0)),
        compiler_params=pltpu.CompilerParams(
            dimension_semantics=("parallel",), vmem_limit_bytes=VMEM_LIMIT),
        name="xattn",
    )(x, gain.reshape(1, D), wq, kv, wo)


def _ffn_kernel(x_ref, g_ref, wup_ref, cw_ref, cb_ref, wdn_ref, o_ref, tail_scr, *, per_seq, tc):
    tm = x_ref.shape[0]

    @pl.when(lax.rem(pl.program_id(0), per_seq) == 0)
    def _():
        tail_scr[...] = jnp.zeros_like(tail_scr)

    x = x_ref[...]
    ms = jnp.mean(x * x, axis=-1, keepdims=True)
    h = (x * lax.rsqrt(ms + NORM_EPS) * g_ref[...]).astype(BF16)
    chunks = [slice(c * tc, (c + 1) * tc) for c in range(D_FF // tc)]
    gates = [_dot(h, wup_ref[:, cols]) for cols in chunks]
    vals = [_dot(h, wup_ref[:, D_FF + cols.start:D_FF + cols.stop]) for cols in chunks]
    acts = []
    for cols, gate, val in zip(chunks, gates, vals):
        prev = tail_scr[:, cols]
        y = cb_ref[:, cols] + gate * cw_ref[FFN_CONV - 1:FFN_CONV, cols]
        for j in range(FFN_CONV - 1):
            y = y + _shift_rows_carry(gate, prev, FFN_CONV - 1 - j) * cw_ref[j:j + 1, cols]
        tail_scr[:, cols] = gate[tm - SUBLANES:]
        acts.append((y * _sigmoid(y) * val).astype(BF16))
    acc = x
    for cols, act in zip(chunks, acts):
        acc = acc + _dot(act, wdn_ref[cols, :])
    o_ref[...] = acc


def _ffn(x, gain, w_up, conv_w, conv_b, w_down, seq_len, tm=256, tc=256):
    T, D = x.shape
    tm = min(tm, seq_len)
    conv_b = conv_b.reshape(1, D_FF)
    const = lambda a: pl.BlockSpec(a.shape, lambda i: (0,) * a.ndim, pipeline_mode=pl.Buffered(1))
    return pl.pallas_call(
        functools.partial(_ffn_kernel, per_seq=seq_len // tm, tc=tc),
        out_shape=jax.ShapeDtypeStruct((T, D), F32),
        grid=(T // tm,),
        in_specs=[pl.BlockSpec((tm, D), lambda i: (i, 0)),
                  pl.BlockSpec((1, D), lambda i: (0, 0)),
                  const(w_up), const(conv_w), const(conv_b), const(w_down)],
        out_specs=pl.BlockSpec((tm, D), lambda i: (i, 0)),
        scratch_shapes=[pltpu.VMEM((SUBLANES, D_FF), F32)],
        compiler_params=pltpu.CompilerParams(
            dimension_semantics=("arbitrary",), vmem_limit_bytes=VMEM_LIMIT),
        name="ffn",
    )(x, gain.reshape(1, D), w_up, conv_w, conv_b, w_down)


def _pad_cols(w, n):
    return jnp.pad(w, ((0, 0), (0, n - w.shape[1])))


def _relayout_in(w):
    m_main = w[:, :4 * M_WIDTH]
    m_gate = w[:, 4 * M_WIDTH:4 * M_WIDTH + 2 * M_HEADS]
    r0 = 4 * M_WIDTH + 2 * M_HEADS
    r_main = w[:, r0:r0 + 3 * R_WIDTH + R_DECAY_LORA + R_AAA_LORA]
    r_gate = w[:, r0 + 3 * R_WIDTH + R_DECAY_LORA + R_AAA_LORA:]
    return jnp.concatenate(
        [m_main, r_main, _pad_cols(m_gate, GATE_PAD), _pad_cols(r_gate, GLORA_PAD)], axis=1)


def kernel(x, mem, norm_mix, w_in, m_conv_w, m_conv_b, m_gate_b, m_norm_g, r_mu, r_w0,
           r_w_up, r_a0, r_a_up, r_g_up, r_kk, r_ka, r_rk, r_gn_g, r_gn_b, w_out,
           norm_x, norm_mem, x_wq, x_wkv, x_wo, norm_ffn, f_up, f_conv_w, f_conv_b,
           f_down, norm_final):
    B, S, D = x.shape
    M = mem.shape[1]
    depth = w_in.shape[0]
    T = B * S
    nc = S // CHUNK
    xf = x.reshape(T, D)
    memf = mem.reshape(B * M, D)
    row = lambda a: a.reshape(1, -1)

    for l in range(depth):
        w_in_p = _relayout_in(w_in[l]).astype(BF16)
        p_all = _mm(xf, w_in_p, gain=norm_mix[l]).reshape(B, S, IN_COLS_P)

        g_col = p_all[:, :, OFF_MG:OFF_MG + 2 * M_HEADS].reshape(B, nc, CHUNK, 2 * M_HEADS)
        g_row = g_col.swapaxes(-1, -2)[:, :, :, None, :]
        y_m = _mlstm(p_all, g_row, g_col, m_gate_b[l].reshape(-1, 1, 1), row(m_gate_b[l]),
                     m_conv_w[l], row(m_conv_b[l]), row(m_norm_g[l]))

        mu = _relayout_in(jnp.pad(row(r_mu[l]), ((0, 0), (4 * M_WIDTH + 2 * M_HEADS, 0))))
        w_up = jnp.pad(r_w_up[l], ((0, R_AAA_LORA), (0, 0))).astype(BF16)
        a_up = jnp.pad(r_a_up[l], ((R_DECAY_LORA, 0), (0, 0))).astype(BF16)
        g_up = jnp.pad(r_g_up[l], ((0, GLORA_PAD - R_GATE_LORA), (0, 0))).astype(BF16)
        y_r = _rwkv(p_all, mu, row(r_w0[l]), w_up, row(r_a0[l]), a_up, g_up,
                    row(r_kk[l]), row(r_ka[l]), row(r_rk[l]), row(r_gn_g[l]), row(r_gn_b[l]))

        y = jnp.concatenate([y_m, y_r], axis=-1).reshape(T, D)
        xf = _mm(y, w_out[l].astype(BF16), resid=xf)

        kv = _mm(memf, x_wkv[l].astype(BF16), gain=norm_mem[l], out_dtype=BF16)
        xf = _xattn(xf, norm_x[l], x_wq[l].astype(BF16), kv.reshape(B, M, 2 * D),
                    x_wo[l].astype(BF16), S)

        xf = _ffn(xf, norm_ffn[l], f_up[l].astype(BF16), f_conv_w[l], f_conv_b[l],
                  f_down[l].astype(BF16), S)

    return _rmsnorm(xf, norm_final).reshape(B, S, D)
```

```python
import functools
import math

import jax
import jax.numpy as jnp
from jax import lax
from jax.experimental import pallas as pl
from jax.experimental.pallas import tpu as pltpu

F32 = jnp.float32
BF16 = jnp.bfloat16

D_MODEL = 1024
M_WIDTH = 512
M_HEADS = 4
M_HDIM = 128
M_CONV = 4
R_WIDTH = 512
R_HDIM = 64
R_HEADS = 8
R_DECAY_LORA = 64
R_AAA_LORA = 64
R_GATE_LORA = 160
DECAY_SCALE = math.exp(-0.5)
X_HEADS = 4
X_HDIM = 256
D_FF = 2816
FFN_CONV = 3
NORM_EPS = 1e-6
GN_EPS = 64e-5
CHUNK = 64

LANES = 128
SUBLANES = 8
GATE_PAD = LANES
GLORA_PAD = 2 * LANES
OFF_MQ, OFF_MK, OFF_MV, OFF_MO = 0, 512, 1024, 1536
OFF_RR = 2048
OFF_RK = OFF_RR + R_WIDTH
OFF_RV = OFF_RK + R_WIDTH
OFF_RWA = OFF_RV + R_WIDTH
OFF_MG = OFF_RWA + LANES
OFF_RG = OFF_MG + GATE_PAD
IN_COLS_P = OFF_RG + GLORA_PAD

SEQ_TILE = 256
VMEM_LIMIT = 48 * 1024 * 1024


def _dot(a, b):
    return jnp.dot(a, b, preferred_element_type=F32)


def _bdot(a, b):
    return jnp.dot(a.astype(BF16), b.astype(BF16), preferred_element_type=F32)


def _bdot_nt(a, b):
    return lax.dot_general(a.astype(BF16), b.astype(BF16), (((1,), (1,)), ((), ())),
                           preferred_element_type=F32)


def _bdot_tn(a, b):
    return lax.dot_general(a.astype(BF16), b.astype(BF16), (((0,), (0,)), ((), ())),
                           preferred_element_type=F32)


def _split2(x):
    hi = x.astype(BF16)
    lo = (x - hi.astype(F32)).astype(BF16)
    return hi, lo


def _sigmoid(x):
    return 0.5 * jnp.tanh(0.5 * x) + 0.5


def _shift_rows_carry(x, prev, sh):
    ext = jnp.concatenate([prev, x], axis=0)
    return pltpu.roll(ext, sh, 0)[SUBLANES:]


def _mm_kernel(x_ref, w_ref, g_ref, o_ref, h_scr, *, tn):
    x = x_ref[...]
    ms = jnp.mean(x * x, axis=-1, keepdims=True)
    h_scr[...] = (x * lax.rsqrt(ms + NORM_EPS) * g_ref[...]).astype(BF16)
    for c in range(w_ref.shape[1] // tn):
        cols = slice(c * tn, (c + 1) * tn)
        o_ref[:, cols] = _dot(h_scr[...], w_ref[:, cols]).astype(o_ref.dtype)


def _mm_tile_rows(K, N, x_bytes, out_bytes):
    budget = (VMEM_LIMIT * 3) // 4 - K * N * 2
    for tm in (512, 256, 128):
        if tm * (2 * K * x_bytes + 2 * N * out_bytes + K * 2) <= budget:
            return tm
    raise ValueError("weight does not fit in VMEM")


def _mm(x, w, gain, out_dtype=F32, tn=512):
    T, K = x.shape
    N = w.shape[1]
    tm = min(T, _mm_tile_rows(K, N, x.dtype.itemsize, jnp.dtype(out_dtype).itemsize))
    tn = min(tn, N)
    assert T % tm == 0 and N % tn == 0
    return pl.pallas_call(
        functools.partial(_mm_kernel, tn=tn),
        out_shape=jax.ShapeDtypeStruct((T, N), out_dtype),
        grid=(T // tm,),
        in_specs=[pl.BlockSpec((tm, K), lambda i: (i, 0)),
                  pl.BlockSpec((K, N), lambda i: (0, 0), pipeline_mode=pl.Buffered(1)),
                  pl.BlockSpec((1, K), lambda i: (0, 0))],
        out_specs=pl.BlockSpec((tm, N), lambda i: (i, 0)),
        scratch_shapes=[pltpu.VMEM((tm, K), BF16)],
        compiler_params=pltpu.CompilerParams(
            dimension_semantics=("parallel",), vmem_limit_bytes=VMEM_LIMIT),
        name="mm",
    )(x, w, gain.reshape(1, K))


def _mlstm_kernel(q_ref, k_ref, v_ref, o_ref, grow_ref, gcol_ref, brow_ref, bcol_ref,
                  cw_ref, cb_ref, ng_ref, out_ref,
                  qp_scr, kp_scr, c_scr, n_scr, m_scr):
    TS = q_ref.shape[1]
    L = CHUNK
    nch = TS // L

    @pl.when(pl.program_id(1) == 0)
    def _():
        qp_scr[...] = jnp.zeros_like(qp_scr)
        kp_scr[...] = jnp.zeros_like(kp_scr)
        c_scr[...] = jnp.zeros_like(c_scr)
        n_scr[...] = jnp.zeros_like(n_scr)
        m_scr[...] = jnp.zeros_like(m_scr)

    def conv_silu(x, prev, w, b):
        y = b + x * w[M_CONV - 1:M_CONV, :]
        for j in range(M_CONV - 1):
            y = y + _shift_rows_carry(x, prev, M_CONV - 1 - j) * w[j:j + 1, :]
        return y * _sigmoid(y)

    q_raw = q_ref[0]
    k_raw = k_ref[0]
    qc_all = conv_silu(q_raw, qp_scr[...], cw_ref[:, :M_WIDTH], cb_ref[:, :M_WIDTH]) * (M_HDIM ** -0.5)
    kc_all = conv_silu(k_raw, kp_scr[...], cw_ref[:, M_WIDTH:], cb_ref[:, M_WIDTH:])
    qp_scr[...] = q_raw[TS - SUBLANES:]
    kp_scr[...] = k_raw[TS - SUBLANES:]
    v_all = v_ref[0]
    o_all = o_ref[0]

    ti = lax.broadcasted_iota(jnp.int32, (L, L), 0)
    si = lax.broadcasted_iota(jnp.int32, (L, L), 1)
    causal = si <= ti
    upper = ti <= si

    def log_sigmoid(x):
        return jnp.minimum(x, 0.0) - jnp.log1p(jnp.exp(-jnp.abs(x)))

    hs = range(M_HEADS)
    diag = ti == si

    H = M_HEADS
    G = nch * H

    def groups(x):
        return jnp.stack([x[c * L:(c + 1) * L, h * M_HDIM:(h + 1) * M_HDIM]
                          for c in range(nch) for h in hs])

    def per_group(f, *xs):
        return jnp.stack([f(*[x[i] for x in xs]) for i in range(G)])

    q = groups(qc_all)
    k = groups(kc_all)
    v = groups(v_all)
    gr = grow_ref[0] + brow_ref[...]
    gc = gcol_ref[0] + bcol_ref[...]
    logi_r = gr[:, :H].reshape(G, 1, L)
    logf_r = log_sigmoid(gr[:, H:].reshape(G, 1, L))
    logi_c = jnp.stack([gc[c, :, h:h + 1] for c in range(nch) for h in hs])
    b_c = jnp.sum(jnp.where(causal, logf_r, 0.0), axis=2, keepdims=True)
    b_r = jnp.sum(jnp.where(diag, b_c, 0.0), axis=1, keepdims=True)
    g = jnp.sum(logf_r, axis=2, keepdims=True)
    a_r = g - b_r + logi_r
    a_c = g - b_c + logi_c
    m_loc = jnp.max(a_r, axis=2, keepdims=True)
    wa_c = jnp.exp(a_c - m_loc)
    c_loc = per_group(_bdot_tn, v * wa_c, k)
    n_loc = jnp.sum(k * wa_c, axis=1, keepdims=True)
    d = jnp.where(causal, b_c - b_r + logi_r, -jnp.inf)
    d_max = jnp.max(d, axis=2, keepdims=True)
    qk = per_group(_bdot_nt, q, k)

    c_prev = c_scr[...]
    n_prev = n_scr[...]
    m_prev = m_scr[...]
    c_in, n_in, m_in = [], [], []
    for c in range(nch):
        gs = slice(c * H, (c + 1) * H)
        c_in.append(c_prev)
        n_in.append(n_prev)
        m_in.append(m_prev)
        m_new = jnp.maximum(g[gs] + m_prev, m_loc[gs])
        s_old = jnp.exp(g[gs] + m_prev - m_new)
        s_loc = jnp.exp(m_loc[gs] - m_new)
        c_prev = s_old * c_prev + s_loc * c_loc[gs]
        n_prev = s_old * n_prev + s_loc * n_loc[gs]
        m_prev = m_new
    c_scr[...] = c_prev
    n_scr[...] = n_prev
    m_scr[...] = m_prev
    c_in = jnp.concatenate(c_in, axis=0)
    n_in = jnp.concatenate(n_in, axis=0)
    m_in = jnp.concatenate(m_in, axis=0)

    inter = b_c + m_in
    m_t = jnp.maximum(inter, d_max)
    s_int = jnp.exp(inter - m_t)
    p = jnp.exp(d - m_t) * qk
    num = s_int * per_group(_bdot_nt, q, c_in) + per_group(_bdot, p, v)
    den = (s_int * jnp.sum(q * n_in, axis=2, keepdims=True)
           + jnp.sum(p, axis=2, keepdims=True))
    hh = num / jnp.maximum(jnp.abs(den), jnp.exp(-m_t))
    mu = jnp.mean(hh, axis=-1, keepdims=True)
    hc = hh - mu
    var = jnp.mean(hc * hc, axis=-1, keepdims=True)
    ng = jnp.stack([ng_ref[:, h * M_HDIM:(h + 1) * M_HDIM] for h in hs] * nch)
    y = _sigmoid(groups(o_all)) * (hc * lax.rsqrt(var + NORM_EPS)) * ng
    for c in range(nch):
        for h in hs:
            out_ref[0, c * L:(c + 1) * L, h * M_HDIM:(h + 1) * M_HDIM] = (
                y[c * H + h].astype(out_ref.dtype))


def _mlstm(p_all, g_row, g_col, b_row, b_col, conv_w, conv_b, norm_g):
    B, S, _ = p_all.shape
    ts = min(SEQ_TILE, S)
    nch = ts // CHUNK
    seq = lambda off: pl.BlockSpec((1, ts, M_WIDTH), lambda b, s, off=off: (b, s, off))
    full = lambda a: pl.BlockSpec(a.shape, lambda b, s: (0,) * a.ndim)
    return pl.pallas_call(
        _mlstm_kernel,
        out_shape=jax.ShapeDtypeStruct((B, S, M_WIDTH), BF16),
        grid=(B, S // ts),
        in_specs=[
            seq(OFF_MQ // M_WIDTH), seq(OFF_MK // M_WIDTH), seq(OFF_MV // M_WIDTH),
            seq(OFF_MO // M_WIDTH),
            pl.BlockSpec((1, nch, 2 * M_HEADS, 1, CHUNK), lambda b, s: (b, s, 0, 0, 0)),
            pl.BlockSpec((1, nch, CHUNK, 2 * M_HEADS), lambda b, s: (b, s, 0, 0)),
            full(b_row), full(b_col), full(conv_w), full(conv_b), full(norm_g),
        ],
        out_specs=pl.BlockSpec((1, ts, M_WIDTH), lambda b, s: (b, s, 0)),
        scratch_shapes=[pltpu.VMEM((SUBLANES, M_WIDTH), F32), pltpu.VMEM((SUBLANES, M_WIDTH), F32),
                        pltpu.VMEM((M_HEADS, M_HDIM, M_HDIM), F32),
                        pltpu.VMEM((M_HEADS, 1, M_HDIM), F32),
                        pltpu.VMEM((M_HEADS, 1, 1), F32)],
        compiler_params=pltpu.CompilerParams(
            dimension_semantics=("parallel", "arbitrary"), vmem_limit_bytes=VMEM_LIMIT),
        name="mlstm",
    )(p_all, p_all, p_all, p_all, g_row, g_col, b_row, b_col, conv_w, conv_b, norm_g)


def _inv_unit_lower_blocks(a, blk):
    n = a[0].shape[0]
    ti = lax.broadcasted_iota(jnp.int32, (n, n), 0)
    si = lax.broadcasted_iota(jnp.int32, (n, n), 1)

    def off_mask(b):
        sh = (2 * b).bit_length() - 1
        same = jnp.right_shift(ti, sh) == jnp.right_shift(si, sh)
        return same & (jnp.bitwise_and(ti, b) != 0) & (jnp.bitwise_and(si, b) == 0)

    eye = jnp.where(ti == si, 1.0, 0.0)
    m1 = off_mask(1)
    invs = [eye - jnp.where(m1, x, 0.0) for x in a]
    b = 2
    while b < blk:
        mb = off_mask(b)
        offs = [jnp.where(mb, x, 0.0).astype(BF16) for x in a]
        xs = [_bdot(i, o) for i, o in zip(invs, offs)]
        ys = [_bdot(x, i) for x, i in zip(xs, invs)]
        invs = [i - y for i, y in zip(invs, ys)]
        b *= 2
    return invs


def _rwkv_kernel(pr_ref, pk_ref, pv_ref, pwa_ref, pg_ref,
                 mur_ref, muk_ref, muv_ref, muwa_ref, mug_ref,
                 w0_ref, wup_ref, a0_ref, aup_ref, gup_ref,
                 kkp_ref, ka_ref, rk_ref, gng_ref, gnb_ref,
                 ym_ref, x_ref, wout_ref,
                 out_ref,
                 cr_scr, ck_scr, cv_scr, cwa_scr, cg_scr, st_scr):
    TS = pr_ref.shape[1]
    L = CHUNK
    N = R_HDIM
    nch = TS // L
    npair = R_WIDTH // LANES
    Q = 2 * LANES

    @pl.when(pl.program_id(1) == 0)
    def _():
        for scr in (cr_scr, ck_scr, cv_scr, cwa_scr, cg_scr, st_scr):
            scr[...] = jnp.zeros_like(scr)

    def tshift(p_ref, mu_ref, c_scr):
        p = p_ref[0]
        prev = _shift_rows_carry(p, c_scr[...], 1)
        c_scr[...] = p[TS - SUBLANES:]
        return p + (prev - p) * mu_ref[...]

    hsh = N.bit_length() - 1
    li = lax.broadcasted_iota(jnp.int32, (LANES, LANES), 0)
    lj = lax.broadcasted_iota(jnp.int32, (LANES, LANES), 1)
    same_head = jnp.right_shift(li, hsh) == jnp.right_shift(lj, hsh)
    ones_bd = jnp.where(same_head, 1.0, 0.0).astype(BF16)

    def seg_sum(x):
        outs = []
        for p in range(x.shape[1] // LANES):
            hi, lo = _split2(x[:, p * LANES:(p + 1) * LANES])
            outs.append(_dot(hi, ones_bd) + _dot(lo, ones_bd))
        return jnp.concatenate(outs, axis=1)

    rr = tshift(pr_ref, mur_ref, cr_scr)
    kr = tshift(pk_ref, muk_ref, ck_scr)
    vr = tshift(pv_ref, muv_ref, cv_scr)
    wa = tshift(pwa_ref, muwa_ref, cwa_scr)
    gd = tshift(pg_ref, mug_ref, cg_scr)
    logw = -DECAY_SCALE * _sigmoid(w0_ref[...] + _bdot(jnp.tanh(wa), wup_ref[...]))
    a = _sigmoid(a0_ref[...] + _bdot(wa, aup_ref[...]))
    g = _bdot(_sigmoid(gd), gup_ref[...])
    kkraw = kr * kkp_ref[...]
    kk = kkraw / jnp.maximum(jnp.sqrt(seg_sum(kkraw * kkraw)), 1e-12)
    km = kr * (1.0 + (a - 1.0) * ka_ref[...])
    be = kk * a

    tq = lax.broadcasted_iota(jnp.int32, (Q, Q), 0)
    sq = lax.broadcasted_iota(jnp.int32, (Q, Q), 1)
    tril = jnp.where((jnp.right_shift(tq, hsh) == jnp.right_shift(sq, hsh)) & (sq <= tq),
                     1.0, 0.0).astype(BF16)
    lw_hi, lw_lo = _split2(logw)
    bincl = jnp.concatenate(
        [_dot(tril, lw_hi[q * Q:(q + 1) * Q]) + _dot(tril, lw_lo[q * Q:(q + 1) * Q])
         for q in range(TS // Q)], axis=0)
    e_in = jnp.exp(bincl)
    e_ng = jnp.exp(-bincl)
    kt = kk * jnp.exp(bincl - logw)
    rt = rr * e_in
    bh = be * e_ng
    kh = km * e_ng

    h0 = lax.broadcasted_iota(jnp.int32, (L, LANES), 1) < N
    ti = lax.broadcasted_iota(jnp.int32, (2 * L, 2 * L), 0)
    si = lax.broadcasted_iota(jnp.int32, (2 * L, 2 * L), 1)
    same_blk = jnp.right_shift(ti, hsh) == jnp.right_shift(si, hsh)
    strict = same_blk & (si < ti)
    incl = same_blk & (si <= ti)

    def stack_heads(x):
        return jnp.concatenate([jnp.where(h0, x, 0.0), jnp.where(h0, 0.0, x)], axis=0)

    def stack_dup(x):
        return jnp.concatenate([x, x], axis=0)

    def comb(x):
        return jnp.where(h0, x[:L], x[L:])

    w_ch = [[None] * nch for _ in range(npair)]
    u0_ch = [[None] * nch for _ in range(npair)]
    arkv_ch = [[None] * nch for _ in range(npair)]
    arb_ch = [[None] * nch for _ in range(npair)]
    probs = [(p, c) for p in range(npair) for c in range(nch)]

    def tile(x, p, c):
        return x[c * L:(c + 1) * L, p * LANES:(p + 1) * LANES]

    lk = [stack_heads(tile(kt, p, c)).astype(BF16) for p, c in probs]
    lr = [stack_heads(tile(rt, p, c)).astype(BF16) for p, c in probs]
    rb = [stack_dup(tile(bh, p, c)).astype(BF16) for p, c in probs]
    rk = [stack_dup(tile(kh, p, c)).astype(BF16) for p, c in probs]
    vs = [stack_dup(tile(vr, p, c)).astype(BF16) for p, c in probs]
    a_bd = [jnp.where(strict, _bdot_nt(x, y), 0.0) for x, y in zip(lk, rb)]
    bk_bd = [jnp.where(strict, _bdot_nt(x, y), 0.0).astype(BF16) for x, y in zip(lk, rk)]
    arb_bd = [jnp.where(incl, _bdot_nt(x, y), 0.0).astype(BF16) for x, y in zip(lr, rb)]
    ark_bd = [jnp.where(incl, _bdot_nt(x, y), 0.0).astype(BF16) for x, y in zip(lr, rk)]
    t_bd = _inv_unit_lower_blocks(a_bd, L)
    bkv = [comb(_bdot(x, y)) for x, y in zip(bk_bd, vs)]
    xs = [jnp.concatenate([stack_dup(tile(kt, p, c)), stack_dup(z)], axis=1)
          for (p, c), z in zip(probs, bkv)]
    tx = [_bdot(x, y) for x, y in zip(t_bd, xs)]
    arkv = [comb(_bdot(x, y)) for x, y in zip(ark_bd, vs)]
    for i, (p, c) in enumerate(probs):
        w_ch[p][c] = comb(tx[i][:, :LANES])
        u0_ch[p][c] = -comb(tx[i][:, LANES:])
        arkv_ch[p][c] = arkv[i]
        arb_ch[p][c] = arb_bd[i]

    y_rows = []
    st = [st_scr[p] for p in range(npair)]
    ones_f = jnp.where(same_head, 1.0, 0.0)
    for c in range(nch):
        rs = slice(c * L, (c + 1) * L)
        p_end = e_in[c * L + L - 1:c * L + L, :]
        pairs = range(npair)
        lsl = [slice(p * LANES, (p + 1) * LANES) for p in pairs]
        pe = [p_end[:, ls] for ls in lsl]
        rw = [_bdot_nt(jnp.concatenate([rt[rs, lsl[p]], w_ch[p][c]], axis=0), st[p])
              for p in pairs]
        u = [u0_ch[p][c] - rw[p][L:] for p in pairs]
        au = [_bdot(arb_ch[p][c], jnp.concatenate([u[p], u[p]], axis=0)) for p in pairs]
        upd = [_bdot_tn(jnp.concatenate([u[p], vr[rs, lsl[p]]], axis=0),
                        jnp.concatenate([bh[rs, lsl[p]] * pe[p], kh[rs, lsl[p]] * pe[p]], axis=0))
               for p in pairs]
        st = [st[p] * pe[p] + upd[p] * ones_f for p in pairs]
        y_rows.append(jnp.concatenate(
            [rw[p][:L] + jnp.where(h0, au[p][:L], au[p][L:]) + arkv_ch[p][c] for p in pairs],
            axis=1))
    for p in range(npair):
        st_scr[p] = st[p]
    y = jnp.concatenate(y_rows, axis=0)

    inv_n = 1.0 / N
    mu = seg_sum(y) * inv_n
    yc = y - mu
    var = seg_sum(yc * yc) * inv_n
    yn = yc * lax.rsqrt(var + GN_EPS) * gng_ref[...] + gnb_ref[...]
    bonus = seg_sum(rr * km * rk_ref[...]) * vr
    y_r = ((yn + bonus) * g).astype(BF16)
    out_ref[0] = (x_ref[0] + _dot(ym_ref[0], wout_ref[:M_WIDTH, :])
                  + _dot(y_r, wout_ref[M_WIDTH:, :]))


def _rwkv(p_all, mu, w0, w_up, a0, a_up, g_up, kkp, ka, rk, gn_g, gn_b, y_m, x, w_out):
    B, S, _ = p_all.shape
    D = x.shape[2]
    ts = min(SEQ_TILE, S)
    npair = R_WIDTH // LANES
    seq = lambda w, off: pl.BlockSpec((1, ts, w), lambda b, s, off=off: (b, s, off))
    vec = lambda w, off: pl.BlockSpec((1, w), lambda b, s, off=off: (0, off))
    full = lambda a: pl.BlockSpec(a.shape, lambda b, s: (0,) * a.ndim)
    return pl.pallas_call(
        _rwkv_kernel,
        out_shape=jax.ShapeDtypeStruct((B, S, D), F32),
        grid=(B, S // ts),
        in_specs=[
            seq(R_WIDTH, OFF_RR // R_WIDTH), seq(R_WIDTH, OFF_RK // R_WIDTH),
            seq(R_WIDTH, OFF_RV // R_WIDTH), seq(LANES, OFF_RWA // LANES),
            seq(GLORA_PAD, OFF_RG // GLORA_PAD),
            vec(R_WIDTH, OFF_RR // R_WIDTH), vec(R_WIDTH, OFF_RK // R_WIDTH),
            vec(R_WIDTH, OFF_RV // R_WIDTH), vec(LANES, OFF_RWA // LANES),
            vec(GLORA_PAD, OFF_RG // GLORA_PAD),
            full(w0), full(w_up), full(a0), full(a_up), full(g_up),
            full(kkp), full(ka), full(rk), full(gn_g), full(gn_b),
            seq(M_WIDTH, 0), seq(D, 0),
            pl.BlockSpec(w_out.shape, lambda b, s: (0, 0), pipeline_mode=pl.Buffered(1)),
        ],
        out_specs=pl.BlockSpec((1, ts, D), lambda b, s: (b, s, 0)),
        scratch_shapes=[pltpu.VMEM((SUBLANES, R_WIDTH), F32), pltpu.VMEM((SUBLANES, R_WIDTH), F32),
                        pltpu.VMEM((SUBLANES, R_WIDTH), F32), pltpu.VMEM((SUBLANES, LANES), F32),
                        pltpu.VMEM((SUBLANES, GLORA_PAD), F32),
                        pltpu.VMEM((npair, LANES, LANES), F32)],
        compiler_params=pltpu.CompilerParams(
            dimension_semantics=("parallel", "arbitrary"), vmem_limit_bytes=VMEM_LIMIT),
        name="rwkv",
    )(p_all, p_all, p_all, p_all, p_all, mu, mu, mu, mu, mu,
      w0, w_up, a0, a_up, g_up, kkp, ka, rk, gn_g, gn_b, y_m, x, w_out)


def _xattn_kernel(x_ref, g_ref, wq_ref, kv_ref, wo_ref, o_ref):
    D = x_ref.shape[1]
    x = x_ref[...]
    ms = jnp.mean(x * x, axis=-1, keepdims=True)
    h = (x * lax.rsqrt(ms + NORM_EPS) * g_ref[...]).astype(BF16)
    q = _dot(h, wq_ref[...]).astype(BF16)
    hsl = [slice(hd * X_HDIM, (hd + 1) * X_HDIM) for hd in range(D // X_HDIM)]
    s = [lax.dot_general(q[:, ls], kv_ref[0, :, ls], (((1,), (1,)), ((), ())),
                         preferred_element_type=F32) * (X_HDIM ** -0.5) for ls in hsl]
    e = [jnp.exp(si - jnp.max(si, axis=-1, keepdims=True)) for si in s]
    p = [(ei / jnp.sum(ei, axis=-1, keepdims=True)).astype(BF16) for ei in e]
    heads = [_dot(pi, kv_ref[0, :, D + ls.start:D + ls.stop]).astype(BF16)
             for pi, ls in zip(p, hsl)]
    o_ref[...] = x + _dot(jnp.concatenate(heads, axis=1), wo_ref[...])


def _xattn(x, gain, wq, kv, wo, seq_len, tm=512):
    T, D = x.shape
    M = kv.shape[1]
    tm = min(tm, seq_len)
    per_seq = seq_len // tm
    const = lambda a: pl.BlockSpec(a.shape, lambda i: (0,) * a.ndim, pipeline_mode=pl.Buffered(1))
    return pl.pallas_call(
        _xattn_kernel,
        out_shape=jax.ShapeDtypeStruct((T, D), F32),
        grid=(T // tm,),
        in_specs=[pl.BlockSpec((tm, D), lambda i: (i, 0)),
                  pl.BlockSpec((1, D), lambda i: (0, 0)),
                  const(wq),
                  pl.BlockSpec((1, M, 2 * D), lambda i: (i // per_seq, 0, 0)),
                  const(wo)],
        out_specs=pl.BlockSpec((tm, D), lambda i: (i, 0)),
        compiler_params=pltpu.CompilerParams(
            dimension_semantics=("parallel",), vmem_limit_bytes=VMEM_LIMIT),
        name="xattn",
    )(x, gain.reshape(1, D), wq, kv, wo)


def _ffn_kernel(*refs, per_seq, tc, norm_out):
    x_ref, g_ref, wup_ref, cw_ref, cb_ref, wdn_ref = refs[:6]
    gout_ref = refs[6] if norm_out else None
    o_ref, tail_scr = refs[-2:]
    tm = x_ref.shape[0]

    @pl.when(lax.rem(pl.program_id(0), per_seq) == 0)
    def _():
        tail_scr[...] = jnp.zeros_like(tail_scr)

    x = x_ref[...]
    ms = jnp.mean(x * x, axis=-1, keepdims=True)
    h = (x * lax.rsqrt(ms + NORM_EPS) * g_ref[...]).astype(BF16)
    chunks = [slice(c * tc, (c + 1) * tc) for c in range(D_FF // tc)]
    gates = [_dot(h, wup_ref[:, cols]) for cols in chunks]
    vals = [_dot(h, wup_ref[:, D_FF + cols.start:D_FF + cols.stop]) for cols in chunks]
    acts = []
    for cols, gate, val in zip(chunks, gates, vals):
        prev = tail_scr[:, cols]
        y = cb_ref[:, cols] + gate * cw_ref[FFN_CONV - 1:FFN_CONV, cols]
        for j in range(FFN_CONV - 1):
            y = y + _shift_rows_carry(gate, prev, FFN_CONV - 1 - j) * cw_ref[j:j + 1, cols]
        tail_scr[:, cols] = gate[tm - SUBLANES:]
        acts.append((y * _sigmoid(y) * val).astype(BF16))
    acc = x
    for cols, act in zip(chunks, acts):
        acc = acc + _dot(act, wdn_ref[cols, :])
    if norm_out:
        ms = jnp.mean(acc * acc, axis=-1, keepdims=True)
        acc = acc * lax.rsqrt(ms + NORM_EPS) * gout_ref[...]
    o_ref[...] = acc


def _ffn(x, gain, w_up, conv_w, conv_b, w_down, seq_len, out_gain=None, tm=256, tc=256):
    T, D = x.shape
    tm = min(tm, seq_len)
    conv_b = conv_b.reshape(1, D_FF)
    norm_out = out_gain is not None
    const = lambda a: pl.BlockSpec(a.shape, lambda i: (0,) * a.ndim, pipeline_mode=pl.Buffered(1))
    consts = [w_up, conv_w, conv_b, w_down] + ([out_gain.reshape(1, D)] if norm_out else [])
    return pl.pallas_call(
        functools.partial(_ffn_kernel, per_seq=seq_len // tm, tc=tc, norm_out=norm_out),
        out_shape=jax.ShapeDtypeStruct((T, D), F32),
        grid=(T // tm,),
        in_specs=[pl.BlockSpec((tm, D), lambda i: (i, 0)),
                  pl.BlockSpec((1, D), lambda i: (0, 0)),
                  ] + [const(a) for a in consts],
        out_specs=pl.BlockSpec((tm, D), lambda i: (i, 0)),
        scratch_shapes=[pltpu.VMEM((SUBLANES, D_FF), F32)],
        compiler_params=pltpu.CompilerParams(
            dimension_semantics=("arbitrary",), vmem_limit_bytes=VMEM_LIMIT),
        name="ffn",
    )(x, gain.reshape(1, D), *consts)


def _pad_cols(w, n):
    return jnp.pad(w, ((0, 0), (0, n - w.shape[1])))


def _relayout_in(w):
    m_main = w[:, :4 * M_WIDTH]
    m_gate = w[:, 4 * M_WIDTH:4 * M_WIDTH + 2 * M_HEADS]
    r0 = 4 * M_WIDTH + 2 * M_HEADS
    r_main = w[:, r0:r0 + 3 * R_WIDTH + R_DECAY_LORA + R_AAA_LORA]
    r_gate = w[:, r0 + 3 * R_WIDTH + R_DECAY_LORA + R_AAA_LORA:]
    return jnp.concatenate(
        [m_main, r_main, _pad_cols(m_gate, GATE_PAD), _pad_cols(r_gate, GLORA_PAD)], axis=1)


def kernel(x, mem, norm_mix, w_in, m_conv_w, m_conv_b, m_gate_b, m_norm_g, r_mu, r_w0,
           r_w_up, r_a0, r_a_up, r_g_up, r_kk, r_ka, r_rk, r_gn_g, r_gn_b, w_out,
           norm_x, norm_mem, x_wq, x_wkv, x_wo, norm_ffn, f_up, f_conv_w, f_conv_b,
           f_down, norm_final):
    B, S, D = x.shape
    M = mem.shape[1]
    depth = w_in.shape[0]
    T = B * S
    nc = S // CHUNK
    xf = x.reshape(T, D)
    memf = mem.reshape(B * M, D)
    row = lambda a: a.reshape(1, -1)

    for l in range(depth):
        w_in_p = _relayout_in(w_in[l]).astype(BF16)
        p_all = _mm(xf, w_in_p, gain=norm_mix[l]).reshape(B, S, IN_COLS_P)

        g_col = p_all[:, :, OFF_MG:OFF_MG + 2 * M_HEADS].reshape(B, nc, CHUNK, 2 * M_HEADS)
        g_row = g_col.swapaxes(-1, -2)[:, :, :, None, :]
        y_m = _mlstm(p_all, g_row, g_col, m_gate_b[l].reshape(-1, 1, 1), row(m_gate_b[l]),
                     m_conv_w[l], row(m_conv_b[l]), row(m_norm_g[l]))

        mu = _relayout_in(jnp.pad(row(r_mu[l]), ((0, 0), (4 * M_WIDTH + 2 * M_HEADS, 0))))
        w_up = jnp.pad(r_w_up[l], ((0, R_AAA_LORA), (0, 0))).astype(BF16)
        a_up = jnp.pad(r_a_up[l], ((R_DECAY_LORA, 0), (0, 0))).astype(BF16)
        g_up = jnp.pad(r_g_up[l], ((0, GLORA_PAD - R_GATE_LORA), (0, 0))).astype(BF16)
        xf = _rwkv(p_all, mu, row(r_w0[l]), w_up, row(r_a0[l]), a_up, g_up,
                   row(r_kk[l]), row(r_ka[l]), row(r_rk[l]), row(r_gn_g[l]), row(r_gn_b[l]),
                   y_m, xf.reshape(B, S, D), w_out[l].astype(BF16)).reshape(T, D)

        kv = _mm(memf, x_wkv[l].astype(BF16), gain=norm_mem[l], out_dtype=BF16)
        xf = _xattn(xf, norm_x[l], x_wq[l].astype(BF16), kv.reshape(B, M, 2 * D),
                    x_wo[l].astype(BF16), S)

        xf = _ffn(xf, norm_ffn[l], f_up[l].astype(BF16), f_conv_w[l], f_conv_b[l],
                  f_down[l].astype(BF16), S, out_gain=norm_final if l == depth - 1 else None)

    return xf.reshape(B, S, D)
```

```python
import functools
import math

import jax
import jax.numpy as jnp
from jax import lax
from jax.experimental import pallas as pl
from jax.experimental.pallas import tpu as pltpu

F32 = jnp.float32
BF16 = jnp.bfloat16

D_MODEL = 1024
M_WIDTH = 512
M_HEADS = 4
M_HDIM = 128
M_CONV = 4
R_WIDTH = 512
R_HDIM = 64
R_HEADS = 8
R_DECAY_LORA = 64
R_AAA_LORA = 64
R_GATE_LORA = 160
DECAY_SCALE = math.exp(-0.5)
X_HEADS = 4
X_HDIM = 256
D_FF = 2816
FFN_CONV = 3
NORM_EPS = 1e-6
GN_EPS = 64e-5
CHUNK = 64

LANES = 128
SUBLANES = 8
GATE_PAD = LANES
GLORA_PAD = 2 * LANES
OFF_MQ, OFF_MK, OFF_MV, OFF_MO = 0, 512, 1024, 1536
OFF_RR = 2048
OFF_RK = OFF_RR + R_WIDTH
OFF_RV = OFF_RK + R_WIDTH
OFF_RWA = OFF_RV + R_WIDTH
OFF_MG = OFF_RWA + LANES
OFF_RG = OFF_MG + GATE_PAD
IN_COLS_P = OFF_RG + GLORA_PAD

SEQ_TILE = 256
VMEM_LIMIT = 48 * 1024 * 1024


def _dot(a, b):
    return jnp.dot(a, b, preferred_element_type=F32)


def _bdot(a, b):
    return jnp.dot(a.astype(BF16), b.astype(BF16), preferred_element_type=F32)


def _bdot_nt(a, b):
    return lax.dot_general(a.astype(BF16), b.astype(BF16), (((1,), (1,)), ((), ())),
                           preferred_element_type=F32)


def _bdot_tn(a, b):
    return lax.dot_general(a.astype(BF16), b.astype(BF16), (((0,), (0,)), ((), ())),
                           preferred_element_type=F32)


def _split2(x):
    hi = x.astype(BF16)
    lo = (x - hi.astype(F32)).astype(BF16)
    return hi, lo


def _sigmoid(x):
    return 0.5 * jnp.tanh(0.5 * x) + 0.5


def _shift_rows_carry(x, prev, sh):
    ext = jnp.concatenate([prev, x], axis=0)
    return pltpu.roll(ext, sh, 0)[SUBLANES:]


def _mm_kernel(x_ref, w_ref, g_ref, o_ref, h_scr, *, tn):
    x = x_ref[...]
    ms = jnp.mean(x * x, axis=-1, keepdims=True)
    h_scr[...] = (x * lax.rsqrt(ms + NORM_EPS) * g_ref[...]).astype(BF16)
    for c in range(w_ref.shape[1] // tn):
        cols = slice(c * tn, (c + 1) * tn)
        o_ref[:, cols] = _dot(h_scr[...], w_ref[:, cols]).astype(o_ref.dtype)


def _mm_tile_rows(K, N, x_bytes, out_bytes):
    budget = (VMEM_LIMIT * 3) // 4 - K * N * 2
    for tm in (512, 256, 128):
        if tm * (2 * K * x_bytes + 2 * N * out_bytes + K * 2) <= budget:
            return tm
    raise ValueError("weight does not fit in VMEM")


def _mm(x, w, gain, out_dtype=F32, tn=512):
    T, K = x.shape
    N = w.shape[1]
    tm = min(T, _mm_tile_rows(K, N, x.dtype.itemsize, jnp.dtype(out_dtype).itemsize))
    tn = min(tn, N)
    assert T % tm == 0 and N % tn == 0
    return pl.pallas_call(
        functools.partial(_mm_kernel, tn=tn),
        out_shape=jax.ShapeDtypeStruct((T, N), out_dtype),
        grid=(T // tm,),
        in_specs=[pl.BlockSpec((tm, K), lambda i: (i, 0)),
                  pl.BlockSpec((K, N), lambda i: (0, 0), pipeline_mode=pl.Buffered(1)),
                  pl.BlockSpec((1, K), lambda i: (0, 0))],
        out_specs=pl.BlockSpec((tm, N), lambda i: (i, 0)),
        scratch_shapes=[pltpu.VMEM((tm, K), BF16)],
        compiler_params=pltpu.CompilerParams(
            dimension_semantics=("parallel",), vmem_limit_bytes=VMEM_LIMIT),
        name="mm",
    )(x, w, gain.reshape(1, K))


def _mlstm_stages(q_ref, k_ref, v_ref, o_ref, grow_ref, gcol_ref, brow_ref, bcol_ref,
                  cw_ref, cb_ref, ng_ref, qp_scr, kp_scr, c_scr, n_scr, m_scr, result):
    TS = q_ref.shape[1]
    L = CHUNK
    nch = TS // L
    H = M_HEADS
    G = nch * H
    hs = range(H)

    def conv_silu(x, prev, w, b):
        y = b + x * w[M_CONV - 1:M_CONV, :]
        for j in range(M_CONV - 1):
            y = y + _shift_rows_carry(x, prev, M_CONV - 1 - j) * w[j:j + 1, :]
        return y * _sigmoid(y)

    def groups(x):
        return jnp.stack([x[c * L:(c + 1) * L, h * M_HDIM:(h + 1) * M_HDIM]
                          for c in range(nch) for h in hs])

    def per_group(f, *xs):
        return jnp.stack([f(*[x[i] for x in xs]) for i in range(G)])

    def log_sigmoid(x):
        return jnp.minimum(x, 0.0) - jnp.log1p(jnp.exp(-jnp.abs(x)))

    q_raw = q_ref[0]
    q = groups(conv_silu(q_raw, qp_scr[...], cw_ref[:, :M_WIDTH], cb_ref[:, :M_WIDTH])
               * (M_HDIM ** -0.5))
    qp_scr[...] = q_raw[TS - SUBLANES:]
    yield
    k_raw = k_ref[0]
    k = groups(conv_silu(k_raw, kp_scr[...], cw_ref[:, M_WIDTH:], cb_ref[:, M_WIDTH:]))
    kp_scr[...] = k_raw[TS - SUBLANES:]
    v = groups(v_ref[0])
    yield

    ti = lax.broadcasted_iota(jnp.int32, (L, L), 0)
    si = lax.broadcasted_iota(jnp.int32, (L, L), 1)
    causal = si <= ti
    diag = ti == si
    gr = grow_ref[0] + brow_ref[...]
    gc = gcol_ref[0] + bcol_ref[...]
    logi_r = gr[:, :H].reshape(G, 1, L)
    logf_r = log_sigmoid(gr[:, H:].reshape(G, 1, L))
    logi_c = jnp.stack([gc[c, :, h:h + 1] for c in range(nch) for h in hs])
    yield
    b_c = jnp.sum(jnp.where(causal, logf_r, 0.0), axis=2, keepdims=True)
    b_r = jnp.sum(jnp.where(diag, b_c, 0.0), axis=1, keepdims=True)
    g = jnp.sum(logf_r, axis=2, keepdims=True)
    a_r = g - b_r + logi_r
    a_c = g - b_c + logi_c
    m_loc = jnp.max(a_r, axis=2, keepdims=True)
    wa_c = jnp.exp(a_c - m_loc)
    yield
    c_loc = per_group(_bdot_tn, v * wa_c, k)
    yield
    n_loc = jnp.sum(k * wa_c, axis=1, keepdims=True)
    d = jnp.where(causal, b_c - b_r + logi_r, -jnp.inf)
    d_max = jnp.max(d, axis=2, keepdims=True)
    yield
    qk = per_group(_bdot_nt, q, k)
    yield

    c_prev = c_scr[...]
    n_prev = n_scr[...]
    m_prev = m_scr[...]
    c_in, n_in, m_in = [], [], []
    for c in range(nch):
        gs = slice(c * H, (c + 1) * H)
        c_in.append(c_prev)
        n_in.append(n_prev)
        m_in.append(m_prev)
        m_new = jnp.maximum(g[gs] + m_prev, m_loc[gs])
        s_old = jnp.exp(g[gs] + m_prev - m_new)
        s_loc = jnp.exp(m_loc[gs] - m_new)
        c_prev = s_old * c_prev + s_loc * c_loc[gs]
        n_prev = s_old * n_prev + s_loc * n_loc[gs]
        m_prev = m_new
    c_scr[...] = c_prev
    n_scr[...] = n_prev
    m_scr[...] = m_prev
    c_in = jnp.concatenate(c_in, axis=0)
    n_in = jnp.concatenate(n_in, axis=0)
    m_in = jnp.concatenate(m_in, axis=0)
    yield

    inter = b_c + m_in
    m_t = jnp.maximum(inter, d_max)
    s_int = jnp.exp(inter - m_t)
    p = jnp.exp(d - m_t) * qk
    yield
    num = s_int * per_group(_bdot_nt, q, c_in) + per_group(_bdot, p, v)
    yield
    den = (s_int * jnp.sum(q * n_in, axis=2, keepdims=True)
           + jnp.sum(p, axis=2, keepdims=True))
    hh = num / jnp.maximum(jnp.abs(den), jnp.exp(-m_t))
    yield
    mu = jnp.mean(hh, axis=-1, keepdims=True)
    hc = hh - mu
    var = jnp.mean(hc * hc, axis=-1, keepdims=True)
    ng = jnp.stack([ng_ref[:, h * M_HDIM:(h + 1) * M_HDIM] for h in hs] * nch)
    y = (_sigmoid(groups(o_ref[0])) * (hc * lax.rsqrt(var + NORM_EPS)) * ng).astype(BF16)
    result.append(jnp.concatenate(
        [jnp.concatenate([y[c * H + h] for h in hs], axis=1) for c in range(nch)], axis=0))


def _rwkv_stages(pr_ref, pk_ref, pv_ref, pwa_ref, pg_ref,
                 mur_ref, muk_ref, muv_ref, muwa_ref, mug_ref,
                 w0_ref, wup_ref, a0_ref, aup_ref, gup_ref,
                 kkp_ref, ka_ref, rk_ref, gng_ref, gnb_ref,
                 cr_scr, ck_scr, cv_scr, cwa_scr, cg_scr, st_scr, result):
    TS = pr_ref.shape[1]
    L = CHUNK
    N = R_HDIM
    nch = TS // L
    npair = R_WIDTH // LANES
    pairs = range(npair)
    Q = 2 * LANES

    def tshift(p_ref, mu_ref, c_scr):
        p = p_ref[0]
        prev = _shift_rows_carry(p, c_scr[...], 1)
        c_scr[...] = p[TS - SUBLANES:]
        return p + (prev - p) * mu_ref[...]

    hsh = N.bit_length() - 1
    li = lax.broadcasted_iota(jnp.int32, (LANES, LANES), 0)
    lj = lax.broadcasted_iota(jnp.int32, (LANES, LANES), 1)
    same_head = jnp.right_shift(li, hsh) == jnp.right_shift(lj, hsh)
    ones_bd = jnp.where(same_head, 1.0, 0.0).astype(BF16)

    def seg_sum(x):
        outs = []
        for p in range(x.shape[1] // LANES):
            hi, lo = _split2(x[:, p * LANES:(p + 1) * LANES])
            outs.append(_dot(hi, ones_bd) + _dot(lo, ones_bd))
        return jnp.concatenate(outs, axis=1)

    rr = tshift(pr_ref, mur_ref, cr_scr)
    kr = tshift(pk_ref, muk_ref, ck_scr)
    vr = tshift(pv_ref, muv_ref, cv_scr)
    yield
    wa = tshift(pwa_ref, muwa_ref, cwa_scr)
    gd = tshift(pg_ref, mug_ref, cg_scr)
    logw = -DECAY_SCALE * _sigmoid(w0_ref[...] + _bdot(jnp.tanh(wa), wup_ref[...]))
    a = _sigmoid(a0_ref[...] + _bdot(wa, aup_ref[...]))
    g = _bdot(_sigmoid(gd), gup_ref[...])
    yield
    kkraw = kr * kkp_ref[...]
    kk = kkraw / jnp.maximum(jnp.sqrt(seg_sum(kkraw * kkraw)), 1e-12)
    km = kr * (1.0 + (a - 1.0) * ka_ref[...])
    be = kk * a
    yield
    tq = lax.broadcasted_iota(jnp.int32, (Q, Q), 0)
    sq = lax.broadcasted_iota(jnp.int32, (Q, Q), 1)
    tril = jnp.where((jnp.right_shift(tq, hsh) == jnp.right_shift(sq, hsh)) & (sq <= tq),
                     1.0, 0.0).astype(BF16)
    lw_hi, lw_lo = _split2(logw)
    bincl = jnp.concatenate(
        [_dot(tril, lw_hi[q * Q:(q + 1) * Q]) + _dot(tril, lw_lo[q * Q:(q + 1) * Q])
         for q in range(TS // Q)], axis=0)
    e_in = jnp.exp(bincl)
    e_ng = jnp.exp(-bincl)
    kt = kk * jnp.exp(bincl - logw)
    rt = rr * e_in
    bh = be * e_ng
    kh = km * e_ng
    yield

    h0 = lax.broadcasted_iota(jnp.int32, (L, LANES), 1) < N
    ti = lax.broadcasted_iota(jnp.int32, (2 * L, 2 * L), 0)
    si = lax.broadcasted_iota(jnp.int32, (2 * L, 2 * L), 1)
    same_blk = jnp.right_shift(ti, hsh) == jnp.right_shift(si, hsh)
    strict = same_blk & (si < ti)
    incl = same_blk & (si <= ti)

    def stack_heads(x):
        return jnp.concatenate([jnp.where(h0, x, 0.0), jnp.where(h0, 0.0, x)], axis=0)

    def stack_dup(x):
        return jnp.concatenate([x, x], axis=0)

    def comb(x):
        return jnp.where(h0, x[:L], x[L:])

    def off_mask(b):
        sh = (2 * b).bit_length() - 1
        same = jnp.right_shift(ti, sh) == jnp.right_shift(si, sh)
        return same & (jnp.bitwise_and(ti, b) != 0) & (jnp.bitwise_and(si, b) == 0)

    probs = [(p, c) for p in pairs for c in range(nch)]

    def tile(x, p, c):
        return x[c * L:(c + 1) * L, p * LANES:(p + 1) * LANES]

    lk = [stack_heads(tile(kt, p, c)).astype(BF16) for p, c in probs]
    lr = [stack_heads(tile(rt, p, c)).astype(BF16) for p, c in probs]
    rb = [stack_dup(tile(bh, p, c)).astype(BF16) for p, c in probs]
    rk = [stack_dup(tile(kh, p, c)).astype(BF16) for p, c in probs]
    vs = [stack_dup(tile(vr, p, c)).astype(BF16) for p, c in probs]
    yield
    a_bd = [jnp.where(strict, _bdot_nt(x, y), 0.0) for x, y in zip(lk, rb)]
    yield
    bk_bd = [jnp.where(strict, _bdot_nt(x, y), 0.0).astype(BF16) for x, y in zip(lk, rk)]
    yield
    arb_bd = [jnp.where(incl, _bdot_nt(x, y), 0.0).astype(BF16) for x, y in zip(lr, rb)]
    yield
    ark_bd = [jnp.where(incl, _bdot_nt(x, y), 0.0).astype(BF16) for x, y in zip(lr, rk)]
    yield
    eye = jnp.where(ti == si, 1.0, 0.0)
    m1 = off_mask(1)
    t_bd = [eye - jnp.where(m1, x, 0.0) for x in a_bd]
    b = 2
    while b < L:
        mb = off_mask(b)
        offs = [jnp.where(mb, x, 0.0).astype(BF16) for x in a_bd]
        xs = [_bdot(i, o) for i, o in zip(t_bd, offs)]
        yield
        ys = [_bdot(x, i) for x, i in zip(xs, t_bd)]
        t_bd = [i - y for i, y in zip(t_bd, ys)]
        yield
        b *= 2
    bkv = [comb(_bdot(x, y)) for x, y in zip(bk_bd, vs)]
    yield
    xs = [jnp.concatenate([stack_dup(tile(kt, p, c)), stack_dup(z)], axis=1)
          for (p, c), z in zip(probs, bkv)]
    tx = [_bdot(x, y) for x, y in zip(t_bd, xs)]
    yield
    arkv = [comb(_bdot(x, y)) for x, y in zip(ark_bd, vs)]
    w_ch = [[None] * nch for _ in pairs]
    u0_ch = [[None] * nch for _ in pairs]
    arkv_ch = [[None] * nch for _ in pairs]
    arb_ch = [[None] * nch for _ in pairs]
    for i, (p, c) in enumerate(probs):
        w_ch[p][c] = comb(tx[i][:, :LANES])
        u0_ch[p][c] = -comb(tx[i][:, LANES:])
        arkv_ch[p][c] = arkv[i]
        arb_ch[p][c] = arb_bd[i]
    yield

    y_rows = []
    st = [st_scr[p] for p in pairs]
    ones_f = jnp.where(same_head, 1.0, 0.0)
    lsl = [slice(p * LANES, (p + 1) * LANES) for p in pairs]
    for c in range(nch):
        rs = slice(c * L, (c + 1) * L)
        p_end = e_in[c * L + L - 1:c * L + L, :]
        pe = [p_end[:, ls] for ls in lsl]
        rw = [_bdot_nt(jnp.concatenate([rt[rs, lsl[p]], w_ch[p][c]], axis=0), st[p])
              for p in pairs]
        u = [u0_ch[p][c] - rw[p][L:] for p in pairs]
        au = [_bdot(arb_ch[p][c], jnp.concatenate([u[p], u[p]], axis=0)) for p in pairs]
        upd = [_bdot_tn(jnp.concatenate([u[p], vr[rs, lsl[p]]], axis=0),
                        jnp.concatenate([bh[rs, lsl[p]] * pe[p], kh[rs, lsl[p]] * pe[p]], axis=0))
               for p in pairs]
        st = [st[p] * pe[p] + upd[p] * ones_f for p in pairs]
        y_rows.append(jnp.concatenate(
            [rw[p][:L] + jnp.where(h0, au[p][:L], au[p][L:]) + arkv_ch[p][c] for p in pairs],
            axis=1))
        yield
    for p in pairs:
        st_scr[p] = st[p]
    y = jnp.concatenate(y_rows, axis=0)

    inv_n = 1.0 / N
    mu = seg_sum(y) * inv_n
    yc = y - mu
    var = seg_sum(yc * yc) * inv_n
    yn = yc * lax.rsqrt(var + GN_EPS) * gng_ref[...] + gnb_ref[...]
    bonus = seg_sum(rr * km * rk_ref[...]) * vr
    result.append(((yn + bonus) * g).astype(BF16))


def _mixer_kernel(*refs):
    (q_ref, k_ref, v_ref, o_ref, grow_ref, gcol_ref, brow_ref, bcol_ref, cw_ref, cb_ref, ng_ref,
     pr_ref, pk_ref, pv_ref, pwa_ref, pg_ref, mur_ref, muk_ref, muv_ref, muwa_ref, mug_ref,
     w0_ref, wup_ref, a0_ref, aup_ref, gup_ref, kkp_ref, ka_ref, rk_ref, gng_ref, gnb_ref,
     x_ref, wout_ref, out_ref,
     qp_scr, kp_scr, c_scr, n_scr, m_scr,
     cr_scr, ck_scr, cv_scr, cwa_scr, cg_scr, st_scr) = refs

    @pl.when(pl.program_id(1) == 0)
    def _():
        for scr in (qp_scr, kp_scr, c_scr, n_scr, m_scr,
                    cr_scr, ck_scr, cv_scr, cwa_scr, cg_scr, st_scr):
            scr[...] = jnp.zeros_like(scr)

    y_m, y_r = [], []
    mlstm = _mlstm_stages(q_ref, k_ref, v_ref, o_ref, grow_ref, gcol_ref, brow_ref, bcol_ref,
                          cw_ref, cb_ref, ng_ref, qp_scr, kp_scr, c_scr, n_scr, m_scr, y_m)
    rwkv = _rwkv_stages(pr_ref, pk_ref, pv_ref, pwa_ref, pg_ref,
                        mur_ref, muk_ref, muv_ref, muwa_ref, mug_ref,
                        w0_ref, wup_ref, a0_ref, aup_ref, gup_ref,
                        kkp_ref, ka_ref, rk_ref, gng_ref, gnb_ref,
                        cr_scr, ck_scr, cv_scr, cwa_scr, cg_scr, st_scr, y_r)
    for tag in "RRRRR" + "RMR" * 9 + "RM" * 4:
        next(rwkv if tag == "R" else mlstm, None)
    for stream in (mlstm, rwkv):
        for _ in stream:
            pass
    out_ref[0] = (x_ref[0] + _dot(y_m[0], wout_ref[:M_WIDTH, :])
                  + _dot(y_r[0], wout_ref[M_WIDTH:, :]))


def _mixer(p_all, g_row, g_col, b_row, b_col, conv_w, conv_b, norm_g,
           mu, w0, w_up, a0, a_up, g_up, kkp, ka, rk, gn_g, gn_b, x, w_out):
    B, S, _ = p_all.shape
    D = x.shape[2]
    ts = min(SEQ_TILE, S)
    nch = ts // CHUNK
    npair = R_WIDTH // LANES
    seq = lambda w, off: pl.BlockSpec((1, ts, w), lambda b, s, off=off: (b, s, off))
    vec = lambda w, off: pl.BlockSpec((1, w), lambda b, s, off=off: (0, off))
    full = lambda a: pl.BlockSpec(a.shape, lambda b, s: (0,) * a.ndim)
    return pl.pallas_call(
        _mixer_kernel,
        out_shape=jax.ShapeDtypeStruct((B, S, D), F32),
        grid=(B, S // ts),
        in_specs=[
            seq(M_WIDTH, OFF_MQ // M_WIDTH), seq(M_WIDTH, OFF_MK // M_WIDTH),
            seq(M_WIDTH, OFF_MV // M_WIDTH), seq(M_WIDTH, OFF_MO // M_WIDTH),
            pl.BlockSpec((1, nch, 2 * M_HEADS, 1, CHUNK), lambda b, s: (b, s, 0, 0, 0)),
            pl.BlockSpec((1, nch, CHUNK, 2 * M_HEADS), lambda b, s: (b, s, 0, 0)),
            full(b_row), full(b_col), full(conv_w), full(conv_b), full(norm_g),
            seq(R_WIDTH, OFF_RR // R_WIDTH), seq(R_WIDTH, OFF_RK // R_WIDTH),
            seq(R_WIDTH, OFF_RV // R_WIDTH), seq(LANES, OFF_RWA // LANES),
            seq(GLORA_PAD, OFF_RG // GLORA_PAD),
            vec(R_WIDTH, OFF_RR // R_WIDTH), vec(R_WIDTH, OFF_RK // R_WIDTH),
            vec(R_WIDTH, OFF_RV // R_WIDTH), vec(LANES, OFF_RWA // LANES),
            vec(GLORA_PAD, OFF_RG // GLORA_PAD),
            full(w0), full(w_up), full(a0), full(a_up), full(g_up),
            full(kkp), full(ka), full(rk), full(gn_g), full(gn_b),
            seq(D, 0),
            pl.BlockSpec(w_out.shape, lambda b, s: (0, 0), pipeline_mode=pl.Buffered(1)),
        ],
        out_specs=pl.BlockSpec((1, ts, D), lambda b, s: (b, s, 0)),
        scratch_shapes=[pltpu.VMEM((SUBLANES, M_WIDTH), F32), pltpu.VMEM((SUBLANES, M_WIDTH), F32),
                        pltpu.VMEM((M_HEADS, M_HDIM, M_HDIM), F32),
                        pltpu.VMEM((M_HEADS, 1, M_HDIM), F32),
                        pltpu.VMEM((M_HEADS, 1, 1), F32),
                        pltpu.VMEM((SUBLANES, R_WIDTH), F32), pltpu.VMEM((SUBLANES, R_WIDTH), F32),
                        pltpu.VMEM((SUBLANES, R_WIDTH), F32), pltpu.VMEM((SUBLANES, LANES), F32),
                        pltpu.VMEM((SUBLANES, GLORA_PAD), F32),
                        pltpu.VMEM((npair, LANES, LANES), F32)],
        compiler_params=pltpu.CompilerParams(
            dimension_semantics=("parallel", "arbitrary"), vmem_limit_bytes=VMEM_LIMIT),
        name="mixer",
    )(p_all, p_all, p_all, p_all, g_row, g_col, b_row, b_col, conv_w, conv_b, norm_g,
      p_all, p_all, p_all, p_all, p_all, mu, mu, mu, mu, mu,
      w0, w_up, a0, a_up, g_up, kkp, ka, rk, gn_g, gn_b, x, w_out)


def _xattn_kernel(x_ref, g_ref, wq_ref, kv_ref, wo_ref, o_ref):
    D = x_ref.shape[1]
    x = x_ref[...]
    ms = jnp.mean(x * x, axis=-1, keepdims=True)
    h = (x * lax.rsqrt(ms + NORM_EPS) * g_ref[...]).astype(BF16)
    q = _dot(h, wq_ref[...]).astype(BF16)
    hsl = [slice(hd * X_HDIM, (hd + 1) * X_HDIM) for hd in range(D // X_HDIM)]
    s = [lax.dot_general(q[:, ls], kv_ref[0, :, ls], (((1,), (1,)), ((), ())),
                         preferred_element_type=F32) * (X_HDIM ** -0.5) for ls in hsl]
    e = [jnp.exp(si - jnp.max(si, axis=-1, keepdims=True)) for si in s]
    p = [(ei / jnp.sum(ei, axis=-1, keepdims=True)).astype(BF16) for ei in e]
    heads = [_dot(pi, kv_ref[0, :, D + ls.start:D + ls.stop]).astype(BF16)
             for pi, ls in zip(p, hsl)]
    o_ref[...] = x + _dot(jnp.concatenate(heads, axis=1), wo_ref[...])


def _xattn(x, gain, wq, kv, wo, seq_len, tm=512):
    T, D = x.shape
    M = kv.shape[1]
    tm = min(tm, seq_len)
    per_seq = seq_len // tm
    const = lambda a: pl.BlockSpec(a.shape, lambda i: (0,) * a.ndim, pipeline_mode=pl.Buffered(1))
    return pl.pallas_call(
        _xattn_kernel,
        out_shape=jax.ShapeDtypeStruct((T, D), F32),
        grid=(T // tm,),
        in_specs=[pl.BlockSpec((tm, D), lambda i: (i, 0)),
                  pl.BlockSpec((1, D), lambda i: (0, 0)),
                  const(wq),
                  pl.BlockSpec((1, M, 2 * D), lambda i: (i // per_seq, 0, 0)),
                  const(wo)],
        out_specs=pl.BlockSpec((tm, D), lambda i: (i, 0)),
        compiler_params=pltpu.CompilerParams(
            dimension_semantics=("parallel",), vmem_limit_bytes=VMEM_LIMIT),
        name="xattn",
    )(x, gain.reshape(1, D), wq, kv, wo)


def _ffn_kernel(*refs, per_seq, tc, norm_out):
    x_ref, g_ref, wup_ref, cw_ref, cb_ref, wdn_ref = refs[:6]
    gout_ref = refs[6] if norm_out else None
    o_ref, tail_scr = refs[-2:]
    tm = x_ref.shape[0]

    @pl.when(lax.rem(pl.program_id(0), per_seq) == 0)
    def _():
        tail_scr[...] = jnp.zeros_like(tail_scr)

    x = x_ref[...]
    ms = jnp.mean(x * x, axis=-1, keepdims=True)
    h = (x * lax.rsqrt(ms + NORM_EPS) * g_ref[...]).astype(BF16)
    chunks = [slice(c * tc, (c + 1) * tc) for c in range(D_FF // tc)]
    gates = [_dot(h, wup_ref[:, cols]) for cols in chunks]
    vals = [_dot(h, wup_ref[:, D_FF + cols.start:D_FF + cols.stop]) for cols in chunks]
    acts = []
    for cols, gate, val in zip(chunks, gates, vals):
        prev = tail_scr[:, cols]
        y = cb_ref[:, cols] + gate * cw_ref[FFN_CONV - 1:FFN_CONV, cols]
        for j in range(FFN_CONV - 1):
            y = y + _shift_rows_carry(gate, prev, FFN_CONV - 1 - j) * cw_ref[j:j + 1, cols]
        tail_scr[:, cols] = gate[tm - SUBLANES:]
        acts.append((y * _sigmoid(y) * val).astype(BF16))
    acc = x
    for cols, act in zip(chunks, acts):
        acc = acc + _dot(act, wdn_ref[cols, :])
    if norm_out:
        ms = jnp.mean(acc * acc, axis=-1, keepdims=True)
        acc = acc * lax.rsqrt(ms + NORM_EPS) * gout_ref[...]
    o_ref[...] = acc


def _ffn(x, gain, w_up, conv_w, conv_b, w_down, seq_len, out_gain=None, tm=256, tc=256):
    T, D = x.shape
    tm = min(tm, seq_len)
    conv_b = conv_b.reshape(1, D_FF)
    norm_out = out_gain is not None
    const = lambda a: pl.BlockSpec(a.shape, lambda i: (0,) * a.ndim, pipeline_mode=pl.Buffered(1))
    consts = [w_up, conv_w, conv_b, w_down] + ([out_gain.reshape(1, D)] if norm_out else [])
    return pl.pallas_call(
        functools.partial(_ffn_kernel, per_seq=seq_len // tm, tc=tc, norm_out=norm_out),
        out_shape=jax.ShapeDtypeStruct((T, D), F32),
        grid=(T // tm,),
        in_specs=[pl.BlockSpec((tm, D), lambda i: (i, 0)),
                  pl.BlockSpec((1, D), lambda i: (0, 0)),
                  ] + [const(a) for a in consts],
        out_specs=pl.BlockSpec((tm, D), lambda i: (i, 0)),
        scratch_shapes=[pltpu.VMEM((SUBLANES, D_FF), F32)],
        compiler_params=pltpu.CompilerParams(
            dimension_semantics=("arbitrary",), vmem_limit_bytes=VMEM_LIMIT),
        name="ffn",
    )(x, gain.reshape(1, D), *consts)


def _pad_cols(w, n):
    return jnp.pad(w, ((0, 0), (0, n - w.shape[1])))


def _relayout_in(w):
    m_main = w[:, :4 * M_WIDTH]
    m_gate = w[:, 4 * M_WIDTH:4 * M_WIDTH + 2 * M_HEADS]
    r0 = 4 * M_WIDTH + 2 * M_HEADS
    r_main = w[:, r0:r0 + 3 * R_WIDTH + R_DECAY_LORA + R_AAA_LORA]
    r_gate = w[:, r0 + 3 * R_WIDTH + R_DECAY_LORA + R_AAA_LORA:]
    return jnp.concatenate(
        [m_main, r_main, _pad_cols(m_gate, GATE_PAD), _pad_cols(r_gate, GLORA_PAD)], axis=1)


def kernel(x, mem, norm_mix, w_in, m_conv_w, m_conv_b, m_gate_b, m_norm_g, r_mu, r_w0,
           r_w_up, r_a0, r_a_up, r_g_up, r_kk, r_ka, r_rk, r_gn_g, r_gn_b, w_out,
           norm_x, norm_mem, x_wq, x_wkv, x_wo, norm_ffn, f_up, f_conv_w, f_conv_b,
           f_down, norm_final):
    B, S, D = x.shape
    M = mem.shape[1]
    depth = w_in.shape[0]
    T = B * S
    nc = S // CHUNK
    xf = x.reshape(T, D)
    memf = mem.reshape(B * M, D)
    row = lambda a: a.reshape(1, -1)

    for l in range(depth):
        w_in_p = _relayout_in(w_in[l]).astype(BF16)
        p_all = _mm(xf, w_in_p, gain=norm_mix[l]).reshape(B, S, IN_COLS_P)

        g_col = p_all[:, :, OFF_MG:OFF_MG + 2 * M_HEADS].reshape(B, nc, CHUNK, 2 * M_HEADS)
        g_row = g_col.swapaxes(-1, -2)[:, :, :, None, :]
        mu = _relayout_in(jnp.pad(row(r_mu[l]), ((0, 0), (4 * M_WIDTH + 2 * M_HEADS, 0))))
        w_up = jnp.pad(r_w_up[l], ((0, R_AAA_LORA), (0, 0))).astype(BF16)
        a_up = jnp.pad(r_a_up[l], ((R_DECAY_LORA, 0), (0, 0))).astype(BF16)
        g_up = jnp.pad(r_g_up[l], ((0, GLORA_PAD - R_GATE_LORA), (0, 0))).astype(BF16)
        xf = _mixer(p_all, g_row, g_col, m_gate_b[l].reshape(-1, 1, 1), row(m_gate_b[l]),
                    m_conv_w[l], row(m_conv_b[l]), row(m_norm_g[l]),
                    mu, row(r_w0[l]), w_up, row(r_a0[l]), a_up, g_up,
                    row(r_kk[l]), row(r_ka[l]), row(r_rk[l]), row(r_gn_g[l]), row(r_gn_b[l]),
                    xf.reshape(B, S, D), w_out[l].astype(BF16)).reshape(T, D)

        kv = _mm(memf, x_wkv[l].astype(BF16), gain=norm_mem[l], out_dtype=BF16)
        xf = _xattn(xf, norm_x[l], x_wq[l].astype(BF16), kv.reshape(B, M, 2 * D),
                    x_wo[l].astype(BF16), S)

        xf = _ffn(xf, norm_ffn[l], f_up[l].astype(BF16), f_conv_w[l], f_conv_b[l],
                  f_down[l].astype(BF16), S, out_gain=norm_final if l == depth - 1 else None)

    return xf.reshape(B, S, D)
```

```python
import functools
import math

import jax
import jax.numpy as jnp
from jax import lax
from jax.experimental import pallas as pl
from jax.experimental.pallas import tpu as pltpu

F32 = jnp.float32
BF16 = jnp.bfloat16

D_MODEL = 1024
M_WIDTH = 512
M_HEADS = 4
M_HDIM = 128
M_CONV = 4
R_WIDTH = 512
R_HDIM = 64
R_HEADS = 8
R_DECAY_LORA = 64
R_AAA_LORA = 64
R_GATE_LORA = 160
DECAY_SCALE = math.exp(-0.5)
X_HEADS = 4
X_HDIM = 256
D_FF = 2816
FFN_CONV = 3
NORM_EPS = 1e-6
GN_EPS = 64e-5
CHUNK = 64

LANES = 128
SUBLANES = 8
GATE_PAD = LANES
GLORA_PAD = 2 * LANES
OFF_MQ, OFF_MK, OFF_MV, OFF_MO = 0, 512, 1024, 1536
OFF_RR = 2048
OFF_RK = OFF_RR + R_WIDTH
OFF_RV = OFF_RK + R_WIDTH
OFF_RWA = OFF_RV + R_WIDTH
OFF_MG = OFF_RWA + LANES
OFF_RG = OFF_MG + GATE_PAD
IN_COLS_P = OFF_RG + GLORA_PAD

SEQ_TILE = 256
VMEM_LIMIT = 48 * 1024 * 1024


def _dot(a, b):
    return jnp.dot(a, b, preferred_element_type=F32)


def _bdot(a, b):
    return jnp.dot(a.astype(BF16), b.astype(BF16), preferred_element_type=F32)


def _bdot_nt(a, b):
    return lax.dot_general(a.astype(BF16), b.astype(BF16), (((1,), (1,)), ((), ())),
                           preferred_element_type=F32)


def _bdot_tn(a, b):
    return lax.dot_general(a.astype(BF16), b.astype(BF16), (((0,), (0,)), ((), ())),
                           preferred_element_type=F32)


def _split2(x):
    hi = x.astype(BF16)
    lo = (x - hi.astype(F32)).astype(BF16)
    return hi, lo


def _sigmoid(x):
    return 0.5 * jnp.tanh(0.5 * x) + 0.5


def _shift_rows_carry(x, prev, sh):
    ext = jnp.concatenate([prev, x], axis=0)
    return pltpu.roll(ext, sh, 0)[SUBLANES:]


def _rms_bf16(x, gain):
    ms = jnp.mean(x * x, axis=-1, keepdims=True)
    return (x * lax.rsqrt(ms + NORM_EPS) * gain).astype(BF16)


def _next_tile_spec(tm, K, n):
    return pl.BlockSpec((tm, K), lambda i: (jnp.minimum(i + 1, n - 1), 0))


def _mm_kernel(x_ref, xn_ref, w_ref, g_ref, o_ref, h_scr, *, tn):
    i = pl.program_id(0)

    @pl.when(i == 0)
    def _():
        h_scr[0] = _rms_bf16(x_ref[...], g_ref[...])

    slot = lax.rem(i, 2)
    for c in range(w_ref.shape[1] // tn):
        cols = slice(c * tn, (c + 1) * tn)
        o_ref[:, cols] = _dot(h_scr[slot], w_ref[:, cols]).astype(o_ref.dtype)
        if c == 0:
            h_scr[1 - slot] = _rms_bf16(xn_ref[...], g_ref[...])


def _mm_tile_rows(K, N, x_bytes, out_bytes):
    budget = (VMEM_LIMIT * 3) // 4 - K * N * 2
    for tm in (512, 256, 128):
        if tm * (4 * K * x_bytes + 2 * N * out_bytes + 2 * K * 2) <= budget:
            return tm
    raise ValueError("weight does not fit in VMEM")


def _mm(x, w, gain, out_dtype=F32, tn=512):
    T, K = x.shape
    N = w.shape[1]
    tm = min(T, _mm_tile_rows(K, N, x.dtype.itemsize, jnp.dtype(out_dtype).itemsize))
    tn = min(tn, N)
    assert T % tm == 0 and N % tn == 0
    n = T // tm
    return pl.pallas_call(
        functools.partial(_mm_kernel, tn=tn),
        out_shape=jax.ShapeDtypeStruct((T, N), out_dtype),
        grid=(n,),
        in_specs=[pl.BlockSpec((tm, K), lambda i: (i, 0)),
                  _next_tile_spec(tm, K, n),
                  pl.BlockSpec((K, N), lambda i: (0, 0), pipeline_mode=pl.Buffered(1)),
                  pl.BlockSpec((1, K), lambda i: (0, 0))],
        out_specs=pl.BlockSpec((tm, N), lambda i: (i, 0)),
        scratch_shapes=[pltpu.VMEM((2, tm, K), BF16)],
        compiler_params=pltpu.CompilerParams(
            dimension_semantics=("arbitrary",), vmem_limit_bytes=VMEM_LIMIT),
        name="mm",
    )(x, x, w, gain.reshape(1, K))


def _mlstm_stages(q_ref, k_ref, v_ref, o_ref, grow_ref, gcol_ref, brow_ref, bcol_ref,
                  cw_ref, cb_ref, ng_ref, qp_scr, kp_scr, c_scr, n_scr, m_scr, result):
    TS = q_ref.shape[1]
    L = CHUNK
    nch = TS // L
    H = M_HEADS
    G = nch * H
    hs = range(H)

    def conv_silu(x, prev, w, b):
        y = b + x * w[M_CONV - 1:M_CONV, :]
        for j in range(M_CONV - 1):
            y = y + _shift_rows_carry(x, prev, M_CONV - 1 - j) * w[j:j + 1, :]
        return y * _sigmoid(y)

    def groups(x):
        return jnp.stack([x[c * L:(c + 1) * L, h * M_HDIM:(h + 1) * M_HDIM]
                          for c in range(nch) for h in hs])

    def per_group(f, *xs):
        return jnp.stack([f(*[x[i] for x in xs]) for i in range(G)])

    def log_sigmoid(x):
        return jnp.minimum(x, 0.0) - jnp.log1p(jnp.exp(-jnp.abs(x)))

    q_raw = q_ref[0]
    q = groups(conv_silu(q_raw, qp_scr[...], cw_ref[:, :M_WIDTH], cb_ref[:, :M_WIDTH])
               * (M_HDIM ** -0.5))
    qp_scr[...] = q_raw[TS - SUBLANES:]
    yield
    k_raw = k_ref[0]
    k = groups(conv_silu(k_raw, kp_scr[...], cw_ref[:, M_WIDTH:], cb_ref[:, M_WIDTH:]))
    kp_scr[...] = k_raw[TS - SUBLANES:]
    v = groups(v_ref[0])
    yield

    ti = lax.broadcasted_iota(jnp.int32, (L, L), 0)
    si = lax.broadcasted_iota(jnp.int32, (L, L), 1)
    causal = si <= ti
    diag = ti == si
    gr = grow_ref[0] + brow_ref[...]
    gc = gcol_ref[0] + bcol_ref[...]
    logi_r = gr[:, :H].reshape(G, 1, L)
    logf_r = log_sigmoid(gr[:, H:].reshape(G, 1, L))
    logi_c = jnp.stack([gc[c, :, h:h + 1] for c in range(nch) for h in hs])
    yield
    b_c = jnp.sum(jnp.where(causal, logf_r, 0.0), axis=2, keepdims=True)
    b_r = jnp.sum(jnp.where(diag, b_c, 0.0), axis=1, keepdims=True)
    g = jnp.sum(logf_r, axis=2, keepdims=True)
    a_r = g - b_r + logi_r
    a_c = g - b_c + logi_c
    m_loc = jnp.max(a_r, axis=2, keepdims=True)
    wa_c = jnp.exp(a_c - m_loc)
    yield
    c_loc = per_group(_bdot_tn, v * wa_c, k)
    yield
    n_loc = jnp.sum(k * wa_c, axis=1, keepdims=True)
    d = jnp.where(causal, b_c - b_r + logi_r, -jnp.inf)
    d_max = jnp.max(d, axis=2, keepdims=True)
    yield
    qk = per_group(_bdot_nt, q, k)
    yield

    c_prev = c_scr[...]
    n_prev = n_scr[...]
    m_prev = m_scr[...]
    c_in, n_in, m_in = [], [], []
    for c in range(nch):
        gs = slice(c * H, (c + 1) * H)
        c_in.append(c_prev)
        n_in.append(n_prev)
        m_in.append(m_prev)
        m_new = jnp.maximum(g[gs] + m_prev, m_loc[gs])
        s_old = jnp.exp(g[gs] + m_prev - m_new)
        s_loc = jnp.exp(m_loc[gs] - m_new)
        c_prev = s_old * c_prev + s_loc * c_loc[gs]
        n_prev = s_old * n_prev + s_loc * n_loc[gs]
        m_prev = m_new
    c_scr[...] = c_prev
    n_scr[...] = n_prev
    m_scr[...] = m_prev
    c_in = jnp.concatenate(c_in, axis=0)
    n_in = jnp.concatenate(n_in, axis=0)
    m_in = jnp.concatenate(m_in, axis=0)
    yield

    inter = b_c + m_in
    m_t = jnp.maximum(inter, d_max)
    s_int = jnp.exp(inter - m_t)
    p = jnp.exp(d - m_t) * qk
    yield
    num = s_int * per_group(_bdot_nt, q, c_in) + per_group(_bdot, p, v)
    yield
    den = (s_int * jnp.sum(q * n_in, axis=2, keepdims=True)
           + jnp.sum(p, axis=2, keepdims=True))
    hh = num / jnp.maximum(jnp.abs(den), jnp.exp(-m_t))
    yield
    mu = jnp.mean(hh, axis=-1, keepdims=True)
    hc = hh - mu
    var = jnp.mean(hc * hc, axis=-1, keepdims=True)
    ng = jnp.stack([ng_ref[:, h * M_HDIM:(h + 1) * M_HDIM] for h in hs] * nch)
    y = (_sigmoid(groups(o_ref[0])) * (hc * lax.rsqrt(var + NORM_EPS)) * ng).astype(BF16)
    result.append(jnp.concatenate(
        [jnp.concatenate([y[c * H + h] for h in hs], axis=1) for c in range(nch)], axis=0))


def _rwkv_stages(pr_ref, pk_ref, pv_ref, pwa_ref, pg_ref,
                 mur_ref, muk_ref, muv_ref, muwa_ref, mug_ref,
                 w0_ref, wup_ref, a0_ref, aup_ref, gup_ref,
                 kkp_ref, ka_ref, rk_ref, gng_ref, gnb_ref,
                 cr_scr, ck_scr, cv_scr, cwa_scr, cg_scr, st_scr, result):
    TS = pr_ref.shape[1]
    L = CHUNK
    N = R_HDIM
    nch = TS // L
    npair = R_WIDTH // LANES
    pairs = range(npair)
    Q = 2 * LANES

    def tshift(p_ref, mu_ref, c_scr):
        p = p_ref[0]
        prev = _shift_rows_carry(p, c_scr[...], 1)
        c_scr[...] = p[TS - SUBLANES:]
        return p + (prev - p) * mu_ref[...]

    hsh = N.bit_length() - 1
    li = lax.broadcasted_iota(jnp.int32, (LANES, LANES), 0)
    lj = lax.broadcasted_iota(jnp.int32, (LANES, LANES), 1)
    same_head = jnp.right_shift(li, hsh) == jnp.right_shift(lj, hsh)
    ones_bd = jnp.where(same_head, 1.0, 0.0).astype(BF16)

    def seg_sum(x):
        outs = []
        for p in range(x.shape[1] // LANES):
            hi, lo = _split2(x[:, p * LANES:(p + 1) * LANES])
            outs.append(_dot(hi, ones_bd) + _dot(lo, ones_bd))
        return jnp.concatenate(outs, axis=1)

    rr = tshift(pr_ref, mur_ref, cr_scr)
    kr = tshift(pk_ref, muk_ref, ck_scr)
    vr = tshift(pv_ref, muv_ref, cv_scr)
    yield
    wa = tshift(pwa_ref, muwa_ref, cwa_scr)
    gd = tshift(pg_ref, mug_ref, cg_scr)
    logw = -DECAY_SCALE * _sigmoid(w0_ref[...] + _bdot(jnp.tanh(wa), wup_ref[...]))
    a = _sigmoid(a0_ref[...] + _bdot(wa, aup_ref[...]))
    g = _bdot(_sigmoid(gd), gup_ref[...])
    yield
    kkraw = kr * kkp_ref[...]
    kk = kkraw / jnp.maximum(jnp.sqrt(seg_sum(kkraw * kkraw)), 1e-12)
    km = kr * (1.0 + (a - 1.0) * ka_ref[...])
    be = kk * a
    yield
    tq = lax.broadcasted_iota(jnp.int32, (Q, Q), 0)
    sq = lax.broadcasted_iota(jnp.int32, (Q, Q), 1)
    tril = jnp.where((jnp.right_shift(tq, hsh) == jnp.right_shift(sq, hsh)) & (sq <= tq),
                     1.0, 0.0).astype(BF16)
    lw_hi, lw_lo = _split2(logw)
    bincl = jnp.concatenate(
        [_dot(tril, lw_hi[q * Q:(q + 1) * Q]) + _dot(tril, lw_lo[q * Q:(q + 1) * Q])
         for q in range(TS // Q)], axis=0)
    e_in = jnp.exp(bincl)
    e_ng = jnp.exp(-bincl)
    kt = kk * jnp.exp(bincl - logw)
    rt = rr * e_in
    bh = be * e_ng
    kh = km * e_ng
    yield

    h0 = lax.broadcasted_iota(jnp.int32, (L, LANES), 1) < N
    ti = lax.broadcasted_iota(jnp.int32, (2 * L, 2 * L), 0)
    si = lax.broadcasted_iota(jnp.int32, (2 * L, 2 * L), 1)
    same_blk = jnp.right_shift(ti, hsh) == jnp.right_shift(si, hsh)
    strict = same_blk & (si < ti)
    incl = same_blk & (si <= ti)

    def stack_heads(x):
        return jnp.concatenate([jnp.where(h0, x, 0.0), jnp.where(h0, 0.0, x)], axis=0)

    def stack_dup(x):
        return jnp.concatenate([x, x], axis=0)

    def comb(x):
        return jnp.where(h0, x[:L], x[L:])

    def off_mask(b):
        sh = (2 * b).bit_length() - 1
        same = jnp.right_shift(ti, sh) == jnp.right_shift(si, sh)
        return same & (jnp.bitwise_and(ti, b) != 0) & (jnp.bitwise_and(si, b) == 0)

    probs = [(p, c) for p in pairs for c in range(nch)]

    def tile(x, p, c):
        return x[c * L:(c + 1) * L, p * LANES:(p + 1) * LANES]

    lk = [stack_heads(tile(kt, p, c)).astype(BF16) for p, c in probs]
    lr = [stack_heads(tile(rt, p, c)).astype(BF16) for p, c in probs]
    rb = [stack_dup(tile(bh, p, c)).astype(BF16) for p, c in probs]
    rk = [stack_dup(tile(kh, p, c)).astype(BF16) for p, c in probs]
    vs = [stack_dup(tile(vr, p, c)).astype(BF16) for p, c in probs]
    yield
    a_bd = [jnp.where(strict, _bdot_nt(x, y), 0.0) for x, y in zip(lk, rb)]
    yield
    bk_bd = [jnp.where(strict, _bdot_nt(x, y), 0.0).astype(BF16) for x, y in zip(lk, rk)]
    yield
    arb_bd = [jnp.where(incl, _bdot_nt(x, y), 0.0).astype(BF16) for x, y in zip(lr, rb)]
    yield
    ark_bd = [jnp.where(incl, _bdot_nt(x, y), 0.0).astype(BF16) for x, y in zip(lr, rk)]
    yield
    eye = jnp.where(ti == si, 1.0, 0.0)
    m1 = off_mask(1)
    t_bd = [eye - jnp.where(m1, x, 0.0) for x in a_bd]
    b = 2
    while b < L:
        mb = off_mask(b)
        offs = [jnp.where(mb, x, 0.0).astype(BF16) for x in a_bd]
        xs = [_bdot(i, o) for i, o in zip(t_bd, offs)]
        yield
        ys = [_bdot(x, i) for x, i in zip(xs, t_bd)]
        t_bd = [i - y for i, y in zip(t_bd, ys)]
        yield
        b *= 2
    bkv = [comb(_bdot(x, y)) for x, y in zip(bk_bd, vs)]
    yield
    xs = [jnp.concatenate([stack_dup(tile(kt, p, c)), stack_dup(z)], axis=1)
          for (p, c), z in zip(probs, bkv)]
    tx = [_bdot(x, y) for x, y in zip(t_bd, xs)]
    yield
    arkv = [comb(_bdot(x, y)) for x, y in zip(ark_bd, vs)]
    w_ch = [[None] * nch for _ in pairs]
    u0_ch = [[None] * nch for _ in pairs]
    arkv_ch = [[None] * nch for _ in pairs]
    arb_ch = [[None] * nch for _ in pairs]
    for i, (p, c) in enumerate(probs):
        w_ch[p][c] = comb(tx[i][:, :LANES])
        u0_ch[p][c] = -comb(tx[i][:, LANES:])
        arkv_ch[p][c] = arkv[i]
        arb_ch[p][c] = arb_bd[i]
    yield

    y_rows = []
    st = [st_scr[p] for p in pairs]
    ones_f = jnp.where(same_head, 1.0, 0.0)
    lsl = [slice(p * LANES, (p + 1) * LANES) for p in pairs]
    for c in range(nch):
        rs = slice(c * L, (c + 1) * L)
        p_end = e_in[c * L + L - 1:c * L + L, :]
        pe = [p_end[:, ls] for ls in lsl]
        rw = [_bdot_nt(jnp.concatenate([rt[rs, lsl[p]], w_ch[p][c]], axis=0), st[p])
              for p in pairs]
        u = [u0_ch[p][c] - rw[p][L:] for p in pairs]
        au = [_bdot(arb_ch[p][c], jnp.concatenate([u[p], u[p]], axis=0)) for p in pairs]
        upd = [_bdot_tn(jnp.concatenate([u[p], vr[rs, lsl[p]]], axis=0),
                        jnp.concatenate([bh[rs, lsl[p]] * pe[p], kh[rs, lsl[p]] * pe[p]], axis=0))
               for p in pairs]
        st = [st[p] * pe[p] + upd[p] * ones_f for p in pairs]
        y_rows.append(jnp.concatenate(
            [rw[p][:L] + jnp.where(h0, au[p][:L], au[p][L:]) + arkv_ch[p][c] for p in pairs],
            axis=1))
        yield
    for p in pairs:
        st_scr[p] = st[p]
    y = jnp.concatenate(y_rows, axis=0)

    inv_n = 1.0 / N
    mu = seg_sum(y) * inv_n
    yc = y - mu
    var = seg_sum(yc * yc) * inv_n
    yn = yc * lax.rsqrt(var + GN_EPS) * gng_ref[...] + gnb_ref[...]
    bonus = seg_sum(rr * km * rk_ref[...]) * vr
    result.append(((yn + bonus) * g).astype(BF16))


def _mixer_kernel(*refs):
    (q_ref, k_ref, v_ref, o_ref, grow_ref, gcol_ref, brow_ref, bcol_ref, cw_ref, cb_ref, ng_ref,
     pr_ref, pk_ref, pv_ref, pwa_ref, pg_ref, mur_ref, muk_ref, muv_ref, muwa_ref, mug_ref,
     w0_ref, wup_ref, a0_ref, aup_ref, gup_ref, kkp_ref, ka_ref, rk_ref, gng_ref, gnb_ref,
     x_ref, wout_ref, out_ref,
     qp_scr, kp_scr, c_scr, n_scr, m_scr,
     cr_scr, ck_scr, cv_scr, cwa_scr, cg_scr, st_scr) = refs

    @pl.when(pl.program_id(1) == 0)
    def _():
        for scr in (qp_scr, kp_scr, c_scr, n_scr, m_scr,
                    cr_scr, ck_scr, cv_scr, cwa_scr, cg_scr, st_scr):
            scr[...] = jnp.zeros_like(scr)

    y_m, y_r = [], []
    mlstm = _mlstm_stages(q_ref, k_ref, v_ref, o_ref, grow_ref, gcol_ref, brow_ref, bcol_ref,
                          cw_ref, cb_ref, ng_ref, qp_scr, kp_scr, c_scr, n_scr, m_scr, y_m)
    rwkv = _rwkv_stages(pr_ref, pk_ref, pv_ref, pwa_ref, pg_ref,
                        mur_ref, muk_ref, muv_ref, muwa_ref, mug_ref,
                        w0_ref, wup_ref, a0_ref, aup_ref, gup_ref,
                        kkp_ref, ka_ref, rk_ref, gng_ref, gnb_ref,
                        cr_scr, ck_scr, cv_scr, cwa_scr, cg_scr, st_scr, y_r)
    for tag in "RRRRR" + "RMR" * 9 + "RM" * 4:
        next(rwkv if tag == "R" else mlstm, None)
    for stream in (mlstm, rwkv):
        for _ in stream:
            pass
    out_ref[0] = (x_ref[0] + _dot(y_m[0], wout_ref[:M_WIDTH, :])
                  + _dot(y_r[0], wout_ref[M_WIDTH:, :]))


def _mixer(p_all, g_row, g_col, b_row, b_col, conv_w, conv_b, norm_g,
           mu, w0, w_up, a0, a_up, g_up, kkp, ka, rk, gn_g, gn_b, x, w_out):
    B, S, _ = p_all.shape
    D = x.shape[2]
    ts = min(SEQ_TILE, S)
    nch = ts // CHUNK
    npair = R_WIDTH // LANES
    seq = lambda w, off: pl.BlockSpec((1, ts, w), lambda b, s, off=off: (b, s, off))
    vec = lambda w, off: pl.BlockSpec((1, w), lambda b, s, off=off: (0, off))
    full = lambda a: pl.BlockSpec(a.shape, lambda b, s: (0,) * a.ndim)
    return pl.pallas_call(
        _mixer_kernel,
        out_shape=jax.ShapeDtypeStruct((B, S, D), F32),
        grid=(B, S // ts),
        in_specs=[
            seq(M_WIDTH, OFF_MQ // M_WIDTH), seq(M_WIDTH, OFF_MK // M_WIDTH),
            seq(M_WIDTH, OFF_MV // M_WIDTH), seq(M_WIDTH, OFF_MO // M_WIDTH),
            pl.BlockSpec((1, nch, 2 * M_HEADS, 1, CHUNK), lambda b, s: (b, s, 0, 0, 0)),
            pl.BlockSpec((1, nch, CHUNK, 2 * M_HEADS), lambda b, s: (b, s, 0, 0)),
            full(b_row), full(b_col), full(conv_w), full(conv_b), full(norm_g),
            seq(R_WIDTH, OFF_RR // R_WIDTH), seq(R_WIDTH, OFF_RK // R_WIDTH),
            seq(R_WIDTH, OFF_RV // R_WIDTH), seq(LANES, OFF_RWA // LANES),
            seq(GLORA_PAD, OFF_RG // GLORA_PAD),
            vec(R_WIDTH, OFF_RR // R_WIDTH), vec(R_WIDTH, OFF_RK // R_WIDTH),
            vec(R_WIDTH, OFF_RV // R_WIDTH), vec(LANES, OFF_RWA // LANES),
            vec(GLORA_PAD, OFF_RG // GLORA_PAD),
            full(w0), full(w_up), full(a0), full(a_up), full(g_up),
            full(kkp), full(ka), full(rk), full(gn_g), full(gn_b),
            seq(D, 0),
            pl.BlockSpec(w_out.shape, lambda b, s: (0, 0), pipeline_mode=pl.Buffered(1)),
        ],
        out_specs=pl.BlockSpec((1, ts, D), lambda b, s: (b, s, 0)),
        scratch_shapes=[pltpu.VMEM((SUBLANES, M_WIDTH), F32), pltpu.VMEM((SUBLANES, M_WIDTH), F32),
                        pltpu.VMEM((M_HEADS, M_HDIM, M_HDIM), F32),
                        pltpu.VMEM((M_HEADS, 1, M_HDIM), F32),
                        pltpu.VMEM((M_HEADS, 1, 1), F32),
                        pltpu.VMEM((SUBLANES, R_WIDTH), F32), pltpu.VMEM((SUBLANES, R_WIDTH), F32),
                        pltpu.VMEM((SUBLANES, R_WIDTH), F32), pltpu.VMEM((SUBLANES, LANES), F32),
                        pltpu.VMEM((SUBLANES, GLORA_PAD), F32),
                        pltpu.VMEM((npair, LANES, LANES), F32)],
        compiler_params=pltpu.CompilerParams(
            dimension_semantics=("parallel", "arbitrary"), vmem_limit_bytes=VMEM_LIMIT),
        name="mixer",
    )(p_all, p_all, p_all, p_all, g_row, g_col, b_row, b_col, conv_w, conv_b, norm_g,
      p_all, p_all, p_all, p_all, p_all, mu, mu, mu, mu, mu,
      w0, w_up, a0, a_up, g_up, kkp, ka, rk, gn_g, gn_b, x, w_out)


def _xattn_kernel(x_ref, xn_ref, g_ref, wq_ref, kv_ref, wo_ref, o_ref, h_scr):
    D = x_ref.shape[1]
    i = pl.program_id(0)

    @pl.when(i == 0)
    def _():
        h_scr[0] = _rms_bf16(x_ref[...], g_ref[...])

    slot = lax.rem(i, 2)
    x = x_ref[...]
    q = _dot(h_scr[slot], wq_ref[...]).astype(BF16)
    h_scr[1 - slot] = _rms_bf16(xn_ref[...], g_ref[...])
    hsl = [slice(hd * X_HDIM, (hd + 1) * X_HDIM) for hd in range(D // X_HDIM)]
    s = [lax.dot_general(q[:, ls], kv_ref[0, :, ls], (((1,), (1,)), ((), ())),
                         preferred_element_type=F32) * (X_HDIM ** -0.5) for ls in hsl]
    e = [jnp.exp(si - jnp.max(si, axis=-1, keepdims=True)) for si in s]
    p = [(ei / jnp.sum(ei, axis=-1, keepdims=True)).astype(BF16) for ei in e]
    heads = [_dot(pi, kv_ref[0, :, D + ls.start:D + ls.stop]).astype(BF16)
             for pi, ls in zip(p, hsl)]
    o_ref[...] = x + _dot(jnp.concatenate(heads, axis=1), wo_ref[...])


def _xattn(x, gain, wq, kv, wo, seq_len, tm=512):
    T, D = x.shape
    M = kv.shape[1]
    tm = min(tm, seq_len)
    per_seq = seq_len // tm
    n = T // tm
    const = lambda a: pl.BlockSpec(a.shape, lambda i: (0,) * a.ndim, pipeline_mode=pl.Buffered(1))
    return pl.pallas_call(
        _xattn_kernel,
        out_shape=jax.ShapeDtypeStruct((T, D), F32),
        grid=(n,),
        in_specs=[pl.BlockSpec((tm, D), lambda i: (i, 0)),
                  _next_tile_spec(tm, D, n),
                  pl.BlockSpec((1, D), lambda i: (0, 0)),
                  const(wq),
                  pl.BlockSpec((1, M, 2 * D), lambda i: (i // per_seq, 0, 0)),
                  const(wo)],
        out_specs=pl.BlockSpec((tm, D), lambda i: (i, 0)),
        scratch_shapes=[pltpu.VMEM((2, tm, D), BF16)],
        compiler_params=pltpu.CompilerParams(
            dimension_semantics=("arbitrary",), vmem_limit_bytes=VMEM_LIMIT),
        name="xattn",
    )(x, x, gain.reshape(1, D), wq, kv, wo)


def _ffn_kernel(*refs, per_seq, tc, norm_out):
    x_ref, xn_ref, g_ref, wup_ref, cw_ref, cb_ref, wdn_ref = refs[:7]
    gout_ref = refs[7] if norm_out else None
    o_ref, tail_scr, h_scr = refs[-3:]
    tm = x_ref.shape[0]
    i = pl.program_id(0)

    @pl.when(lax.rem(i, per_seq) == 0)
    def _():
        tail_scr[...] = jnp.zeros_like(tail_scr)

    @pl.when(i == 0)
    def _():
        h_scr[0] = _rms_bf16(x_ref[...], g_ref[...])

    slot = lax.rem(i, 2)
    h = h_scr[slot]
    chunks = [slice(c * tc, (c + 1) * tc) for c in range(D_FF // tc)]
    gates = [_dot(h, wup_ref[:, cols]) for cols in chunks]
    h_scr[1 - slot] = _rms_bf16(xn_ref[...], g_ref[...])
    vals = [_dot(h, wup_ref[:, D_FF + cols.start:D_FF + cols.stop]) for cols in chunks]
    acts = []
    for cols, gate, val in zip(chunks, gates, vals):
        prev = tail_scr[:, cols]
        y = cb_ref[:, cols] + gate * cw_ref[FFN_CONV - 1:FFN_CONV, cols]
        for j in range(FFN_CONV - 1):
            y = y + _shift_rows_carry(gate, prev, FFN_CONV - 1 - j) * cw_ref[j:j + 1, cols]
        tail_scr[:, cols] = gate[tm - SUBLANES:]
        acts.append((y * _sigmoid(y) * val).astype(BF16))
    acc = x_ref[...]
    for cols, act in zip(chunks, acts):
        acc = acc + _dot(act, wdn_ref[cols, :])
    if norm_out:
        ms = jnp.mean(acc * acc, axis=-1, keepdims=True)
        acc = acc * lax.rsqrt(ms + NORM_EPS) * gout_ref[...]
    o_ref[...] = acc


def _ffn(x, gain, w_up, conv_w, conv_b, w_down, seq_len, out_gain=None, tm=256, tc=256):
    T, D = x.shape
    tm = min(tm, seq_len)
    n = T // tm
    conv_b = conv_b.reshape(1, D_FF)
    norm_out = out_gain is not None
    const = lambda a: pl.BlockSpec(a.shape, lambda i: (0,) * a.ndim, pipeline_mode=pl.Buffered(1))
    consts = [w_up, conv_w, conv_b, w_down] + ([out_gain.reshape(1, D)] if norm_out else [])
    return pl.pallas_call(
        functools.partial(_ffn_kernel, per_seq=seq_len // tm, tc=tc, norm_out=norm_out),
        out_shape=jax.ShapeDtypeStruct((T, D), F32),
        grid=(n,),
        in_specs=[pl.BlockSpec((tm, D), lambda i: (i, 0)),
                  _next_tile_spec(tm, D, n),
                  pl.BlockSpec((1, D), lambda i: (0, 0)),
                  ] + [const(a) for a in consts],
        out_specs=pl.BlockSpec((tm, D), lambda i: (i, 0)),
        scratch_shapes=[pltpu.VMEM((SUBLANES, D_FF), F32), pltpu.VMEM((2, tm, D), BF16)],
        compiler_params=pltpu.CompilerParams(
            dimension_semantics=("arbitrary",), vmem_limit_bytes=VMEM_LIMIT),
        name="ffn",
    )(x, x, gain.reshape(1, D), *consts)


def _pad_cols(w, n):
    return jnp.pad(w, ((0, 0), (0, n - w.shape[1])))


def _relayout_in(w):
    m_main = w[:, :4 * M_WIDTH]
    m_gate = w[:, 4 * M_WIDTH:4 * M_WIDTH + 2 * M_HEADS]
    r0 = 4 * M_WIDTH + 2 * M_HEADS
    r_main = w[:, r0:r0 + 3 * R_WIDTH + R_DECAY_LORA + R_AAA_LORA]
    r_gate = w[:, r0 + 3 * R_WIDTH + R_DECAY_LORA + R_AAA_LORA:]
    return jnp.concatenate(
        [m_main, r_main, _pad_cols(m_gate, GATE_PAD), _pad_cols(r_gate, GLORA_PAD)], axis=1)


def kernel(x, mem, norm_mix, w_in, m_conv_w, m_conv_b, m_gate_b, m_norm_g, r_mu, r_w0,
           r_w_up, r_a0, r_a_up, r_g_up, r_kk, r_ka, r_rk, r_gn_g, r_gn_b, w_out,
           norm_x, norm_mem, x_wq, x_wkv, x_wo, norm_ffn, f_up, f_conv_w, f_conv_b,
           f_down, norm_final):
    B, S, D = x.shape
    M = mem.shape[1]
    depth = w_in.shape[0]
    T = B * S
    nc = S // CHUNK
    xf = x.reshape(T, D)
    memf = mem.reshape(B * M, D)
    row = lambda a: a.reshape(1, -1)

    for l in range(depth):
        w_in_p = _relayout_in(w_in[l]).astype(BF16)
        p_all = _mm(xf, w_in_p, gain=norm_mix[l]).reshape(B, S, IN_COLS_P)

        g_col = p_all[:, :, OFF_MG:OFF_MG + 2 * M_HEADS].reshape(B, nc, CHUNK, 2 * M_HEADS)
        g_row = g_col.swapaxes(-1, -2)[:, :, :, None, :]
        mu = _relayout_in(jnp.pad(row(r_mu[l]), ((0, 0), (4 * M_WIDTH + 2 * M_HEADS, 0))))
        w_up = jnp.pad(r_w_up[l], ((0, R_AAA_LORA), (0, 0))).astype(BF16)
        a_up = jnp.pad(r_a_up[l], ((R_DECAY_LORA, 0), (0, 0))).astype(BF16)
        g_up = jnp.pad(r_g_up[l], ((0, GLORA_PAD - R_GATE_LORA), (0, 0))).astype(BF16)
        xf = _mixer(p_all, g_row, g_col, m_gate_b[l].reshape(-1, 1, 1), row(m_gate_b[l]),
                    m_conv_w[l], row(m_conv_b[l]), row(m_norm_g[l]),
                    mu, row(r_w0[l]), w_up, row(r_a0[l]), a_up, g_up,
                    row(r_kk[l]), row(r_ka[l]), row(r_rk[l]), row(r_gn_g[l]), row(r_gn_b[l]),
                    xf.reshape(B, S, D), w_out[l].astype(BF16)).reshape(T, D)

        kv = _mm(memf, x_wkv[l].astype(BF16), gain=norm_mem[l], out_dtype=BF16)
        xf = _xattn(xf, norm_x[l], x_wq[l].astype(BF16), kv.reshape(B, M, 2 * D),
                    x_wo[l].astype(BF16), S)

        xf = _ffn(xf, norm_ffn[l], f_up[l].astype(BF16), f_conv_w[l], f_conv_b[l],
                  f_down[l].astype(BF16), S, out_gain=norm_final if l == depth - 1 else None)

    return xf.reshape(B, S, D)
```

```python
import functools
import math

import jax
import jax.numpy as jnp
from jax import lax
from jax.experimental import pallas as pl
from jax.experimental.pallas import tpu as pltpu

F32 = jnp.float32
BF16 = jnp.bfloat16

D_MODEL = 1024
M_WIDTH = 512
M_HEADS = 4
M_HDIM = 128
M_CONV = 4
R_WIDTH = 512
R_HDIM = 64
R_HEADS = 8
R_DECAY_LORA = 64
R_AAA_LORA = 64
R_GATE_LORA = 160
DECAY_SCALE = math.exp(-0.5)
X_HEADS = 4
X_HDIM = 256
D_FF = 2816
FFN_CONV = 3
NORM_EPS = 1e-6
GN_EPS = 64e-5
CHUNK = 64

LANES = 128
SUBLANES = 8
GATE_PAD = LANES
GLORA_PAD = 2 * LANES
OFF_MQ, OFF_MK, OFF_MV, OFF_MO = 0, 512, 1024, 1536
OFF_RR = 2048
OFF_RK = OFF_RR + R_WIDTH
OFF_RV = OFF_RK + R_WIDTH
OFF_RWA = OFF_RV + R_WIDTH
OFF_MG = OFF_RWA + LANES
OFF_RG = OFF_MG + GATE_PAD
IN_COLS_P = OFF_RG + GLORA_PAD

SEQ_TILE = 256
VMEM_LIMIT = 48 * 1024 * 1024


def _dot(a, b):
    return jnp.dot(a, b, preferred_element_type=F32)


def _bdot(a, b):
    return jnp.dot(a.astype(BF16), b.astype(BF16), preferred_element_type=F32)


def _bdot_nt(a, b):
    return lax.dot_general(a.astype(BF16), b.astype(BF16), (((1,), (1,)), ((), ())),
                           preferred_element_type=F32)


def _bdot_tn(a, b):
    return lax.dot_general(a.astype(BF16), b.astype(BF16), (((0,), (0,)), ((), ())),
                           preferred_element_type=F32)


def _split2(x):
    hi = x.astype(BF16)
    lo = (x - hi.astype(F32)).astype(BF16)
    return hi, lo


def _sigmoid(x):
    return 0.5 * jnp.tanh(0.5 * x) + 0.5


def _shift_rows_carry(x, prev, sh):
    ext = jnp.concatenate([prev, x], axis=0)
    return pltpu.roll(ext, sh, 0)[SUBLANES:]


def _rms_bf16(x, gain):
    ms = jnp.mean(x * x, axis=-1, keepdims=True)
    return (x * lax.rsqrt(ms + NORM_EPS) * gain).astype(BF16)


def _layer_weight_spec(w, layer):
    return pl.BlockSpec((None,) + w.shape[1:], lambda *_: (layer,) + (0,) * (w.ndim - 1),
                        pipeline_mode=pl.Buffered(1))


def _mm_kernel(x_ref, w_ref, g_ref, o_ref, h_scr, *, tn):
    h_scr[...] = _rms_bf16(x_ref[...], g_ref[...])
    for c in range(w_ref.shape[1] // tn):
        cols = slice(c * tn, (c + 1) * tn)
        o_ref[:, cols] = _dot(h_scr[...], w_ref[:, cols]).astype(o_ref.dtype)


def _mm_tile_rows(K, N, x_bytes, out_bytes):
    budget = (VMEM_LIMIT * 3) // 4 - K * N * 2
    for tm in (512, 256, 128):
        if tm * (2 * K * x_bytes + 2 * N * out_bytes + K * 2) <= budget:
            return tm
    raise ValueError("weight does not fit in VMEM")


def _mm(x, w, layer, gain, out_dtype=F32, tn=512):
    T, K = x.shape
    N = w.shape[2]
    tm = min(T, _mm_tile_rows(K, N, x.dtype.itemsize, jnp.dtype(out_dtype).itemsize))
    tn = min(tn, N)
    assert T % tm == 0 and N % tn == 0
    return pl.pallas_call(
        functools.partial(_mm_kernel, tn=tn),
        out_shape=jax.ShapeDtypeStruct((T, N), out_dtype),
        grid=(T // tm,),
        in_specs=[pl.BlockSpec((tm, K), lambda i: (i, 0)),
                  _layer_weight_spec(w, layer),
                  pl.BlockSpec((1, K), lambda i: (0, 0))],
        out_specs=pl.BlockSpec((tm, N), lambda i: (i, 0)),
        scratch_shapes=[pltpu.VMEM((tm, K), BF16)],
        compiler_params=pltpu.CompilerParams(
            dimension_semantics=("parallel",), vmem_limit_bytes=VMEM_LIMIT),
        name="mm",
    )(x, w, gain.reshape(1, K))


def _mlstm_stages(q_ref, k_ref, v_ref, o_ref, grow_ref, gcol_ref, brow_ref, bcol_ref,
                  cw_ref, cb_ref, ng_ref, qp_scr, kp_scr, c_scr, n_scr, m_scr, result):
    TS = q_ref.shape[1]
    L = CHUNK
    nch = TS // L
    H = M_HEADS
    G = nch * H
    hs = range(H)

    def conv_silu(x, prev, w, b):
        y = b + x * w[M_CONV - 1:M_CONV, :]
        for j in range(M_CONV - 1):
            y = y + _shift_rows_carry(x, prev, M_CONV - 1 - j) * w[j:j + 1, :]
        return y * _sigmoid(y)

    def groups(x):
        return jnp.stack([x[c * L:(c + 1) * L, h * M_HDIM:(h + 1) * M_HDIM]
                          for c in range(nch) for h in hs])

    def per_group(f, *xs):
        return jnp.stack([f(*[x[i] for x in xs]) for i in range(G)])

    def log_sigmoid(x):
        return jnp.minimum(x, 0.0) - jnp.log1p(jnp.exp(-jnp.abs(x)))

    q_raw = q_ref[0]
    q = groups(conv_silu(q_raw, qp_scr[...], cw_ref[:, :M_WIDTH], cb_ref[:, :M_WIDTH])
               * (M_HDIM ** -0.5))
    qp_scr[...] = q_raw[TS - SUBLANES:]
    yield
    k_raw = k_ref[0]
    k = groups(conv_silu(k_raw, kp_scr[...], cw_ref[:, M_WIDTH:], cb_ref[:, M_WIDTH:]))
    kp_scr[...] = k_raw[TS - SUBLANES:]
    v = groups(v_ref[0])
    yield

    ti = lax.broadcasted_iota(jnp.int32, (L, L), 0)
    si = lax.broadcasted_iota(jnp.int32, (L, L), 1)
    causal = si <= ti
    diag = ti == si
    gr = grow_ref[0] + brow_ref[...]
    gc = gcol_ref[0] + bcol_ref[...]
    logi_r = gr[:, :H].reshape(G, 1, L)
    logf_r = log_sigmoid(gr[:, H:].reshape(G, 1, L))
    logi_c = jnp.stack([gc[c, :, h:h + 1] for c in range(nch) for h in hs])
    yield
    b_c = jnp.sum(jnp.where(causal, logf_r, 0.0), axis=2, keepdims=True)
    b_r = jnp.sum(jnp.where(diag, b_c, 0.0), axis=1, keepdims=True)
    g = jnp.sum(logf_r, axis=2, keepdims=True)
    a_r = g - b_r + logi_r
    a_c = g - b_c + logi_c
    m_loc = jnp.max(a_r, axis=2, keepdims=True)
    wa_c = jnp.exp(a_c - m_loc)
    yield
    c_loc = per_group(_bdot_tn, v * wa_c, k)
    yield
    n_loc = jnp.sum(k * wa_c, axis=1, keepdims=True)
    d = jnp.where(causal, b_c - b_r + logi_r, -jnp.inf)
    d_max = jnp.max(d, axis=2, keepdims=True)
    yield
    qk = per_group(_bdot_nt, q, k)
    yield

    c_prev = c_scr[...]
    n_prev = n_scr[...]
    m_prev = m_scr[...]
    c_in, n_in, m_in = [], [], []
    for c in range(nch):
        gs = slice(c * H, (c + 1) * H)
        c_in.append(c_prev)
        n_in.append(n_prev)
        m_in.append(m_prev)
        m_new = jnp.maximum(g[gs] + m_prev, m_loc[gs])
        s_old = jnp.exp(g[gs] + m_prev - m_new)
        s_loc = jnp.exp(m_loc[gs] - m_new)
        c_prev = s_old * c_prev + s_loc * c_loc[gs]
        n_prev = s_old * n_prev + s_loc * n_loc[gs]
        m_prev = m_new
    c_scr[...] = c_prev
    n_scr[...] = n_prev
    m_scr[...] = m_prev
    c_in = jnp.concatenate(c_in, axis=0)
    n_in = jnp.concatenate(n_in, axis=0)
    m_in = jnp.concatenate(m_in, axis=0)
    yield

    inter = b_c + m_in
    m_t = jnp.maximum(inter, d_max)
    s_int = jnp.exp(inter - m_t)
    p = jnp.exp(d - m_t) * qk
    yield
    num = s_int * per_group(_bdot_nt, q, c_in) + per_group(_bdot, p, v)
    yield
    den = (s_int * jnp.sum(q * n_in, axis=2, keepdims=True)
           + jnp.sum(p, axis=2, keepdims=True))
    hh = num / jnp.maximum(jnp.abs(den), jnp.exp(-m_t))
    yield
    mu = jnp.mean(hh, axis=-1, keepdims=True)
    hc = hh - mu
    var = jnp.mean(hc * hc, axis=-1, keepdims=True)
    ng = jnp.stack([ng_ref[:, h * M_HDIM:(h + 1) * M_HDIM] for h in hs] * nch)
    y = (_sigmoid(groups(o_ref[0])) * (hc * lax.rsqrt(var + NORM_EPS)) * ng).astype(BF16)
    result.append(jnp.concatenate(
        [jnp.concatenate([y[c * H + h] for h in hs], axis=1) for c in range(nch)], axis=0))


def _rwkv_stages(pr_ref, pk_ref, pv_ref, pwa_ref, pg_ref,
                 mur_ref, muk_ref, muv_ref, muwa_ref, mug_ref,
                 w0_ref, wup_ref, a0_ref, aup_ref, gup_ref,
                 kkp_ref, ka_ref, rk_ref, gng_ref, gnb_ref,
                 cr_scr, ck_scr, cv_scr, cwa_scr, cg_scr, st_scr, result):
    TS = pr_ref.shape[1]
    L = CHUNK
    N = R_HDIM
    nch = TS // L
    npair = R_WIDTH // LANES
    pairs = range(npair)
    Q = 2 * LANES

    def tshift(p_ref, mu_ref, c_scr):
        p = p_ref[0]
        prev = _shift_rows_carry(p, c_scr[...], 1)
        c_scr[...] = p[TS - SUBLANES:]
        return p + (prev - p) * mu_ref[...]

    hsh = N.bit_length() - 1
    li = lax.broadcasted_iota(jnp.int32, (LANES, LANES), 0)
    lj = lax.broadcasted_iota(jnp.int32, (LANES, LANES), 1)
    same_head = jnp.right_shift(li, hsh) == jnp.right_shift(lj, hsh)
    ones_bd = jnp.where(same_head, 1.0, 0.0).astype(BF16)

    def seg_sum(x):
        outs = []
        for p in range(x.shape[1] // LANES):
            hi, lo = _split2(x[:, p * LANES:(p + 1) * LANES])
            outs.append(_dot(hi, ones_bd) + _dot(lo, ones_bd))
        return jnp.concatenate(outs, axis=1)

    rr = tshift(pr_ref, mur_ref, cr_scr)
    kr = tshift(pk_ref, muk_ref, ck_scr)
    vr = tshift(pv_ref, muv_ref, cv_scr)
    yield
    wa = tshift(pwa_ref, muwa_ref, cwa_scr)
    gd = tshift(pg_ref, mug_ref, cg_scr)
    logw = -DECAY_SCALE * _sigmoid(w0_ref[...] + _bdot(jnp.tanh(wa), wup_ref[...]))
    a = _sigmoid(a0_ref[...] + _bdot(wa, aup_ref[...]))
    g = _bdot(_sigmoid(gd), gup_ref[...])
    yield
    kkraw = kr * kkp_ref[...]
    kk = kkraw / jnp.maximum(jnp.sqrt(seg_sum(kkraw * kkraw)), 1e-12)
    km = kr * (1.0 + (a - 1.0) * ka_ref[...])
    be = kk * a
    yield
    tq = lax.broadcasted_iota(jnp.int32, (Q, Q), 0)
    sq = lax.broadcasted_iota(jnp.int32, (Q, Q), 1)
    tril = jnp.where((jnp.right_shift(tq, hsh) == jnp.right_shift(sq, hsh)) & (sq <= tq),
                     1.0, 0.0).astype(BF16)
    lw_hi, lw_lo = _split2(logw)
    bincl = jnp.concatenate(
        [_dot(tril, lw_hi[q * Q:(q + 1) * Q]) + _dot(tril, lw_lo[q * Q:(q + 1) * Q])
         for q in range(TS // Q)], axis=0)
    e_in = jnp.exp(bincl)
    e_ng = jnp.exp(-bincl)
    kt = kk * jnp.exp(bincl - logw)
    rt = rr * e_in
    bh = be * e_ng
    kh = km * e_ng
    yield

    h0 = lax.broadcasted_iota(jnp.int32, (L, LANES), 1) < N
    ti = lax.broadcasted_iota(jnp.int32, (2 * L, 2 * L), 0)
    si = lax.broadcasted_iota(jnp.int32, (2 * L, 2 * L), 1)
    same_blk = jnp.right_shift(ti, hsh) == jnp.right_shift(si, hsh)
    strict = same_blk & (si < ti)
    incl = same_blk & (si <= ti)

    def stack_heads(x):
        return jnp.concatenate([jnp.where(h0, x, 0.0), jnp.where(h0, 0.0, x)], axis=0)

    def stack_dup(x):
        return jnp.concatenate([x, x], axis=0)

    def comb(x):
        return jnp.where(h0, x[:L], x[L:])

    def off_mask(b):
        sh = (2 * b).bit_length() - 1
        same = jnp.right_shift(ti, sh) == jnp.right_shift(si, sh)
        return same & (jnp.bitwise_and(ti, b) != 0) & (jnp.bitwise_and(si, b) == 0)

    probs = [(p, c) for p in pairs for c in range(nch)]

    def tile(x, p, c):
        return x[c * L:(c + 1) * L, p * LANES:(p + 1) * LANES]

    lk = [stack_heads(tile(kt, p, c)).astype(BF16) for p, c in probs]
    lr = [stack_heads(tile(rt, p, c)).astype(BF16) for p, c in probs]
    rb = [stack_dup(tile(bh, p, c)).astype(BF16) for p, c in probs]
    rk = [stack_dup(tile(kh, p, c)).astype(BF16) for p, c in probs]
    vs = [stack_dup(tile(vr, p, c)).astype(BF16) for p, c in probs]
    yield
    a_bd = [jnp.where(strict, _bdot_nt(x, y), 0.0) for x, y in zip(lk, rb)]
    yield
    bk_bd = [jnp.where(strict, _bdot_nt(x, y), 0.0).astype(BF16) for x, y in zip(lk, rk)]
    yield
    arb_bd = [jnp.where(incl, _bdot_nt(x, y), 0.0).astype(BF16) for x, y in zip(lr, rb)]
    yield
    ark_bd = [jnp.where(incl, _bdot_nt(x, y), 0.0).astype(BF16) for x, y in zip(lr, rk)]
    yield
    eye = jnp.where(ti == si, 1.0, 0.0)
    m1 = off_mask(1)
    t_bd = [eye - jnp.where(m1, x, 0.0) for x in a_bd]
    b = 2
    while b < L:
        mb = off_mask(b)
        offs = [jnp.where(mb, x, 0.0).astype(BF16) for x in a_bd]
        xs = [_bdot(i, o) for i, o in zip(t_bd, offs)]
        yield
        ys = [_bdot(x, i) for x, i in zip(xs, t_bd)]
        t_bd = [i - y for i, y in zip(t_bd, ys)]
        yield
        b *= 2
    bkv = [comb(_bdot(x, y)) for x, y in zip(bk_bd, vs)]
    yield
    xs = [jnp.concatenate([stack_dup(tile(kt, p, c)), stack_dup(z)], axis=1)
          for (p, c), z in zip(probs, bkv)]
    tx = [_bdot(x, y) for x, y in zip(t_bd, xs)]
    yield
    arkv = [comb(_bdot(x, y)) for x, y in zip(ark_bd, vs)]
    w_ch = [[None] * nch for _ in pairs]
    u0_ch = [[None] * nch for _ in pairs]
    arkv_ch = [[None] * nch for _ in pairs]
    arb_ch = [[None] * nch for _ in pairs]
    for i, (p, c) in enumerate(probs):
        w_ch[p][c] = comb(tx[i][:, :LANES])
        u0_ch[p][c] = -comb(tx[i][:, LANES:])
        arkv_ch[p][c] = arkv[i]
        arb_ch[p][c] = arb_bd[i]
    yield

    y_rows = []
    st = [st_scr[p] for p in pairs]
    ones_f = jnp.where(same_head, 1.0, 0.0)
    lsl = [slice(p * LANES, (p + 1) * LANES) for p in pairs]
    for c in range(nch):
        rs = slice(c * L, (c + 1) * L)
        p_end = e_in[c * L + L - 1:c * L + L, :]
        pe = [p_end[:, ls] for ls in lsl]
        rw = [_bdot_nt(jnp.concatenate([rt[rs, lsl[p]], w_ch[p][c]], axis=0), st[p])
              for p in pairs]
        u = [u0_ch[p][c] - rw[p][L:] for p in pairs]
        au = [_bdot(arb_ch[p][c], jnp.concatenate([u[p], u[p]], axis=0)) for p in pairs]
        upd = [_bdot_tn(jnp.concatenate([u[p], vr[rs, lsl[p]]], axis=0),
                        jnp.concatenate([bh[rs, lsl[p]] * pe[p], kh[rs, lsl[p]] * pe[p]], axis=0))
               for p in pairs]
        st = [st[p] * pe[p] + upd[p] * ones_f for p in pairs]
        y_rows.append(jnp.concatenate(
            [rw[p][:L] + jnp.where(h0, au[p][:L], au[p][L:]) + arkv_ch[p][c] for p in pairs],
            axis=1))
        yield
    for p in pairs:
        st_scr[p] = st[p]
    y = jnp.concatenate(y_rows, axis=0)

    inv_n = 1.0 / N
    mu = seg_sum(y) * inv_n
    yc = y - mu
    var = seg_sum(yc * yc) * inv_n
    yn = yc * lax.rsqrt(var + GN_EPS) * gng_ref[...] + gnb_ref[...]
    bonus = seg_sum(rr * km * rk_ref[...]) * vr
    result.append(((yn + bonus) * g).astype(BF16))


def _mixer_kernel(*refs):
    (q_ref, k_ref, v_ref, o_ref, grow_ref, gcol_ref, brow_ref, bcol_ref, cw_ref, cb_ref, ng_ref,
     pr_ref, pk_ref, pv_ref, pwa_ref, pg_ref, mur_ref, muk_ref, muv_ref, muwa_ref, mug_ref,
     w0_ref, wup_ref, a0_ref, aup_ref, gup_ref, kkp_ref, ka_ref, rk_ref, gng_ref, gnb_ref,
     x_ref, wout_ref, out_ref,
     qp_scr, kp_scr, c_scr, n_scr, m_scr,
     cr_scr, ck_scr, cv_scr, cwa_scr, cg_scr, st_scr) = refs

    @pl.when(pl.program_id(1) == 0)
    def _():
        for scr in (qp_scr, kp_scr, c_scr, n_scr, m_scr,
                    cr_scr, ck_scr, cv_scr, cwa_scr, cg_scr, st_scr):
            scr[...] = jnp.zeros_like(scr)

    y_m, y_r = [], []
    mlstm = _mlstm_stages(q_ref, k_ref, v_ref, o_ref, grow_ref, gcol_ref, brow_ref, bcol_ref,
                          cw_ref, cb_ref, ng_ref, qp_scr, kp_scr, c_scr, n_scr, m_scr, y_m)
    rwkv = _rwkv_stages(pr_ref, pk_ref, pv_ref, pwa_ref, pg_ref,
                        mur_ref, muk_ref, muv_ref, muwa_ref, mug_ref,
                        w0_ref, wup_ref, a0_ref, aup_ref, gup_ref,
                        kkp_ref, ka_ref, rk_ref, gng_ref, gnb_ref,
                        cr_scr, ck_scr, cv_scr, cwa_scr, cg_scr, st_scr, y_r)
    for tag in "RRRRR" + "RMR" * 9 + "RM" * 4:
        next(rwkv if tag == "R" else mlstm, None)
    for stream in (mlstm, rwkv):
        for _ in stream:
            pass
    out_ref[0] = (x_ref[0] + _dot(y_m[0], wout_ref[:M_WIDTH, :])
                  + _dot(y_r[0], wout_ref[M_WIDTH:, :]))


def _mixer(p_all, g_row, g_col, b_row, b_col, conv_w, conv_b, norm_g,
           mu, w0, w_up, a0, a_up, g_up, kkp, ka, rk, gn_g, gn_b, x, w_out, layer):
    B, S, _ = p_all.shape
    D = x.shape[2]
    ts = min(SEQ_TILE, S)
    nch = ts // CHUNK
    npair = R_WIDTH // LANES
    seq = lambda w, off: pl.BlockSpec((1, ts, w), lambda b, s, off=off: (b, s, off))
    vec = lambda w, off: pl.BlockSpec((1, w), lambda b, s, off=off: (0, off))
    full = lambda a: pl.BlockSpec(a.shape, lambda b, s: (0,) * a.ndim)
    return pl.pallas_call(
        _mixer_kernel,
        out_shape=jax.ShapeDtypeStruct((B, S, D), F32),
        grid=(B, S // ts),
        in_specs=[
            seq(M_WIDTH, OFF_MQ // M_WIDTH), seq(M_WIDTH, OFF_MK // M_WIDTH),
            seq(M_WIDTH, OFF_MV // M_WIDTH), seq(M_WIDTH, OFF_MO // M_WIDTH),
            pl.BlockSpec((1, nch, 2 * M_HEADS, 1, CHUNK), lambda b, s: (b, s, 0, 0, 0)),
            pl.BlockSpec((1, nch, CHUNK, 2 * M_HEADS), lambda b, s: (b, s, 0, 0)),
            full(b_row), full(b_col), full(conv_w), full(conv_b), full(norm_g),
            seq(R_WIDTH, OFF_RR // R_WIDTH), seq(R_WIDTH, OFF_RK // R_WIDTH),
            seq(R_WIDTH, OFF_RV // R_WIDTH), seq(LANES, OFF_RWA // LANES),
            seq(GLORA_PAD, OFF_RG // GLORA_PAD),
            vec(R_WIDTH, OFF_RR // R_WIDTH), vec(R_WIDTH, OFF_RK // R_WIDTH),
            vec(R_WIDTH, OFF_RV // R_WIDTH), vec(LANES, OFF_RWA // LANES),
            vec(GLORA_PAD, OFF_RG // GLORA_PAD),
            full(w0), full(w_up), full(a0), full(a_up), full(g_up),
            full(kkp), full(ka), full(rk), full(gn_g), full(gn_b),
            seq(D, 0),
            _layer_weight_spec(w_out, layer),
        ],
        out_specs=pl.BlockSpec((1, ts, D), lambda b, s: (b, s, 0)),
        scratch_shapes=[pltpu.VMEM((SUBLANES, M_WIDTH), F32), pltpu.VMEM((SUBLANES, M_WIDTH), F32),
                        pltpu.VMEM((M_HEADS, M_HDIM, M_HDIM), F32),
                        pltpu.VMEM((M_HEADS, 1, M_HDIM), F32),
                        pltpu.VMEM((M_HEADS, 1, 1), F32),
                        pltpu.VMEM((SUBLANES, R_WIDTH), F32), pltpu.VMEM((SUBLANES, R_WIDTH), F32),
                        pltpu.VMEM((SUBLANES, R_WIDTH), F32), pltpu.VMEM((SUBLANES, LANES), F32),
                        pltpu.VMEM((SUBLANES, GLORA_PAD), F32),
                        pltpu.VMEM((npair, LANES, LANES), F32)],
        compiler_params=pltpu.CompilerParams(
            dimension_semantics=("parallel", "arbitrary"), vmem_limit_bytes=VMEM_LIMIT),
        name="mixer",
    )(p_all, p_all, p_all, p_all, g_row, g_col, b_row, b_col, conv_w, conv_b, norm_g,
      p_all, p_all, p_all, p_all, p_all, mu, mu, mu, mu, mu,
      w0, w_up, a0, a_up, g_up, kkp, ka, rk, gn_g, gn_b, x, w_out)


def _xattn_kernel(x_ref, g_ref, wq_ref, kv_ref, wo_ref, o_ref):
    D = x_ref.shape[1]
    x = x_ref[...]
    q = _dot(_rms_bf16(x, g_ref[...]), wq_ref[...]).astype(BF16)
    hsl = [slice(hd * X_HDIM, (hd + 1) * X_HDIM) for hd in range(D // X_HDIM)]
    s = [lax.dot_general(q[:, ls], kv_ref[0, :, ls], (((1,), (1,)), ((), ())),
                         preferred_element_type=F32) * (X_HDIM ** -0.5) for ls in hsl]
    e = [jnp.exp(si - jnp.max(si, axis=-1, keepdims=True)) for si in s]
    p = [(ei / jnp.sum(ei, axis=-1, keepdims=True)).astype(BF16) for ei in e]
    heads = [_dot(pi, kv_ref[0, :, D + ls.start:D + ls.stop]).astype(BF16)
             for pi, ls in zip(p, hsl)]
    o_ref[...] = x + _dot(jnp.concatenate(heads, axis=1), wo_ref[...])


def _xattn(x, gain, wq, kv, wo, layer, seq_len, tm=512):
    T, D = x.shape
    M = kv.shape[1]
    tm = min(tm, seq_len)
    per_seq = seq_len // tm
    return pl.pallas_call(
        _xattn_kernel,
        out_shape=jax.ShapeDtypeStruct((T, D), F32),
        grid=(T // tm,),
        in_specs=[pl.BlockSpec((tm, D), lambda i: (i, 0)),
                  pl.BlockSpec((1, D), lambda i: (0, 0)),
                  _layer_weight_spec(wq, layer),
                  pl.BlockSpec((1, M, 2 * D), lambda i: (i // per_seq, 0, 0)),
                  _layer_weight_spec(wo, layer)],
        out_specs=pl.BlockSpec((tm, D), lambda i: (i, 0)),
        compiler_params=pltpu.CompilerParams(
            dimension_semantics=("parallel",), vmem_limit_bytes=VMEM_LIMIT),
        name="xattn",
    )(x, gain.reshape(1, D), wq, kv, wo)


def _ffn_kernel(*refs, per_seq, tc, norm_out):
    x_ref, g_ref, wup_ref, cw_ref, cb_ref, wdn_ref = refs[:6]
    gout_ref = refs[6] if norm_out else None
    o_ref, tail_scr = refs[-2:]
    tm = x_ref.shape[0]

    @pl.when(lax.rem(pl.program_id(0), per_seq) == 0)
    def _():
        tail_scr[...] = jnp.zeros_like(tail_scr)

    x = x_ref[...]
    h = _rms_bf16(x, g_ref[...])
    chunks = [slice(c * tc, (c + 1) * tc) for c in range(D_FF // tc)]
    gates = [_dot(h, wup_ref[:, cols]) for cols in chunks]
    vals = [_dot(h, wup_ref[:, D_FF + cols.start:D_FF + cols.stop]) for cols in chunks]
    acts = []
    for cols, gate, val in zip(chunks, gates, vals):
        prev = tail_scr[:, cols]
        y = cb_ref[:, cols] + gate * cw_ref[FFN_CONV - 1:FFN_CONV, cols]
        for j in range(FFN_CONV - 1):
            y = y + _shift_rows_carry(gate, prev, FFN_CONV - 1 - j) * cw_ref[j:j + 1, cols]
        tail_scr[:, cols] = gate[tm - SUBLANES:]
        acts.append((y * _sigmoid(y) * val).astype(BF16))
    acc = x
    for cols, act in zip(chunks, acts):
        acc = acc + _dot(act, wdn_ref[cols, :])
    if norm_out:
        ms = jnp.mean(acc * acc, axis=-1, keepdims=True)
        acc = acc * lax.rsqrt(ms + NORM_EPS) * gout_ref[...]
    o_ref[...] = acc


def _ffn(x, gain, w_up, conv_w, conv_b, w_down, layer, seq_len, out_gain=None, tm=256, tc=256):
    T, D = x.shape
    tm = min(tm, seq_len)
    norm_out = out_gain is not None
    const = lambda a: pl.BlockSpec(a.shape, lambda i: (0,) * a.ndim, pipeline_mode=pl.Buffered(1))
    small = [conv_w, conv_b.reshape(1, D_FF)] + ([out_gain.reshape(1, D)] if norm_out else [])
    specs = [const(a) for a in small]
    return pl.pallas_call(
        functools.partial(_ffn_kernel, per_seq=seq_len // tm, tc=tc, norm_out=norm_out),
        out_shape=jax.ShapeDtypeStruct((T, D), F32),
        grid=(T // tm,),
        in_specs=[pl.BlockSpec((tm, D), lambda i: (i, 0)),
                  pl.BlockSpec((1, D), lambda i: (0, 0)),
                  _layer_weight_spec(w_up, layer), specs[0], specs[1],
                  _layer_weight_spec(w_down, layer)] + specs[2:],
        out_specs=pl.BlockSpec((tm, D), lambda i: (i, 0)),
        scratch_shapes=[pltpu.VMEM((SUBLANES, D_FF), F32)],
        compiler_params=pltpu.CompilerParams(
            dimension_semantics=("arbitrary",), vmem_limit_bytes=VMEM_LIMIT),
        name="ffn",
    )(x, gain.reshape(1, D), w_up, small[0], small[1], w_down, *small[2:])


def _pad_cols(w, n):
    return jnp.pad(w, [(0, 0)] * (w.ndim - 1) + [(0, n - w.shape[-1])])


def _relayout_in(w):
    m_main = w[..., :4 * M_WIDTH]
    m_gate = w[..., 4 * M_WIDTH:4 * M_WIDTH + 2 * M_HEADS]
    r0 = 4 * M_WIDTH + 2 * M_HEADS
    r_main = w[..., r0:r0 + 3 * R_WIDTH + R_DECAY_LORA + R_AAA_LORA]
    r_gate = w[..., r0 + 3 * R_WIDTH + R_DECAY_LORA + R_AAA_LORA:]
    return jnp.concatenate(
        [m_main, r_main, _pad_cols(m_gate, GATE_PAD), _pad_cols(r_gate, GLORA_PAD)], axis=-1)


def kernel(x, mem, norm_mix, w_in, m_conv_w, m_conv_b, m_gate_b, m_norm_g, r_mu, r_w0,
           r_w_up, r_a0, r_a_up, r_g_up, r_kk, r_ka, r_rk, r_gn_g, r_gn_b, w_out,
           norm_x, norm_mem, x_wq, x_wkv, x_wo, norm_ffn, f_up, f_conv_w, f_conv_b,
           f_down, norm_final):
    B, S, D = x.shape
    M = mem.shape[1]
    depth = w_in.shape[0]
    T = B * S
    nc = S // CHUNK
    xf = x.reshape(T, D)
    memf = mem.reshape(B * M, D)
    row = lambda a: a.reshape(1, -1)
    w_in_p, w_out_b = _relayout_in(w_in).astype(BF16), w_out.astype(BF16)
    wq_b, wkv_b, wo_b = x_wq.astype(BF16), x_wkv.astype(BF16), x_wo.astype(BF16)
    f_up_b, f_down_b = f_up.astype(BF16), f_down.astype(BF16)

    for l in range(depth):
        p_all = _mm(xf, w_in_p, l, gain=norm_mix[l]).reshape(B, S, IN_COLS_P)

        g_col = p_all[:, :, OFF_MG:OFF_MG + 2 * M_HEADS].reshape(B, nc, CHUNK, 2 * M_HEADS)
        g_row = g_col.swapaxes(-1, -2)[:, :, :, None, :]
        mu = _relayout_in(jnp.pad(row(r_mu[l]), ((0, 0), (4 * M_WIDTH + 2 * M_HEADS, 0))))
        w_up = jnp.pad(r_w_up[l], ((0, R_AAA_LORA), (0, 0))).astype(BF16)
        a_up = jnp.pad(r_a_up[l], ((R_DECAY_LORA, 0), (0, 0))).astype(BF16)
        g_up = jnp.pad(r_g_up[l], ((0, GLORA_PAD - R_GATE_LORA), (0, 0))).astype(BF16)
        xf = _mixer(p_all, g_row, g_col, m_gate_b[l].reshape(-1, 1, 1), row(m_gate_b[l]),
                    m_conv_w[l], row(m_conv_b[l]), row(m_norm_g[l]),
                    mu, row(r_w0[l]), w_up, row(r_a0[l]), a_up, g_up,
                    row(r_kk[l]), row(r_ka[l]), row(r_rk[l]), row(r_gn_g[l]), row(r_gn_b[l]),
                    xf.reshape(B, S, D), w_out_b, l).reshape(T, D)

        kv = _mm(memf, wkv_b, l, gain=norm_mem[l], out_dtype=BF16)
        xf = _xattn(xf, norm_x[l], wq_b, kv.reshape(B, M, 2 * D), wo_b, l, S)

        xf = _ffn(xf, norm_ffn[l], f_up_b, f_conv_w[l], f_conv_b[l], f_down_b, l, S,
                  out_gain=norm_final if l == depth - 1 else None)

    return xf.reshape(B, S, D)
```

```python
import functools
import math

import jax
import jax.numpy as jnp
from jax import lax
from jax.experimental import pallas as pl
from jax.experimental.pallas import tpu as pltpu

F32 = jnp.float32
BF16 = jnp.bfloat16

D_MODEL = 1024
M_WIDTH = 512
M_HEADS = 4
M_HDIM = 128
M_CONV = 4
R_WIDTH = 512
R_HDIM = 64
R_HEADS = 8
R_DECAY_LORA = 64
R_AAA_LORA = 64
R_GATE_LORA = 160
DECAY_SCALE = math.exp(-0.5)
X_HEADS = 4
X_HDIM = 256
D_FF = 2816
FFN_CONV = 3
NORM_EPS = 1e-6
GN_EPS = 64e-5
CHUNK = 64

LANES = 128
SUBLANES = 8
GATE_PAD = LANES
GLORA_PAD = 2 * LANES
OFF_MQ, OFF_MK, OFF_MV, OFF_MO = 0, 512, 1024, 1536
OFF_RR = 2048
OFF_RK = OFF_RR + R_WIDTH
OFF_RV = OFF_RK + R_WIDTH
OFF_RWA = OFF_RV + R_WIDTH
OFF_MG = OFF_RWA + LANES
OFF_RG = OFF_MG + GATE_PAD
IN_COLS_P = OFF_RG + GLORA_PAD

SEQ_TILE = 256
VMEM_LIMIT = 48 * 1024 * 1024


def _dot(a, b):
    return jnp.dot(a, b, preferred_element_type=F32)


def _bdot(a, b):
    return jnp.dot(a.astype(BF16), b.astype(BF16), preferred_element_type=F32)


def _bdot_nt(a, b):
    return lax.dot_general(a.astype(BF16), b.astype(BF16), (((1,), (1,)), ((), ())),
                           preferred_element_type=F32)


def _bdot_tn(a, b):
    return lax.dot_general(a.astype(BF16), b.astype(BF16), (((0,), (0,)), ((), ())),
                           preferred_element_type=F32)


def _split2(x):
    hi = x.astype(BF16)
    lo = (x - hi.astype(F32)).astype(BF16)
    return hi, lo


def _sigmoid(x):
    return 0.5 * jnp.tanh(0.5 * x) + 0.5


def _shift_rows_carry(x, prev, sh):
    r = pltpu.roll(x, sh, 0)
    row = lax.broadcasted_iota(jnp.int32, prev.shape, 0)
    head = jnp.where(row < sh, pltpu.roll(prev, sh, 0), r[:SUBLANES])
    return jnp.concatenate([head, r[SUBLANES:]], axis=0)


def _rms_bf16(x, gain):
    ms = jnp.mean(x * x, axis=-1, keepdims=True)
    return (x * lax.rsqrt(ms + NORM_EPS) * gain).astype(BF16)


def _layer_weight_spec(w, layer):
    return pl.BlockSpec((None,) + w.shape[1:], lambda *_: (layer,) + (0,) * (w.ndim - 1),
                        pipeline_mode=pl.Buffered(1))


def _mm_kernel(x_ref, w_ref, g_ref, o_ref, h_scr, *, tn):
    h_scr[...] = _rms_bf16(x_ref[...], g_ref[...])
    for c in range(w_ref.shape[1] // tn):
        cols = slice(c * tn, (c + 1) * tn)
        o_ref[:, cols] = _dot(h_scr[...], w_ref[:, cols]).astype(o_ref.dtype)


def _mm_tile_rows(K, N, x_bytes, out_bytes):
    budget = (VMEM_LIMIT * 3) // 4 - K * N * 2
    for tm in (512, 256, 128):
        if tm * (2 * K * x_bytes + 2 * N * out_bytes + K * 2) <= budget:
            return tm
    raise ValueError("weight does not fit in VMEM")


def _mm(x, w, layer, gain, out_dtype=F32, tn=512):
    T, K = x.shape
    N = w.shape[2]
    tm = min(T, _mm_tile_rows(K, N, x.dtype.itemsize, jnp.dtype(out_dtype).itemsize))
    tn = min(tn, N)
    assert T % tm == 0 and N % tn == 0
    return pl.pallas_call(
        functools.partial(_mm_kernel, tn=tn),
        out_shape=jax.ShapeDtypeStruct((T, N), out_dtype),
        grid=(T // tm,),
        in_specs=[pl.BlockSpec((tm, K), lambda i: (i, 0)),
                  _layer_weight_spec(w, layer),
                  pl.BlockSpec((1, K), lambda i: (0, 0))],
        out_specs=pl.BlockSpec((tm, N), lambda i: (i, 0)),
        scratch_shapes=[pltpu.VMEM((tm, K), BF16)],
        compiler_params=pltpu.CompilerParams(
            dimension_semantics=("parallel",), vmem_limit_bytes=VMEM_LIMIT),
        name="mm",
    )(x, w, gain.reshape(1, K))


def _mlstm_stages(q_ref, k_ref, v_ref, o_ref, grow_ref, gcol_ref, brow_ref, bcol_ref,
                  cw_ref, cb_ref, ng_ref, qp_scr, kp_scr, c_scr, n_scr, m_scr, result):
    TS = q_ref.shape[1]
    L = CHUNK
    nch = TS // L
    H = M_HEADS
    G = nch * H
    hs = range(H)

    def conv_silu(x, prev, w, b):
        y = b + x * w[M_CONV - 1:M_CONV, :]
        for j in range(M_CONV - 1):
            y = y + _shift_rows_carry(x, prev, M_CONV - 1 - j) * w[j:j + 1, :]
        return y * _sigmoid(y)

    def groups(x):
        return jnp.stack([x[c * L:(c + 1) * L, h * M_HDIM:(h + 1) * M_HDIM]
                          for c in range(nch) for h in hs])

    def per_group(f, *xs):
        return jnp.stack([f(*[x[i] for x in xs]) for i in range(G)])

    def log_sigmoid(x):
        return jnp.minimum(x, 0.0) - jnp.log1p(jnp.exp(-jnp.abs(x)))

    q_raw = q_ref[0]
    q = groups(conv_silu(q_raw, qp_scr[...], cw_ref[:, :M_WIDTH], cb_ref[:, :M_WIDTH])
               * (M_HDIM ** -0.5))
    qp_scr[...] = q_raw[TS - SUBLANES:]
    yield
    k_raw = k_ref[0]
    k = groups(conv_silu(k_raw, kp_scr[...], cw_ref[:, M_WIDTH:], cb_ref[:, M_WIDTH:]))
    kp_scr[...] = k_raw[TS - SUBLANES:]
    v = groups(v_ref[0])
    yield

    ti = lax.broadcasted_iota(jnp.int32, (L, L), 0)
    si = lax.broadcasted_iota(jnp.int32, (L, L), 1)
    causal = si <= ti
    diag = ti == si
    gr = grow_ref[0] + brow_ref[...]
    gc = gcol_ref[0] + bcol_ref[...]
    logi_r = gr[:, :H].reshape(G, 1, L)
    logf_r = log_sigmoid(gr[:, H:].reshape(G, 1, L))
    logi_c = jnp.stack([gc[c, :, h:h + 1] for c in range(nch) for h in hs])
    yield
    b_c = jnp.sum(jnp.where(causal, logf_r, 0.0), axis=2, keepdims=True)
    b_r = jnp.sum(jnp.where(diag, b_c, 0.0), axis=1, keepdims=True)
    g = jnp.sum(logf_r, axis=2, keepdims=True)
    a_r = g - b_r + logi_r
    a_c = g - b_c + logi_c
    m_loc = jnp.max(a_r, axis=2, keepdims=True)
    wa_c = jnp.exp(a_c - m_loc)
    yield
    c_loc = per_group(_bdot_tn, v * wa_c, k)
    yield
    n_loc = jnp.sum(k * wa_c, axis=1, keepdims=True)
    d = jnp.where(causal, b_c - b_r + logi_r, -jnp.inf)
    d_max = jnp.max(d, axis=2, keepdims=True)
    yield
    qk = per_group(_bdot_nt, q, k)
    yield

    c_prev = c_scr[...]
    n_prev = n_scr[...]
    m_prev = m_scr[...]
    c_in, n_in, m_in = [], [], []
    for c in range(nch):
        gs = slice(c * H, (c + 1) * H)
        c_in.append(c_prev)
        n_in.append(n_prev)
        m_in.append(m_prev)
        m_new = jnp.maximum(g[gs] + m_prev, m_loc[gs])
        s_old = jnp.exp(g[gs] + m_prev - m_new)
        s_loc = jnp.exp(m_loc[gs] - m_new)
        c_prev = s_old * c_prev + s_loc * c_loc[gs]
        n_prev = s_old * n_prev + s_loc * n_loc[gs]
        m_prev = m_new
    c_scr[...] = c_prev
    n_scr[...] = n_prev
    m_scr[...] = m_prev
    c_in = jnp.concatenate(c_in, axis=0)
    n_in = jnp.concatenate(n_in, axis=0)
    m_in = jnp.concatenate(m_in, axis=0)
    yield

    inter = b_c + m_in
    m_t = jnp.maximum(inter, d_max)
    s_int = jnp.exp(inter - m_t)
    p = jnp.exp(d - m_t) * qk
    yield
    num = s_int * per_group(_bdot_nt, q, c_in) + per_group(_bdot, p, v)
    yield
    den = (s_int * jnp.sum(q * n_in, axis=2, keepdims=True)
           + jnp.sum(p, axis=2, keepdims=True))
    hh = num / jnp.maximum(jnp.abs(den), jnp.exp(-m_t))
    yield
    mu = jnp.mean(hh, axis=-1, keepdims=True)
    hc = hh - mu
    var = jnp.mean(hc * hc, axis=-1, keepdims=True)
    ng = jnp.stack([ng_ref[:, h * M_HDIM:(h + 1) * M_HDIM] for h in hs] * nch)
    y = (_sigmoid(groups(o_ref[0])) * (hc * lax.rsqrt(var + NORM_EPS)) * ng).astype(BF16)
    result.append(jnp.concatenate(
        [jnp.concatenate([y[c * H + h] for h in hs], axis=1) for c in range(nch)], axis=0))


def _rwkv_stages(pr_ref, pk_ref, pv_ref, pwa_ref, pg_ref,
                 mur_ref, muk_ref, muv_ref, muwa_ref, mug_ref,
                 w0_ref, wup_ref, a0_ref, aup_ref, gup_ref,
                 kkp_ref, ka_ref, rk_ref, gng_ref, gnb_ref,
                 cr_scr, ck_scr, cv_scr, cwa_scr, cg_scr, st_scr, result):
    TS = pr_ref.shape[1]
    L = CHUNK
    N = R_HDIM
    nch = TS // L
    npair = R_WIDTH // LANES
    pairs = range(npair)
    Q = 2 * LANES

    def tshift(p_ref, mu_ref, c_scr):
        p = p_ref[0]
        prev = _shift_rows_carry(p, c_scr[...], 1)
        c_scr[...] = p[TS - SUBLANES:]
        return p + (prev - p) * mu_ref[...]

    hsh = N.bit_length() - 1
    li = lax.broadcasted_iota(jnp.int32, (LANES, LANES), 0)
    lj = lax.broadcasted_iota(jnp.int32, (LANES, LANES), 1)
    same_head = jnp.right_shift(li, hsh) == jnp.right_shift(lj, hsh)
    ones_bd = jnp.where(same_head, 1.0, 0.0).astype(BF16)

    def seg_sum(x):
        outs = []
        for p in range(x.shape[1] // LANES):
            hi, lo = _split2(x[:, p * LANES:(p + 1) * LANES])
            outs.append(_dot(hi, ones_bd) + _dot(lo, ones_bd))
        return jnp.concatenate(outs, axis=1)

    rr = tshift(pr_ref, mur_ref, cr_scr)
    kr = tshift(pk_ref, muk_ref, ck_scr)
    vr = tshift(pv_ref, muv_ref, cv_scr)
    yield
    wa = tshift(pwa_ref, muwa_ref, cwa_scr)
    gd = tshift(pg_ref, mug_ref, cg_scr)
    logw = -DECAY_SCALE * _sigmoid(w0_ref[...] + _bdot(jnp.tanh(wa), wup_ref[...]))
    a = _sigmoid(a0_ref[...] + _bdot(wa, aup_ref[...]))
    g = _bdot(_sigmoid(gd), gup_ref[...])
    yield
    kkraw = kr * kkp_ref[...]
    kk = kkraw * lax.rsqrt(jnp.maximum(seg_sum(kkraw * kkraw), 1e-24))
    km = kr * (1.0 + (a - 1.0) * ka_ref[...])
    be = kk * a
    yield
    tq = lax.broadcasted_iota(jnp.int32, (Q, Q), 0)
    sq = lax.broadcasted_iota(jnp.int32, (Q, Q), 1)
    tril = jnp.where((jnp.right_shift(tq, hsh) == jnp.right_shift(sq, hsh)) & (sq <= tq),
                     1.0, 0.0).astype(BF16)
    lw_hi, lw_lo = _split2(logw)
    bincl = jnp.concatenate(
        [_dot(tril, lw_hi[q * Q:(q + 1) * Q]) + _dot(tril, lw_lo[q * Q:(q + 1) * Q])
         for q in range(TS // Q)], axis=0)
    e_in = jnp.exp(bincl)
    e_ng = jnp.exp(-bincl)
    kt = kk * jnp.exp(bincl - logw)
    rt = rr * e_in
    bh = be * e_ng
    kh = km * e_ng
    yield

    h0 = lax.broadcasted_iota(jnp.int32, (L, LANES), 1) < N
    ti = lax.broadcasted_iota(jnp.int32, (2 * L, 2 * L), 0)
    si = lax.broadcasted_iota(jnp.int32, (2 * L, 2 * L), 1)
    same_blk = jnp.right_shift(ti, hsh) == jnp.right_shift(si, hsh)
    strict = same_blk & (si < ti)
    incl = same_blk & (si <= ti)

    def stack_heads(x):
        return jnp.concatenate([jnp.where(h0, x, 0.0), jnp.where(h0, 0.0, x)], axis=0)

    def stack_dup(x):
        return jnp.concatenate([x, x], axis=0)

    def comb(x):
        return jnp.where(h0, x[:L], x[L:])

    def off_mask(b):
        sh = (2 * b).bit_length() - 1
        same = jnp.right_shift(ti, sh) == jnp.right_shift(si, sh)
        return same & (jnp.bitwise_and(ti, b) != 0) & (jnp.bitwise_and(si, b) == 0)

    probs = [(p, c) for p in pairs for c in range(nch)]

    def tile(x, p, c):
        return x[c * L:(c + 1) * L, p * LANES:(p + 1) * LANES]

    lk = [stack_heads(tile(kt, p, c)).astype(BF16) for p, c in probs]
    lr = [stack_heads(tile(rt, p, c)).astype(BF16) for p, c in probs]
    rb = [stack_dup(tile(bh, p, c)).astype(BF16) for p, c in probs]
    rk = [stack_dup(tile(kh, p, c)).astype(BF16) for p, c in probs]
    vs = [stack_dup(tile(vr, p, c)).astype(BF16) for p, c in probs]
    yield
    a_bd = [jnp.where(strict, _bdot_nt(x, y), 0.0) for x, y in zip(lk, rb)]
    yield
    bk_bd = [jnp.where(strict, _bdot_nt(x, y), 0.0).astype(BF16) for x, y in zip(lk, rk)]
    yield
    arb_bd = [jnp.where(incl, _bdot_nt(x, y), 0.0).astype(BF16) for x, y in zip(lr, rb)]
    yield
    ark_bd = [jnp.where(incl, _bdot_nt(x, y), 0.0).astype(BF16) for x, y in zip(lr, rk)]
    yield
    eye = jnp.where(ti == si, 1.0, 0.0)
    m1 = off_mask(1)
    t_bd = [eye - jnp.where(m1, x, 0.0) for x in a_bd]
    b = 2
    while b < L:
        mb = off_mask(b)
        offs = [jnp.where(mb, x, 0.0).astype(BF16) for x in a_bd]
        xs = [_bdot(i, o) for i, o in zip(t_bd, offs)]
        yield
        ys = [_bdot(x, i) for x, i in zip(xs, t_bd)]
        t_bd = [i - y for i, y in zip(t_bd, ys)]
        yield
        b *= 2
    bkv = [comb(_bdot(x, y)) for x, y in zip(bk_bd, vs)]
    yield
    xs = [jnp.concatenate([stack_dup(tile(kt, p, c)), stack_dup(z)], axis=1)
          for (p, c), z in zip(probs, bkv)]
    tx = [_bdot(x, y) for x, y in zip(t_bd, xs)]
    yield
    arkv = [comb(_bdot(x, y)) for x, y in zip(ark_bd, vs)]
    w_ch = [[None] * nch for _ in pairs]
    u0_ch = [[None] * nch for _ in pairs]
    arkv_ch = [[None] * nch for _ in pairs]
    arb_ch = [[None] * nch for _ in pairs]
    for i, (p, c) in enumerate(probs):
        w_ch[p][c] = comb(tx[i][:, :LANES])
        u0_ch[p][c] = -comb(tx[i][:, LANES:])
        arkv_ch[p][c] = arkv[i]
        arb_ch[p][c] = arb_bd[i]
    yield

    y_rows = []
    st = [st_scr[p] for p in pairs]
    ones_f = jnp.where(same_head, 1.0, 0.0)
    lsl = [slice(p * LANES, (p + 1) * LANES) for p in pairs]
    for c in range(nch):
        rs = slice(c * L, (c + 1) * L)
        p_end = e_in[c * L + L - 1:c * L + L, :]
        pe = [p_end[:, ls] for ls in lsl]
        rw = [_bdot_nt(jnp.concatenate([rt[rs, lsl[p]], w_ch[p][c]], axis=0), st[p])
              for p in pairs]
        u = [u0_ch[p][c] - rw[p][L:] for p in pairs]
        au = [_bdot(arb_ch[p][c], jnp.concatenate([u[p], u[p]], axis=0)) for p in pairs]
        upd = [_bdot_tn(jnp.concatenate([u[p], vr[rs, lsl[p]]], axis=0),
                        jnp.concatenate([bh[rs, lsl[p]] * pe[p], kh[rs, lsl[p]] * pe[p]], axis=0))
               for p in pairs]
        st = [st[p] * pe[p] + upd[p] * ones_f for p in pairs]
        y_rows.append(jnp.concatenate(
            [rw[p][:L] + jnp.where(h0, au[p][:L], au[p][L:]) + arkv_ch[p][c] for p in pairs],
            axis=1))
        yield
    for p in pairs:
        st_scr[p] = st[p]
    y = jnp.concatenate(y_rows, axis=0)

    inv_n = 1.0 / N
    mu = seg_sum(y) * inv_n
    yc = y - mu
    var = seg_sum(yc * yc) * inv_n
    yn = yc * lax.rsqrt(var + GN_EPS) * gng_ref[...] + gnb_ref[...]
    bonus = seg_sum(rr * km * rk_ref[...]) * vr
    result.append(((yn + bonus) * g).astype(BF16))


def _mixer_kernel(*refs):
    (q_ref, k_ref, v_ref, o_ref, grow_ref, gcol_ref, brow_ref, bcol_ref, cw_ref, cb_ref, ng_ref,
     pr_ref, pk_ref, pv_ref, pwa_ref, pg_ref, mur_ref, muk_ref, muv_ref, muwa_ref, mug_ref,
     w0_ref, wup_ref, a0_ref, aup_ref, gup_ref, kkp_ref, ka_ref, rk_ref, gng_ref, gnb_ref,
     x_ref, wout_ref, out_ref,
     qp_scr, kp_scr, c_scr, n_scr, m_scr,
     cr_scr, ck_scr, cv_scr, cwa_scr, cg_scr, st_scr) = refs

    @pl.when(pl.program_id(1) == 0)
    def _():
        for scr in (qp_scr, kp_scr, c_scr, n_scr, m_scr,
                    cr_scr, ck_scr, cv_scr, cwa_scr, cg_scr, st_scr):
            scr[...] = jnp.zeros_like(scr)

    y_m, y_r = [], []
    mlstm = _mlstm_stages(q_ref, k_ref, v_ref, o_ref, grow_ref, gcol_ref, brow_ref, bcol_ref,
                          cw_ref, cb_ref, ng_ref, qp_scr, kp_scr, c_scr, n_scr, m_scr, y_m)
    rwkv = _rwkv_stages(pr_ref, pk_ref, pv_ref, pwa_ref, pg_ref,
                        mur_ref, muk_ref, muv_ref, muwa_ref, mug_ref,
                        w0_ref, wup_ref, a0_ref, aup_ref, gup_ref,
                        kkp_ref, ka_ref, rk_ref, gng_ref, gnb_ref,
                        cr_scr, ck_scr, cv_scr, cwa_scr, cg_scr, st_scr, y_r)
    for tag in "RRRRR" + "RMR" * 9 + "RM" * 4:
        next(rwkv if tag == "R" else mlstm, None)
    for stream in (mlstm, rwkv):
        for _ in stream:
            pass
    out_ref[0] = (x_ref[0] + _dot(y_m[0], wout_ref[:M_WIDTH, :])
                  + _dot(y_r[0], wout_ref[M_WIDTH:, :]))


def _mixer(p_all, g_row, g_col, b_row, b_col, conv_w, conv_b, norm_g,
           mu, w0, w_up, a0, a_up, g_up, kkp, ka, rk, gn_g, gn_b, x, w_out, layer):
    B, S, _ = p_all.shape
    D = x.shape[2]
    ts = min(SEQ_TILE, S)
    nch = ts // CHUNK
    npair = R_WIDTH // LANES
    seq = lambda w, off: pl.BlockSpec((1, ts, w), lambda b, s, off=off: (b, s, off))
    vec = lambda w, off: pl.BlockSpec((1, w), lambda b, s, off=off: (0, off))
    full = lambda a: pl.BlockSpec(a.shape, lambda b, s: (0,) * a.ndim)
    return pl.pallas_call(
        _mixer_kernel,
        out_shape=jax.ShapeDtypeStruct((B, S, D), F32),
        grid=(B, S // ts),
        in_specs=[
            seq(M_WIDTH, OFF_MQ // M_WIDTH), seq(M_WIDTH, OFF_MK // M_WIDTH),
            seq(M_WIDTH, OFF_MV // M_WIDTH), seq(M_WIDTH, OFF_MO // M_WIDTH),
            pl.BlockSpec((1, nch, 2 * M_HEADS, 1, CHUNK), lambda b, s: (b, s, 0, 0, 0)),
            pl.BlockSpec((1, nch, CHUNK, 2 * M_HEADS), lambda b, s: (b, s, 0, 0)),
            full(b_row), full(b_col), full(conv_w), full(conv_b), full(norm_g),
            seq(R_WIDTH, OFF_RR // R_WIDTH), seq(R_WIDTH, OFF_RK // R_WIDTH),
            seq(R_WIDTH, OFF_RV // R_WIDTH), seq(LANES, OFF_RWA // LANES),
            seq(GLORA_PAD, OFF_RG // GLORA_PAD),
            vec(R_WIDTH, OFF_RR // R_WIDTH), vec(R_WIDTH, OFF_RK // R_WIDTH),
            vec(R_WIDTH, OFF_RV // R_WIDTH), vec(LANES, OFF_RWA // LANES),
            vec(GLORA_PAD, OFF_RG // GLORA_PAD),
            full(w0), full(w_up), full(a0), full(a_up), full(g_up),
            full(kkp), full(ka), full(rk), full(gn_g), full(gn_b),
            seq(D, 0),
            _layer_weight_spec(w_out, layer),
        ],
        out_specs=pl.BlockSpec((1, ts, D), lambda b, s: (b, s, 0)),
        scratch_shapes=[pltpu.VMEM((SUBLANES, M_WIDTH), F32), pltpu.VMEM((SUBLANES, M_WIDTH), F32),
                        pltpu.VMEM((M_HEADS, M_HDIM, M_HDIM), F32),
                        pltpu.VMEM((M_HEADS, 1, M_HDIM), F32),
                        pltpu.VMEM((M_HEADS, 1, 1), F32),
                        pltpu.VMEM((SUBLANES, R_WIDTH), F32), pltpu.VMEM((SUBLANES, R_WIDTH), F32),
                        pltpu.VMEM((SUBLANES, R_WIDTH), F32), pltpu.VMEM((SUBLANES, LANES), F32),
                        pltpu.VMEM((SUBLANES, GLORA_PAD), F32),
                        pltpu.VMEM((npair, LANES, LANES), F32)],
        compiler_params=pltpu.CompilerParams(
            dimension_semantics=("parallel", "arbitrary"), vmem_limit_bytes=VMEM_LIMIT),
        name="mixer",
    )(p_all, p_all, p_all, p_all, g_row, g_col, b_row, b_col, conv_w, conv_b, norm_g,
      p_all, p_all, p_all, p_all, p_all, mu, mu, mu, mu, mu,
      w0, w_up, a0, a_up, g_up, kkp, ka, rk, gn_g, gn_b, x, w_out)


def _xattn_kernel(x_ref, g_ref, wq_ref, kv_ref, wo_ref, o_ref):
    D = x_ref.shape[1]
    x = x_ref[...]
    q = _dot(_rms_bf16(x, g_ref[...]), wq_ref[...]).astype(BF16)
    hsl = [slice(hd * X_HDIM, (hd + 1) * X_HDIM) for hd in range(D // X_HDIM)]
    s = [lax.dot_general(q[:, ls], kv_ref[0, :, ls], (((1,), (1,)), ((), ())),
                         preferred_element_type=F32) * (X_HDIM ** -0.5) for ls in hsl]
    e = [jnp.exp(si - jnp.max(si, axis=-1, keepdims=True)) for si in s]
    p = [(ei / jnp.sum(ei, axis=-1, keepdims=True)).astype(BF16) for ei in e]
    heads = [_dot(pi, kv_ref[0, :, D + ls.start:D + ls.stop]).astype(BF16)
             for pi, ls in zip(p, hsl)]
    o_ref[...] = x + _dot(jnp.concatenate(heads, axis=1), wo_ref[...])


def _xattn(x, gain, wq, kv, wo, layer, seq_len, tm=1024):
    T, D = x.shape
    M = kv.shape[1]
    tm = min(tm, seq_len)
    per_seq = seq_len // tm
    return pl.pallas_call(
        _xattn_kernel,
        out_shape=jax.ShapeDtypeStruct((T, D), F32),
        grid=(T // tm,),
        in_specs=[pl.BlockSpec((tm, D), lambda i: (i, 0)),
                  pl.BlockSpec((1, D), lambda i: (0, 0)),
                  _layer_weight_spec(wq, layer),
                  pl.BlockSpec((1, M, 2 * D), lambda i: (i // per_seq, 0, 0)),
                  _layer_weight_spec(wo, layer)],
        out_specs=pl.BlockSpec((tm, D), lambda i: (i, 0)),
        compiler_params=pltpu.CompilerParams(
            dimension_semantics=("parallel",), vmem_limit_bytes=VMEM_LIMIT),
        name="xattn",
    )(x, gain.reshape(1, D), wq, kv, wo)


def _ffn_kernel(*refs, per_seq, tc, norm_out):
    x_ref, g_ref, wup_ref, cw_ref, cb_ref, wdn_ref = refs[:6]
    gout_ref = refs[6] if norm_out else None
    o_ref, tail_scr = refs[-2:]
    tm = x_ref.shape[0]

    @pl.when(lax.rem(pl.program_id(0), per_seq) == 0)
    def _():
        tail_scr[...] = jnp.zeros_like(tail_scr)

    x = x_ref[...]
    h = _rms_bf16(x, g_ref[...])
    chunks = [slice(c * tc, (c + 1) * tc) for c in range(D_FF // tc)]
    gates = [_dot(h, wup_ref[:, cols]) for cols in chunks]
    vals = [_dot(h, wup_ref[:, D_FF + cols.start:D_FF + cols.stop]) for cols in chunks]
    acts = []
    for cols, gate, val in zip(chunks, gates, vals):
        prev = tail_scr[:, cols]
        y = cb_ref[:, cols] + gate * cw_ref[FFN_CONV - 1:FFN_CONV, cols]
        for j in range(FFN_CONV - 1):
            y = y + _shift_rows_carry(gate, prev, FFN_CONV - 1 - j) * cw_ref[j:j + 1, cols]
        tail_scr[:, cols] = gate[tm - SUBLANES:]
        acts.append((y * _sigmoid(y) * val).astype(BF16))
    acc = x
    for cols, act in zip(chunks, acts):
        acc = acc + _dot(act, wdn_ref[cols, :])
    if norm_out:
        ms = jnp.mean(acc * acc, axis=-1, keepdims=True)
        acc = acc * lax.rsqrt(ms + NORM_EPS) * gout_ref[...]
    o_ref[...] = acc


def _ffn(x, gain, w_up, conv_w, conv_b, w_down, layer, seq_len, out_gain=None, tm=512, tc=256):
    T, D = x.shape
    tm = min(tm, seq_len)
    norm_out = out_gain is not None
    const = lambda a: pl.BlockSpec(a.shape, lambda i: (0,) * a.ndim, pipeline_mode=pl.Buffered(1))
    small = [conv_w, conv_b.reshape(1, D_FF)] + ([out_gain.reshape(1, D)] if norm_out else [])
    specs = [const(a) for a in small]
    return pl.pallas_call(
        functools.partial(_ffn_kernel, per_seq=seq_len // tm, tc=tc, norm_out=norm_out),
        out_shape=jax.ShapeDtypeStruct((T, D), F32),
        grid=(T // tm,),
        in_specs=[pl.BlockSpec((tm, D), lambda i: (i, 0)),
                  pl.BlockSpec((1, D), lambda i: (0, 0)),
                  _layer_weight_spec(w_up, layer), specs[0], specs[1],
                  _layer_weight_spec(w_down, layer)] + specs[2:],
        out_specs=pl.BlockSpec((tm, D), lambda i: (i, 0)),
        scratch_shapes=[pltpu.VMEM((SUBLANES, D_FF), F32)],
        compiler_params=pltpu.CompilerParams(
            dimension_semantics=("arbitrary",), vmem_limit_bytes=VMEM_LIMIT),
        name="ffn",
    )(x, gain.reshape(1, D), w_up, small[0], small[1], w_down, *small[2:])


def _pad_cols(w, n):
    return jnp.pad(w, [(0, 0)] * (w.ndim - 1) + [(0, n - w.shape[-1])])


def _relayout_in(w):
    m_main = w[..., :4 * M_WIDTH]
    m_gate = w[..., 4 * M_WIDTH:4 * M_WIDTH + 2 * M_HEADS]
    r0 = 4 * M_WIDTH + 2 * M_HEADS
    r_main = w[..., r0:r0 + 3 * R_WIDTH + R_DECAY_LORA + R_AAA_LORA]
    r_gate = w[..., r0 + 3 * R_WIDTH + R_DECAY_LORA + R_AAA_LORA:]
    return jnp.concatenate(
        [m_main, r_main, _pad_cols(m_gate, GATE_PAD), _pad_cols(r_gate, GLORA_PAD)], axis=-1)


def kernel(x, mem, norm_mix, w_in, m_conv_w, m_conv_b, m_gate_b, m_norm_g, r_mu, r_w0,
           r_w_up, r_a0, r_a_up, r_g_up, r_kk, r_ka, r_rk, r_gn_g, r_gn_b, w_out,
           norm_x, norm_mem, x_wq, x_wkv, x_wo, norm_ffn, f_up, f_conv_w, f_conv_b,
           f_down, norm_final):
    B, S, D = x.shape
    M = mem.shape[1]
    depth = w_in.shape[0]
    T = B * S
    nc = S // CHUNK
    xf = x.reshape(T, D)
    memf = mem.reshape(B * M, D)
    row = lambda a: a.reshape(1, -1)
    w_in_p, w_out_b = _relayout_in(w_in).astype(BF16), w_out.astype(BF16)
    wq_b, wkv_b, wo_b = x_wq.astype(BF16), x_wkv.astype(BF16), x_wo.astype(BF16)
    f_up_b, f_down_b = f_up.astype(BF16), f_down.astype(BF16)

    for l in range(depth):
        p_all = _mm(xf, w_in_p, l, gain=norm_mix[l]).reshape(B, S, IN_COLS_P)

        g_col = p_all[:, :, OFF_MG:OFF_MG + 2 * M_HEADS].reshape(B, nc, CHUNK, 2 * M_HEADS)
        g_row = g_col.swapaxes(-1, -2)[:, :, :, None, :]
        mu = _relayout_in(jnp.pad(row(r_mu[l]), ((0, 0), (4 * M_WIDTH + 2 * M_HEADS, 0))))
        w_up = jnp.pad(r_w_up[l], ((0, R_AAA_LORA), (0, 0))).astype(BF16)
        a_up = jnp.pad(r_a_up[l], ((R_DECAY_LORA, 0), (0, 0))).astype(BF16)
        g_up = jnp.pad(r_g_up[l], ((0, GLORA_PAD - R_GATE_LORA), (0, 0))).astype(BF16)
        xf = _mixer(p_all, g_row, g_col, m_gate_b[l].reshape(-1, 1, 1), row(m_gate_b[l]),
                    m_conv_w[l], row(m_conv_b[l]), row(m_norm_g[l]),
                    mu, row(r_w0[l]), w_up, row(r_a0[l]), a_up, g_up,
                    row(r_kk[l]), row(r_ka[l]), row(r_rk[l]), row(r_gn_g[l]), row(r_gn_b[l]),
                    xf.reshape(B, S, D), w_out_b, l).reshape(T, D)

        kv = _mm(memf, wkv_b, l, gain=norm_mem[l], out_dtype=BF16)
        xf = _xattn(xf, norm_x[l], wq_b, kv.reshape(B, M, 2 * D), wo_b, l, S)

        xf = _ffn(xf, norm_ffn[l], f_up_b, f_conv_w[l], f_conv_b[l], f_down_b, l, S,
                  out_gain=norm_final if l == depth - 1 else None)

    return xf.reshape(B, S, D)
```

```python
import functools
import math

import jax
import jax.numpy as jnp
from jax import lax
from jax.experimental import pallas as pl
from jax.experimental.pallas import tpu as pltpu

F32 = jnp.float32
BF16 = jnp.bfloat16

D_MODEL = 1024
M_WIDTH = 512
M_HEADS = 4
M_HDIM = 128
M_CONV = 4
R_WIDTH = 512
R_HDIM = 64
R_HEADS = 8
R_DECAY_LORA = 64
R_AAA_LORA = 64
R_GATE_LORA = 160
DECAY_SCALE = math.exp(-0.5)
X_HEADS = 4
X_HDIM = 256
D_FF = 2816
FFN_CONV = 3
NORM_EPS = 1e-6
GN_EPS = 64e-5
CHUNK = 64

LANES = 128
SUBLANES = 8
GATE_PAD = LANES
GLORA_PAD = 2 * LANES
OFF_MQ, OFF_MK, OFF_MV, OFF_MO = 0, 512, 1024, 1536
OFF_RR = 2048
OFF_RK = OFF_RR + R_WIDTH
OFF_RV = OFF_RK + R_WIDTH
OFF_RWA = OFF_RV + R_WIDTH
OFF_MG = OFF_RWA + LANES
OFF_RG = OFF_MG + GATE_PAD
IN_COLS_P = OFF_RG + GLORA_PAD

SEQ_TILE = 256
VMEM_LIMIT = 48 * 1024 * 1024


def _dot(a, b):
    return jnp.dot(a, b, preferred_element_type=F32)


def _bdot(a, b):
    return jnp.dot(a.astype(BF16), b.astype(BF16), preferred_element_type=F32)


def _bdot_nt(a, b):
    return lax.dot_general(a.astype(BF16), b.astype(BF16), (((1,), (1,)), ((), ())),
                           preferred_element_type=F32)


def _bdot_tn(a, b):
    return lax.dot_general(a.astype(BF16), b.astype(BF16), (((0,), (0,)), ((), ())),
                           preferred_element_type=F32)


def _split2(x):
    hi = x.astype(BF16)
    lo = (x - hi.astype(F32)).astype(BF16)
    return hi, lo


def _sigmoid(x):
    return 0.5 * jnp.tanh(0.5 * x) + 0.5


def _shift_rows_carry(x, prev, sh):
    r = pltpu.roll(x, sh, 0)
    row = lax.broadcasted_iota(jnp.int32, prev.shape, 0)
    head = jnp.where(row < sh, pltpu.roll(prev, sh, 0), r[:SUBLANES])
    return jnp.concatenate([head, r[SUBLANES:]], axis=0)


def _rms_bf16(x, gain):
    ms = jnp.mean(x * x, axis=-1, keepdims=True)
    return (x * lax.rsqrt(ms + NORM_EPS) * gain).astype(BF16)


def _layer_weight_spec(w, layer):
    return pl.BlockSpec((None,) + w.shape[1:], lambda *_: (layer,) + (0,) * (w.ndim - 1),
                        pipeline_mode=pl.Buffered(1))


def _mm_kernel(x_ref, w_ref, g_ref, o_ref, h_scr, *, tn):
    h_scr[...] = _rms_bf16(x_ref[...], g_ref[...])
    for c in range(w_ref.shape[1] // tn):
        cols = slice(c * tn, (c + 1) * tn)
        o_ref[:, cols] = _dot(h_scr[...], w_ref[:, cols]).astype(o_ref.dtype)


def _mm_tile_rows(K, N, x_bytes, out_bytes):
    budget = (VMEM_LIMIT * 3) // 4 - K * N * 2
    for tm in (512, 256, 128):
        if tm * (2 * K * x_bytes + 2 * N * out_bytes + K * 2) <= budget:
            return tm
    raise ValueError("weight does not fit in VMEM")


def _mm(x, w, layer, gain, out_dtype=F32, tn=512):
    T, K = x.shape
    N = w.shape[2]
    tm = min(T, _mm_tile_rows(K, N, x.dtype.itemsize, jnp.dtype(out_dtype).itemsize))
    tn = min(tn, N)
    assert T % tm == 0 and N % tn == 0
    return pl.pallas_call(
        functools.partial(_mm_kernel, tn=tn),
        out_shape=jax.ShapeDtypeStruct((T, N), out_dtype),
        grid=(T // tm,),
        in_specs=[pl.BlockSpec((tm, K), lambda i: (i, 0)),
                  _layer_weight_spec(w, layer),
                  pl.BlockSpec((1, K), lambda i: (0, 0))],
        out_specs=pl.BlockSpec((tm, N), lambda i: (i, 0)),
        scratch_shapes=[pltpu.VMEM((tm, K), BF16)],
        compiler_params=pltpu.CompilerParams(
            dimension_semantics=("parallel",), vmem_limit_bytes=VMEM_LIMIT),
        name="mm",
    )(x, w, gain.reshape(1, K))


def _mlstm_stages(q_ref, k_ref, v_ref, o_ref, g_ref, brow_ref, bcol_ref,
                  cw_ref, cb_ref, ng_ref, qp_scr, kp_scr, c_scr, n_scr, m_scr, result):
    TS = q_ref.shape[0]
    L = CHUNK
    nch = TS // L
    H = M_HEADS
    G = nch * H
    hs = range(H)

    def conv_silu(x, prev, w, b):
        y = b + x * w[M_CONV - 1:M_CONV, :]
        for j in range(M_CONV - 1):
            y = y + _shift_rows_carry(x, prev, M_CONV - 1 - j) * w[j:j + 1, :]
        return y * _sigmoid(y)

    def groups(x):
        return jnp.stack([x[c * L:(c + 1) * L, h * M_HDIM:(h + 1) * M_HDIM]
                          for c in range(nch) for h in hs])

    def per_group(f, *xs):
        return jnp.stack([f(*[x[i] for x in xs]) for i in range(G)])

    def log_sigmoid(x):
        return jnp.minimum(x, 0.0) - jnp.log1p(jnp.exp(-jnp.abs(x)))

    q_raw = q_ref[...]
    q = groups(conv_silu(q_raw, qp_scr[...], cw_ref[:, :M_WIDTH], cb_ref[:, :M_WIDTH])
               * (M_HDIM ** -0.5))
    qp_scr[...] = q_raw[TS - SUBLANES:]
    yield
    k_raw = k_ref[...]
    k = groups(conv_silu(k_raw, kp_scr[...], cw_ref[:, M_WIDTH:], cb_ref[:, M_WIDTH:]))
    kp_scr[...] = k_raw[TS - SUBLANES:]
    v = groups(v_ref[...])
    yield

    ti = lax.broadcasted_iota(jnp.int32, (L, L), 0)
    si = lax.broadcasted_iota(jnp.int32, (L, L), 1)
    causal = si <= ti
    diag = ti == si
    pg = g_ref[...]
    zpad = jnp.zeros((GATE_PAD - L, GATE_PAD), F32)
    pg_t = [jnp.concatenate([pg[c * L:(c + 1) * L], zpad], axis=0).T for c in range(nch)]
    logi_r = jnp.stack([pg_t[c][h:h + 1, :L] + brow_ref[h]
                        for c in range(nch) for h in hs])
    logf_r = log_sigmoid(jnp.stack([pg_t[c][H + h:H + h + 1, :L] + brow_ref[H + h]
                                    for c in range(nch) for h in hs]))
    gc = pg[:, :2 * H] + bcol_ref[...]
    logi_c = jnp.stack([gc[c * L:(c + 1) * L, h:h + 1] for c in range(nch) for h in hs])
    yield
    b_c = jnp.sum(jnp.where(causal, logf_r, 0.0), axis=2, keepdims=True)
    b_r = jnp.sum(jnp.where(diag, b_c, 0.0), axis=1, keepdims=True)
    g = jnp.sum(logf_r, axis=2, keepdims=True)
    a_r = g - b_r + logi_r
    a_c = g - b_c + logi_c
    m_loc = jnp.max(a_r, axis=2, keepdims=True)
    wa_c = jnp.exp(a_c - m_loc)
    yield
    c_loc = per_group(_bdot_tn, v * wa_c, k)
    yield
    n_loc = jnp.sum(k * wa_c, axis=1, keepdims=True)
    d = jnp.where(causal, b_c - b_r + logi_r, -jnp.inf)
    d_max = jnp.max(d, axis=2, keepdims=True)
    yield
    qk = per_group(_bdot_nt, q, k)
    yield

    c_prev = c_scr[...]
    n_prev = n_scr[...]
    m_prev = m_scr[...]
    c_in, n_in, m_in = [], [], []
    for c in range(nch):
        gs = slice(c * H, (c + 1) * H)
        c_in.append(c_prev)
        n_in.append(n_prev)
        m_in.append(m_prev)
        m_new = jnp.maximum(g[gs] + m_prev, m_loc[gs])
        s_old = jnp.exp(g[gs] + m_prev - m_new)
        s_loc = jnp.exp(m_loc[gs] - m_new)
        c_prev = s_old * c_prev + s_loc * c_loc[gs]
        n_prev = s_old * n_prev + s_loc * n_loc[gs]
        m_prev = m_new
    c_scr[...] = c_prev
    n_scr[...] = n_prev
    m_scr[...] = m_prev
    c_in = jnp.concatenate(c_in, axis=0)
    n_in = jnp.concatenate(n_in, axis=0)
    m_in = jnp.concatenate(m_in, axis=0)
    yield

    inter = b_c + m_in
    m_t = jnp.maximum(inter, d_max)
    s_int = jnp.exp(inter - m_t)
    p = jnp.exp(d - m_t) * qk
    yield
    num = s_int * per_group(_bdot_nt, q, c_in) + per_group(_bdot, p, v)
    yield
    den = (s_int * jnp.sum(q * n_in, axis=2, keepdims=True)
           + jnp.sum(p, axis=2, keepdims=True))
    hh = num / jnp.maximum(jnp.abs(den), jnp.exp(-m_t))
    yield
    mu = jnp.mean(hh, axis=-1, keepdims=True)
    hc = hh - mu
    var = jnp.mean(hc * hc, axis=-1, keepdims=True)
    ng = jnp.stack([ng_ref[:, h * M_HDIM:(h + 1) * M_HDIM] for h in hs] * nch)
    y = (_sigmoid(groups(o_ref[...])) * (hc * lax.rsqrt(var + NORM_EPS)) * ng).astype(BF16)
    result.append(jnp.concatenate(
        [jnp.concatenate([y[c * H + h] for h in hs], axis=1) for c in range(nch)], axis=0))


def _rwkv_stages(pr_ref, pk_ref, pv_ref, pwa_ref, pg_ref,
                 mur_ref, muk_ref, muv_ref, muwa_ref, mug_ref,
                 w0_ref, wup_ref, a0_ref, aup_ref, gup_ref,
                 kkp_ref, ka_ref, rk_ref, gng_ref, gnb_ref,
                 cr_scr, ck_scr, cv_scr, cwa_scr, cg_scr, st_scr, result):
    TS = pr_ref.shape[0]
    L = CHUNK
    N = R_HDIM
    nch = TS // L
    npair = R_WIDTH // LANES
    pairs = range(npair)
    Q = 2 * LANES

    def tshift(p_ref, mu_ref, c_scr):
        p = p_ref[...]
        prev = _shift_rows_carry(p, c_scr[...], 1)
        c_scr[...] = p[TS - SUBLANES:]
        return p + (prev - p) * mu_ref[...]

    hsh = N.bit_length() - 1
    li = lax.broadcasted_iota(jnp.int32, (LANES, LANES), 0)
    lj = lax.broadcasted_iota(jnp.int32, (LANES, LANES), 1)
    same_head = jnp.right_shift(li, hsh) == jnp.right_shift(lj, hsh)
    ones_bd = jnp.where(same_head, 1.0, 0.0).astype(BF16)

    def seg_sum(x):
        outs = []
        for p in range(x.shape[1] // LANES):
            hi, lo = _split2(x[:, p * LANES:(p + 1) * LANES])
            outs.append(_dot(hi, ones_bd) + _dot(lo, ones_bd))
        return jnp.concatenate(outs, axis=1)

    rr = tshift(pr_ref, mur_ref, cr_scr)
    kr = tshift(pk_ref, muk_ref, ck_scr)
    vr = tshift(pv_ref, muv_ref, cv_scr)
    yield
    wa = tshift(pwa_ref, muwa_ref, cwa_scr)
    gd = tshift(pg_ref, mug_ref, cg_scr)
    logw = -DECAY_SCALE * _sigmoid(w0_ref[...] + _bdot(jnp.tanh(wa), wup_ref[...]))
    a = _sigmoid(a0_ref[...] + _bdot(wa, aup_ref[...]))
    g = _bdot(_sigmoid(gd), gup_ref[...])
    yield
    kkraw = kr * kkp_ref[...]
    kk = kkraw * lax.rsqrt(jnp.maximum(seg_sum(kkraw * kkraw), 1e-24))
    km = kr * (1.0 + (a - 1.0) * ka_ref[...])
    be = kk * a
    yield
    tq = lax.broadcasted_iota(jnp.int32, (Q, Q), 0)
    sq = lax.broadcasted_iota(jnp.int32, (Q, Q), 1)
    tril = jnp.where((jnp.right_shift(tq, hsh) == jnp.right_shift(sq, hsh)) & (sq <= tq),
                     1.0, 0.0).astype(BF16)
    lw_hi, lw_lo = _split2(logw)
    bincl = jnp.concatenate(
        [_dot(tril, lw_hi[q * Q:(q + 1) * Q]) + _dot(tril, lw_lo[q * Q:(q + 1) * Q])
         for q in range(TS // Q)], axis=0)
    e_in = jnp.exp(bincl)
    e_ng = jnp.exp(-bincl)
    kt = kk * jnp.exp(bincl - logw)
    rt = rr * e_in
    bh = be * e_ng
    kh = km * e_ng
    yield

    h0 = lax.broadcasted_iota(jnp.int32, (L, LANES), 1) < N
    ti = lax.broadcasted_iota(jnp.int32, (2 * L, 2 * L), 0)
    si = lax.broadcasted_iota(jnp.int32, (2 * L, 2 * L), 1)
    same_blk = jnp.right_shift(ti, hsh) == jnp.right_shift(si, hsh)
    strict = same_blk & (si < ti)
    incl = same_blk & (si <= ti)

    def stack_heads(x):
        return jnp.concatenate([jnp.where(h0, x, 0.0), jnp.where(h0, 0.0, x)], axis=0)

    def stack_dup(x):
        return jnp.concatenate([x, x], axis=0)

    def comb(x):
        return jnp.where(h0, x[:L], x[L:])

    def off_mask(b):
        sh = (2 * b).bit_length() - 1
        same = jnp.right_shift(ti, sh) == jnp.right_shift(si, sh)
        return same & (jnp.bitwise_and(ti, b) != 0) & (jnp.bitwise_and(si, b) == 0)

    probs = [(p, c) for p in pairs for c in range(nch)]

    def tile(x, p, c):
        return x[c * L:(c + 1) * L, p * LANES:(p + 1) * LANES]

    lk = [stack_heads(tile(kt, p, c)).astype(BF16) for p, c in probs]
    lr = [stack_heads(tile(rt, p, c)).astype(BF16) for p, c in probs]
    rb = [stack_dup(tile(bh, p, c)).astype(BF16) for p, c in probs]
    rk = [stack_dup(tile(kh, p, c)).astype(BF16) for p, c in probs]
    vs = [stack_dup(tile(vr, p, c)).astype(BF16) for p, c in probs]
    yield
    a_bd = [jnp.where(strict, _bdot_nt(x, y), 0.0) for x, y in zip(lk, rb)]
    yield
    bk_bd = [jnp.where(strict, _bdot_nt(x, y), 0.0).astype(BF16) for x, y in zip(lk, rk)]
    yield
    arb_bd = [jnp.where(incl, _bdot_nt(x, y), 0.0).astype(BF16) for x, y in zip(lr, rb)]
    yield
    ark_bd = [jnp.where(incl, _bdot_nt(x, y), 0.0).astype(BF16) for x, y in zip(lr, rk)]
    yield
    eye = jnp.where(ti == si, 1.0, 0.0)
    m1 = off_mask(1)
    t_bd = [eye - jnp.where(m1, x, 0.0) for x in a_bd]
    b = 2
    while b < L:
        mb = off_mask(b)
        offs = [jnp.where(mb, x, 0.0).astype(BF16) for x in a_bd]
        xs = [_bdot(i, o) for i, o in zip(t_bd, offs)]
        yield
        ys = [_bdot(x, i) for x, i in zip(xs, t_bd)]
        t_bd = [i - y for i, y in zip(t_bd, ys)]
        yield
        b *= 2
    bkv = [comb(_bdot(x, y)) for x, y in zip(bk_bd, vs)]
    yield
    xs = [jnp.concatenate([stack_dup(tile(kt, p, c)), stack_dup(z)], axis=1)
          for (p, c), z in zip(probs, bkv)]
    tx = [_bdot(x, y) for x, y in zip(t_bd, xs)]
    yield
    arkv = [comb(_bdot(x, y)) for x, y in zip(ark_bd, vs)]
    w_ch = [[None] * nch for _ in pairs]
    u0_ch = [[None] * nch for _ in pairs]
    arkv_ch = [[None] * nch for _ in pairs]
    arb_ch = [[None] * nch for _ in pairs]
    for i, (p, c) in enumerate(probs):
        w_ch[p][c] = comb(tx[i][:, :LANES])
        u0_ch[p][c] = -comb(tx[i][:, LANES:])
        arkv_ch[p][c] = arkv[i]
        arb_ch[p][c] = arb_bd[i]
    yield

    y_rows = []
    st = [st_scr[p] for p in pairs]
    ones_f = jnp.where(same_head, 1.0, 0.0)
    lsl = [slice(p * LANES, (p + 1) * LANES) for p in pairs]
    for c in range(nch):
        rs = slice(c * L, (c + 1) * L)
        p_end = e_in[c * L + L - 1:c * L + L, :]
        pe = [p_end[:, ls] for ls in lsl]
        rw = [_bdot_nt(jnp.concatenate([rt[rs, lsl[p]], w_ch[p][c]], axis=0), st[p])
              for p in pairs]
        u = [u0_ch[p][c] - rw[p][L:] for p in pairs]
        au = [_bdot(arb_ch[p][c], jnp.concatenate([u[p], u[p]], axis=0)) for p in pairs]
        upd = [_bdot_tn(jnp.concatenate([u[p], vr[rs, lsl[p]]], axis=0),
                        jnp.concatenate([bh[rs, lsl[p]] * pe[p], kh[rs, lsl[p]] * pe[p]], axis=0))
               for p in pairs]
        st = [st[p] * pe[p] + upd[p] * ones_f for p in pairs]
        y_rows.append(jnp.concatenate(
            [rw[p][:L] + jnp.where(h0, au[p][:L], au[p][L:]) + arkv_ch[p][c] for p in pairs],
            axis=1))
        yield
    for p in pairs:
        st_scr[p] = st[p]
    y = jnp.concatenate(y_rows, axis=0)

    inv_n = 1.0 / N
    mu = seg_sum(y) * inv_n
    yc = y - mu
    var = seg_sum(yc * yc) * inv_n
    yn = yc * lax.rsqrt(var + GN_EPS) * gng_ref[...] + gnb_ref[...]
    bonus = seg_sum(rr * km * rk_ref[...]) * vr
    result.append(((yn + bonus) * g).astype(BF16))


def _mixer_kernel(*refs, tn):
    (x_ref, xn_ref, win_ref, gin_ref, brow_ref, bcol_ref, cw_ref, cb_ref, ng_ref,
     mur_ref, muk_ref, muv_ref, muwa_ref, mug_ref,
     w0_ref, wup_ref, a0_ref, aup_ref, gup_ref, kkp_ref, ka_ref, rk_ref, gng_ref, gnb_ref,
     wout_ref, out_ref,
     qp_scr, kp_scr, c_scr, n_scr, m_scr,
     cr_scr, ck_scr, cv_scr, cwa_scr, cg_scr, st_scr, p_scr) = refs
    step = pl.program_id(0) * pl.num_programs(1) + pl.program_id(1)
    chunks = [slice(c * tn, (c + 1) * tn) for c in range(win_ref.shape[1] // tn)]

    def project(h, dst_slot, cols_list):
        for cols in cols_list:
            p_scr[dst_slot, :, cols] = _dot(h, win_ref[:, cols])

    @pl.when(step == 0)
    def _():
        project(_rms_bf16(x_ref[0], gin_ref[...]), 0, chunks)

    @pl.when(pl.program_id(1) == 0)
    def _():
        for scr in (qp_scr, kp_scr, c_scr, n_scr, m_scr,
                    cr_scr, ck_scr, cv_scr, cwa_scr, cg_scr, st_scr):
            scr[...] = jnp.zeros_like(scr)

    slot = lax.rem(step, 2)
    view = lambda off, width: p_scr.at[slot, :, pl.ds(off, width)]
    y_m, y_r = [], []
    mlstm = _mlstm_stages(view(OFF_MQ, M_WIDTH), view(OFF_MK, M_WIDTH), view(OFF_MV, M_WIDTH),
                          view(OFF_MO, M_WIDTH), view(OFF_MG, GATE_PAD), brow_ref, bcol_ref,
                          cw_ref, cb_ref, ng_ref, qp_scr, kp_scr, c_scr, n_scr, m_scr, y_m)
    rwkv = _rwkv_stages(view(OFF_RR, R_WIDTH), view(OFF_RK, R_WIDTH), view(OFF_RV, R_WIDTH),
                        view(OFF_RWA, LANES), view(OFF_RG, GLORA_PAD),
                        mur_ref, muk_ref, muv_ref, muwa_ref, mug_ref,
                        w0_ref, wup_ref, a0_ref, aup_ref, gup_ref,
                        kkp_ref, ka_ref, rk_ref, gng_ref, gnb_ref,
                        cr_scr, ck_scr, cv_scr, cwa_scr, cg_scr, st_scr, y_r)
    def project_next():
        h_next = _rms_bf16(xn_ref[0], gin_ref[...])
        for cols in chunks:
            project(h_next, 1 - slot, [cols])
            yield

    proj = project_next()
    streams = {"R": rwkv, "M": mlstm, "P": proj}
    for tag in "PRPRPRPRR" + "RMR" * 9 + "PRMPRMPRMPRM":
        next(streams[tag], None)
    for stream in (proj, mlstm, rwkv):
        for _ in stream:
            pass
    out_ref[0] = (x_ref[0] + _dot(y_m[0], wout_ref[:M_WIDTH, :])
                  + _dot(y_r[0], wout_ref[M_WIDTH:, :]))


def _mixer(x, w_in, gain, b_row, b_col, conv_w, conv_b, norm_g,
           mu, w0, w_up, a0, a_up, g_up, kkp, ka, rk, gn_g, gn_b, w_out, layer, tn=512):
    B, S, D = x.shape
    ts = min(SEQ_TILE, S)
    n_seq = S // ts
    last = B * n_seq - 1
    npair = R_WIDTH // LANES
    vec = lambda w, off: pl.BlockSpec((1, w), lambda b, s, off=off: (0, off))
    full = lambda a: pl.BlockSpec(a.shape, lambda b, s: (0,) * a.ndim)

    def next_tile(b, s):
        t = jnp.minimum(b * n_seq + s + 1, last)
        return (t // n_seq, t % n_seq, 0)

    return pl.pallas_call(
        functools.partial(_mixer_kernel, tn=tn),
        out_shape=jax.ShapeDtypeStruct((B, S, D), F32),
        grid=(B, n_seq),
        in_specs=[
            pl.BlockSpec((1, ts, D), lambda b, s: (b, s, 0)),
            pl.BlockSpec((1, ts, D), next_tile),
            _layer_weight_spec(w_in, layer), full(gain),
            full(b_row), full(b_col), full(conv_w), full(conv_b), full(norm_g),
            vec(R_WIDTH, OFF_RR // R_WIDTH), vec(R_WIDTH, OFF_RK // R_WIDTH),
            vec(R_WIDTH, OFF_RV // R_WIDTH), vec(LANES, OFF_RWA // LANES),
            vec(GLORA_PAD, OFF_RG // GLORA_PAD),
            full(w0), full(w_up), full(a0), full(a_up), full(g_up),
            full(kkp), full(ka), full(rk), full(gn_g), full(gn_b),
            _layer_weight_spec(w_out, layer),
        ],
        out_specs=pl.BlockSpec((1, ts, D), lambda b, s: (b, s, 0)),
        scratch_shapes=[pltpu.VMEM((SUBLANES, M_WIDTH), F32), pltpu.VMEM((SUBLANES, M_WIDTH), F32),
                        pltpu.VMEM((M_HEADS, M_HDIM, M_HDIM), F32),
                        pltpu.VMEM((M_HEADS, 1, M_HDIM), F32),
                        pltpu.VMEM((M_HEADS, 1, 1), F32),
                        pltpu.VMEM((SUBLANES, R_WIDTH), F32), pltpu.VMEM((SUBLANES, R_WIDTH), F32),
                        pltpu.VMEM((SUBLANES, R_WIDTH), F32), pltpu.VMEM((SUBLANES, LANES), F32),
                        pltpu.VMEM((SUBLANES, GLORA_PAD), F32),
                        pltpu.VMEM((npair, LANES, LANES), F32),
                        pltpu.VMEM((2, ts, IN_COLS_P), F32)],
        compiler_params=pltpu.CompilerParams(
            dimension_semantics=("arbitrary", "arbitrary"), vmem_limit_bytes=VMEM_LIMIT),
        name="mixer",
    )(x, x, w_in, gain, b_row, b_col, conv_w, conv_b, norm_g, mu, mu, mu, mu, mu,
      w0, w_up, a0, a_up, g_up, kkp, ka, rk, gn_g, gn_b, w_out)


def _xattn_kernel(x_ref, g_ref, wq_ref, kv_ref, wo_ref, o_ref):
    D = x_ref.shape[1]
    x = x_ref[...]
    q = _dot(_rms_bf16(x, g_ref[...]), wq_ref[...]).astype(BF16)
    hsl = [slice(hd * X_HDIM, (hd + 1) * X_HDIM) for hd in range(D // X_HDIM)]
    s = [lax.dot_general(q[:, ls], kv_ref[0, :, ls], (((1,), (1,)), ((), ())),
                         preferred_element_type=F32) * (X_HDIM ** -0.5) for ls in hsl]
    e = [jnp.exp(si - jnp.max(si, axis=-1, keepdims=True)) for si in s]
    p = [(ei / jnp.sum(ei, axis=-1, keepdims=True)).astype(BF16) for ei in e]
    heads = [_dot(pi, kv_ref[0, :, D + ls.start:D + ls.stop]).astype(BF16)
             for pi, ls in zip(p, hsl)]
    o_ref[...] = x + _dot(jnp.concatenate(heads, axis=1), wo_ref[...])


def _xattn(x, gain, wq, kv, wo, layer, seq_len, tm=1024):
    T, D = x.shape
    M = kv.shape[1]
    tm = min(tm, seq_len)
    per_seq = seq_len // tm
    return pl.pallas_call(
        _xattn_kernel,
        out_shape=jax.ShapeDtypeStruct((T, D), F32),
        grid=(T // tm,),
        in_specs=[pl.BlockSpec((tm, D), lambda i: (i, 0)),
                  pl.BlockSpec((1, D), lambda i: (0, 0)),
                  _layer_weight_spec(wq, layer),
                  pl.BlockSpec((1, M, 2 * D), lambda i: (i // per_seq, 0, 0)),
                  _layer_weight_spec(wo, layer)],
        out_specs=pl.BlockSpec((tm, D), lambda i: (i, 0)),
        compiler_params=pltpu.CompilerParams(
            dimension_semantics=("parallel",), vmem_limit_bytes=VMEM_LIMIT),
        name="xattn",
    )(x, gain.reshape(1, D), wq, kv, wo)


def _ffn_kernel(*refs, per_seq, tc, norm_out):
    x_ref, g_ref, wup_ref, cw_ref, cb_ref, wdn_ref = refs[:6]
    gout_ref = refs[6] if norm_out else None
    o_ref, tail_scr = refs[-2:]
    tm = x_ref.shape[0]

    @pl.when(lax.rem(pl.program_id(0), per_seq) == 0)
    def _():
        tail_scr[...] = jnp.zeros_like(tail_scr)

    x = x_ref[...]
    h = _rms_bf16(x, g_ref[...])
    chunks = [slice(c * tc, (c + 1) * tc) for c in range(D_FF // tc)]
    gates = [_dot(h, wup_ref[:, cols]) for cols in chunks]
    vals = [_dot(h, wup_ref[:, D_FF + cols.start:D_FF + cols.stop]) for cols in chunks]
    acts = []
    for cols, gate, val in zip(chunks, gates, vals):
        prev = tail_scr[:, cols]
        y = cb_ref[:, cols] + gate * cw_ref[FFN_CONV - 1:FFN_CONV, cols]
        for j in range(FFN_CONV - 1):
            y = y + _shift_rows_carry(gate, prev, FFN_CONV - 1 - j) * cw_ref[j:j + 1, cols]
        tail_scr[:, cols] = gate[tm - SUBLANES:]
        acts.append((y * _sigmoid(y) * val).astype(BF16))
    acc = x
    for cols, act in zip(chunks, acts):
        acc = acc + _dot(act, wdn_ref[cols, :])
    if norm_out:
        ms = jnp.mean(acc * acc, axis=-1, keepdims=True)
        acc = acc * lax.rsqrt(ms + NORM_EPS) * gout_ref[...]
    o_ref[...] = acc


def _ffn(x, gain, w_up, conv_w, conv_b, w_down, layer, seq_len, out_gain=None, tm=512, tc=256):
    T, D = x.shape
    tm = min(tm, seq_len)
    norm_out = out_gain is not None
    const = lambda a: pl.BlockSpec(a.shape, lambda i: (0,) * a.ndim, pipeline_mode=pl.Buffered(1))
    small = [conv_w, conv_b.reshape(1, D_FF)] + ([out_gain.reshape(1, D)] if norm_out else [])
    specs = [const(a) for a in small]
    return pl.pallas_call(
        functools.partial(_ffn_kernel, per_seq=seq_len // tm, tc=tc, norm_out=norm_out),
        out_shape=jax.ShapeDtypeStruct((T, D), F32),
        grid=(T // tm,),
        in_specs=[pl.BlockSpec((tm, D), lambda i: (i, 0)),
                  pl.BlockSpec((1, D), lambda i: (0, 0)),
                  _layer_weight_spec(w_up, layer), specs[0], specs[1],
                  _layer_weight_spec(w_down, layer)] + specs[2:],
        out_specs=pl.BlockSpec((tm, D), lambda i: (i, 0)),
        scratch_shapes=[pltpu.VMEM((SUBLANES, D_FF), F32)],
        compiler_params=pltpu.CompilerParams(
            dimension_semantics=("arbitrary",), vmem_limit_bytes=VMEM_LIMIT),
        name="ffn",
    )(x, gain.reshape(1, D), w_up, small[0], small[1], w_down, *small[2:])


def _pad_cols(w, n):
    return jnp.pad(w, [(0, 0)] * (w.ndim - 1) + [(0, n - w.shape[-1])])


def _relayout_in(w):
    m_main = w[..., :4 * M_WIDTH]
    m_gate = w[..., 4 * M_WIDTH:4 * M_WIDTH + 2 * M_HEADS]
    r0 = 4 * M_WIDTH + 2 * M_HEADS
    r_main = w[..., r0:r0 + 3 * R_WIDTH + R_DECAY_LORA + R_AAA_LORA]
    r_gate = w[..., r0 + 3 * R_WIDTH + R_DECAY_LORA + R_AAA_LORA:]
    return jnp.concatenate(
        [m_main, r_main, _pad_cols(m_gate, GATE_PAD), _pad_cols(r_gate, GLORA_PAD)], axis=-1)


def kernel(x, mem, norm_mix, w_in, m_conv_w, m_conv_b, m_gate_b, m_norm_g, r_mu, r_w0,
           r_w_up, r_a0, r_a_up, r_g_up, r_kk, r_ka, r_rk, r_gn_g, r_gn_b, w_out,
           norm_x, norm_mem, x_wq, x_wkv, x_wo, norm_ffn, f_up, f_conv_w, f_conv_b,
           f_down, norm_final):
    B, S, D = x.shape
    M = mem.shape[1]
    depth = w_in.shape[0]
    T = B * S
    xf = x.reshape(T, D)
    memf = mem.reshape(B * M, D)
    row = lambda a: a.reshape(1, -1)
    w_in_p, w_out_b = _relayout_in(w_in).astype(BF16), w_out.astype(BF16)
    wq_b, wkv_b, wo_b = x_wq.astype(BF16), x_wkv.astype(BF16), x_wo.astype(BF16)
    f_up_b, f_down_b = f_up.astype(BF16), f_down.astype(BF16)

    for l in range(depth):
        mu = _relayout_in(jnp.pad(row(r_mu[l]), ((0, 0), (4 * M_WIDTH + 2 * M_HEADS, 0))))
        w_up = jnp.pad(r_w_up[l], ((0, R_AAA_LORA), (0, 0))).astype(BF16)
        a_up = jnp.pad(r_a_up[l], ((R_DECAY_LORA, 0), (0, 0))).astype(BF16)
        g_up = jnp.pad(r_g_up[l], ((0, GLORA_PAD - R_GATE_LORA), (0, 0))).astype(BF16)
        xf = _mixer(xf.reshape(B, S, D), w_in_p, row(norm_mix[l]),
                    m_gate_b[l].reshape(-1, 1, 1), row(m_gate_b[l]),
                    m_conv_w[l], row(m_conv_b[l]), row(m_norm_g[l]),
                    mu, row(r_w0[l]), w_up, row(r_a0[l]), a_up, g_up,
                    row(r_kk[l]), row(r_ka[l]), row(r_rk[l]), row(r_gn_g[l]), row(r_gn_b[l]),
                    w_out_b, l).reshape(T, D)

        kv = _mm(memf, wkv_b, l, gain=norm_mem[l], out_dtype=BF16)
        xf = _xattn(xf, norm_x[l], wq_b, kv.reshape(B, M, 2 * D), wo_b, l, S)

        xf = _ffn(xf, norm_ffn[l], f_up_b, f_conv_w[l], f_conv_b[l], f_down_b, l, S,
                  out_gain=norm_final if l == depth - 1 else None)

    return xf.reshape(B, S, D)
```

```python
import functools
import math

import jax
import jax.numpy as jnp
from jax import lax
from jax.experimental import pallas as pl
from jax.experimental.pallas import tpu as pltpu

F32 = jnp.float32
BF16 = jnp.bfloat16

M_WIDTH = 512
M_HEADS = 4
M_HDIM = 128
M_CONV = 4
R_WIDTH = 512
R_HDIM = 64
R_DECAY_LORA = 64
R_AAA_LORA = 64
R_GATE_LORA = 160
DECAY_SCALE = math.exp(-0.5)
X_HDIM = 256
D_FF = 2816
FFN_CONV = 3
NORM_EPS = 1e-6
GN_EPS = 64e-5
CHUNK = 64

LANES = 128
SUBLANES = 8
GATE_PAD = LANES
GLORA_PAD = 2 * LANES
OFF_MQ, OFF_MK, OFF_MV, OFF_MO = 0, 512, 1024, 1536
OFF_RR = 2048
OFF_RK = OFF_RR + R_WIDTH
OFF_RV = OFF_RK + R_WIDTH
OFF_RWA = OFF_RV + R_WIDTH
OFF_MG = OFF_RWA + LANES
OFF_RG = OFF_MG + GATE_PAD
IN_COLS_P = OFF_RG + GLORA_PAD

SEQ_TILE = 256
VMEM_LIMIT = 48 * 1024 * 1024


def _dot(a, b):
    return jnp.dot(a, b, preferred_element_type=F32)


def _bdot(a, b):
    return jnp.dot(a.astype(BF16), b.astype(BF16), preferred_element_type=F32)


def _bdot_nt(a, b):
    return lax.dot_general(a.astype(BF16), b.astype(BF16), (((1,), (1,)), ((), ())),
                           preferred_element_type=F32)


def _bdot_tn(a, b):
    return lax.dot_general(a.astype(BF16), b.astype(BF16), (((0,), (0,)), ((), ())),
                           preferred_element_type=F32)


def _split2(x):
    hi = x.astype(BF16)
    lo = (x - hi.astype(F32)).astype(BF16)
    return hi, lo


def _sigmoid(x):
    return 0.5 * jnp.tanh(0.5 * x) + 0.5


def _shift_rows_carry(x, prev, sh):
    r = pltpu.roll(x, sh, 0)
    row = lax.broadcasted_iota(jnp.int32, prev.shape, 0)
    head = jnp.where(row < sh, pltpu.roll(prev, sh, 0), r[:SUBLANES])
    return jnp.concatenate([head, r[SUBLANES:]], axis=0)


def _rms_bf16(x, gain):
    ms = jnp.mean(x * x, axis=-1, keepdims=True)
    return (x * lax.rsqrt(ms + NORM_EPS) * gain).astype(BF16)


def _layer_weight_spec(w, layer):
    return pl.BlockSpec((None,) + w.shape[1:], lambda *_: (layer,) + (0,) * (w.ndim - 1),
                        pipeline_mode=pl.Buffered(1))


def _mm_kernel(x_ref, w_ref, g_ref, o_ref, h_scr, *, tn):
    h_scr[...] = _rms_bf16(x_ref[...], g_ref[...])
    for c in range(w_ref.shape[1] // tn):
        cols = slice(c * tn, (c + 1) * tn)
        o_ref[:, cols] = _dot(h_scr[...], w_ref[:, cols]).astype(o_ref.dtype)


def _mm_tile_rows(K, N, x_bytes, out_bytes):
    budget = (VMEM_LIMIT * 3) // 4 - K * N * 2
    for tm in (512, 256, 128):
        if tm * (2 * K * x_bytes + 2 * N * out_bytes + K * 2) <= budget:
            return tm
    raise ValueError("weight does not fit in VMEM")


def _mm(x, w, layer, gain, out_dtype=F32, tn=512):
    T, K = x.shape
    N = w.shape[2]
    tm = min(T, _mm_tile_rows(K, N, x.dtype.itemsize, jnp.dtype(out_dtype).itemsize))
    tn = min(tn, N)
    assert T % tm == 0 and N % tn == 0
    return pl.pallas_call(
        functools.partial(_mm_kernel, tn=tn),
        out_shape=jax.ShapeDtypeStruct((T, N), out_dtype),
        grid=(T // tm,),
        in_specs=[pl.BlockSpec((tm, K), lambda i: (i, 0)),
                  _layer_weight_spec(w, layer),
                  pl.BlockSpec((1, K), lambda i: (0, 0))],
        out_specs=pl.BlockSpec((tm, N), lambda i: (i, 0)),
        scratch_shapes=[pltpu.VMEM((tm, K), BF16)],
        compiler_params=pltpu.CompilerParams(
            dimension_semantics=("parallel",), vmem_limit_bytes=VMEM_LIMIT),
        name="mm",
    )(x, w, gain.reshape(1, K))


def _mlstm_stages(q_ref, k_ref, v_ref, o_ref, g_ref, brow_ref, bcol_ref,
                  cw_ref, cb_ref, ng_ref, qp_scr, kp_scr, c_scr, n_scr, m_scr, result):
    TS = q_ref.shape[0]
    L = CHUNK
    nch = TS // L
    H = M_HEADS
    G = nch * H
    hs = range(H)

    def conv_silu(x, prev, w, b):
        y = b + x * w[M_CONV - 1:M_CONV, :]
        for j in range(M_CONV - 1):
            y = y + _shift_rows_carry(x, prev, M_CONV - 1 - j) * w[j:j + 1, :]
        return y * _sigmoid(y)

    def groups(x):
        return jnp.stack([x[c * L:(c + 1) * L, h * M_HDIM:(h + 1) * M_HDIM]
                          for c in range(nch) for h in hs])

    def per_group(f, *xs):
        return jnp.stack([f(*[x[i] for x in xs]) for i in range(G)])

    def log_sigmoid(x):
        return jnp.minimum(x, 0.0) - jnp.log1p(jnp.exp(-jnp.abs(x)))

    q_raw = q_ref[...]
    q = groups(conv_silu(q_raw, qp_scr[...], cw_ref[:, :M_WIDTH], cb_ref[:, :M_WIDTH])
               * (M_HDIM ** -0.5))
    qp_scr[...] = q_raw[TS - SUBLANES:]
    yield
    k_raw = k_ref[...]
    k = groups(conv_silu(k_raw, kp_scr[...], cw_ref[:, M_WIDTH:], cb_ref[:, M_WIDTH:]))
    kp_scr[...] = k_raw[TS - SUBLANES:]
    v = groups(v_ref[...])
    yield

    ti = lax.broadcasted_iota(jnp.int32, (L, L), 0)
    si = lax.broadcasted_iota(jnp.int32, (L, L), 1)
    causal = si <= ti
    diag = ti == si
    pg = g_ref[...]
    zpad = jnp.zeros((GATE_PAD - L, GATE_PAD), F32)
    pg_t = [jnp.concatenate([pg[c * L:(c + 1) * L], zpad], axis=0).T for c in range(nch)]
    logi_r = jnp.stack([pg_t[c][h:h + 1, :L] + brow_ref[h]
                        for c in range(nch) for h in hs])
    logf_r = log_sigmoid(jnp.stack([pg_t[c][H + h:H + h + 1, :L] + brow_ref[H + h]
                                    for c in range(nch) for h in hs]))
    gc = pg[:, :2 * H] + bcol_ref[...]
    logi_c = jnp.stack([gc[c * L:(c + 1) * L, h:h + 1] for c in range(nch) for h in hs])
    yield
    b_c = jnp.sum(jnp.where(causal, logf_r, 0.0), axis=2, keepdims=True)
    b_r = jnp.sum(jnp.where(diag, b_c, 0.0), axis=1, keepdims=True)
    g = jnp.sum(logf_r, axis=2, keepdims=True)
    a_r = g - b_r + logi_r
    a_c = g - b_c + logi_c
    m_loc = jnp.max(a_r, axis=2, keepdims=True)
    wa_c = jnp.exp(a_c - m_loc)
    yield
    c_loc = per_group(_bdot_tn, v * wa_c, k)
    yield
    n_loc = jnp.sum(k * wa_c, axis=1, keepdims=True)
    d = jnp.where(causal, b_c - b_r + logi_r, -jnp.inf)
    d_max = jnp.max(d, axis=2, keepdims=True)
    yield
    qk = per_group(_bdot_nt, q, k)
    yield

    c_prev = c_scr[...]
    n_prev = n_scr[...]
    m_prev = m_scr[...]
    c_in, n_in, m_in = [], [], []
    for c in range(nch):
        gs = slice(c * H, (c + 1) * H)
        c_in.append(c_prev)
        n_in.append(n_prev)
        m_in.append(m_prev)
        m_new = jnp.maximum(g[gs] + m_prev, m_loc[gs])
        s_old = jnp.exp(g[gs] + m_prev - m_new)
        s_loc = jnp.exp(m_loc[gs] - m_new)
        c_prev = s_old * c_prev + s_loc * c_loc[gs]
        n_prev = s_old * n_prev + s_loc * n_loc[gs]
        m_prev = m_new
    c_scr[...] = c_prev
    n_scr[...] = n_prev
    m_scr[...] = m_prev
    c_in = jnp.concatenate(c_in, axis=0)
    n_in = jnp.concatenate(n_in, axis=0)
    m_in = jnp.concatenate(m_in, axis=0)
    yield

    inter = b_c + m_in
    m_t = jnp.maximum(inter, d_max)
    s_int = jnp.exp(inter - m_t)
    p = jnp.exp(d - m_t) * qk
    yield
    num = s_int * per_group(_bdot_nt, q, c_in) + per_group(_bdot, p, v)
    yield
    den = (s_int * jnp.sum(q * n_in, axis=2, keepdims=True)
           + jnp.sum(p, axis=2, keepdims=True))
    hh = num / jnp.maximum(jnp.abs(den), jnp.exp(-m_t))
    yield
    mu = jnp.mean(hh, axis=-1, keepdims=True)
    hc = hh - mu
    var = jnp.mean(hc * hc, axis=-1, keepdims=True)
    ng = jnp.stack([ng_ref[:, h * M_HDIM:(h + 1) * M_HDIM] for h in hs] * nch)
    y = (_sigmoid(groups(o_ref[...])) * (hc * lax.rsqrt(var + NORM_EPS)) * ng).astype(BF16)
    result.append(jnp.concatenate(
        [jnp.concatenate([y[c * H + h] for h in hs], axis=1) for c in range(nch)], axis=0))


def _rwkv_stages(pr_ref, pk_ref, pv_ref, pwa_ref, pg_ref,
                 mur_ref, muk_ref, muv_ref, muwa_ref, mug_ref,
                 w0_ref, wup_ref, a0_ref, aup_ref, gup_ref,
                 kkp_ref, ka_ref, rk_ref, gng_ref, gnb_ref,
                 cr_scr, ck_scr, cv_scr, cwa_scr, cg_scr, st_scr, result):
    TS = pr_ref.shape[0]
    L = CHUNK
    N = R_HDIM
    nch = TS // L
    npair = R_WIDTH // LANES
    pairs = range(npair)
    Q = 2 * LANES

    def tshift(p_ref, mu_ref, c_scr):
        p = p_ref[...]
        prev = _shift_rows_carry(p, c_scr[...], 1)
        c_scr[...] = p[TS - SUBLANES:]
        return p + (prev - p) * mu_ref[...]

    hsh = N.bit_length() - 1
    li = lax.broadcasted_iota(jnp.int32, (LANES, LANES), 0)
    lj = lax.broadcasted_iota(jnp.int32, (LANES, LANES), 1)
    same_head = jnp.right_shift(li, hsh) == jnp.right_shift(lj, hsh)
    ones_bd = jnp.where(same_head, 1.0, 0.0).astype(BF16)

    def seg_sum(x):
        outs = []
        for p in range(x.shape[1] // LANES):
            hi, lo = _split2(x[:, p * LANES:(p + 1) * LANES])
            outs.append(_dot(hi, ones_bd) + _dot(lo, ones_bd))
        return jnp.concatenate(outs, axis=1)

    rr = tshift(pr_ref, mur_ref, cr_scr)
    kr = tshift(pk_ref, muk_ref, ck_scr)
    vr = tshift(pv_ref, muv_ref, cv_scr)
    yield
    wa = tshift(pwa_ref, muwa_ref, cwa_scr)
    gd = tshift(pg_ref, mug_ref, cg_scr)
    logw = -DECAY_SCALE * _sigmoid(w0_ref[...] + _bdot(jnp.tanh(wa), wup_ref[...]))
    a = _sigmoid(a0_ref[...] + _bdot(wa, aup_ref[...]))
    g = _bdot(_sigmoid(gd), gup_ref[...])
    yield
    kkraw = kr * kkp_ref[...]
    kk = kkraw * lax.rsqrt(jnp.maximum(seg_sum(kkraw * kkraw), 1e-24))
    km = kr * (1.0 + (a - 1.0) * ka_ref[...])
    be = kk * a
    yield
    tq = lax.broadcasted_iota(jnp.int32, (Q, Q), 0)
    sq = lax.broadcasted_iota(jnp.int32, (Q, Q), 1)
    tril = jnp.where((jnp.right_shift(tq, hsh) == jnp.right_shift(sq, hsh)) & (sq <= tq),
                     1.0, 0.0).astype(BF16)
    lw_hi, lw_lo = _split2(logw)
    bincl = jnp.concatenate(
        [_dot(tril, lw_hi[q * Q:(q + 1) * Q]) + _dot(tril, lw_lo[q * Q:(q + 1) * Q])
         for q in range(TS // Q)], axis=0)
    e_in = jnp.exp(bincl)
    e_ng = jnp.exp(-bincl)
    kt = kk * jnp.exp(bincl - logw)
    rt = rr * e_in
    bh = be * e_ng
    kh = km * e_ng
    yield

    h0 = lax.broadcasted_iota(jnp.int32, (L, LANES), 1) < N
    ti = lax.broadcasted_iota(jnp.int32, (2 * L, 2 * L), 0)
    si = lax.broadcasted_iota(jnp.int32, (2 * L, 2 * L), 1)
    same_blk = jnp.right_shift(ti, hsh) == jnp.right_shift(si, hsh)
    strict = same_blk & (si < ti)
    incl = same_blk & (si <= ti)

    def stack_heads(x):
        return jnp.concatenate([jnp.where(h0, x, 0.0), jnp.where(h0, 0.0, x)], axis=0)

    def stack_dup(x):
        return jnp.concatenate([x, x], axis=0)

    def comb(x):
        return jnp.where(h0, x[:L], x[L:])

    def off_mask(b):
        sh = (2 * b).bit_length() - 1
        same = jnp.right_shift(ti, sh) == jnp.right_shift(si, sh)
        return same & (jnp.bitwise_and(ti, b) != 0) & (jnp.bitwise_and(si, b) == 0)

    probs = [(p, c) for p in pairs for c in range(nch)]

    def tile(x, p, c):
        return x[c * L:(c + 1) * L, p * LANES:(p + 1) * LANES]

    lk = [stack_heads(tile(kt, p, c)).astype(BF16) for p, c in probs]
    lr = [stack_heads(tile(rt, p, c)).astype(BF16) for p, c in probs]
    rb = [stack_dup(tile(bh, p, c)).astype(BF16) for p, c in probs]
    rk = [stack_dup(tile(kh, p, c)).astype(BF16) for p, c in probs]
    vs = [stack_dup(tile(vr, p, c)).astype(BF16) for p, c in probs]
    yield
    a_bd = [jnp.where(strict, _bdot_nt(x, y), 0.0) for x, y in zip(lk, rb)]
    yield
    bk_bd = [jnp.where(strict, _bdot_nt(x, y), 0.0).astype(BF16) for x, y in zip(lk, rk)]
    yield
    arb_bd = [jnp.where(incl, _bdot_nt(x, y), 0.0).astype(BF16) for x, y in zip(lr, rb)]
    yield
    ark_bd = [jnp.where(incl, _bdot_nt(x, y), 0.0).astype(BF16) for x, y in zip(lr, rk)]
    yield
    eye = jnp.where(ti == si, 1.0, 0.0)
    m1 = off_mask(1)
    t_bd = [eye - jnp.where(m1, x, 0.0) for x in a_bd]
    b = 2
    while b < L:
        mb = off_mask(b)
        offs = [jnp.where(mb, x, 0.0).astype(BF16) for x in a_bd]
        xs = [_bdot(i, o) for i, o in zip(t_bd, offs)]
        yield
        ys = [_bdot(x, i) for x, i in zip(xs, t_bd)]
        t_bd = [i - y for i, y in zip(t_bd, ys)]
        yield
        b *= 2
    bkv = [comb(_bdot(x, y)) for x, y in zip(bk_bd, vs)]
    yield
    xs = [jnp.concatenate([stack_dup(tile(kt, p, c)), stack_dup(z)], axis=1)
          for (p, c), z in zip(probs, bkv)]
    tx = [_bdot(x, y) for x, y in zip(t_bd, xs)]
    yield
    arkv = [comb(_bdot(x, y)) for x, y in zip(ark_bd, vs)]
    w_ch = [[None] * nch for _ in pairs]
    u0_ch = [[None] * nch for _ in pairs]
    arkv_ch = [[None] * nch for _ in pairs]
    arb_ch = [[None] * nch for _ in pairs]
    for i, (p, c) in enumerate(probs):
        w_ch[p][c] = comb(tx[i][:, :LANES])
        u0_ch[p][c] = -comb(tx[i][:, LANES:])
        arkv_ch[p][c] = arkv[i]
        arb_ch[p][c] = arb_bd[i]
    yield

    y_rows = []
    st = [st_scr[p] for p in pairs]
    ones_f = jnp.where(same_head, 1.0, 0.0)
    lsl = [slice(p * LANES, (p + 1) * LANES) for p in pairs]
    for c in range(nch):
        rs = slice(c * L, (c + 1) * L)
        p_end = e_in[c * L + L - 1:c * L + L, :]
        pe = [p_end[:, ls] for ls in lsl]
        rw = [_bdot_nt(jnp.concatenate([rt[rs, lsl[p]], w_ch[p][c]], axis=0), st[p])
              for p in pairs]
        u = [u0_ch[p][c] - rw[p][L:] for p in pairs]
        au = [_bdot(arb_ch[p][c], jnp.concatenate([u[p], u[p]], axis=0)) for p in pairs]
        upd = [_bdot_tn(jnp.concatenate([u[p], vr[rs, lsl[p]]], axis=0),
                        jnp.concatenate([bh[rs, lsl[p]] * pe[p], kh[rs, lsl[p]] * pe[p]], axis=0))
               for p in pairs]
        st = [st[p] * pe[p] + upd[p] * ones_f for p in pairs]
        y_rows.append(jnp.concatenate(
            [rw[p][:L] + jnp.where(h0, au[p][:L], au[p][L:]) + arkv_ch[p][c] for p in pairs],
            axis=1))
        yield
    for p in pairs:
        st_scr[p] = st[p]
    y = jnp.concatenate(y_rows, axis=0)

    inv_n = 1.0 / N
    mu = seg_sum(y) * inv_n
    yc = y - mu
    var = seg_sum(yc * yc) * inv_n
    yn = yc * lax.rsqrt(var + GN_EPS) * gng_ref[...] + gnb_ref[...]
    bonus = seg_sum(rr * km * rk_ref[...]) * vr
    result.append(((yn + bonus) * g).astype(BF16))


def _mixer_kernel(*refs, tn):
    (x_ref, xn_ref, win_ref, gin_ref, brow_ref, bcol_ref, cw_ref, cb_ref, ng_ref,
     mur_ref, muk_ref, muv_ref, muwa_ref, mug_ref,
     w0_ref, wup_ref, a0_ref, aup_ref, gup_ref, kkp_ref, ka_ref, rk_ref, gng_ref, gnb_ref,
     wout_ref, out_ref,
     qp_scr, kp_scr, c_scr, n_scr, m_scr,
     cr_scr, ck_scr, cv_scr, cwa_scr, cg_scr, st_scr, p_scr) = refs
    step = pl.program_id(0) * pl.num_programs(1) + pl.program_id(1)
    chunks = [slice(c * tn, (c + 1) * tn) for c in range(win_ref.shape[1] // tn)]

    def project(h, dst_slot, cols_list):
        for cols in cols_list:
            p_scr[dst_slot, :, cols] = _dot(h, win_ref[:, cols])

    @pl.when(step == 0)
    def _():
        project(_rms_bf16(x_ref[0], gin_ref[...]), 0, chunks)

    @pl.when(pl.program_id(1) == 0)
    def _():
        for scr in (qp_scr, kp_scr, c_scr, n_scr, m_scr,
                    cr_scr, ck_scr, cv_scr, cwa_scr, cg_scr, st_scr):
            scr[...] = jnp.zeros_like(scr)

    slot = lax.rem(step, 2)
    view = lambda off, width: p_scr.at[slot, :, pl.ds(off, width)]
    y_m, y_r = [], []
    mlstm = _mlstm_stages(view(OFF_MQ, M_WIDTH), view(OFF_MK, M_WIDTH), view(OFF_MV, M_WIDTH),
                          view(OFF_MO, M_WIDTH), view(OFF_MG, GATE_PAD), brow_ref, bcol_ref,
                          cw_ref, cb_ref, ng_ref, qp_scr, kp_scr, c_scr, n_scr, m_scr, y_m)
    rwkv = _rwkv_stages(view(OFF_RR, R_WIDTH), view(OFF_RK, R_WIDTH), view(OFF_RV, R_WIDTH),
                        view(OFF_RWA, LANES), view(OFF_RG, GLORA_PAD),
                        mur_ref, muk_ref, muv_ref, muwa_ref, mug_ref,
                        w0_ref, wup_ref, a0_ref, aup_ref, gup_ref,
                        kkp_ref, ka_ref, rk_ref, gng_ref, gnb_ref,
                        cr_scr, ck_scr, cv_scr, cwa_scr, cg_scr, st_scr, y_r)
    def project_next():
        h_next = _rms_bf16(xn_ref[0], gin_ref[...])
        for cols in chunks:
            project(h_next, 1 - slot, [cols])
            yield

    proj = project_next()
    streams = {"R": rwkv, "M": mlstm, "P": proj}
    for tag in "PRPRPRPRR" + "RMR" * 9 + "PRMPRMPRMPRM":
        next(streams[tag], None)
    for stream in (proj, mlstm, rwkv):
        for _ in stream:
            pass
    out_ref[0] = (x_ref[0] + _dot(y_m[0], wout_ref[:M_WIDTH, :])
                  + _dot(y_r[0], wout_ref[M_WIDTH:, :]))


def _mixer(x, w_in, gain, b_row, b_col, conv_w, conv_b, norm_g,
           mu, w0, w_up, a0, a_up, g_up, kkp, ka, rk, gn_g, gn_b, w_out, layer, tn=512):
    B, S, D = x.shape
    ts = min(SEQ_TILE, S)
    n_seq = S // ts
    last = B * n_seq - 1
    npair = R_WIDTH // LANES
    vec = lambda w, off: pl.BlockSpec((1, w), lambda b, s, off=off: (0, off))
    full = lambda a: pl.BlockSpec(a.shape, lambda b, s: (0,) * a.ndim)

    def next_tile(b, s):
        t = jnp.minimum(b * n_seq + s + 1, last)
        return (t // n_seq, t % n_seq, 0)

    return pl.pallas_call(
        functools.partial(_mixer_kernel, tn=tn),
        out_shape=jax.ShapeDtypeStruct((B, S, D), F32),
        grid=(B, n_seq),
        in_specs=[
            pl.BlockSpec((1, ts, D), lambda b, s: (b, s, 0)),
            pl.BlockSpec((1, ts, D), next_tile),
            _layer_weight_spec(w_in, layer), full(gain),
            full(b_row), full(b_col), full(conv_w), full(conv_b), full(norm_g),
            vec(R_WIDTH, OFF_RR // R_WIDTH), vec(R_WIDTH, OFF_RK // R_WIDTH),
            vec(R_WIDTH, OFF_RV // R_WIDTH), vec(LANES, OFF_RWA // LANES),
            vec(GLORA_PAD, OFF_RG // GLORA_PAD),
            full(w0), full(w_up), full(a0), full(a_up), full(g_up),
            full(kkp), full(ka), full(rk), full(gn_g), full(gn_b),
            _layer_weight_spec(w_out, layer),
        ],
        out_specs=pl.BlockSpec((1, ts, D), lambda b, s: (b, s, 0)),
        scratch_shapes=[pltpu.VMEM((SUBLANES, M_WIDTH), F32), pltpu.VMEM((SUBLANES, M_WIDTH), F32),
                        pltpu.VMEM((M_HEADS, M_HDIM, M_HDIM), F32),
                        pltpu.VMEM((M_HEADS, 1, M_HDIM), F32),
                        pltpu.VMEM((M_HEADS, 1, 1), F32),
                        pltpu.VMEM((SUBLANES, R_WIDTH), F32), pltpu.VMEM((SUBLANES, R_WIDTH), F32),
                        pltpu.VMEM((SUBLANES, R_WIDTH), F32), pltpu.VMEM((SUBLANES, LANES), F32),
                        pltpu.VMEM((SUBLANES, GLORA_PAD), F32),
                        pltpu.VMEM((npair, LANES, LANES), F32),
                        pltpu.VMEM((2, ts, IN_COLS_P), F32)],
        compiler_params=pltpu.CompilerParams(
            dimension_semantics=("arbitrary", "arbitrary"), vmem_limit_bytes=VMEM_LIMIT),
        name="mixer",
    )(x, x, w_in, gain, b_row, b_col, conv_w, conv_b, norm_g, mu, mu, mu, mu, mu,
      w0, w_up, a0, a_up, g_up, kkp, ka, rk, gn_g, gn_b, w_out)


def _xattn_kernel(x_ref, g_ref, wq_ref, kv_ref, wo_ref, o_ref):
    D = x_ref.shape[1]
    x = x_ref[...]
    q = _dot(_rms_bf16(x, g_ref[...]), wq_ref[...]).astype(BF16)
    hsl = [slice(hd * X_HDIM, (hd + 1) * X_HDIM) for hd in range(D // X_HDIM)]
    s = [lax.dot_general(q[:, ls], kv_ref[0, :, ls], (((1,), (1,)), ((), ())),
                         preferred_element_type=F32) * (X_HDIM ** -0.5) for ls in hsl]
    e = [jnp.exp(si - jnp.max(si, axis=-1, keepdims=True)) for si in s]
    p = [(ei / jnp.sum(ei, axis=-1, keepdims=True)).astype(BF16) for ei in e]
    heads = [_dot(pi, kv_ref[0, :, D + ls.start:D + ls.stop]).astype(BF16)
             for pi, ls in zip(p, hsl)]
    o_ref[...] = x + _dot(jnp.concatenate(heads, axis=1), wo_ref[...])


def _xattn(x, gain, wq, kv, wo, layer, seq_len, tm=1024):
    T, D = x.shape
    M = kv.shape[1]
    tm = min(tm, seq_len)
    per_seq = seq_len // tm
    return pl.pallas_call(
        _xattn_kernel,
        out_shape=jax.ShapeDtypeStruct((T, D), F32),
        grid=(T // tm,),
        in_specs=[pl.BlockSpec((tm, D), lambda i: (i, 0)),
                  pl.BlockSpec((1, D), lambda i: (0, 0)),
                  _layer_weight_spec(wq, layer),
                  pl.BlockSpec((1, M, 2 * D), lambda i: (i // per_seq, 0, 0)),
                  _layer_weight_spec(wo, layer)],
        out_specs=pl.BlockSpec((tm, D), lambda i: (i, 0)),
        compiler_params=pltpu.CompilerParams(
            dimension_semantics=("parallel",), vmem_limit_bytes=VMEM_LIMIT),
        name="xattn",
    )(x, gain.reshape(1, D), wq, kv, wo)


def _ffn_kernel(*refs, per_seq, tc, norm_out):
    x_ref, g_ref, wup_ref, cw_ref, cb_ref, wdn_ref = refs[:6]
    gout_ref = refs[6] if norm_out else None
    o_ref, tail_scr = refs[-2:]
    tm = x_ref.shape[0]

    @pl.when(lax.rem(pl.program_id(0), per_seq) == 0)
    def _():
        tail_scr[...] = jnp.zeros_like(tail_scr)

    x = x_ref[...]
    h = _rms_bf16(x, g_ref[...])
    chunks = [slice(c * tc, (c + 1) * tc) for c in range(D_FF // tc)]
    gates = [_dot(h, wup_ref[:, cols]) for cols in chunks]
    vals = [_dot(h, wup_ref[:, D_FF + cols.start:D_FF + cols.stop]) for cols in chunks]
    acts = []
    for cols, gate, val in zip(chunks, gates, vals):
        prev = tail_scr[:, cols]
        y = cb_ref[:, cols] + gate * cw_ref[FFN_CONV - 1:FFN_CONV, cols]
        for j in range(FFN_CONV - 1):
            y = y + _shift_rows_carry(gate, prev, FFN_CONV - 1 - j) * cw_ref[j:j + 1, cols]
        tail_scr[:, cols] = gate[tm - SUBLANES:]
        acts.append((y * _sigmoid(y) * val).astype(BF16))
    acc = x
    for cols, act in zip(chunks, acts):
        acc = acc + _dot(act, wdn_ref[cols, :])
    if norm_out:
        ms = jnp.mean(acc * acc, axis=-1, keepdims=True)
        acc = acc * lax.rsqrt(ms + NORM_EPS) * gout_ref[...]
    o_ref[...] = acc


def _ffn(x, gain, w_up, conv_w, conv_b, w_down, layer, seq_len, out_gain=None, tm=512, tc=256):
    T, D = x.shape
    tm = min(tm, seq_len)
    norm_out = out_gain is not None
    const = lambda a: pl.BlockSpec(a.shape, lambda i: (0,) * a.ndim, pipeline_mode=pl.Buffered(1))
    small = [conv_w, conv_b.reshape(1, D_FF)] + ([out_gain.reshape(1, D)] if norm_out else [])
    specs = [const(a) for a in small]
    return pl.pallas_call(
        functools.partial(_ffn_kernel, per_seq=seq_len // tm, tc=tc, norm_out=norm_out),
        out_shape=jax.ShapeDtypeStruct((T, D), F32),
        grid=(T // tm,),
        in_specs=[pl.BlockSpec((tm, D), lambda i: (i, 0)),
                  pl.BlockSpec((1, D), lambda i: (0, 0)),
                  _layer_weight_spec(w_up, layer), specs[0], specs[1],
                  _layer_weight_spec(w_down, layer)] + specs[2:],
        out_specs=pl.BlockSpec((tm, D), lambda i: (i, 0)),
        scratch_shapes=[pltpu.VMEM((SUBLANES, D_FF), F32)],
        compiler_params=pltpu.CompilerParams(
            dimension_semantics=("arbitrary",), vmem_limit_bytes=VMEM_LIMIT),
        name="ffn",
    )(x, gain.reshape(1, D), w_up, small[0], small[1], w_down, *small[2:])


def _pad_cols(w, n):
    return jnp.pad(w, [(0, 0)] * (w.ndim - 1) + [(0, n - w.shape[-1])])


def _relayout_in(w):
    m_main = w[..., :4 * M_WIDTH]
    m_gate = w[..., 4 * M_WIDTH:4 * M_WIDTH + 2 * M_HEADS]
    r0 = 4 * M_WIDTH + 2 * M_HEADS
    r_main = w[..., r0:r0 + 3 * R_WIDTH + R_DECAY_LORA + R_AAA_LORA]
    r_gate = w[..., r0 + 3 * R_WIDTH + R_DECAY_LORA + R_AAA_LORA:]
    return jnp.concatenate(
        [m_main, r_main, _pad_cols(m_gate, GATE_PAD), _pad_cols(r_gate, GLORA_PAD)], axis=-1)


def kernel(x, mem, norm_mix, w_in, m_conv_w, m_conv_b, m_gate_b, m_norm_g, r_mu, r_w0,
           r_w_up, r_a0, r_a_up, r_g_up, r_kk, r_ka, r_rk, r_gn_g, r_gn_b, w_out,
           norm_x, norm_mem, x_wq, x_wkv, x_wo, norm_ffn, f_up, f_conv_w, f_conv_b,
           f_down, norm_final):
    B, S, D = x.shape
    M = mem.shape[1]
    depth = w_in.shape[0]
    T = B * S
    xf = x.reshape(T, D)
    memf = mem.reshape(B * M, D)
    row = lambda a: a.reshape(1, -1)
    w_in_p, w_out_b = _relayout_in(w_in).astype(BF16), w_out.astype(BF16)
    wq_b, wkv_b, wo_b = x_wq.astype(BF16), x_wkv.astype(BF16), x_wo.astype(BF16)
    f_up_b, f_down_b = f_up.astype(BF16), f_down.astype(BF16)

    for l in range(depth):
        mu = _relayout_in(jnp.pad(row(r_mu[l]), ((0, 0), (4 * M_WIDTH + 2 * M_HEADS, 0))))
        w_up = jnp.pad(r_w_up[l], ((0, R_AAA_LORA), (0, 0))).astype(BF16)
        a_up = jnp.pad(r_a_up[l], ((R_DECAY_LORA, 0), (0, 0))).astype(BF16)
        g_up = jnp.pad(r_g_up[l], ((0, GLORA_PAD - R_GATE_LORA), (0, 0))).astype(BF16)
        xf = _mixer(xf.reshape(B, S, D), w_in_p, row(norm_mix[l]),
                    m_gate_b[l].reshape(-1, 1, 1), row(m_gate_b[l]),
                    m_conv_w[l], row(m_conv_b[l]), row(m_norm_g[l]),
                    mu, row(r_w0[l]), w_up, row(r_a0[l]), a_up, g_up,
                    row(r_kk[l]), row(r_ka[l]), row(r_rk[l]), row(r_gn_g[l]), row(r_gn_b[l]),
                    w_out_b, l).reshape(T, D)

        kv = _mm(memf, wkv_b, l, gain=norm_mem[l], out_dtype=BF16)
        xf = _xattn(xf, norm_x[l], wq_b, kv.reshape(B, M, 2 * D), wo_b, l, S)

        xf = _ffn(xf, norm_ffn[l], f_up_b, f_conv_w[l], f_conv_b[l], f_down_b, l, S,
                  out_gain=norm_final if l == depth - 1 else None)

    return xf.reshape(B, S, D)
```

```python
import functools
import math

import jax
import jax.numpy as jnp
from jax import lax
from jax.experimental import pallas as pl
from jax.experimental.pallas import tpu as pltpu

F32 = jnp.float32
BF16 = jnp.bfloat16

M_WIDTH = 512
M_HEADS = 4
M_HDIM = 128
M_CONV = 4
R_WIDTH = 512
R_HDIM = 64
R_DECAY_LORA = 64
R_AAA_LORA = 64
R_GATE_LORA = 160
DECAY_SCALE = math.exp(-0.5)
X_HDIM = 256
D_FF = 2816
FFN_CONV = 3
NORM_EPS = 1e-6
GN_EPS = 64e-5
CHUNK = 64

LANES = 128
SUBLANES = 8
GATE_PAD = LANES
GLORA_PAD = 2 * LANES
OFF_MQ, OFF_MK, OFF_MV, OFF_MO = 0, 512, 1024, 1536
OFF_RR = 2048
OFF_RK = OFF_RR + R_WIDTH
OFF_RV = OFF_RK + R_WIDTH
OFF_RWA = OFF_RV + R_WIDTH
OFF_MG = OFF_RWA + LANES
OFF_RG = OFF_MG + GATE_PAD
IN_COLS_P = OFF_RG + GLORA_PAD

SEQ_TILE = 256
VMEM_LIMIT = 48 * 1024 * 1024


def _dot(a, b):
    return jnp.dot(a, b, preferred_element_type=F32)


def _bdot(a, b):
    return jnp.dot(a.astype(BF16), b.astype(BF16), preferred_element_type=F32)


def _bdot_nt(a, b):
    return lax.dot_general(a.astype(BF16), b.astype(BF16), (((1,), (1,)), ((), ())),
                           preferred_element_type=F32)


def _bdot_tn(a, b):
    return lax.dot_general(a.astype(BF16), b.astype(BF16), (((0,), (0,)), ((), ())),
                           preferred_element_type=F32)


def _split2(x):
    hi = x.astype(BF16)
    lo = (x - hi.astype(F32)).astype(BF16)
    return hi, lo


def _sigmoid(x):
    return 0.5 * jnp.tanh(0.5 * x) + 0.5


def _shift_rows_carry(x, prev, sh):
    r = pltpu.roll(x, sh, 0)
    row = lax.broadcasted_iota(jnp.int32, prev.shape, 0)
    head = jnp.where(row < sh, pltpu.roll(prev, sh, 0), r[:SUBLANES])
    return jnp.concatenate([head, r[SUBLANES:]], axis=0)


def _rms_bf16(x, gain):
    ms = jnp.mean(x * x, axis=-1, keepdims=True)
    return (x * lax.rsqrt(ms + NORM_EPS) * gain).astype(BF16)


def _layer_weight_spec(w, layer):
    return pl.BlockSpec((None,) + w.shape[1:], lambda *_: (layer,) + (0,) * (w.ndim - 1),
                        pipeline_mode=pl.Buffered(1))


def _mm_kernel(x_ref, w_ref, g_ref, o_ref, h_scr, *, tn):
    h_scr[...] = _rms_bf16(x_ref[...], g_ref[...])
    for c in range(w_ref.shape[1] // tn):
        cols = slice(c * tn, (c + 1) * tn)
        o_ref[:, cols] = _dot(h_scr[...], w_ref[:, cols]).astype(o_ref.dtype)


def _mm_tile_rows(K, N, x_bytes, out_bytes):
    budget = (VMEM_LIMIT * 3) // 4 - K * N * 2
    for tm in (512, 256, 128):
        if tm * (2 * K * x_bytes + 2 * N * out_bytes + K * 2) <= budget:
            return tm
    raise ValueError("weight does not fit in VMEM")


def _mm(x, w, layer, gain, out_dtype=F32, tn=512):
    T, K = x.shape
    N = w.shape[2]
    tm = min(T, _mm_tile_rows(K, N, x.dtype.itemsize, jnp.dtype(out_dtype).itemsize))
    tn = min(tn, N)
    assert T % tm == 0 and N % tn == 0
    return pl.pallas_call(
        functools.partial(_mm_kernel, tn=tn),
        out_shape=jax.ShapeDtypeStruct((T, N), out_dtype),
        grid=(T // tm,),
        in_specs=[pl.BlockSpec((tm, K), lambda i: (i, 0)),
                  _layer_weight_spec(w, layer),
                  pl.BlockSpec((1, K), lambda i: (0, 0))],
        out_specs=pl.BlockSpec((tm, N), lambda i: (i, 0)),
        scratch_shapes=[pltpu.VMEM((tm, K), BF16)],
        compiler_params=pltpu.CompilerParams(
            dimension_semantics=("parallel",), vmem_limit_bytes=VMEM_LIMIT),
        name="mm",
    )(x, w, gain.reshape(1, K))


def _mlstm_stages(q_ref, k_ref, v_ref, o_ref, g_ref, brow_ref, bcol_ref,
                  cw_ref, cb_ref, ng_ref, qp_scr, kp_scr, c_scr, n_scr, m_scr, result):
    TS = q_ref.shape[0]
    L = CHUNK
    nch = TS // L
    H = M_HEADS
    G = nch * H
    hs = range(H)

    def conv_silu(x, prev, w, b):
        y = b + x * w[M_CONV - 1:M_CONV, :]
        for j in range(M_CONV - 1):
            y = y + _shift_rows_carry(x, prev, M_CONV - 1 - j) * w[j:j + 1, :]
        return y * _sigmoid(y)

    def groups(x):
        return jnp.stack([x[c * L:(c + 1) * L, h * M_HDIM:(h + 1) * M_HDIM]
                          for c in range(nch) for h in hs])

    def per_group(f, *xs):
        return jnp.stack([f(*[x[i] for x in xs]) for i in range(G)])

    def log_sigmoid(x):
        return jnp.minimum(x, 0.0) - jnp.log1p(jnp.exp(-jnp.abs(x)))

    q_raw = q_ref[...]
    q = groups(conv_silu(q_raw, qp_scr[...], cw_ref[:, :M_WIDTH], cb_ref[:, :M_WIDTH])
               * (M_HDIM ** -0.5))
    qp_scr[...] = q_raw[TS - SUBLANES:]
    yield
    k_raw = k_ref[...]
    k = groups(conv_silu(k_raw, kp_scr[...], cw_ref[:, M_WIDTH:], cb_ref[:, M_WIDTH:]))
    kp_scr[...] = k_raw[TS - SUBLANES:]
    v = groups(v_ref[...])
    yield

    ti = lax.broadcasted_iota(jnp.int32, (L, L), 0)
    si = lax.broadcasted_iota(jnp.int32, (L, L), 1)
    causal = si <= ti
    diag = ti == si
    pg = g_ref[...]
    zpad = jnp.zeros((GATE_PAD - L, GATE_PAD), F32)
    pg_t = [jnp.concatenate([pg[c * L:(c + 1) * L], zpad], axis=0).T for c in range(nch)]
    logi_r = jnp.stack([pg_t[c][h:h + 1, :L] + brow_ref[h]
                        for c in range(nch) for h in hs])
    logf_r = log_sigmoid(jnp.stack([pg_t[c][H + h:H + h + 1, :L] + brow_ref[H + h]
                                    for c in range(nch) for h in hs]))
    gc = pg[:, :2 * H] + bcol_ref[...]
    logi_c = jnp.stack([gc[c * L:(c + 1) * L, h:h + 1] for c in range(nch) for h in hs])
    yield
    b_c = jnp.sum(jnp.where(causal, logf_r, 0.0), axis=2, keepdims=True)
    b_r = jnp.sum(jnp.where(diag, b_c, 0.0), axis=1, keepdims=True)
    g = jnp.sum(logf_r, axis=2, keepdims=True)
    a_r = g - b_r + logi_r
    a_c = g - b_c + logi_c
    m_loc = jnp.max(a_r, axis=2, keepdims=True)
    wa_c = jnp.exp(a_c - m_loc)
    yield
    c_loc = per_group(_bdot_tn, v * wa_c, k)
    yield
    n_loc = jnp.sum(k * wa_c, axis=1, keepdims=True)
    d = jnp.where(causal, b_c - b_r + logi_r, -jnp.inf)
    d_max = jnp.max(d, axis=2, keepdims=True)
    yield
    qk = per_group(_bdot_nt, q, k)
    yield

    c_prev = c_scr[...]
    n_prev = n_scr[...]
    m_prev = m_scr[...]
    c_in, n_in, m_in = [], [], []
    for c in range(nch):
        gs = slice(c * H, (c + 1) * H)
        c_in.append(c_prev)
        n_in.append(n_prev)
        m_in.append(m_prev)
        m_new = jnp.maximum(g[gs] + m_prev, m_loc[gs])
        s_old = jnp.exp(g[gs] + m_prev - m_new)
        s_loc = jnp.exp(m_loc[gs] - m_new)
        c_prev = s_old * c_prev + s_loc * c_loc[gs]
        n_prev = s_old * n_prev + s_loc * n_loc[gs]
        m_prev = m_new
    c_scr[...] = c_prev
    n_scr[...] = n_prev
    m_scr[...] = m_prev
    c_in = jnp.concatenate(c_in, axis=0)
    n_in = jnp.concatenate(n_in, axis=0)
    m_in = jnp.concatenate(m_in, axis=0)
    yield

    inter = b_c + m_in
    m_t = jnp.maximum(inter, d_max)
    s_int = jnp.exp(inter - m_t)
    p = jnp.exp(d - m_t) * qk
    yield
    num = s_int * per_group(_bdot_nt, q, c_in) + per_group(_bdot, p, v)
    yield
    den = (s_int * jnp.sum(q * n_in, axis=2, keepdims=True)
           + jnp.sum(p, axis=2, keepdims=True))
    hh = num / jnp.maximum(jnp.abs(den), jnp.exp(-m_t))
    yield
    mu = jnp.mean(hh, axis=-1, keepdims=True)
    hc = hh - mu
    var = jnp.mean(hc * hc, axis=-1, keepdims=True)
    ng = jnp.stack([ng_ref[:, h * M_HDIM:(h + 1) * M_HDIM] for h in hs] * nch)
    y = (_sigmoid(groups(o_ref[...])) * (hc * lax.rsqrt(var + NORM_EPS)) * ng).astype(BF16)
    result.append(jnp.concatenate(
        [jnp.concatenate([y[c * H + h] for h in hs], axis=1) for c in range(nch)], axis=0))


def _rwkv_stages(pr_ref, pk_ref, pv_ref, pwa_ref, pg_ref,
                 mur_ref, muk_ref, muv_ref, muwa_ref, mug_ref,
                 w0_ref, wup_ref, a0_ref, aup_ref, gup_ref,
                 kkp_ref, ka_ref, rk_ref, gng_ref, gnb_ref,
                 cr_scr, ck_scr, cv_scr, cwa_scr, cg_scr, st_scr, result):
    TS = pr_ref.shape[0]
    L = CHUNK
    N = R_HDIM
    nch = TS // L
    npair = R_WIDTH // LANES
    pairs = range(npair)
    Q = 2 * LANES

    def tshift(p_ref, mu_ref, c_scr):
        p = p_ref[...]
        prev = _shift_rows_carry(p, c_scr[...], 1)
        c_scr[...] = p[TS - SUBLANES:]
        return p + (prev - p) * mu_ref[...]

    hsh = N.bit_length() - 1
    li = lax.broadcasted_iota(jnp.int32, (LANES, LANES), 0)
    lj = lax.broadcasted_iota(jnp.int32, (LANES, LANES), 1)
    same_head = jnp.right_shift(li, hsh) == jnp.right_shift(lj, hsh)
    ones_bd = jnp.where(same_head, 1.0, 0.0).astype(BF16)

    def seg_sum(x):
        outs = []
        for p in range(x.shape[1] // LANES):
            hi, lo = _split2(x[:, p * LANES:(p + 1) * LANES])
            outs.append(_dot(hi, ones_bd) + _dot(lo, ones_bd))
        return jnp.concatenate(outs, axis=1)

    rr = tshift(pr_ref, mur_ref, cr_scr)
    kr = tshift(pk_ref, muk_ref, ck_scr)
    vr = tshift(pv_ref, muv_ref, cv_scr)
    yield
    wa = tshift(pwa_ref, muwa_ref, cwa_scr)
    gd = tshift(pg_ref, mug_ref, cg_scr)
    logw = -DECAY_SCALE * _sigmoid(w0_ref[...] + _bdot(jnp.tanh(wa), wup_ref[...]))
    a = _sigmoid(a0_ref[...] + _bdot(wa, aup_ref[...]))
    g = _bdot(_sigmoid(gd), gup_ref[...])
    yield
    kkraw = kr * kkp_ref[...]
    kk = kkraw * lax.rsqrt(jnp.maximum(seg_sum(kkraw * kkraw), 1e-24))
    km = kr * (1.0 + (a - 1.0) * ka_ref[...])
    be = kk * a
    yield
    tq = lax.broadcasted_iota(jnp.int32, (Q, Q), 0)
    sq = lax.broadcasted_iota(jnp.int32, (Q, Q), 1)
    tril = jnp.where((jnp.right_shift(tq, hsh) == jnp.right_shift(sq, hsh)) & (sq <= tq),
                     1.0, 0.0).astype(BF16)
    lw_hi, lw_lo = _split2(logw)
    bincl = jnp.concatenate(
        [_dot(tril, lw_hi[q * Q:(q + 1) * Q]) + _dot(tril, lw_lo[q * Q:(q + 1) * Q])
         for q in range(TS // Q)], axis=0)
    e_in = jnp.exp(bincl)
    e_ng = jnp.exp(-bincl)
    kt = kk * jnp.exp(bincl - logw)
    rt = rr * e_in
    bh = be * e_ng
    kh = km * e_ng
    yield

    h0 = lax.broadcasted_iota(jnp.int32, (L, LANES), 1) < N
    ti = lax.broadcasted_iota(jnp.int32, (2 * L, 2 * L), 0)
    si = lax.broadcasted_iota(jnp.int32, (2 * L, 2 * L), 1)
    same_blk = jnp.right_shift(ti, hsh) == jnp.right_shift(si, hsh)
    strict = same_blk & (si < ti)
    incl = same_blk & (si <= ti)

    def stack_heads(x):
        return jnp.concatenate([jnp.where(h0, x, 0.0), jnp.where(h0, 0.0, x)], axis=0)

    def stack_dup(x):
        return jnp.concatenate([x, x], axis=0)

    def comb(x):
        return jnp.where(h0, x[:L], x[L:])

    def off_mask(b):
        sh = (2 * b).bit_length() - 1
        same = jnp.right_shift(ti, sh) == jnp.right_shift(si, sh)
        return same & (jnp.bitwise_and(ti, b) != 0) & (jnp.bitwise_and(si, b) == 0)

    probs = [(p, c) for p in pairs for c in range(nch)]

    def tile(x, p, c):
        return x[c * L:(c + 1) * L, p * LANES:(p + 1) * LANES]

    lk = [stack_heads(tile(kt, p, c)).astype(BF16) for p, c in probs]
    lr = [stack_heads(tile(rt, p, c)).astype(BF16) for p, c in probs]
    rbk = [jnp.concatenate([stack_dup(tile(bh, p, c)), stack_dup(tile(kh, p, c))],
                           axis=0).astype(BF16) for p, c in probs]
    vs = [stack_dup(tile(vr, p, c)).astype(BF16) for p, c in probs]
    yield
    gk = [_bdot_nt(x, y) for x, y in zip(lk, rbk)]
    yield
    a_bd = [jnp.where(strict, x[:, :2 * L], 0.0) for x in gk]
    bk_bd = [jnp.where(strict, x[:, 2 * L:], 0.0).astype(BF16) for x in gk]
    yield
    gr = [_bdot_nt(x, y) for x, y in zip(lr, rbk)]
    yield
    arb_bd = [jnp.where(incl, x[:, :2 * L], 0.0).astype(BF16) for x in gr]
    ark_bd = [jnp.where(incl, x[:, 2 * L:], 0.0).astype(BF16) for x in gr]
    yield
    eye = jnp.where(ti == si, 1.0, 0.0)
    m1 = off_mask(1)
    t_bd = [eye - jnp.where(m1, x, 0.0) for x in a_bd]
    b = 2
    while b < L:
        mb = off_mask(b)
        offs = [jnp.where(mb, x, 0.0).astype(BF16) for x in a_bd]
        xs = [_bdot(i, o) for i, o in zip(t_bd, offs)]
        yield
        ys = [_bdot(x, i) for x, i in zip(xs, t_bd)]
        t_bd = [i - y for i, y in zip(t_bd, ys)]
        yield
        b *= 2
    bkv = [comb(_bdot(x, y)) for x, y in zip(bk_bd, vs)]
    yield
    xs = [jnp.concatenate([stack_dup(tile(kt, p, c)), stack_dup(z)], axis=1)
          for (p, c), z in zip(probs, bkv)]
    tx = [_bdot(x, y) for x, y in zip(t_bd, xs)]
    yield
    arkv = [comb(_bdot(x, y)) for x, y in zip(ark_bd, vs)]
    w_ch = [[None] * nch for _ in pairs]
    u0_ch = [[None] * nch for _ in pairs]
    arkv_ch = [[None] * nch for _ in pairs]
    arb_ch = [[None] * nch for _ in pairs]
    for i, (p, c) in enumerate(probs):
        w_ch[p][c] = comb(tx[i][:, :LANES])
        u0_ch[p][c] = -comb(tx[i][:, LANES:])
        arkv_ch[p][c] = arkv[i]
        arb_ch[p][c] = arb_bd[i]
    yield

    y_rows = []
    st = [st_scr[p] for p in pairs]
    ones_f = jnp.where(same_head, 1.0, 0.0)
    lsl = [slice(p * LANES, (p + 1) * LANES) for p in pairs]
    for c in range(nch):
        rs = slice(c * L, (c + 1) * L)
        p_end = e_in[c * L + L - 1:c * L + L, :]
        pe = [p_end[:, ls] for ls in lsl]
        rw = [_bdot_nt(jnp.concatenate([rt[rs, lsl[p]], w_ch[p][c]], axis=0), st[p])
              for p in pairs]
        u = [u0_ch[p][c] - rw[p][L:] for p in pairs]
        au = [_bdot(arb_ch[p][c], jnp.concatenate([u[p], u[p]], axis=0)) for p in pairs]
        upd = [_bdot_tn(jnp.concatenate([u[p], vr[rs, lsl[p]]], axis=0),
                        jnp.concatenate([bh[rs, lsl[p]] * pe[p], kh[rs, lsl[p]] * pe[p]], axis=0))
               for p in pairs]
        st = [st[p] * pe[p] + upd[p] * ones_f for p in pairs]
        y_rows.append(jnp.concatenate(
            [rw[p][:L] + jnp.where(h0, au[p][:L], au[p][L:]) + arkv_ch[p][c] for p in pairs],
            axis=1))
        yield
    for p in pairs:
        st_scr[p] = st[p]
    y = jnp.concatenate(y_rows, axis=0)

    inv_n = 1.0 / N
    mu = seg_sum(y) * inv_n
    yc = y - mu
    var = seg_sum(yc * yc) * inv_n
    yn = yc * lax.rsqrt(var + GN_EPS) * gng_ref[...] + gnb_ref[...]
    bonus = seg_sum(rr * km * rk_ref[...]) * vr
    result.append(((yn + bonus) * g).astype(BF16))


def _mixer_kernel(*refs, tn):
    (x_ref, xn_ref, win_ref, gin_ref, brow_ref, bcol_ref, cw_ref, cb_ref, ng_ref,
     mur_ref, muk_ref, muv_ref, muwa_ref, mug_ref,
     w0_ref, wup_ref, a0_ref, aup_ref, gup_ref, kkp_ref, ka_ref, rk_ref, gng_ref, gnb_ref,
     wout_ref, out_ref,
     qp_scr, kp_scr, c_scr, n_scr, m_scr,
     cr_scr, ck_scr, cv_scr, cwa_scr, cg_scr, st_scr, p_scr) = refs
    step = pl.program_id(0) * pl.num_programs(1) + pl.program_id(1)
    chunks = [slice(c * tn, (c + 1) * tn) for c in range(win_ref.shape[1] // tn)]

    def project(h, dst_slot, cols_list):
        for cols in cols_list:
            p_scr[dst_slot, :, cols] = _dot(h, win_ref[:, cols])

    @pl.when(step == 0)
    def _():
        project(_rms_bf16(x_ref[0], gin_ref[...]), 0, chunks)

    @pl.when(pl.program_id(1) == 0)
    def _():
        for scr in (qp_scr, kp_scr, c_scr, n_scr, m_scr,
                    cr_scr, ck_scr, cv_scr, cwa_scr, cg_scr, st_scr):
            scr[...] = jnp.zeros_like(scr)

    slot = lax.rem(step, 2)
    view = lambda off, width: p_scr.at[slot, :, pl.ds(off, width)]
    y_m, y_r = [], []
    mlstm = _mlstm_stages(view(OFF_MQ, M_WIDTH), view(OFF_MK, M_WIDTH), view(OFF_MV, M_WIDTH),
                          view(OFF_MO, M_WIDTH), view(OFF_MG, GATE_PAD), brow_ref, bcol_ref,
                          cw_ref, cb_ref, ng_ref, qp_scr, kp_scr, c_scr, n_scr, m_scr, y_m)
    rwkv = _rwkv_stages(view(OFF_RR, R_WIDTH), view(OFF_RK, R_WIDTH), view(OFF_RV, R_WIDTH),
                        view(OFF_RWA, LANES), view(OFF_RG, GLORA_PAD),
                        mur_ref, muk_ref, muv_ref, muwa_ref, mug_ref,
                        w0_ref, wup_ref, a0_ref, aup_ref, gup_ref,
                        kkp_ref, ka_ref, rk_ref, gng_ref, gnb_ref,
                        cr_scr, ck_scr, cv_scr, cwa_scr, cg_scr, st_scr, y_r)
    def project_next():
        h_next = _rms_bf16(xn_ref[0], gin_ref[...])
        for cols in chunks:
            project(h_next, 1 - slot, [cols])
            yield

    proj = project_next()
    streams = {"R": rwkv, "M": mlstm, "P": proj}
    for tag in "PRPRPRPRR" + "RMR" * 9 + "PRMPRMPRMPRM":
        next(streams[tag], None)
    for stream in (proj, mlstm, rwkv):
        for _ in stream:
            pass
    out_ref[0] = (x_ref[0] + _dot(y_m[0], wout_ref[:M_WIDTH, :])
                  + _dot(y_r[0], wout_ref[M_WIDTH:, :]))


def _mixer(x, w_in, gain, b_row, b_col, conv_w, conv_b, norm_g,
           mu, w0, w_up, a0, a_up, g_up, kkp, ka, rk, gn_g, gn_b, w_out, layer, tn=512):
    B, S, D = x.shape
    ts = min(SEQ_TILE, S)
    n_seq = S // ts
    last = B * n_seq - 1
    npair = R_WIDTH // LANES
    vec = lambda w, off: pl.BlockSpec((1, w), lambda b, s, off=off: (0, off))
    full = lambda a: pl.BlockSpec(a.shape, lambda b, s: (0,) * a.ndim)

    def next_tile(b, s):
        t = jnp.minimum(b * n_seq + s + 1, last)
        return (t // n_seq, t % n_seq, 0)

    return pl.pallas_call(
        functools.partial(_mixer_kernel, tn=tn),
        out_shape=jax.ShapeDtypeStruct((B, S, D), F32),
        grid=(B, n_seq),
        in_specs=[
            pl.BlockSpec((1, ts, D), lambda b, s: (b, s, 0)),
            pl.BlockSpec((1, ts, D), next_tile),
            _layer_weight_spec(w_in, layer), full(gain),
            full(b_row), full(b_col), full(conv_w), full(conv_b), full(norm_g),
            vec(R_WIDTH, OFF_RR // R_WIDTH), vec(R_WIDTH, OFF_RK // R_WIDTH),
            vec(R_WIDTH, OFF_RV // R_WIDTH), vec(LANES, OFF_RWA // LANES),
            vec(GLORA_PAD, OFF_RG // GLORA_PAD),
            full(w0), full(w_up), full(a0), full(a_up), full(g_up),
            full(kkp), full(ka), full(rk), full(gn_g), full(gn_b),
            _layer_weight_spec(w_out, layer),
        ],
        out_specs=pl.BlockSpec((1, ts, D), lambda b, s: (b, s, 0)),
        scratch_shapes=[pltpu.VMEM((SUBLANES, M_WIDTH), F32), pltpu.VMEM((SUBLANES, M_WIDTH), F32),
                        pltpu.VMEM((M_HEADS, M_HDIM, M_HDIM), F32),
                        pltpu.VMEM((M_HEADS, 1, M_HDIM), F32),
                        pltpu.VMEM((M_HEADS, 1, 1), F32),
                        pltpu.VMEM((SUBLANES, R_WIDTH), F32), pltpu.VMEM((SUBLANES, R_WIDTH), F32),
                        pltpu.VMEM((SUBLANES, R_WIDTH), F32), pltpu.VMEM((SUBLANES, LANES), F32),
                        pltpu.VMEM((SUBLANES, GLORA_PAD), F32),
                        pltpu.VMEM((npair, LANES, LANES), F32),
                        pltpu.VMEM((2, ts, IN_COLS_P), F32)],
        compiler_params=pltpu.CompilerParams(
            dimension_semantics=("arbitrary", "arbitrary"), vmem_limit_bytes=VMEM_LIMIT),
        name="mixer",
    )(x, x, w_in, gain, b_row, b_col, conv_w, conv_b, norm_g, mu, mu, mu, mu, mu,
      w0, w_up, a0, a_up, g_up, kkp, ka, rk, gn_g, gn_b, w_out)


def _xattn_kernel(x_ref, g_ref, wq_ref, kv_ref, wo_ref, o_ref):
    D = x_ref.shape[1]
    x = x_ref[...]
    q = _dot(_rms_bf16(x, g_ref[...]), wq_ref[...]).astype(BF16)
    hsl = [slice(hd * X_HDIM, (hd + 1) * X_HDIM) for hd in range(D // X_HDIM)]
    s = [lax.dot_general(q[:, ls], kv_ref[0, :, ls], (((1,), (1,)), ((), ())),
                         preferred_element_type=F32) * (X_HDIM ** -0.5) for ls in hsl]
    e = [jnp.exp(si - jnp.max(si, axis=-1, keepdims=True)) for si in s]
    p = [(ei / jnp.sum(ei, axis=-1, keepdims=True)).astype(BF16) for ei in e]
    heads = [_dot(pi, kv_ref[0, :, D + ls.start:D + ls.stop]).astype(BF16)
             for pi, ls in zip(p, hsl)]
    o_ref[...] = x + _dot(jnp.concatenate(heads, axis=1), wo_ref[...])


def _xattn(x, gain, wq, kv, wo, layer, seq_len, tm=1024):
    T, D = x.shape
    M = kv.shape[1]
    tm = min(tm, seq_len)
    per_seq = seq_len // tm
    return pl.pallas_call(
        _xattn_kernel,
        out_shape=jax.ShapeDtypeStruct((T, D), F32),
        grid=(T // tm,),
        in_specs=[pl.BlockSpec((tm, D), lambda i: (i, 0)),
                  pl.BlockSpec((1, D), lambda i: (0, 0)),
                  _layer_weight_spec(wq, layer),
                  pl.BlockSpec((1, M, 2 * D), lambda i: (i // per_seq, 0, 0)),
                  _layer_weight_spec(wo, layer)],
        out_specs=pl.BlockSpec((tm, D), lambda i: (i, 0)),
        compiler_params=pltpu.CompilerParams(
            dimension_semantics=("parallel",), vmem_limit_bytes=VMEM_LIMIT),
        name="xattn",
    )(x, gain.reshape(1, D), wq, kv, wo)


def _ffn_kernel(*refs, per_seq, tc, norm_out):
    x_ref, g_ref, wup_ref, cw_ref, cb_ref, wdn_ref = refs[:6]
    gout_ref = refs[6] if norm_out else None
    o_ref, tail_scr = refs[-2:]
    tm = x_ref.shape[0]

    @pl.when(lax.rem(pl.program_id(0), per_seq) == 0)
    def _():
        tail_scr[...] = jnp.zeros_like(tail_scr)

    x = x_ref[...]
    h = _rms_bf16(x, g_ref[...])
    chunks = [slice(c * tc, (c + 1) * tc) for c in range(D_FF // tc)]
    gates = [_dot(h, wup_ref[:, cols]) for cols in chunks]
    vals = [_dot(h, wup_ref[:, D_FF + cols.start:D_FF + cols.stop]) for cols in chunks]
    acts = []
    for cols, gate, val in zip(chunks, gates, vals):
        prev = tail_scr[:, cols]
        y = cb_ref[:, cols] + gate * cw_ref[FFN_CONV - 1:FFN_CONV, cols]
        for j in range(FFN_CONV - 1):
            y = y + _shift_rows_carry(gate, prev, FFN_CONV - 1 - j) * cw_ref[j:j + 1, cols]
        tail_scr[:, cols] = gate[tm - SUBLANES:]
        acts.append((y * _sigmoid(y) * val).astype(BF16))
    acc = x
    for cols, act in zip(chunks, acts):
        acc = acc + _dot(act, wdn_ref[cols, :])
    if norm_out:
        ms = jnp.mean(acc * acc, axis=-1, keepdims=True)
        acc = acc * lax.rsqrt(ms + NORM_EPS) * gout_ref[...]
    o_ref[...] = acc


def _ffn(x, gain, w_up, conv_w, conv_b, w_down, layer, seq_len, out_gain=None, tm=512, tc=256):
    T, D = x.shape
    tm = min(tm, seq_len)
    norm_out = out_gain is not None
    const = lambda a: pl.BlockSpec(a.shape, lambda i: (0,) * a.ndim, pipeline_mode=pl.Buffered(1))
    small = [conv_w, conv_b.reshape(1, D_FF)] + ([out_gain.reshape(1, D)] if norm_out else [])
    specs = [const(a) for a in small]
    return pl.pallas_call(
        functools.partial(_ffn_kernel, per_seq=seq_len // tm, tc=tc, norm_out=norm_out),
        out_shape=jax.ShapeDtypeStruct((T, D), F32),
        grid=(T // tm,),
        in_specs=[pl.BlockSpec((tm, D), lambda i: (i, 0)),
                  pl.BlockSpec((1, D), lambda i: (0, 0)),
                  _layer_weight_spec(w_up, layer), specs[0], specs[1],
                  _layer_weight_spec(w_down, layer)] + specs[2:],
        out_specs=pl.BlockSpec((tm, D), lambda i: (i, 0)),
        scratch_shapes=[pltpu.VMEM((SUBLANES, D_FF), F32)],
        compiler_params=pltpu.CompilerParams(
            dimension_semantics=("arbitrary",), vmem_limit_bytes=VMEM_LIMIT),
        name="ffn",
    )(x, gain.reshape(1, D), w_up, small[0], small[1], w_down, *small[2:])


def _pad_cols(w, n):
    return jnp.pad(w, [(0, 0)] * (w.ndim - 1) + [(0, n - w.shape[-1])])


def _relayout_in(w):
    m_main = w[..., :4 * M_WIDTH]
    m_gate = w[..., 4 * M_WIDTH:4 * M_WIDTH + 2 * M_HEADS]
    r0 = 4 * M_WIDTH + 2 * M_HEADS
    r_main = w[..., r0:r0 + 3 * R_WIDTH + R_DECAY_LORA + R_AAA_LORA]
    r_gate = w[..., r0 + 3 * R_WIDTH + R_DECAY_LORA + R_AAA_LORA:]
    return jnp.concatenate(
        [m_main, r_main, _pad_cols(m_gate, GATE_PAD), _pad_cols(r_gate, GLORA_PAD)], axis=-1)


def kernel(x, mem, norm_mix, w_in, m_conv_w, m_conv_b, m_gate_b, m_norm_g, r_mu, r_w0,
           r_w_up, r_a0, r_a_up, r_g_up, r_kk, r_ka, r_rk, r_gn_g, r_gn_b, w_out,
           norm_x, norm_mem, x_wq, x_wkv, x_wo, norm_ffn, f_up, f_conv_w, f_conv_b,
           f_down, norm_final):
    B, S, D = x.shape
    M = mem.shape[1]
    depth = w_in.shape[0]
    T = B * S
    xf = x.reshape(T, D)
    memf = mem.reshape(B * M, D)
    row = lambda a: a.reshape(1, -1)
    w_in_p, w_out_b = _relayout_in(w_in).astype(BF16), w_out.astype(BF16)
    wq_b, wkv_b, wo_b = x_wq.astype(BF16), x_wkv.astype(BF16), x_wo.astype(BF16)
    f_up_b, f_down_b = f_up.astype(BF16), f_down.astype(BF16)

    for l in range(depth):
        mu = _relayout_in(jnp.pad(row(r_mu[l]), ((0, 0), (4 * M_WIDTH + 2 * M_HEADS, 0))))
        w_up = jnp.pad(r_w_up[l], ((0, R_AAA_LORA), (0, 0))).astype(BF16)
        a_up = jnp.pad(r_a_up[l], ((R_DECAY_LORA, 0), (0, 0))).astype(BF16)
        g_up = jnp.pad(r_g_up[l], ((0, GLORA_PAD - R_GATE_LORA), (0, 0))).astype(BF16)
        xf = _mixer(xf.reshape(B, S, D), w_in_p, row(norm_mix[l]),
                    m_gate_b[l].reshape(-1, 1, 1), row(m_gate_b[l]),
                    m_conv_w[l], row(m_conv_b[l]), row(m_norm_g[l]),
                    mu, row(r_w0[l]), w_up, row(r_a0[l]), a_up, g_up,
                    row(r_kk[l]), row(r_ka[l]), row(r_rk[l]), row(r_gn_g[l]), row(r_gn_b[l]),
                    w_out_b, l).reshape(T, D)

        kv = _mm(memf, wkv_b, l, gain=norm_mem[l], out_dtype=BF16)
        xf = _xattn(xf, norm_x[l], wq_b, kv.reshape(B, M, 2 * D), wo_b, l, S)

        xf = _ffn(xf, norm_ffn[l], f_up_b, f_conv_w[l], f_conv_b[l], f_down_b, l, S,
                  out_gain=norm_final if l == depth - 1 else None)

    return xf.reshape(B, S, D)
```

```python
import functools
import math

import jax
import jax.numpy as jnp
from jax import lax
from jax.experimental import pallas as pl
from jax.experimental.pallas import tpu as pltpu

F32 = jnp.float32
BF16 = jnp.bfloat16

M_WIDTH = 512
M_HEADS = 4
M_HDIM = 128
M_CONV = 4
R_WIDTH = 512
R_HDIM = 64
R_DECAY_LORA = 64
R_AAA_LORA = 64
R_GATE_LORA = 160
DECAY_SCALE = math.exp(-0.5)
X_HDIM = 256
D_FF = 2816
FFN_CONV = 3
NORM_EPS = 1e-6
GN_EPS = 64e-5
CHUNK = 64

LANES = 128
SUBLANES = 8
GATE_PAD = LANES
GLORA_PAD = 2 * LANES
OFF_MQ, OFF_MK, OFF_MV, OFF_MO = 0, 512, 1024, 1536
OFF_RR = 2048
OFF_RK = OFF_RR + R_WIDTH
OFF_RV = OFF_RK + R_WIDTH
OFF_RWA = OFF_RV + R_WIDTH
OFF_MG = OFF_RWA + LANES
OFF_RG = OFF_MG + GATE_PAD
IN_COLS_P = OFF_RG + GLORA_PAD

SEQ_TILE = 256
VMEM_LIMIT = 48 * 1024 * 1024


def _dot(a, b):
    return jnp.dot(a, b, preferred_element_type=F32)


def _bdot(a, b):
    return jnp.dot(a.astype(BF16), b.astype(BF16), preferred_element_type=F32)


def _bdot_nt(a, b):
    return lax.dot_general(a.astype(BF16), b.astype(BF16), (((1,), (1,)), ((), ())),
                           preferred_element_type=F32)


def _bdot_tn(a, b):
    return lax.dot_general(a.astype(BF16), b.astype(BF16), (((0,), (0,)), ((), ())),
                           preferred_element_type=F32)


def _split2(x):
    hi = x.astype(BF16)
    lo = (x - hi.astype(F32)).astype(BF16)
    return hi, lo


def _sigmoid(x):
    return 0.5 * jnp.tanh(0.5 * x) + 0.5


def _shift_rows_carry(x, prev, sh):
    r = pltpu.roll(x, sh, 0)
    row = lax.broadcasted_iota(jnp.int32, prev.shape, 0)
    head = jnp.where(row < sh, pltpu.roll(prev, sh, 0), r[:SUBLANES])
    return jnp.concatenate([head, r[SUBLANES:]], axis=0)


def _rms_bf16(x, gain):
    ms = jnp.mean(x * x, axis=-1, keepdims=True)
    return (x * lax.rsqrt(ms + NORM_EPS) * gain).astype(BF16)


def _layer_weight_spec(w, layer):
    return pl.BlockSpec((None,) + w.shape[1:], lambda *_: (layer,) + (0,) * (w.ndim - 1),
                        pipeline_mode=pl.Buffered(1))


def _mm_kernel(x_ref, w_ref, g_ref, o_ref, h_scr, *, tn):
    h_scr[...] = _rms_bf16(x_ref[...], g_ref[...])
    for c in range(w_ref.shape[1] // tn):
        cols = slice(c * tn, (c + 1) * tn)
        o_ref[:, cols] = _dot(h_scr[...], w_ref[:, cols]).astype(o_ref.dtype)


def _mm_tile_rows(K, N, x_bytes, out_bytes):
    budget = (VMEM_LIMIT * 3) // 4 - K * N * 2
    for tm in (512, 256, 128):
        if tm * (2 * K * x_bytes + 2 * N * out_bytes + K * 2) <= budget:
            return tm
    raise ValueError("weight does not fit in VMEM")


def _mm(x, w, layer, gain, out_dtype=F32, tn=512):
    T, K = x.shape
    N = w.shape[2]
    tm = min(T, _mm_tile_rows(K, N, x.dtype.itemsize, jnp.dtype(out_dtype).itemsize))
    tn = min(tn, N)
    assert T % tm == 0 and N % tn == 0
    return pl.pallas_call(
        functools.partial(_mm_kernel, tn=tn),
        out_shape=jax.ShapeDtypeStruct((T, N), out_dtype),
        grid=(T // tm,),
        in_specs=[pl.BlockSpec((tm, K), lambda i: (i, 0)),
                  _layer_weight_spec(w, layer),
                  pl.BlockSpec((1, K), lambda i: (0, 0))],
        out_specs=pl.BlockSpec((tm, N), lambda i: (i, 0)),
        scratch_shapes=[pltpu.VMEM((tm, K), BF16)],
        compiler_params=pltpu.CompilerParams(
            dimension_semantics=("parallel",), vmem_limit_bytes=VMEM_LIMIT),
        name="mm",
    )(x, w, gain.reshape(1, K))


def _mlstm_stages(q_ref, k_ref, v_ref, o_ref, g_ref, brow_ref, bcol_ref,
                  cw_ref, cb_ref, ng_ref, qp_scr, kp_scr, c_scr, n_scr, m_scr, result):
    TS = q_ref.shape[0]
    L = CHUNK
    nch = TS // L
    H = M_HEADS
    G = nch * H
    hs = range(H)

    def conv_silu(x, prev, w, b):
        y = b + x * w[M_CONV - 1:M_CONV, :]
        for j in range(M_CONV - 1):
            y = y + _shift_rows_carry(x, prev, M_CONV - 1 - j) * w[j:j + 1, :]
        return y * _sigmoid(y)

    def groups(x):
        return jnp.stack([x[c * L:(c + 1) * L, h * M_HDIM:(h + 1) * M_HDIM]
                          for c in range(nch) for h in hs])

    def per_group(f, *xs):
        return jnp.stack([f(*[x[i] for x in xs]) for i in range(G)])

    def log_sigmoid(x):
        return jnp.minimum(x, 0.0) - jnp.log1p(jnp.exp(-jnp.abs(x)))

    q_raw = q_ref[...]
    q = groups(conv_silu(q_raw, qp_scr[...], cw_ref[:, :M_WIDTH], cb_ref[:, :M_WIDTH])
               * (M_HDIM ** -0.5))
    qp_scr[...] = q_raw[TS - SUBLANES:]
    yield
    k_raw = k_ref[...]
    k = groups(conv_silu(k_raw, kp_scr[...], cw_ref[:, M_WIDTH:], cb_ref[:, M_WIDTH:]))
    kp_scr[...] = k_raw[TS - SUBLANES:]
    v = groups(v_ref[...])
    yield

    ti = lax.broadcasted_iota(jnp.int32, (L, L), 0)
    si = lax.broadcasted_iota(jnp.int32, (L, L), 1)
    causal = si <= ti
    diag = ti == si
    pg = g_ref[...]
    zpad = jnp.zeros((GATE_PAD - L, GATE_PAD), F32)
    pg_t = [jnp.concatenate([pg[c * L:(c + 1) * L], zpad], axis=0).T for c in range(nch)]
    logi_r = jnp.stack([pg_t[c][h:h + 1, :L] + brow_ref[h]
                        for c in range(nch) for h in hs])
    logf_r = log_sigmoid(jnp.stack([pg_t[c][H + h:H + h + 1, :L] + brow_ref[H + h]
                                    for c in range(nch) for h in hs]))
    gc = pg[:, :2 * H] + bcol_ref[...]
    logi_c = jnp.stack([gc[c * L:(c + 1) * L, h:h + 1] for c in range(nch) for h in hs])
    yield
    b_c = jnp.sum(jnp.where(causal, logf_r, 0.0), axis=2, keepdims=True)
    b_r = jnp.sum(jnp.where(diag, b_c, 0.0), axis=1, keepdims=True)
    g = jnp.sum(logf_r, axis=2, keepdims=True)
    a_r = g - b_r + logi_r
    a_c = g - b_c + logi_c
    m_loc = jnp.max(a_r, axis=2, keepdims=True)
    wa_c = jnp.exp(a_c - m_loc)
    yield
    c_loc = per_group(_bdot_tn, v * wa_c, k)
    yield
    n_loc = jnp.sum(k * wa_c, axis=1, keepdims=True)
    d = jnp.where(causal, b_c - b_r + logi_r, -jnp.inf)
    d_max = jnp.max(d, axis=2, keepdims=True)
    yield
    qk = per_group(_bdot_nt, q, k)
    yield

    c_prev = c_scr[...]
    n_prev = n_scr[...]
    m_prev = m_scr[...]
    c_in, n_in, m_in = [], [], []
    for c in range(nch):
        gs = slice(c * H, (c + 1) * H)
        c_in.append(c_prev)
        n_in.append(n_prev)
        m_in.append(m_prev)
        m_new = jnp.maximum(g[gs] + m_prev, m_loc[gs])
        s_old = jnp.exp(g[gs] + m_prev - m_new)
        s_loc = jnp.exp(m_loc[gs] - m_new)
        c_prev = s_old * c_prev + s_loc * c_loc[gs]
        n_prev = s_old * n_prev + s_loc * n_loc[gs]
        m_prev = m_new
    c_scr[...] = c_prev
    n_scr[...] = n_prev
    m_scr[...] = m_prev
    c_in = jnp.concatenate(c_in, axis=0)
    n_in = jnp.concatenate(n_in, axis=0)
    m_in = jnp.concatenate(m_in, axis=0)
    yield

    inter = b_c + m_in
    m_t = jnp.maximum(inter, d_max)
    s_int = jnp.exp(inter - m_t)
    p = jnp.exp(d - m_t) * qk
    yield
    c_aug = jnp.concatenate([c_in, jnp.broadcast_to(n_in, c_in.shape)], axis=1)
    v_aug = jnp.concatenate([v, jnp.ones_like(v)], axis=2)
    qc = per_group(_bdot_nt, q, c_aug)
    pv = per_group(_bdot, p, v_aug)
    num = s_int * qc[:, :, :M_HDIM] + pv[:, :, :M_HDIM]
    yield
    den = s_int * qc[:, :, M_HDIM:M_HDIM + 1] + pv[:, :, M_HDIM:M_HDIM + 1]
    hh = num / jnp.maximum(jnp.abs(den), jnp.exp(-m_t))
    yield
    mu = jnp.mean(hh, axis=-1, keepdims=True)
    hc = hh - mu
    var = jnp.mean(hc * hc, axis=-1, keepdims=True)
    ng = jnp.stack([ng_ref[:, h * M_HDIM:(h + 1) * M_HDIM] for h in hs] * nch)
    y = (_sigmoid(groups(o_ref[...])) * (hc * lax.rsqrt(var + NORM_EPS)) * ng).astype(BF16)
    result.append(jnp.concatenate(
        [jnp.concatenate([y[c * H + h] for h in hs], axis=1) for c in range(nch)], axis=0))


def _rwkv_stages(pr_ref, pk_ref, pv_ref, pwa_ref, pg_ref,
                 mur_ref, muk_ref, muv_ref, muwa_ref, mug_ref,
                 w0_ref, wup_ref, a0_ref, aup_ref, gup_ref,
                 kkp_ref, ka_ref, rk_ref, gng_ref, gnb_ref,
                 cr_scr, ck_scr, cv_scr, cwa_scr, cg_scr, st_scr, result):
    TS = pr_ref.shape[0]
    L = CHUNK
    N = R_HDIM
    nch = TS // L
    npair = R_WIDTH // LANES
    pairs = range(npair)
    Q = 2 * LANES

    def tshift(p_ref, mu_ref, c_scr):
        p = p_ref[...]
        prev = _shift_rows_carry(p, c_scr[...], 1)
        c_scr[...] = p[TS - SUBLANES:]
        return p + (prev - p) * mu_ref[...]

    hsh = N.bit_length() - 1
    li = lax.broadcasted_iota(jnp.int32, (LANES, LANES), 0)
    lj = lax.broadcasted_iota(jnp.int32, (LANES, LANES), 1)
    same_head = jnp.right_shift(li, hsh) == jnp.right_shift(lj, hsh)
    ones_bd = jnp.where(same_head, 1.0, 0.0).astype(BF16)

    def seg_sum(x):
        outs = []
        for p in range(x.shape[1] // LANES):
            hi, lo = _split2(x[:, p * LANES:(p + 1) * LANES])
            outs.append(_dot(hi, ones_bd) + _dot(lo, ones_bd))
        return jnp.concatenate(outs, axis=1)

    rr = tshift(pr_ref, mur_ref, cr_scr)
    kr = tshift(pk_ref, muk_ref, ck_scr)
    vr = tshift(pv_ref, muv_ref, cv_scr)
    yield
    wa = tshift(pwa_ref, muwa_ref, cwa_scr)
    gd = tshift(pg_ref, mug_ref, cg_scr)
    logw = -DECAY_SCALE * _sigmoid(w0_ref[...] + _bdot(jnp.tanh(wa), wup_ref[...]))
    a = _sigmoid(a0_ref[...] + _bdot(wa, aup_ref[...]))
    g = _bdot(_sigmoid(gd), gup_ref[...])
    yield
    kkraw = kr * kkp_ref[...]
    kk = kkraw * lax.rsqrt(jnp.maximum(seg_sum(kkraw * kkraw), 1e-24))
    km = kr * (1.0 + (a - 1.0) * ka_ref[...])
    be = kk * a
    yield
    tq = lax.broadcasted_iota(jnp.int32, (Q, Q), 0)
    sq = lax.broadcasted_iota(jnp.int32, (Q, Q), 1)
    tril = jnp.where((jnp.right_shift(tq, hsh) == jnp.right_shift(sq, hsh)) & (sq <= tq),
                     1.0, 0.0).astype(BF16)
    lw_hi, lw_lo = _split2(logw)
    bincl = jnp.concatenate(
        [_dot(tril, lw_hi[q * Q:(q + 1) * Q]) + _dot(tril, lw_lo[q * Q:(q + 1) * Q])
         for q in range(TS // Q)], axis=0)
    e_in = jnp.exp(bincl)
    e_ng = jnp.exp(-bincl)
    kt = kk * jnp.exp(bincl - logw)
    rt = rr * e_in
    bh = be * e_ng
    kh = km * e_ng
    yield

    h0 = lax.broadcasted_iota(jnp.int32, (L, LANES), 1) < N
    ti = lax.broadcasted_iota(jnp.int32, (2 * L, 2 * L), 0)
    si = lax.broadcasted_iota(jnp.int32, (2 * L, 2 * L), 1)
    same_blk = jnp.right_shift(ti, hsh) == jnp.right_shift(si, hsh)
    strict = same_blk & (si < ti)
    incl = same_blk & (si <= ti)

    def stack_heads(x):
        return jnp.concatenate([jnp.where(h0, x, 0.0), jnp.where(h0, 0.0, x)], axis=0)

    def stack_dup(x):
        return jnp.concatenate([x, x], axis=0)

    def comb(x):
        return jnp.where(h0, x[:L], x[L:])

    def off_mask(b):
        sh = (2 * b).bit_length() - 1
        same = jnp.right_shift(ti, sh) == jnp.right_shift(si, sh)
        return same & (jnp.bitwise_and(ti, b) != 0) & (jnp.bitwise_and(si, b) == 0)

    probs = [(p, c) for p in pairs for c in range(nch)]

    def tile(x, p, c):
        return x[c * L:(c + 1) * L, p * LANES:(p + 1) * LANES]

    lk = [stack_heads(tile(kt, p, c)).astype(BF16) for p, c in probs]
    lr = [stack_heads(tile(rt, p, c)).astype(BF16) for p, c in probs]
    rbk = [jnp.concatenate([stack_dup(tile(bh, p, c)), stack_dup(tile(kh, p, c))],
                           axis=0).astype(BF16) for p, c in probs]
    vs = [stack_dup(tile(vr, p, c)).astype(BF16) for p, c in probs]
    yield
    gk = [_bdot_nt(x, y) for x, y in zip(lk, rbk)]
    yield
    a_bd = [jnp.where(strict, x[:, :2 * L], 0.0) for x in gk]
    bk_bd = [jnp.where(strict, x[:, 2 * L:], 0.0).astype(BF16) for x in gk]
    yield
    gr = [_bdot_nt(x, y) for x, y in zip(lr, rbk)]
    yield
    arb_bd = [jnp.where(incl, x[:, :2 * L], 0.0).astype(BF16) for x in gr]
    ark_bd = [jnp.where(incl, x[:, 2 * L:], 0.0).astype(BF16) for x in gr]
    yield
    eye = jnp.where(ti == si, 1.0, 0.0)
    m1 = off_mask(1)
    t_bd = [eye - jnp.where(m1, x, 0.0) for x in a_bd]
    b = 2
    while b < L:
        mb = off_mask(b)
        offs = [jnp.where(mb, x, 0.0).astype(BF16) for x in a_bd]
        xs = [_bdot(i, o) for i, o in zip(t_bd, offs)]
        yield
        ys = [_bdot(x, i) for x, i in zip(xs, t_bd)]
        t_bd = [i - y for i, y in zip(t_bd, ys)]
        yield
        b *= 2
    bkv = [comb(_bdot(x, y)) for x, y in zip(bk_bd, vs)]
    yield
    xs = [jnp.concatenate([stack_dup(tile(kt, p, c)), stack_dup(z)], axis=1)
          for (p, c), z in zip(probs, bkv)]
    tx = [_bdot(x, y) for x, y in zip(t_bd, xs)]
    yield
    arkv = [comb(_bdot(x, y)) for x, y in zip(ark_bd, vs)]
    w_ch = [[None] * nch for _ in pairs]
    u0_ch = [[None] * nch for _ in pairs]
    arkv_ch = [[None] * nch for _ in pairs]
    arb_ch = [[None] * nch for _ in pairs]
    for i, (p, c) in enumerate(probs):
        w_ch[p][c] = comb(tx[i][:, :LANES])
        u0_ch[p][c] = -comb(tx[i][:, LANES:])
        arkv_ch[p][c] = arkv[i]
        arb_ch[p][c] = arb_bd[i]
    yield

    y_rows = []
    st = [st_scr[p] for p in pairs]
    ones_f = jnp.where(same_head, 1.0, 0.0)
    lsl = [slice(p * LANES, (p + 1) * LANES) for p in pairs]
    for c in range(nch):
        rs = slice(c * L, (c + 1) * L)
        p_end = e_in[c * L + L - 1:c * L + L, :]
        pe = [p_end[:, ls] for ls in lsl]
        rw = [_bdot_nt(jnp.concatenate([rt[rs, lsl[p]], w_ch[p][c]], axis=0), st[p])
              for p in pairs]
        u = [u0_ch[p][c] - rw[p][L:] for p in pairs]
        au = [_bdot(arb_ch[p][c], jnp.concatenate([u[p], u[p]], axis=0)) for p in pairs]
        upd = [_bdot_tn(jnp.concatenate([u[p], vr[rs, lsl[p]]], axis=0),
                        jnp.concatenate([bh[rs, lsl[p]] * pe[p], kh[rs, lsl[p]] * pe[p]], axis=0))
               for p in pairs]
        st = [st[p] * pe[p] + upd[p] * ones_f for p in pairs]
        y_rows.append(jnp.concatenate(
            [rw[p][:L] + jnp.where(h0, au[p][:L], au[p][L:]) + arkv_ch[p][c] for p in pairs],
            axis=1))
        yield
    for p in pairs:
        st_scr[p] = st[p]
    y = jnp.concatenate(y_rows, axis=0)

    inv_n = 1.0 / N
    mu = seg_sum(y) * inv_n
    yc = y - mu
    var = seg_sum(yc * yc) * inv_n
    yn = yc * lax.rsqrt(var + GN_EPS) * gng_ref[...] + gnb_ref[...]
    bonus = seg_sum(rr * km * rk_ref[...]) * vr
    result.append(((yn + bonus) * g).astype(BF16))


def _mixer_kernel(*refs, tn):
    (x_ref, xn_ref, win_ref, gin_ref, brow_ref, bcol_ref, cw_ref, cb_ref, ng_ref,
     mur_ref, muk_ref, muv_ref, muwa_ref, mug_ref,
     w0_ref, wup_ref, a0_ref, aup_ref, gup_ref, kkp_ref, ka_ref, rk_ref, gng_ref, gnb_ref,
     wout_ref, out_ref,
     qp_scr, kp_scr, c_scr, n_scr, m_scr,
     cr_scr, ck_scr, cv_scr, cwa_scr, cg_scr, st_scr, p_scr) = refs
    step = pl.program_id(0) * pl.num_programs(1) + pl.program_id(1)
    chunks = [slice(c * tn, (c + 1) * tn) for c in range(win_ref.shape[1] // tn)]

    def project(h, dst_slot, cols_list):
        for cols in cols_list:
            p_scr[dst_slot, :, cols] = _dot(h, win_ref[:, cols])

    @pl.when(step == 0)
    def _():
        project(_rms_bf16(x_ref[0], gin_ref[...]), 0, chunks)

    @pl.when(pl.program_id(1) == 0)
    def _():
        for scr in (qp_scr, kp_scr, c_scr, n_scr, m_scr,
                    cr_scr, ck_scr, cv_scr, cwa_scr, cg_scr, st_scr):
            scr[...] = jnp.zeros_like(scr)

    slot = lax.rem(step, 2)
    view = lambda off, width: p_scr.at[slot, :, pl.ds(off, width)]
    y_m, y_r = [], []
    mlstm = _mlstm_stages(view(OFF_MQ, M_WIDTH), view(OFF_MK, M_WIDTH), view(OFF_MV, M_WIDTH),
                          view(OFF_MO, M_WIDTH), view(OFF_MG, GATE_PAD), brow_ref, bcol_ref,
                          cw_ref, cb_ref, ng_ref, qp_scr, kp_scr, c_scr, n_scr, m_scr, y_m)
    rwkv = _rwkv_stages(view(OFF_RR, R_WIDTH), view(OFF_RK, R_WIDTH), view(OFF_RV, R_WIDTH),
                        view(OFF_RWA, LANES), view(OFF_RG, GLORA_PAD),
                        mur_ref, muk_ref, muv_ref, muwa_ref, mug_ref,
                        w0_ref, wup_ref, a0_ref, aup_ref, gup_ref,
                        kkp_ref, ka_ref, rk_ref, gng_ref, gnb_ref,
                        cr_scr, ck_scr, cv_scr, cwa_scr, cg_scr, st_scr, y_r)
    def project_next():
        h_next = _rms_bf16(xn_ref[0], gin_ref[...])
        for cols in chunks:
            project(h_next, 1 - slot, [cols])
            yield

    proj = project_next()
    streams = {"R": rwkv, "M": mlstm, "P": proj}
    for tag in "PRPRPRPRR" + "RMR" * 9 + "PRMPRMPRMPRM":
        next(streams[tag], None)
    for stream in (proj, mlstm, rwkv):
        for _ in stream:
            pass
    out_ref[0] = (x_ref[0] + _dot(y_m[0], wout_ref[:M_WIDTH, :])
                  + _dot(y_r[0], wout_ref[M_WIDTH:, :]))


def _mixer(x, w_in, gain, b_row, b_col, conv_w, conv_b, norm_g,
           mu, w0, w_up, a0, a_up, g_up, kkp, ka, rk, gn_g, gn_b, w_out, layer, tn=512):
    B, S, D = x.shape
    ts = min(SEQ_TILE, S)
    n_seq = S // ts
    last = B * n_seq - 1
    npair = R_WIDTH // LANES
    vec = lambda w, off: pl.BlockSpec((1, w), lambda b, s, off=off: (0, off))
    full = lambda a: pl.BlockSpec(a.shape, lambda b, s: (0,) * a.ndim)

    def next_tile(b, s):
        t = jnp.minimum(b * n_seq + s + 1, last)
        return (t // n_seq, t % n_seq, 0)

    return pl.pallas_call(
        functools.partial(_mixer_kernel, tn=tn),
        out_shape=jax.ShapeDtypeStruct((B, S, D), F32),
        grid=(B, n_seq),
        in_specs=[
            pl.BlockSpec((1, ts, D), lambda b, s: (b, s, 0)),
            pl.BlockSpec((1, ts, D), next_tile),
            _layer_weight_spec(w_in, layer), full(gain),
            full(b_row), full(b_col), full(conv_w), full(conv_b), full(norm_g),
            vec(R_WIDTH, OFF_RR // R_WIDTH), vec(R_WIDTH, OFF_RK // R_WIDTH),
            vec(R_WIDTH, OFF_RV // R_WIDTH), vec(LANES, OFF_RWA // LANES),
            vec(GLORA_PAD, OFF_RG // GLORA_PAD),
            full(w0), full(w_up), full(a0), full(a_up), full(g_up),
            full(kkp), full(ka), full(rk), full(gn_g), full(gn_b),
            _layer_weight_spec(w_out, layer),
        ],
        out_specs=pl.BlockSpec((1, ts, D), lambda b, s: (b, s, 0)),
        scratch_shapes=[pltpu.VMEM((SUBLANES, M_WIDTH), F32), pltpu.VMEM((SUBLANES, M_WIDTH), F32),
                        pltpu.VMEM((M_HEADS, M_HDIM, M_HDIM), F32),
                        pltpu.VMEM((M_HEADS, 1, M_HDIM), F32),
                        pltpu.VMEM((M_HEADS, 1, 1), F32),
                        pltpu.VMEM((SUBLANES, R_WIDTH), F32), pltpu.VMEM((SUBLANES, R_WIDTH), F32),
                        pltpu.VMEM((SUBLANES, R_WIDTH), F32), pltpu.VMEM((SUBLANES, LANES), F32),
                        pltpu.VMEM((SUBLANES, GLORA_PAD), F32),
                        pltpu.VMEM((npair, LANES, LANES), F32),
                        pltpu.VMEM((2, ts, IN_COLS_P), F32)],
        compiler_params=pltpu.CompilerParams(
            dimension_semantics=("arbitrary", "arbitrary"), vmem_limit_bytes=VMEM_LIMIT),
        name="mixer",
    )(x, x, w_in, gain, b_row, b_col, conv_w, conv_b, norm_g, mu, mu, mu, mu, mu,
      w0, w_up, a0, a_up, g_up, kkp, ka, rk, gn_g, gn_b, w_out)


def _xattn_kernel(x_ref, g_ref, wq_ref, kv_ref, wo_ref, o_ref):
    D = x_ref.shape[1]
    x = x_ref[...]
    q = _dot(_rms_bf16(x, g_ref[...]), wq_ref[...]).astype(BF16)
    hsl = [slice(hd * X_HDIM, (hd + 1) * X_HDIM) for hd in range(D // X_HDIM)]
    s = [lax.dot_general(q[:, ls], kv_ref[0, :, ls], (((1,), (1,)), ((), ())),
                         preferred_element_type=F32) * (X_HDIM ** -0.5) for ls in hsl]
    e = [jnp.exp(si - jnp.max(si, axis=-1, keepdims=True)) for si in s]
    p = [(ei / jnp.sum(ei, axis=-1, keepdims=True)).astype(BF16) for ei in e]
    heads = [_dot(pi, kv_ref[0, :, D + ls.start:D + ls.stop]).astype(BF16)
             for pi, ls in zip(p, hsl)]
    o_ref[...] = x + _dot(jnp.concatenate(heads, axis=1), wo_ref[...])


def _xattn(x, gain, wq, kv, wo, layer, seq_len, tm=1024):
    T, D = x.shape
    M = kv.shape[1]
    tm = min(tm, seq_len)
    per_seq = seq_len // tm
    return pl.pallas_call(
        _xattn_kernel,
        out_shape=jax.ShapeDtypeStruct((T, D), F32),
        grid=(T // tm,),
        in_specs=[pl.BlockSpec((tm, D), lambda i: (i, 0)),
                  pl.BlockSpec((1, D), lambda i: (0, 0)),
                  _layer_weight_spec(wq, layer),
                  pl.BlockSpec((1, M, 2 * D), lambda i: (i // per_seq, 0, 0)),
                  _layer_weight_spec(wo, layer)],
        out_specs=pl.BlockSpec((tm, D), lambda i: (i, 0)),
        compiler_params=pltpu.CompilerParams(
            dimension_semantics=("parallel",), vmem_limit_bytes=VMEM_LIMIT),
        name="xattn",
    )(x, gain.reshape(1, D), wq, kv, wo)


def _ffn_kernel(*refs, per_seq, tc, norm_out):
    x_ref, g_ref, wup_ref, cw_ref, cb_ref, wdn_ref = refs[:6]
    gout_ref = refs[6] if norm_out else None
    o_ref, tail_scr = refs[-2:]
    tm = x_ref.shape[0]

    @pl.when(lax.rem(pl.program_id(0), per_seq) == 0)
    def _():
        tail_scr[...] = jnp.zeros_like(tail_scr)

    x = x_ref[...]
    h = _rms_bf16(x, g_ref[...])
    chunks = [slice(c * tc, (c + 1) * tc) for c in range(D_FF // tc)]
    gates = [_dot(h, wup_ref[:, cols]) for cols in chunks]
    vals = [_dot(h, wup_ref[:, D_FF + cols.start:D_FF + cols.stop]) for cols in chunks]
    acts = []
    for cols, gate, val in zip(chunks, gates, vals):
        prev = tail_scr[:, cols]
        y = cb_ref[:, cols] + gate * cw_ref[FFN_CONV - 1:FFN_CONV, cols]
        for j in range(FFN_CONV - 1):
            y = y + _shift_rows_carry(gate, prev, FFN_CONV - 1 - j) * cw_ref[j:j + 1, cols]
        tail_scr[:, cols] = gate[tm - SUBLANES:]
        acts.append((y * _sigmoid(y) * val).astype(BF16))
    acc = x
    for cols, act in zip(chunks, acts):
        acc = acc + _dot(act, wdn_ref[cols, :])
    if norm_out:
        ms = jnp.mean(acc * acc, axis=-1, keepdims=True)
        acc = acc * lax.rsqrt(ms + NORM_EPS) * gout_ref[...]
    o_ref[...] = acc


def _ffn(x, gain, w_up, conv_w, conv_b, w_down, layer, seq_len, out_gain=None, tm=512, tc=256):
    T, D = x.shape
    tm = min(tm, seq_len)
    norm_out = out_gain is not None
    const = lambda a: pl.BlockSpec(a.shape, lambda i: (0,) * a.ndim, pipeline_mode=pl.Buffered(1))
    small = [conv_w, conv_b.reshape(1, D_FF)] + ([out_gain.reshape(1, D)] if norm_out else [])
    specs = [const(a) for a in small]
    return pl.pallas_call(
        functools.partial(_ffn_kernel, per_seq=seq_len // tm, tc=tc, norm_out=norm_out),
        out_shape=jax.ShapeDtypeStruct((T, D), F32),
        grid=(T // tm,),
        in_specs=[pl.BlockSpec((tm, D), lambda i: (i, 0)),
                  pl.BlockSpec((1, D), lambda i: (0, 0)),
                  _layer_weight_spec(w_up, layer), specs[0], specs[1],
                  _layer_weight_spec(w_down, layer)] + specs[2:],
        out_specs=pl.BlockSpec((tm, D), lambda i: (i, 0)),
        scratch_shapes=[pltpu.VMEM((SUBLANES, D_FF), F32)],
        compiler_params=pltpu.CompilerParams(
            dimension_semantics=("arbitrary",), vmem_limit_bytes=VMEM_LIMIT),
        name="ffn",
    )(x, gain.reshape(1, D), w_up, small[0], small[1], w_down, *small[2:])


def _pad_cols(w, n):
    return jnp.pad(w, [(0, 0)] * (w.ndim - 1) + [(0, n - w.shape[-1])])


def _relayout_in(w):
    m_main = w[..., :4 * M_WIDTH]
    m_gate = w[..., 4 * M_WIDTH:4 * M_WIDTH + 2 * M_HEADS]
    r0 = 4 * M_WIDTH + 2 * M_HEADS
    r_main = w[..., r0:r0 + 3 * R_WIDTH + R_DECAY_LORA + R_AAA_LORA]
    r_gate = w[..., r0 + 3 * R_WIDTH + R_DECAY_LORA + R_AAA_LORA:]
    return jnp.concatenate(
        [m_main, r_main, _pad_cols(m_gate, GATE_PAD), _pad_cols(r_gate, GLORA_PAD)], axis=-1)


def kernel(x, mem, norm_mix, w_in, m_conv_w, m_conv_b, m_gate_b, m_norm_g, r_mu, r_w0,
           r_w_up, r_a0, r_a_up, r_g_up, r_kk, r_ka, r_rk, r_gn_g, r_gn_b, w_out,
           norm_x, norm_mem, x_wq, x_wkv, x_wo, norm_ffn, f_up, f_conv_w, f_conv_b,
           f_down, norm_final):
    B, S, D = x.shape
    M = mem.shape[1]
    depth = w_in.shape[0]
    T = B * S
    xf = x.reshape(T, D)
    memf = mem.reshape(B * M, D)
    row = lambda a: a.reshape(1, -1)
    w_in_p, w_out_b = _relayout_in(w_in).astype(BF16), w_out.astype(BF16)
    wq_b, wkv_b, wo_b = x_wq.astype(BF16), x_wkv.astype(BF16), x_wo.astype(BF16)
    f_up_b, f_down_b = f_up.astype(BF16), f_down.astype(BF16)

    for l in range(depth):
        mu = _relayout_in(jnp.pad(row(r_mu[l]), ((0, 0), (4 * M_WIDTH + 2 * M_HEADS, 0))))
        w_up = jnp.pad(r_w_up[l], ((0, R_AAA_LORA), (0, 0))).astype(BF16)
        a_up = jnp.pad(r_a_up[l], ((R_DECAY_LORA, 0), (0, 0))).astype(BF16)
        g_up = jnp.pad(r_g_up[l], ((0, GLORA_PAD - R_GATE_LORA), (0, 0))).astype(BF16)
        xf = _mixer(xf.reshape(B, S, D), w_in_p, row(norm_mix[l]),
                    m_gate_b[l].reshape(-1, 1, 1), row(m_gate_b[l]),
                    m_conv_w[l], row(m_conv_b[l]), row(m_norm_g[l]),
                    mu, row(r_w0[l]), w_up, row(r_a0[l]), a_up, g_up,
                    row(r_kk[l]), row(r_ka[l]), row(r_rk[l]), row(r_gn_g[l]), row(r_gn_b[l]),
                    w_out_b, l).reshape(T, D)

        kv = _mm(memf, wkv_b, l, gain=norm_mem[l], out_dtype=BF16)
        xf = _xattn(xf, norm_x[l], wq_b, kv.reshape(B, M, 2 * D), wo_b, l, S)

        xf = _ffn(xf, norm_ffn[l], f_up_b, f_conv_w[l], f_conv_b[l], f_down_b, l, S,
                  out_gain=norm_final if l == depth - 1 else None)

    return xf.reshape(B, S, D)
```

```python
import functools
import math

import jax
import jax.numpy as jnp
from jax import lax
from jax.experimental import pallas as pl
from jax.experimental.pallas import tpu as pltpu

F32 = jnp.float32
BF16 = jnp.bfloat16

M_WIDTH = 512
M_HEADS = 4
M_HDIM = 128
M_CONV = 4
R_WIDTH = 512
R_HDIM = 64
R_DECAY_LORA = 64
R_AAA_LORA = 64
R_GATE_LORA = 160
DECAY_SCALE = math.exp(-0.5)
X_HDIM = 256
D_FF = 2816
FFN_CONV = 3
NORM_EPS = 1e-6
GN_EPS = 64e-5
CHUNK = 64

LANES = 128
SUBLANES = 8
GATE_PAD = LANES
GLORA_PAD = 2 * LANES
OFF_MQ, OFF_MK, OFF_MV, OFF_MO = 0, 512, 1024, 1536
OFF_RR = 2048
OFF_RK = OFF_RR + R_WIDTH
OFF_RV = OFF_RK + R_WIDTH
OFF_RWA = OFF_RV + R_WIDTH
OFF_MG = OFF_RWA + LANES
OFF_RG = OFF_MG + GATE_PAD
IN_COLS_P = OFF_RG + GLORA_PAD

SEQ_TILE = 256
VMEM_LIMIT = 48 * 1024 * 1024


def _dot(a, b):
    return jnp.dot(a, b, preferred_element_type=F32)


def _bdot(a, b):
    return jnp.dot(a.astype(BF16), b.astype(BF16), preferred_element_type=F32)


def _bdot_nt(a, b):
    return lax.dot_general(a.astype(BF16), b.astype(BF16), (((1,), (1,)), ((), ())),
                           preferred_element_type=F32)


def _bdot_tn(a, b):
    return lax.dot_general(a.astype(BF16), b.astype(BF16), (((0,), (0,)), ((), ())),
                           preferred_element_type=F32)


def _split2(x):
    hi = x.astype(BF16)
    lo = (x - hi.astype(F32)).astype(BF16)
    return hi, lo


def _sigmoid(x):
    return 0.5 * jnp.tanh(0.5 * x) + 0.5


def _shift_rows_carry(x, prev, sh):
    r = pltpu.roll(x, sh, 0)
    row = lax.broadcasted_iota(jnp.int32, prev.shape, 0)
    head = jnp.where(row < sh, pltpu.roll(prev, sh, 0), r[:SUBLANES])
    return jnp.concatenate([head, r[SUBLANES:]], axis=0)


def _rms_bf16(x, gain):
    ms = jnp.mean(x * x, axis=-1, keepdims=True)
    return (x * lax.rsqrt(ms + NORM_EPS) * gain).astype(BF16)


def _layer_weight_spec(w, layer):
    return pl.BlockSpec((None,) + w.shape[1:], lambda *_: (layer,) + (0,) * (w.ndim - 1),
                        pipeline_mode=pl.Buffered(1))


def _mm_kernel(x_ref, w_ref, g_ref, o_ref, h_scr, *, tn):
    h_scr[...] = _rms_bf16(x_ref[...], g_ref[...])
    for c in range(w_ref.shape[1] // tn):
        cols = slice(c * tn, (c + 1) * tn)
        o_ref[:, cols] = _dot(h_scr[...], w_ref[:, cols]).astype(o_ref.dtype)


def _mm_tile_rows(K, N, x_bytes, out_bytes):
    budget = (VMEM_LIMIT * 3) // 4 - K * N * 2
    for tm in (512, 256, 128):
        if tm * (2 * K * x_bytes + 2 * N * out_bytes + K * 2) <= budget:
            return tm
    raise ValueError("weight does not fit in VMEM")


def _mm(x, w, layer, gain, out_dtype=F32, tn=512):
    T, K = x.shape
    N = w.shape[2]
    tm = min(T, _mm_tile_rows(K, N, x.dtype.itemsize, jnp.dtype(out_dtype).itemsize))
    tn = min(tn, N)
    assert T % tm == 0 and N % tn == 0
    return pl.pallas_call(
        functools.partial(_mm_kernel, tn=tn),
        out_shape=jax.ShapeDtypeStruct((T, N), out_dtype),
        grid=(T // tm,),
        in_specs=[pl.BlockSpec((tm, K), lambda i: (i, 0)),
                  _layer_weight_spec(w, layer),
                  pl.BlockSpec((1, K), lambda i: (0, 0))],
        out_specs=pl.BlockSpec((tm, N), lambda i: (i, 0)),
        scratch_shapes=[pltpu.VMEM((tm, K), BF16)],
        compiler_params=pltpu.CompilerParams(
            dimension_semantics=("parallel",), vmem_limit_bytes=VMEM_LIMIT),
        name="mm",
    )(x, w, gain.reshape(1, K))


def _mlstm_stages(q_ref, k_ref, v_ref, o_ref, g_ref, brow_ref, bcol_ref,
                  cw_ref, cb_ref, ng_ref, qp_scr, kp_scr, c_scr, n_scr, m_scr, result):
    TS = q_ref.shape[0]
    L = CHUNK
    nch = TS // L
    H = M_HEADS
    G = nch * H
    hs = range(H)

    def conv_silu(x, prev, w, b):
        y = b + x * w[M_CONV - 1:M_CONV, :]
        for j in range(M_CONV - 1):
            y = y + _shift_rows_carry(x, prev, M_CONV - 1 - j) * w[j:j + 1, :]
        return y * _sigmoid(y)

    def groups(x):
        return jnp.stack([x[c * L:(c + 1) * L, h * M_HDIM:(h + 1) * M_HDIM]
                          for c in range(nch) for h in hs])

    def per_group(f, *xs):
        return jnp.stack([f(*[x[i] for x in xs]) for i in range(G)])

    def log_sigmoid(x):
        return jnp.minimum(x, 0.0) - jnp.log1p(jnp.exp(-jnp.abs(x)))

    q_raw = q_ref[...]
    q = groups(conv_silu(q_raw, qp_scr[...], cw_ref[:, :M_WIDTH], cb_ref[:, :M_WIDTH])
               * (M_HDIM ** -0.5))
    qp_scr[...] = q_raw[TS - SUBLANES:]
    yield
    k_raw = k_ref[...]
    k = groups(conv_silu(k_raw, kp_scr[...], cw_ref[:, M_WIDTH:], cb_ref[:, M_WIDTH:]))
    kp_scr[...] = k_raw[TS - SUBLANES:]
    v = groups(v_ref[...])
    yield

    ti = lax.broadcasted_iota(jnp.int32, (L, L), 0)
    si = lax.broadcasted_iota(jnp.int32, (L, L), 1)
    causal = si <= ti
    diag = ti == si
    pg = g_ref[...]
    zpad = jnp.zeros((GATE_PAD - L, GATE_PAD), F32)
    pg_t = [jnp.concatenate([pg[c * L:(c + 1) * L], zpad], axis=0).T for c in range(nch)]
    logi_r = jnp.stack([pg_t[c][h:h + 1, :L] + brow_ref[h]
                        for c in range(nch) for h in hs])
    logf_r = log_sigmoid(jnp.stack([pg_t[c][H + h:H + h + 1, :L] + brow_ref[H + h]
                                    for c in range(nch) for h in hs]))
    gc = pg[:, :2 * H] + bcol_ref[...]
    logi_c = jnp.stack([gc[c * L:(c + 1) * L, h:h + 1] for c in range(nch) for h in hs])
    yield
    b_c = jnp.sum(jnp.where(causal, logf_r, 0.0), axis=2, keepdims=True)
    b_r = jnp.sum(jnp.where(diag, b_c, 0.0), axis=1, keepdims=True)
    g = jnp.sum(logf_r, axis=2, keepdims=True)
    a_r = g - b_r + logi_r
    a_c = g - b_c + logi_c
    m_loc = jnp.max(a_r, axis=2, keepdims=True)
    wa_c = jnp.exp(a_c - m_loc)
    yield
    c_loc = per_group(_bdot_tn, v * wa_c, k)
    yield
    n_loc = jnp.sum(k * wa_c, axis=1, keepdims=True)
    d = jnp.where(causal, b_c - b_r + logi_r, -jnp.inf)
    d_max = jnp.max(d, axis=2, keepdims=True)
    yield
    qk = per_group(_bdot_nt, q, k)
    yield

    c_prev = c_scr[...]
    n_prev = n_scr[...]
    m_prev = m_scr[...]
    c_in, n_in, m_in = [], [], []
    for c in range(nch):
        gs = slice(c * H, (c + 1) * H)
        c_in.append(c_prev)
        n_in.append(n_prev)
        m_in.append(m_prev)
        m_new = jnp.maximum(g[gs] + m_prev, m_loc[gs])
        s_old = jnp.exp(g[gs] + m_prev - m_new)
        s_loc = jnp.exp(m_loc[gs] - m_new)
        c_prev = s_old * c_prev + s_loc * c_loc[gs]
        n_prev = s_old * n_prev + s_loc * n_loc[gs]
        m_prev = m_new
    c_scr[...] = c_prev
    n_scr[...] = n_prev
    m_scr[...] = m_prev
    c_in = jnp.concatenate(c_in, axis=0)
    n_in = jnp.concatenate(n_in, axis=0)
    m_in = jnp.concatenate(m_in, axis=0)
    yield

    inter = b_c + m_in
    m_t = jnp.maximum(inter, d_max)
    s_int = jnp.exp(inter - m_t)
    p = jnp.exp(d - m_t) * qk
    yield
    c_aug = jnp.concatenate([c_in, jnp.broadcast_to(n_in, c_in.shape)], axis=1)
    v_aug = jnp.concatenate([v, jnp.ones_like(v)], axis=2)
    qc = per_group(_bdot_nt, q, c_aug)
    pv = per_group(_bdot, p, v_aug)
    num = s_int * qc[:, :, :M_HDIM] + pv[:, :, :M_HDIM]
    yield
    den = s_int * qc[:, :, M_HDIM:M_HDIM + 1] + pv[:, :, M_HDIM:M_HDIM + 1]
    hh = num / jnp.maximum(jnp.abs(den), jnp.exp(-m_t))
    yield
    mu = jnp.mean(hh, axis=-1, keepdims=True)
    hc = hh - mu
    var = jnp.mean(hc * hc, axis=-1, keepdims=True)
    ng = jnp.stack([ng_ref[:, h * M_HDIM:(h + 1) * M_HDIM] for h in hs] * nch)
    y = (_sigmoid(groups(o_ref[...])) * (hc * lax.rsqrt(var + NORM_EPS)) * ng).astype(BF16)
    result.append(jnp.concatenate(
        [jnp.concatenate([y[c * H + h] for h in hs], axis=1) for c in range(nch)], axis=0))


def _rwkv_stages(pr_ref, pk_ref, pv_ref, pwa_ref, pg_ref,
                 mur_ref, muk_ref, muv_ref, muwa_ref, mug_ref,
                 w0_ref, wup_ref, a0_ref, aup_ref, gup_ref,
                 kkp_ref, ka_ref, rk_ref, gng_ref, gnb_ref,
                 cr_scr, ck_scr, cv_scr, cwa_scr, cg_scr, st_scr, result):
    TS = pr_ref.shape[0]
    L = CHUNK
    N = R_HDIM
    nch = TS // L
    npair = R_WIDTH // LANES
    pairs = range(npair)
    Q = 2 * LANES

    def tshift(p_ref, mu_ref, c_scr):
        p = p_ref[...]
        prev = _shift_rows_carry(p, c_scr[...], 1)
        c_scr[...] = p[TS - SUBLANES:]
        return p + (prev - p) * mu_ref[...]

    hsh = N.bit_length() - 1
    li = lax.broadcasted_iota(jnp.int32, (LANES, LANES), 0)
    lj = lax.broadcasted_iota(jnp.int32, (LANES, LANES), 1)
    same_head = jnp.right_shift(li, hsh) == jnp.right_shift(lj, hsh)
    ones_bd = jnp.where(same_head, 1.0, 0.0).astype(BF16)

    def seg_sum(x):
        outs = []
        for p in range(x.shape[1] // LANES):
            hi, lo = _split2(x[:, p * LANES:(p + 1) * LANES])
            outs.append(_dot(hi, ones_bd) + _dot(lo, ones_bd))
        return jnp.concatenate(outs, axis=1)

    rr = tshift(pr_ref, mur_ref, cr_scr)
    kr = tshift(pk_ref, muk_ref, ck_scr)
    vr = tshift(pv_ref, muv_ref, cv_scr)
    yield
    wa = tshift(pwa_ref, muwa_ref, cwa_scr)
    gd = tshift(pg_ref, mug_ref, cg_scr)
    logw = -DECAY_SCALE * _sigmoid(w0_ref[...] + _bdot(jnp.tanh(wa), wup_ref[...]))
    a = _sigmoid(a0_ref[...] + _bdot(wa, aup_ref[...]))
    g = _bdot(_sigmoid(gd), gup_ref[...])
    yield
    kkraw = kr * kkp_ref[...]
    kk = kkraw * lax.rsqrt(jnp.maximum(seg_sum(kkraw * kkraw), 1e-24))
    km = kr * (1.0 + (a - 1.0) * ka_ref[...])
    be = kk * a
    yield
    tq = lax.broadcasted_iota(jnp.int32, (Q, Q), 0)
    sq = lax.broadcasted_iota(jnp.int32, (Q, Q), 1)
    tril = jnp.where((jnp.right_shift(tq, hsh) == jnp.right_shift(sq, hsh)) & (sq <= tq),
                     1.0, 0.0).astype(BF16)
    lw_hi, lw_lo = _split2(logw)
    bincl = jnp.concatenate(
        [_dot(tril, lw_hi[q * Q:(q + 1) * Q]) + _dot(tril, lw_lo[q * Q:(q + 1) * Q])
         for q in range(TS // Q)], axis=0)
    e_in = jnp.exp(bincl)
    e_ng = jnp.exp(-bincl)
    kt = kk * jnp.exp(bincl - logw)
    rt = rr * e_in
    bh = be * e_ng
    kh = km * e_ng
    yield

    h0 = lax.broadcasted_iota(jnp.int32, (L, LANES), 1) < N
    ti = lax.broadcasted_iota(jnp.int32, (2 * L, 2 * L), 0)
    si = lax.broadcasted_iota(jnp.int32, (2 * L, 2 * L), 1)
    same_blk = jnp.right_shift(ti, hsh) == jnp.right_shift(si, hsh)
    strict = same_blk & (si < ti)
    incl = same_blk & (si <= ti)

    def stack_heads(x):
        return jnp.concatenate([jnp.where(h0, x, 0.0), jnp.where(h0, 0.0, x)], axis=0)

    def stack_dup(x):
        return jnp.concatenate([x, x], axis=0)

    def comb(x):
        return jnp.where(h0, x[:L], x[L:])

    def off_mask(b):
        sh = (2 * b).bit_length() - 1
        same = jnp.right_shift(ti, sh) == jnp.right_shift(si, sh)
        return same & (jnp.bitwise_and(ti, b) != 0) & (jnp.bitwise_and(si, b) == 0)

    probs = [(p, c) for p in pairs for c in range(nch)]

    def tile(x, p, c):
        return x[c * L:(c + 1) * L, p * LANES:(p + 1) * LANES]

    lk = [stack_heads(tile(kt, p, c)).astype(BF16) for p, c in probs]
    lr = [stack_heads(tile(rt, p, c)).astype(BF16) for p, c in probs]
    rbk = [jnp.concatenate([stack_dup(tile(bh, p, c)), stack_dup(tile(kh, p, c))],
                           axis=0).astype(BF16) for p, c in probs]
    vs = [stack_dup(tile(vr, p, c)).astype(BF16) for p, c in probs]
    yield
    gk = [_bdot_nt(x, y) for x, y in zip(lk, rbk)]
    yield
    a_bd = [jnp.where(strict, x[:, :2 * L], 0.0) for x in gk]
    bk_bd = [jnp.where(strict, x[:, 2 * L:], 0.0).astype(BF16) for x in gk]
    yield
    gr = [_bdot_nt(x, y) for x, y in zip(lr, rbk)]
    yield
    arb_bd = [jnp.where(incl, x[:, :2 * L], 0.0).astype(BF16) for x in gr]
    ark_bd = [jnp.where(incl, x[:, 2 * L:], 0.0).astype(BF16) for x in gr]
    yield
    eye = jnp.where(ti == si, 1.0, 0.0)
    m1 = off_mask(1)
    t_bd = [eye - jnp.where(m1, x, 0.0) for x in a_bd]
    b = 2
    while b < L:
        mb = off_mask(b)
        offs = [jnp.where(mb, x, 0.0).astype(BF16) for x in a_bd]
        xs = [_bdot(i, o) for i, o in zip(t_bd, offs)]
        yield
        ys = [_bdot(x, i) for x, i in zip(xs, t_bd)]
        t_bd = [i - y for i, y in zip(t_bd, ys)]
        yield
        b *= 2
    bkv = [comb(_bdot(x, y)) for x, y in zip(bk_bd, vs)]
    yield
    xs = [jnp.concatenate([stack_dup(tile(kt, p, c)), stack_dup(z)], axis=1)
          for (p, c), z in zip(probs, bkv)]
    tx = [_bdot(x, y) for x, y in zip(t_bd, xs)]
    yield
    arkv = [comb(_bdot(x, y)) for x, y in zip(ark_bd, vs)]
    w_ch = [[None] * nch for _ in pairs]
    u0_ch = [[None] * nch for _ in pairs]
    arkv_ch = [[None] * nch for _ in pairs]
    arb_ch = [[None] * nch for _ in pairs]
    for i, (p, c) in enumerate(probs):
        w_ch[p][c] = comb(tx[i][:, :LANES])
        u0_ch[p][c] = -comb(tx[i][:, LANES:])
        arkv_ch[p][c] = arkv[i]
        arb_ch[p][c] = arb_bd[i]
    yield

    y_rows = []
    st = [st_scr[p] for p in pairs]
    ones_f = jnp.where(same_head, 1.0, 0.0)
    lsl = [slice(p * LANES, (p + 1) * LANES) for p in pairs]
    for c in range(nch):
        rs = slice(c * L, (c + 1) * L)
        p_end = e_in[c * L + L - 1:c * L + L, :]
        pe = [p_end[:, ls] for ls in lsl]
        rw = [_bdot_nt(jnp.concatenate([rt[rs, lsl[p]], w_ch[p][c]], axis=0), st[p])
              for p in pairs]
        u = [u0_ch[p][c] - rw[p][L:] for p in pairs]
        au = [_bdot(arb_ch[p][c], jnp.concatenate([u[p], u[p]], axis=0)) for p in pairs]
        upd = [_bdot_tn(jnp.concatenate([u[p], vr[rs, lsl[p]]], axis=0),
                        jnp.concatenate([bh[rs, lsl[p]] * pe[p], kh[rs, lsl[p]] * pe[p]], axis=0))
               for p in pairs]
        st = [st[p] * pe[p] + upd[p] * ones_f for p in pairs]
        y_rows.append(jnp.concatenate(
            [rw[p][:L] + jnp.where(h0, au[p][:L], au[p][L:]) + arkv_ch[p][c] for p in pairs],
            axis=1))
        yield
    for p in pairs:
        st_scr[p] = st[p]
    y = jnp.concatenate(y_rows, axis=0)

    inv_n = 1.0 / N
    mu = seg_sum(y) * inv_n
    yc = y - mu
    var = seg_sum(yc * yc) * inv_n
    yn = yc * lax.rsqrt(var + GN_EPS) * gng_ref[...] + gnb_ref[...]
    bonus = seg_sum(rr * km * rk_ref[...]) * vr
    result.append(((yn + bonus) * g).astype(BF16))


def _mixer_kernel(*refs, tn):
    (x_ref, xn_ref, win_ref, gin_ref, brow_ref, bcol_ref, cw_ref, cb_ref, ng_ref,
     mur_ref, muk_ref, muv_ref, muwa_ref, mug_ref,
     w0_ref, wup_ref, a0_ref, aup_ref, gup_ref, kkp_ref, ka_ref, rk_ref, gng_ref, gnb_ref,
     wout_ref, out_ref,
     qp_scr, kp_scr, c_scr, n_scr, m_scr,
     cr_scr, ck_scr, cv_scr, cwa_scr, cg_scr, st_scr, p_scr) = refs
    step = pl.program_id(0) * pl.num_programs(1) + pl.program_id(1)
    chunks = [slice(c * tn, (c + 1) * tn) for c in range(win_ref.shape[1] // tn)]

    def project(h, dst_slot, cols_list):
        for cols in cols_list:
            p_scr[dst_slot, :, cols] = _dot(h, win_ref[:, cols])

    @pl.when(step == 0)
    def _():
        project(_rms_bf16(x_ref[0], gin_ref[...]), 0, chunks)

    @pl.when(pl.program_id(1) == 0)
    def _():
        for scr in (qp_scr, kp_scr, c_scr, n_scr, m_scr,
                    cr_scr, ck_scr, cv_scr, cwa_scr, cg_scr, st_scr):
            scr[...] = jnp.zeros_like(scr)

    slot = lax.rem(step, 2)
    view = lambda off, width: p_scr.at[slot, :, pl.ds(off, width)]
    y_m, y_r = [], []
    mlstm = _mlstm_stages(view(OFF_MQ, M_WIDTH), view(OFF_MK, M_WIDTH), view(OFF_MV, M_WIDTH),
                          view(OFF_MO, M_WIDTH), view(OFF_MG, GATE_PAD), brow_ref, bcol_ref,
                          cw_ref, cb_ref, ng_ref, qp_scr, kp_scr, c_scr, n_scr, m_scr, y_m)
    rwkv = _rwkv_stages(view(OFF_RR, R_WIDTH), view(OFF_RK, R_WIDTH), view(OFF_RV, R_WIDTH),
                        view(OFF_RWA, LANES), view(OFF_RG, GLORA_PAD),
                        mur_ref, muk_ref, muv_ref, muwa_ref, mug_ref,
                        w0_ref, wup_ref, a0_ref, aup_ref, gup_ref,
                        kkp_ref, ka_ref, rk_ref, gng_ref, gnb_ref,
                        cr_scr, ck_scr, cv_scr, cwa_scr, cg_scr, st_scr, y_r)
    def project_next():
        h_next = _rms_bf16(xn_ref[0], gin_ref[...])
        for cols in chunks:
            project(h_next, 1 - slot, [cols])
            yield

    proj = project_next()
    streams = {"R": rwkv, "M": mlstm, "P": proj}
    for tag in "PRPRPRPRR" + "RMR" * 9 + "PRMPRMPRMPRM":
        next(streams[tag], None)
    for stream in (proj, mlstm, rwkv):
        for _ in stream:
            pass
    out_ref[0] = (x_ref[0] + _dot(y_m[0], wout_ref[:M_WIDTH, :])
                  + _dot(y_r[0], wout_ref[M_WIDTH:, :]))


def _mixer(x, w_in, gain, b_row, b_col, conv_w, conv_b, norm_g,
           mu, w0, w_up, a0, a_up, g_up, kkp, ka, rk, gn_g, gn_b, w_out, layer, tn=512):
    B, S, D = x.shape
    ts = min(SEQ_TILE, S)
    n_seq = S // ts
    last = B * n_seq - 1
    npair = R_WIDTH // LANES
    vec = lambda w, off: pl.BlockSpec((1, w), lambda b, s, off=off: (0, off))
    full = lambda a: pl.BlockSpec(a.shape, lambda b, s: (0,) * a.ndim)

    def next_tile(b, s):
        t = jnp.minimum(b * n_seq + s + 1, last)
        return (t // n_seq, t % n_seq, 0)

    return pl.pallas_call(
        functools.partial(_mixer_kernel, tn=tn),
        out_shape=jax.ShapeDtypeStruct((B, S, D), F32),
        grid=(B, n_seq),
        in_specs=[
            pl.BlockSpec((1, ts, D), lambda b, s: (b, s, 0)),
            pl.BlockSpec((1, ts, D), next_tile),
            _layer_weight_spec(w_in, layer), full(gain),
            full(b_row), full(b_col), full(conv_w), full(conv_b), full(norm_g),
            vec(R_WIDTH, OFF_RR // R_WIDTH), vec(R_WIDTH, OFF_RK // R_WIDTH),
            vec(R_WIDTH, OFF_RV // R_WIDTH), vec(LANES, OFF_RWA // LANES),
            vec(GLORA_PAD, OFF_RG // GLORA_PAD),
            full(w0), full(w_up), full(a0), full(a_up), full(g_up),
            full(kkp), full(ka), full(rk), full(gn_g), full(gn_b),
            _layer_weight_spec(w_out, layer),
        ],
        out_specs=pl.BlockSpec((1, ts, D), lambda b, s: (b, s, 0)),
        scratch_shapes=[pltpu.VMEM((SUBLANES, M_WIDTH), F32), pltpu.VMEM((SUBLANES, M_WIDTH), F32),
                        pltpu.VMEM((M_HEADS, M_HDIM, M_HDIM), F32),
                        pltpu.VMEM((M_HEADS, 1, M_HDIM), F32),
                        pltpu.VMEM((M_HEADS, 1, 1), F32),
                        pltpu.VMEM((SUBLANES, R_WIDTH), F32), pltpu.VMEM((SUBLANES, R_WIDTH), F32),
                        pltpu.VMEM((SUBLANES, R_WIDTH), F32), pltpu.VMEM((SUBLANES, LANES), F32),
                        pltpu.VMEM((SUBLANES, GLORA_PAD), F32),
                        pltpu.VMEM((npair, LANES, LANES), F32),
                        pltpu.VMEM((2, ts, IN_COLS_P), F32)],
        compiler_params=pltpu.CompilerParams(
            dimension_semantics=("arbitrary", "arbitrary"), vmem_limit_bytes=VMEM_LIMIT),
        name="mixer",
    )(x, x, w_in, gain, b_row, b_col, conv_w, conv_b, norm_g, mu, mu, mu, mu, mu,
      w0, w_up, a0, a_up, g_up, kkp, ka, rk, gn_g, gn_b, w_out)


def _xattn_kernel(x_ref, g_ref, wq_ref, kv_ref, wo_ref, o_ref):
    D = x_ref.shape[1]
    x = x_ref[...]
    q = _dot(_rms_bf16(x, g_ref[...]), wq_ref[...]).astype(BF16)
    hsl = [slice(hd * X_HDIM, (hd + 1) * X_HDIM) for hd in range(D // X_HDIM)]
    s = [lax.dot_general(q[:, ls], kv_ref[0, :, ls], (((1,), (1,)), ((), ())),
                         preferred_element_type=F32) * (X_HDIM ** -0.5) for ls in hsl]
    e = [jnp.exp(si - jnp.max(si, axis=-1, keepdims=True)) for si in s]
    p = [(ei * (1.0 / jnp.sum(ei, axis=-1, keepdims=True))).astype(BF16) for ei in e]
    heads = [_dot(pi, kv_ref[0, :, D + ls.start:D + ls.stop]).astype(BF16)
             for pi, ls in zip(p, hsl)]
    o_ref[...] = x + _dot(jnp.concatenate(heads, axis=1), wo_ref[...])


def _xattn(x, gain, wq, kv, wo, layer, seq_len, tm=1024):
    T, D = x.shape
    M = kv.shape[1]
    tm = min(tm, seq_len)
    per_seq = seq_len // tm
    return pl.pallas_call(
        _xattn_kernel,
        out_shape=jax.ShapeDtypeStruct((T, D), F32),
        grid=(T // tm,),
        in_specs=[pl.BlockSpec((tm, D), lambda i: (i, 0)),
                  pl.BlockSpec((1, D), lambda i: (0, 0)),
                  _layer_weight_spec(wq, layer),
                  pl.BlockSpec((1, M, 2 * D), lambda i: (i // per_seq, 0, 0)),
                  _layer_weight_spec(wo, layer)],
        out_specs=pl.BlockSpec((tm, D), lambda i: (i, 0)),
        compiler_params=pltpu.CompilerParams(
            dimension_semantics=("parallel",), vmem_limit_bytes=VMEM_LIMIT),
        name="xattn",
    )(x, gain.reshape(1, D), wq, kv, wo)


def _ffn_kernel(*refs, per_seq, tc, norm_out):
    x_ref, g_ref, wup_ref, cw_ref, cb_ref, wdn_ref = refs[:6]
    gout_ref = refs[6] if norm_out else None
    o_ref, tail_scr = refs[-2:]
    tm = x_ref.shape[0]

    @pl.when(lax.rem(pl.program_id(0), per_seq) == 0)
    def _():
        tail_scr[...] = jnp.zeros_like(tail_scr)

    x = x_ref[...]
    h = _rms_bf16(x, g_ref[...])
    chunks = [slice(c * tc, (c + 1) * tc) for c in range(D_FF // tc)]
    gates = [_dot(h, wup_ref[:, cols]) for cols in chunks]
    vals = [_dot(h, wup_ref[:, D_FF + cols.start:D_FF + cols.stop]) for cols in chunks]
    acts = []
    for cols, gate, val in zip(chunks, gates, vals):
        prev = tail_scr[:, cols]
        y = cb_ref[:, cols] + gate * cw_ref[FFN_CONV - 1:FFN_CONV, cols]
        for j in range(FFN_CONV - 1):
            y = y + _shift_rows_carry(gate, prev, FFN_CONV - 1 - j) * cw_ref[j:j + 1, cols]
        tail_scr[:, cols] = gate[tm - SUBLANES:]
        acts.append((y * _sigmoid(y) * val).astype(BF16))
    acc = x
    for cols, act in zip(chunks, acts):
        acc = acc + _dot(act, wdn_ref[cols, :])
    if norm_out:
        ms = jnp.mean(acc * acc, axis=-1, keepdims=True)
        acc = acc * lax.rsqrt(ms + NORM_EPS) * gout_ref[...]
    o_ref[...] = acc


def _ffn(x, gain, w_up, conv_w, conv_b, w_down, layer, seq_len, out_gain=None, tm=512, tc=256):
    T, D = x.shape
    tm = min(tm, seq_len)
    norm_out = out_gain is not None
    const = lambda a: pl.BlockSpec(a.shape, lambda i: (0,) * a.ndim, pipeline_mode=pl.Buffered(1))
    small = [conv_w, conv_b.reshape(1, D_FF)] + ([out_gain.reshape(1, D)] if norm_out else [])
    specs = [const(a) for a in small]
    return pl.pallas_call(
        functools.partial(_ffn_kernel, per_seq=seq_len // tm, tc=tc, norm_out=norm_out),
        out_shape=jax.ShapeDtypeStruct((T, D), F32),
        grid=(T // tm,),
        in_specs=[pl.BlockSpec((tm, D), lambda i: (i, 0)),
                  pl.BlockSpec((1, D), lambda i: (0, 0)),
                  _layer_weight_spec(w_up, layer), specs[0], specs[1],
                  _layer_weight_spec(w_down, layer)] + specs[2:],
        out_specs=pl.BlockSpec((tm, D), lambda i: (i, 0)),
        scratch_shapes=[pltpu.VMEM((SUBLANES, D_FF), F32)],
        compiler_params=pltpu.CompilerParams(
            dimension_semantics=("arbitrary",), vmem_limit_bytes=VMEM_LIMIT),
        name="ffn",
    )(x, gain.reshape(1, D), w_up, small[0], small[1], w_down, *small[2:])


def _pad_cols(w, n):
    return jnp.pad(w, [(0, 0)] * (w.ndim - 1) + [(0, n - w.shape[-1])])


def _relayout_in(w):
    m_main = w[..., :4 * M_WIDTH]
    m_gate = w[..., 4 * M_WIDTH:4 * M_WIDTH + 2 * M_HEADS]
    r0 = 4 * M_WIDTH + 2 * M_HEADS
    r_main = w[..., r0:r0 + 3 * R_WIDTH + R_DECAY_LORA + R_AAA_LORA]
    r_gate = w[..., r0 + 3 * R_WIDTH + R_DECAY_LORA + R_AAA_LORA:]
    return jnp.concatenate(
        [m_main, r_main, _pad_cols(m_gate, GATE_PAD), _pad_cols(r_gate, GLORA_PAD)], axis=-1)


def kernel(x, mem, norm_mix, w_in, m_conv_w, m_conv_b, m_gate_b, m_norm_g, r_mu, r_w0,
           r_w_up, r_a0, r_a_up, r_g_up, r_kk, r_ka, r_rk, r_gn_g, r_gn_b, w_out,
           norm_x, norm_mem, x_wq, x_wkv, x_wo, norm_ffn, f_up, f_conv_w, f_conv_b,
           f_down, norm_final):
    B, S, D = x.shape
    M = mem.shape[1]
    depth = w_in.shape[0]
    T = B * S
    xf = x.reshape(T, D)
    memf = mem.reshape(B * M, D)
    row = lambda a: a.reshape(1, -1)
    w_in_p, w_out_b = _relayout_in(w_in).astype(BF16), w_out.astype(BF16)
    wq_b, wkv_b, wo_b = x_wq.astype(BF16), x_wkv.astype(BF16), x_wo.astype(BF16)
    f_up_b, f_down_b = f_up.astype(BF16), f_down.astype(BF16)

    for l in range(depth):
        mu = _relayout_in(jnp.pad(row(r_mu[l]), ((0, 0), (4 * M_WIDTH + 2 * M_HEADS, 0))))
        w_up = jnp.pad(r_w_up[l], ((0, R_AAA_LORA), (0, 0))).astype(BF16)
        a_up = jnp.pad(r_a_up[l], ((R_DECAY_LORA, 0), (0, 0))).astype(BF16)
        g_up = jnp.pad(r_g_up[l], ((0, GLORA_PAD - R_GATE_LORA), (0, 0))).astype(BF16)
        xf = _mixer(xf.reshape(B, S, D), w_in_p, row(norm_mix[l]),
                    m_gate_b[l].reshape(-1, 1, 1), row(m_gate_b[l]),
                    m_conv_w[l], row(m_conv_b[l]), row(m_norm_g[l]),
                    mu, row(r_w0[l]), w_up, row(r_a0[l]), a_up, g_up,
                    row(r_kk[l]), row(r_ka[l]), row(r_rk[l]), row(r_gn_g[l]), row(r_gn_b[l]),
                    w_out_b, l).reshape(T, D)

        kv = _mm(memf, wkv_b, l, gain=norm_mem[l], out_dtype=BF16)
        xf = _xattn(xf, norm_x[l], wq_b, kv.reshape(B, M, 2 * D), wo_b, l, S)

        xf = _ffn(xf, norm_ffn[l], f_up_b, f_conv_w[l], f_conv_b[l], f_down_b, l, S,
                  out_gain=norm_final if l == depth - 1 else None)

    return xf.reshape(B, S, D)
```
